```python
import math
import jax, jax.numpy as jnp
from jax import lax
import numpy as np

D_MODEL = 1024
BATCH = 4
SEQ = 8192
DEPTH = 4

D_FF = 2816
N_DIFF_HEADS = 4
DIFF_HEAD_DIM = 64
DIFF_WIDTH = N_DIFF_HEADS * 2 * DIFF_HEAD_DIM
Q_BLOCK = 128
NUM_BUCKETS = 32
MAX_DISTANCE = 128
SSM_HEADS = 8
SSM_HEAD_DIM = 64
SSM_WIDTH = SSM_HEADS * SSM_HEAD_DIM
SSM_GROUPS = 2
SSM_STATE = 128
SSM_CONV = 4
SSM_CHUNK = 128
HEADS_PER_GROUP = SSM_HEADS // SSM_GROUPS
SSM_XBC = SSM_WIDTH + 2 * SSM_GROUPS * SSM_STATE
QK_WIDTH = N_DIFF_HEADS * 2 * DIFF_HEAD_DIM
HYB_SPLITS = (QK_WIDTH, 2 * QK_WIDTH, 2 * QK_WIDTH + DIFF_WIDTH,
              2 * QK_WIDTH + DIFF_WIDTH + SSM_WIDTH,
              2 * QK_WIDTH + DIFF_WIDTH + SSM_WIDTH + SSM_XBC)
HYB_IN = HYB_SPLITS[-1] + SSM_HEADS
MIX_WIDTH = DIFF_WIDTH + SSM_WIDTH
SC_WIDTH = D_MODEL
SC_CONV = 3

kernel_name = "hybrid_diffattn_ssd_shortconv_macaron"


def rms_norm(x, w, eps=1e-6):
    xf = x.astype(jnp.float32)
    y = xf * lax.rsqrt(jnp.mean(xf * xf, axis=-1, keepdims=True) + eps)
    return (y * w.astype(jnp.float32)).astype(x.dtype)


def swiglu(h, wg, wu, wd):
    return (jax.nn.silu(h @ wg) * (h @ wu)) @ wd


def causal_dwconv(x, w):
    K, C = w.shape
    return lax.conv_general_dilated(
        x, w[:, None, :].astype(x.dtype), window_strides=(1,), padding=[(K - 1, 0)],
        dimension_numbers=("NWC", "WIO", "NWC"), feature_group_count=C)


def t5_bucket(dist):
    max_exact = NUM_BUCKETS // 2
    is_small = dist < max_exact
    large = max_exact + (
        jnp.log(jnp.maximum(dist, 1).astype(jnp.float32) / max_exact)
        / math.log(MAX_DISTANCE / max_exact) * (NUM_BUCKETS - max_exact)).astype(jnp.int32)
    large = jnp.minimum(large, NUM_BUCKETS - 1)
    return jnp.where(is_small, dist, large)


def diff_attention(q1, q2, k1, k2, v, lam, rel_bias):
    Bsz, S, H, d = q1.shape
    nblk = S // Q_BLOCK
    scale = d ** -0.5
    kpos = jnp.arange(S, dtype=jnp.int32)
    to_blocks = lambda t: jnp.swapaxes(t.reshape(Bsz, nblk, Q_BLOCK, H, d), 0, 1)
    starts = jnp.arange(nblk, dtype=jnp.int32) * Q_BLOCK
    table = rel_bias.astype(jnp.float32)

    def block(args):
        qb1, qb2, start = args
        qpos = start + jnp.arange(Q_BLOCK, dtype=jnp.int32)
        dist = qpos[:, None] - kpos[None, :]
        causal = dist >= 0
        bias = jnp.transpose(table[t5_bucket(jnp.maximum(dist, 0))], (2, 0, 1))
        s1 = jnp.einsum("bqhd,bkhd->bhqk", qb1, k1) * scale + bias
        s2 = jnp.einsum("bqhd,bkhd->bhqk", qb2, k2) * scale + bias
        p = (jax.nn.softmax(jnp.where(causal, s1, -jnp.inf), axis=-1)
             - lam * jax.nn.softmax(jnp.where(causal, s2, -jnp.inf), axis=-1))
        return jnp.einsum("bhqk,bkhe->bqhe", p, v)

    out = lax.map(block, (to_blocks(q1), to_blocks(q2), starts))
    return jnp.swapaxes(out, 0, 1).reshape(Bsz, S, H, 2 * d)


def segsum(a):
    T = a.shape[-1]
    cs = jnp.cumsum(a, axis=-1)
    seg = cs[..., :, None] - cs[..., None, :]
    mask = jnp.tril(jnp.ones((T, T), dtype=bool))
    return jnp.where(mask, seg, -jnp.inf)


def ssd_chunked(X, A, Bm, Cm):
    b, S, h, p = X.shape
    n = Bm.shape[-1]
    c = S // SSM_CHUNK
    X = X.reshape(b, c, SSM_CHUNK, h, p)
    Bm = Bm.reshape(b, c, SSM_CHUNK, h, n)
    Cm = Cm.reshape(b, c, SSM_CHUNK, h, n)
    A = jnp.transpose(A.reshape(b, c, SSM_CHUNK, h), (0, 3, 1, 2))
    A_cs = jnp.cumsum(A, axis=-1)
    L = jnp.exp(segsum(A))
    CB = jnp.einsum("bclhn,bcshn->bhcls", Cm, Bm)
    y_diag = jnp.einsum("bhcls,bcshp->bclhp", CB * L, X)
    decay_states = jnp.exp(A_cs[..., -1:] - A_cs)
    states = jnp.einsum("bclhn,bhcl,bclhp->bchpn", Bm, decay_states, X)
    states = jnp.concatenate([jnp.zeros_like(states[:, :1]), states], axis=1)
    decay_chunk = jnp.exp(segsum(jnp.pad(A_cs[..., -1], ((0, 0), (0, 0), (1, 0)))))
    states = jnp.einsum("bhzc,bchpn->bzhpn", decay_chunk, states)[:, :-1]
    y_off = jnp.einsum("bclhn,bchpn,bhcl->bclhp", Cm, states, jnp.exp(A_cs))
    return (y_diag + y_off).reshape(b, S, h, p)


def gated_group_rms_norm(y, z, w, eps=1e-5):
    g = y.astype(jnp.float32) * jax.nn.silu(z.astype(jnp.float32))
    shp = g.shape
    g = g.reshape(*shp[:-1], SSM_GROUPS, shp[-1] // SSM_GROUPS)
    g = g * lax.rsqrt(jnp.mean(g * g, axis=-1, keepdims=True) + eps)
    return g.reshape(shp) * w.astype(jnp.float32)


def hybrid_mixer(h, w_in, w_out, lq1, lk1, lq2, lk2, subln_w, conv_w, conv_b,
                 dt_bias, a_log, d_skip, norm_w, rel_bias, lam_init):
    Bsz, S, _ = h.shape
    q, k, v, z, xbc, dt = jnp.split(h @ w_in, HYB_SPLITS, axis=-1)
    q = q.reshape(Bsz, S, N_DIFF_HEADS, 2, DIFF_HEAD_DIM).astype(jnp.float32)
    k = k.reshape(Bsz, S, N_DIFF_HEADS, 2, DIFF_HEAD_DIM).astype(jnp.float32)
    v = v.reshape(Bsz, S, N_DIFF_HEADS, 2 * DIFF_HEAD_DIM).astype(jnp.float32)
    f32 = lambda t: t.astype(jnp.float32)
    lam = (jnp.exp(jnp.sum(f32(lq1) * f32(lk1))) - jnp.exp(jnp.sum(f32(lq2) * f32(lk2))) + lam_init)
    o = diff_attention(q[..., 0, :], q[..., 1, :], k[..., 0, :], k[..., 1, :], v, lam, rel_bias)
    o = rms_norm(o, subln_w, eps=1e-5) * (1.0 - lam_init)
    o = o.reshape(Bsz, S, DIFF_WIDTH).astype(h.dtype)
    xbc = jax.nn.silu(causal_dwconv(xbc, conv_w) + conv_b)
    xs, bm, cm = jnp.split(xbc, (SSM_WIDTH, SSM_WIDTH + SSM_GROUPS * SSM_STATE), axis=-1)
    xs = xs.reshape(Bsz, S, SSM_HEADS, SSM_HEAD_DIM).astype(jnp.float32)
    bm = jnp.repeat(bm.reshape(Bsz, S, SSM_GROUPS, SSM_STATE).astype(jnp.float32), HEADS_PER_GROUP, axis=2)
    cm = jnp.repeat(cm.reshape(Bsz, S, SSM_GROUPS, SSM_STATE).astype(jnp.float32), HEADS_PER_GROUP, axis=2)
    dt = jax.nn.softplus(dt.astype(jnp.float32) + f32(dt_bias))
    a = -jnp.exp(f32(a_log))
    y = ssd_chunked(xs * dt[..., None], a * dt, bm, cm) + f32(d_skip)[:, None] * xs
    y = gated_group_rms_norm(y.reshape(Bsz, S, SSM_WIDTH), z, norm_w).astype(h.dtype)
    return jnp.concatenate([o, y], axis=-1) @ w_out


def short_conv_mixer(h, w_in, conv_w, w_out):
    bg, cg, u = jnp.split(h @ w_in, 3, axis=-1)
    return (bg * causal_dwconv(cg * u, conv_w)) @ w_out


def setup_inputs(seed: int = 0) -> dict:
    key = jax.random.key(seed)
    ks = iter(jax.random.split(key, 48))
    nrm = lambda shape, scale: scale * jax.random.normal(next(ks), shape, jnp.float32)
    gain = lambda shape: 1.0 + nrm(shape, 0.02)
    NE = (DEPTH + 1) // 2
    NO = DEPTH // 2
    dt0 = jnp.exp(jax.random.uniform(next(ks), (NE, SSM_HEADS), jnp.float32,
                                     math.log(1e-3), math.log(1e-1)))
    return {
        "x": nrm((BATCH, SEQ, D_MODEL), 1.0),
        "rel_bias": nrm((NUM_BUCKETS, N_DIFF_HEADS), 0.5),
        "final_norm_w": gain((D_MODEL,)),
        "ffn1_norm": gain((DEPTH, D_MODEL)),
        "ffn1_wg": nrm((DEPTH, D_MODEL, D_FF), D_MODEL ** -0.5),
        "ffn1_wu": nrm((DEPTH, D_MODEL, D_FF), D_MODEL ** -0.5),
        "ffn1_wd": nrm((DEPTH, D_FF, D_MODEL), D_FF ** -0.5),
        "mix_norm": gain((DEPTH, D_MODEL)),
        "ffn2_norm": gain((DEPTH, D_MODEL)),
        "ffn2_wg": nrm((DEPTH, D_MODEL, D_FF), D_MODEL ** -0.5),
        "ffn2_wu": nrm((DEPTH, D_MODEL, D_FF), D_MODEL ** -0.5),
        "ffn2_wd": nrm((DEPTH, D_FF, D_MODEL), D_FF ** -0.5),
        "hyb_w_in": nrm((NE, D_MODEL, HYB_IN), D_MODEL ** -0.5),
        "hyb_w_out": nrm((NE, MIX_WIDTH, D_MODEL), MIX_WIDTH ** -0.5),
        "diff_lq1": nrm((NE, DIFF_HEAD_DIM), 0.1),
        "diff_lk1": nrm((NE, DIFF_HEAD_DIM), 0.1),
        "diff_lq2": nrm((NE, DIFF_HEAD_DIM), 0.1),
        "diff_lk2": nrm((NE, DIFF_HEAD_DIM), 0.1),
        "diff_subln_w": gain((NE, 2 * DIFF_HEAD_DIM)),
        "ssm_conv_w": nrm((NE, SSM_CONV, SSM_XBC), SSM_CONV ** -0.5),
        "ssm_conv_b": nrm((NE, SSM_XBC), 0.02),
        "ssm_dt_bias": dt0 + jnp.log(-jnp.expm1(-dt0)),
        "ssm_a_log": jnp.log(jax.random.uniform(next(ks), (NE, SSM_HEADS), jnp.float32, 1.0, 16.0)),
        "ssm_d": 1.0 + nrm((NE, SSM_HEADS), 0.1),
        "ssm_norm_w": gain((NE, SSM_WIDTH)),
        "sc_w_in": nrm((NO, D_MODEL, 3 * SC_WIDTH), D_MODEL ** -0.5),
        "sc_conv_w": nrm((NO, SC_CONV, SC_WIDTH), SC_CONV ** -0.5),
        "sc_w_out": nrm((NO, SC_WIDTH, D_MODEL), SC_WIDTH ** -0.5),
    }


def reference(x, rel_bias, final_norm_w, ffn1_norm, ffn1_wg, ffn1_wu, ffn1_wd, mix_norm,
              ffn2_norm, ffn2_wg, ffn2_wu, ffn2_wd, hyb_w_in, hyb_w_out, diff_lq1, diff_lk1,
              diff_lq2, diff_lk2, diff_subln_w, ssm_conv_w, ssm_conv_b, ssm_dt_bias, ssm_a_log,
              ssm_d, ssm_norm_w, sc_w_in, sc_conv_w, sc_w_out):
    for i in range(DEPTH):
        j = i // 2
        x = x + 0.5 * swiglu(rms_norm(x, ffn1_norm[i]), ffn1_wg[i], ffn1_wu[i], ffn1_wd[i])
        h = rms_norm(x, mix_norm[i])
        if i % 2 == 0:
            lam_init = 0.8 - 0.6 * math.exp(-0.3 * i)
            x = x + hybrid_mixer(h, hyb_w_in[j], hyb_w_out[j], diff_lq1[j], diff_lk1[j],
                                 diff_lq2[j], diff_lk2[j], diff_subln_w[j], ssm_conv_w[j],
                                 ssm_conv_b[j], ssm_dt_bias[j], ssm_a_log[j], ssm_d[j],
                                 ssm_norm_w[j], rel_bias, lam_init)
        else:
            x = x + short_conv_mixer(h, sc_w_in[j], sc_conv_w[j], sc_w_out[j])
        x = x + 0.5 * swiglu(rms_norm(x, ffn2_norm[i]), ffn2_wg[i], ffn2_wu[i], ffn2_wd[i])
    return rms_norm(x, final_norm_w)
```

```python
import functools
import math

import jax
import jax.numpy as jnp
from jax import lax
from jax.experimental import pallas as pl
from jax.experimental.pallas import tpu as pltpu

F32 = jnp.float32
BF16 = jnp.bfloat16

DEPTH = 4
N_DIFF_HEADS = 4
DIFF_HEAD_DIM = 64
DIFF_WIDTH = N_DIFF_HEADS * 2 * DIFF_HEAD_DIM
NUM_BUCKETS = 32
MAX_DISTANCE = 128
SSM_HEADS = 8
SSM_HEAD_DIM = 64
SSM_WIDTH = SSM_HEADS * SSM_HEAD_DIM
SSM_GROUPS = 2
SSM_STATE = 128
SSM_CONV = 4
SSM_CHUNK = 128
HEADS_PER_GROUP = SSM_HEADS // SSM_GROUPS
SSM_XBC = SSM_WIDTH + 2 * SSM_GROUPS * SSM_STATE
SC_CONV = 3

LANES = 128
SUBLANES = 8
V7X_SCOPED_VMEM_CAP = 60000 * 1024

FFN_TM = 512
FFN_FCHUNK = 256
PROJ_TM = 512
ATT_T = 512
SSD_LB = 512
HALO = SUBLANES

NEG_BIG = -1e30


def _vmem_limit(nbytes):
    return int(min(V7X_SCOPED_VMEM_CAP, nbytes * 5 // 4 + (8 << 20)))


def _dot(a, b):
    return jnp.dot(a, b, preferred_element_type=F32)


def _dot_nt(a, b):
    return lax.dot_general(a, b, (((1,), (1,)), ((), ())), preferred_element_type=F32)


def _rms(x, w, eps):
    ms = jnp.mean(x * x, axis=-1, keepdims=True)
    return x * lax.rsqrt(ms + eps) * w


def _silu(x):
    return x * jax.nn.sigmoid(x)


def _resident(shape):
    nd = len(shape)
    return pl.BlockSpec(shape, lambda *_: (0,) * nd, pipeline_mode=pl.Buffered(1))


def _ffn_kernel(*refs, pre_mix, final_norm, f_total):
    refs = list(refs)
    x_ref = refs.pop(0)
    if pre_mix:
        o_ref, y_ref, wo_ref = refs.pop(0), refs.pop(0), refs.pop(0)
    nw_ref, wg_ref, wu_ref, wd_ref = refs.pop(0), refs.pop(0), refs.pop(0), refs.pop(0)
    if final_norm:
        fw_ref = refs.pop(0)
    out_ref = refs.pop(0)

    x = x_ref[...]
    if pre_mix:
        half = o_ref.shape[1]
        x = x + _dot(o_ref[...], wo_ref[0:half, :]) + _dot(y_ref[...], wo_ref[half:, :])
    h = _rms(x, nw_ref[...], 1e-6).astype(BF16)
    acc = None
    for c0 in range(0, f_total, FFN_FCHUNK):
        c1 = c0 + FFN_FCHUNK
        g = _dot(h, wg_ref[:, c0:c1])
        u = _dot(h, wu_ref[:, c0:c1])
        a = (_silu(g) * u).astype(BF16)
        d = _dot(a, wd_ref[c0:c1, :])
        acc = d if acc is None else acc + d
    y = x + 0.5 * acc
    if final_norm:
        y = _rms(y, fw_ref[...], 1e-6)
    out_ref[...] = y


def _ffn_call(x, nw, wg, wu, wd, *, mix=None, final_w=None, tm=FFN_TM):
    T, D = x.shape
    F = wg.shape[1]
    assert T % tm == 0 and F % FFN_FCHUNK == 0
    row = lambda i: (i, 0)
    args = [x]
    in_specs = [pl.BlockSpec((tm, D), row)]
    nbytes = 4 * tm * D * 4 + 3 * D * F * 2 + 4 * tm * F * 4 // (F // FFN_FCHUNK) + 3 * tm * D * 4
    if mix is not None:
        o, y, wo = mix
        args += [o, y, wo]
        in_specs += [pl.BlockSpec((tm, o.shape[1]), row), pl.BlockSpec((tm, y.shape[1]), row),
                     _resident(wo.shape)]
        nbytes += wo.size * 2 + 4 * tm * o.shape[1] * 2 * 2
    args += [nw.reshape(1, D), wg, wu, wd]
    in_specs += [_resident((1, D)), _resident(wg.shape), _resident(wu.shape), _resident(wd.shape)]
    if final_w is not None:
        args.append(final_w.reshape(1, D))
        in_specs.append(_resident((1, D)))
    kern = functools.partial(_ffn_kernel, pre_mix=mix is not None, final_norm=final_w is not None,
                             f_total=F)
    return pl.pallas_call(
        kern,
        grid=(T // tm,),
        in_specs=in_specs,
        out_specs=pl.BlockSpec((tm, D), row),
        out_shape=jax.ShapeDtypeStruct((T, D), F32),
        compiler_params=pltpu.CompilerParams(
            dimension_semantics=("arbitrary",), vmem_limit_bytes=_vmem_limit(nbytes)),
        name="ffn",
    )(*args)


def _hyb_in_kernel(x_ref, nw_ref, w_ref, wvt_ref, q_ref, k_ref, vt_ref, z_ref, xbc_ref, dt_ref):
    h = _rms(x_ref[...], nw_ref[...], 1e-6).astype(BF16)
    qw = DIFF_WIDTH
    scale = DIFF_HEAD_DIM ** -0.5
    q_ref[...] = (_dot(h, w_ref[:, 0:qw]) * scale).astype(BF16)
    k_ref[...] = _dot(h, w_ref[:, qw:2 * qw]).astype(BF16)
    vt_ref[...] = _dot_nt(wvt_ref[...], h).astype(BF16)
    c = 2 * qw + DIFF_WIDTH
    z_ref[...] = _dot(h, w_ref[:, c:c + SSM_WIDTH])
    c += SSM_WIDTH
    xbc_ref[...] = _dot(h, w_ref[:, c:c + SSM_XBC])
    c += SSM_XBC
    dt_ref[...] = _dot(h, w_ref[:, c:c + LANES])


def _hyb_in_call(x, nw, w_pad, wvt, *, tm=PROJ_TM):
    T, D = x.shape
    row = lambda i: (i, 0)
    outs = [
        jax.ShapeDtypeStruct((T, DIFF_WIDTH), BF16),
        jax.ShapeDtypeStruct((T, DIFF_WIDTH), BF16),
        jax.ShapeDtypeStruct((DIFF_WIDTH, T), BF16),
        jax.ShapeDtypeStruct((T, SSM_WIDTH), F32),
        jax.ShapeDtypeStruct((T, SSM_XBC), F32),
        jax.ShapeDtypeStruct((T, LANES), F32),
    ]
    out_specs = [
        pl.BlockSpec((tm, DIFF_WIDTH), row), pl.BlockSpec((tm, DIFF_WIDTH), row),
        pl.BlockSpec((DIFF_WIDTH, tm), lambda i: (0, i)),
        pl.BlockSpec((tm, SSM_WIDTH), row), pl.BlockSpec((tm, SSM_XBC), row),
        pl.BlockSpec((tm, LANES), row),
    ]
    nbytes = (w_pad.size + wvt.size) * 2 + 2 * tm * D * 4 + 3 * tm * 4 * (3 * DIFF_WIDTH + SSM_WIDTH + SSM_XBC + LANES)
    return pl.pallas_call(
        _hyb_in_kernel,
        grid=(T // tm,),
        in_specs=[pl.BlockSpec((tm, D), row), _resident((1, D)), _resident(w_pad.shape),
                  _resident(wvt.shape)],
        out_specs=out_specs,
        out_shape=outs,
        compiler_params=pltpu.CompilerParams(
            dimension_semantics=("arbitrary",), vmem_limit_bytes=_vmem_limit(nbytes)),
        name="hyb_in",
    )(x, nw.reshape(1, D), w_pad, wvt)


def _prep_kernel(tab_ref, lq1_ref, lk1_ref, lq2_ref, lk2_ref, tile_ref, lam_ref, *, t, lam_inits):
    hd = pl.program_id(0)
    key = lax.broadcasted_iota(jnp.int32, (t, t), 0)
    qry = lax.broadcasted_iota(jnp.int32, (t, t), 1)
    max_exact = NUM_BUCKETS // 2
    far = tab_ref[hd, NUM_BUCKETS - 1]
    for off in range(2):
        dist = off * t + qry - key
        d = jnp.maximum(dist, 0)
        large = max_exact + (
            jnp.log(jnp.maximum(d, 1).astype(F32) / max_exact)
            / math.log(MAX_DISTANCE / max_exact) * (NUM_BUCKETS - max_exact)).astype(jnp.int32)
        bucket = jnp.where(d < max_exact, d, jnp.minimum(large, NUM_BUCKETS - 1))
        bias = jnp.full((t, t), far, F32)
        for b in range(NUM_BUCKETS - 1):
            bias = jnp.where(bucket == b, tab_ref[hd, b], bias)
        val = bias - far
        if off == 0:
            val = jnp.where(dist >= 0, val, NEG_BIG)
        tile_ref[off] = val
    for j, lam_init in enumerate(lam_inits):
        s1 = jnp.sum(lq1_ref[j:j + 1, :] * lk1_ref[j:j + 1, :], axis=-1, keepdims=True)
        s2 = jnp.sum(lq2_ref[j:j + 1, :] * lk2_ref[j:j + 1, :], axis=-1, keepdims=True)
        lam = jnp.exp(s1) - jnp.exp(s2) + lam_init
        lam_ref[j * SUBLANES:(j + 1) * SUBLANES, :] = jnp.broadcast_to(lam, (SUBLANES, LANES))


def _prep_call(rel_bias, lq1, lk1, lq2, lk2, lam_inits, *, t=ATT_T):
    ne = lq1.shape[0]
    H = rel_bias.shape[1]
    whole = lambda shape: pl.BlockSpec(shape, lambda h: (0,) * len(shape))
    return pl.pallas_call(
        functools.partial(_prep_kernel, t=t, lam_inits=tuple(lam_inits)),
        grid=(H,),
        in_specs=[pl.BlockSpec(memory_space=pltpu.SMEM)] + [whole(lq1.shape)] * 4,
        out_specs=[pl.BlockSpec((None, 2, t, t), lambda h: (h, 0, 0, 0)),
                   whole((ne * SUBLANES, LANES))],
        out_shape=[jax.ShapeDtypeStruct((H, 2, t, t), F32),
                   jax.ShapeDtypeStruct((ne * SUBLANES, LANES), F32)],
        compiler_params=pltpu.CompilerParams(
            dimension_semantics=("arbitrary",), vmem_limit_bytes=_vmem_limit(8 * t * t * 4)),
        name="attn_prep",
    )(rel_bias.T.astype(F32), lq1, lk1, lq2, lk2)


def _attn_kernel(q_ref, k_ref, vt_ref, tile_ref, lam_ref, sw_ref, o_ref,
                 m1_ref, l1_ref, a1_ref, m2_ref, l2_ref, a2_ref, *, t, lam_init):
    qi = pl.program_id(2)
    q = q_ref[...]
    first = lax.broadcasted_iota(jnp.int32, (t, LANES), 1) < DIFF_HEAD_DIM

    m1_ref[...] = jnp.full(m1_ref.shape, NEG_BIG, F32)
    m2_ref[...] = jnp.full(m2_ref.shape, NEG_BIG, F32)
    l1_ref[...] = jnp.zeros(l1_ref.shape, F32)
    l2_ref[...] = jnp.zeros(l2_ref.shape, F32)
    a1_ref[...] = jnp.zeros(a1_ref.shape, F32)
    a2_ref[...] = jnp.zeros(a2_ref.shape, F32)

    def online(m_ref, l_ref, a_ref, s, vt):
        m_old = m_ref[...]
        m_new = jnp.maximum(m_old, jnp.max(s, axis=0, keepdims=True))
        alpha = jnp.exp(m_old - m_new)
        p = jnp.exp(s - m_new)
        l_ref[...] = alpha * l_ref[...] + jnp.sum(p, axis=0, keepdims=True)
        a_ref[...] = alpha * a_ref[...] + _dot(vt, p.astype(BF16))
        m_ref[...] = m_new

    def block(j, tile_idx):
        start = pl.multiple_of(j * t, t)
        k = k_ref[pl.ds(start, t), :]
        vt = vt_ref[:, pl.ds(start, t)]
        zero = jnp.zeros_like(k)
        s1 = _dot_nt(jnp.where(first, k, zero), q)
        s2 = _dot_nt(jnp.where(first, zero, k), q)
        if tile_idx is not None:
            bias = tile_ref[tile_idx]
            s1 = s1 + bias
            s2 = s2 + bias
        online(m1_ref, l1_ref, a1_ref, s1, vt)
        online(m2_ref, l2_ref, a2_ref, s2, vt)

    def far_body(j, carry):
        block(j, None)
        return carry

    lax.fori_loop(0, jnp.maximum(qi - 1, 0), far_body, 0)

    @pl.when(qi >= 1)
    def _():
        block(qi - 1, 1)

    block(qi, 0)

    lam = lam_ref[0:1, 0:1]
    out_t = a1_ref[...] * (1.0 / l1_ref[...]) - lam * (a2_ref[...] * (1.0 / l2_ref[...]))
    ms = jnp.mean(out_t * out_t, axis=0, keepdims=True)
    out_t = out_t * lax.rsqrt(ms + 1e-5) * sw_ref[...] * (1.0 - lam_init)
    o_ref[...] = out_t.T.astype(BF16)


def _attn_call(q, k, vt, tiles, lam, subln_w, *, batch, lam_init, t=ATT_T):
    T, W = q.shape
    S = T // batch
    H = W // LANES
    nq = S // t
    e = 2 * DIFF_HEAD_DIM
    assert e == LANES and S % t == 0
    nbytes = 2 * (S * LANES * 2) * 2 + 2 * 2 * t * t * 4 + 8 * t * t * 4 + 4 * t * LANES * 4
    return pl.pallas_call(
        functools.partial(_attn_kernel, t=t, lam_init=lam_init),
        grid=(batch, H, nq),
        in_specs=[
            pl.BlockSpec((t, LANES), lambda b, h, i: (b * nq + i, h)),
            pl.BlockSpec((S, LANES), lambda b, h, i: (b, h)),
            pl.BlockSpec((LANES, S), lambda b, h, i: (h, b)),
            pl.BlockSpec((None, 2, t, t), lambda b, h, i: (h, 0, 0, 0)),
            pl.BlockSpec((SUBLANES, LANES), lambda b, h, i: (0, 0)),
            pl.BlockSpec((e, 1), lambda b, h, i: (0, 0)),
        ],
        out_specs=pl.BlockSpec((t, LANES), lambda b, h, i: (b * nq + i, h)),
        out_shape=jax.ShapeDtypeStruct((T, W), BF16),
        scratch_shapes=[pltpu.VMEM((1, t), F32), pltpu.VMEM((1, t), F32), pltpu.VMEM((e, t), F32),
                        pltpu.VMEM((1, t), F32), pltpu.VMEM((1, t), F32), pltpu.VMEM((e, t), F32)],
        compiler_params=pltpu.CompilerParams(
            dimension_semantics=("arbitrary", "arbitrary", "arbitrary"),
            vmem_limit_bytes=_vmem_limit(nbytes)),
        name="diff_attn",
    )(q, k, vt, tiles, lam, subln_w.reshape(e, 1))


def _split3(a):
    hi = a.astype(BF16)
    r1 = a - hi.astype(F32)
    mid = r1.astype(BF16)
    lo = (r1 - mid.astype(F32)).astype(BF16)
    return hi, mid, lo


def _ssd_kernel(xbc_ref, z_ref, dt_ref, cw_ref, cb_ref, dtb_ref, alog_ref, dsk_ref, nw_ref, y_ref,
                xpad_ref, st_ref, *, lb):
    L = SSM_CHUNK
    gw = HEADS_PER_GROUP * SSM_HEAD_DIM

    @pl.when(pl.program_id(1) == 0)
    def _():
        xpad_ref[0:HALO, :] = jnp.zeros((HALO, SSM_XBC), F32)
        st_ref[...] = jnp.zeros(st_ref.shape, F32)

    xpad_ref[HALO:HALO + lb, :] = xbc_ref[...]
    conv = cb_ref[...]
    for kk in range(SSM_CONV):
        o = HALO - (SSM_CONV - 1) + kk
        conv = conv + cw_ref[kk:kk + 1, :] * xpad_ref[o:o + lb, :]
    xc = _silu(conv)
    xpad_ref[0:HALO, :] = xpad_ref[lb:lb + HALO, :]

    dt_raw = dt_ref[...] + dtb_ref[...]
    dt = jnp.maximum(dt_raw, 0.0) + jnp.log1p(jnp.exp(-jnp.abs(dt_raw)))
    a_dt = -jnp.exp(alog_ref[...]) * dt

    row = lax.broadcasted_iota(jnp.int32, (L, L), 0)
    col = lax.broadcasted_iota(jnp.int32, (L, L), 1)
    causal = row >= col
    tri = jnp.where(causal, 1.0, 0.0).astype(BF16)
    lane_g = lax.broadcasted_iota(jnp.int32, (L, gw), 1) // SSM_HEAD_DIM
    lane_lo = lax.broadcasted_iota(jnp.int32, (L, LANES), 1) < SSM_HEAD_DIM
    lane_lo1 = lax.broadcasted_iota(jnp.int32, (1, LANES), 1) < SSM_HEAD_DIM

    for c in range(lb // L):
        r0 = c * L
        hi, mid, lo = _split3(a_dt[r0:r0 + L, :])
        cs = _dot(tri, hi) + _dot(tri, mid) + _dot(tri, lo)
        cs_t = cs.T
        dt_t = dt[r0:r0 + L, :].T
        xs = xc[r0:r0 + L, 0:SSM_WIDTH]
        ys = []
        for g in range(SSM_GROUPS):
            b0 = SSM_WIDTH + g * SSM_STATE
            c0 = SSM_WIDTH + SSM_GROUPS * SSM_STATE + g * SSM_STATE
            bg = xc[r0:r0 + L, b0:b0 + SSM_STATE]
            cg = xc[r0:r0 + L, c0:c0 + SSM_STATE].astype(BF16)
            gmat = _dot_nt(cg, bg.astype(BF16))
            bg_t = bg.T
            xg = xs[:, g * gw:(g + 1) * gw]
            ydiag = None
            st_new = None
            ecols, decs = [], []
            for hh in range(HEADS_PER_GROUP):
                h = g * HEADS_PER_GROUP + hh
                colb = jnp.broadcast_to(cs[:, h:h + 1], (L, L))
                rowb = cs_t[h:h + 1, :]
                dtrow = dt_t[h:h + 1, :]
                decay = jnp.exp(jnp.where(causal, colb - rowb, NEG_BIG))
                mh = (gmat * decay * dtrow).astype(BF16)
                xm = jnp.where(lane_g == hh, xg, 0.0).astype(BF16)
                last = colb[L - 1:L, :]
                wrow = dtrow * jnp.exp(last - rowb)
                btw = (bg_t * wrow).astype(BF16)
                d1 = _dot(mh, xm)
                d2 = _dot(btw, xm)
                ydiag = d1 if ydiag is None else ydiag + d1
                st_new = d2 if st_new is None else st_new + d2
                ecols.append(jnp.exp(colb))
                decs.append(jnp.exp(last))
            e_mat = jnp.concatenate([jnp.where(lane_lo, ecols[0], ecols[1]),
                                     jnp.where(lane_lo, ecols[2], ecols[3])], axis=1)
            dec = jnp.concatenate([jnp.where(lane_lo1, decs[0], decs[1]),
                                   jnp.where(lane_lo1, decs[2], decs[3])], axis=1)
            st_old = st_ref[g]
            ys.append(ydiag + _dot(cg, st_old.astype(BF16)) * e_mat)
            st_ref[g] = st_old * dec + st_new
        y = jnp.concatenate(ys, axis=1) + dsk_ref[...] * xs
        gated = y * _silu(z_ref[r0:r0 + L, :])
        outs = []
        for g in range(SSM_GROUPS):
            part = gated[:, g * gw:(g + 1) * gw]
            ms = jnp.mean(part * part, axis=-1, keepdims=True)
            outs.append(part * lax.rsqrt(ms + 1e-5))
        y_ref[r0:r0 + L, :] = (jnp.concatenate(outs, axis=1) * nw_ref[...]).astype(BF16)


def _ssd_call(xbc, z, dt, conv_w, conv_b, dt_bias, a_log, d_skip, norm_w, *, batch, lb=SSD_LB):
    T = xbc.shape[0]
    S = T // batch
    nb = S // lb
    assert S % lb == 0 and lb % SSM_CHUNK == 0
    pad = lambda v: jnp.pad(v.astype(F32), (0, LANES - v.shape[0])).reshape(1, LANES)
    dsk = jnp.repeat(d_skip.astype(F32), SSM_HEAD_DIM).reshape(1, SSM_WIDTH)
    row = lambda b, j: (b * nb + j, 0)
    const = lambda shape: pl.BlockSpec(shape, lambda b, j: (0, 0))
    nbytes = 3 * lb * (SSM_XBC + SSM_WIDTH + LANES) * 4 + 12 * lb * SSM_XBC * 4
    return pl.pallas_call(
        functools.partial(_ssd_kernel, lb=lb),
        grid=(batch, nb),
        in_specs=[pl.BlockSpec((lb, SSM_XBC), row), pl.BlockSpec((lb, SSM_WIDTH), row),
                  pl.BlockSpec((lb, LANES), row),
                  const((SSM_CONV, SSM_XBC)), const((1, SSM_XBC)), const((1, LANES)), const((1, LANES)),
                  const((1, SSM_WIDTH)), const((1, SSM_WIDTH))],
        out_specs=pl.BlockSpec((lb, SSM_WIDTH), row),
        out_shape=jax.ShapeDtypeStruct((T, SSM_WIDTH), BF16),
        scratch_shapes=[pltpu.VMEM((lb + HALO, SSM_XBC), F32),
                        pltpu.VMEM((SSM_GROUPS, SSM_STATE, HEADS_PER_GROUP * SSM_HEAD_DIM), F32)],
        compiler_params=pltpu.CompilerParams(
            dimension_semantics=("arbitrary", "arbitrary"), vmem_limit_bytes=_vmem_limit(nbytes)),
        name="ssd",
    )(xbc, z, dt, conv_w.astype(F32), conv_b.reshape(1, SSM_XBC).astype(F32), pad(dt_bias), pad(a_log),
      dsk, norm_w.reshape(1, SSM_WIDTH).astype(F32))


def _sc_kernel(x_ref, nw_ref, win_ref, cw_ref, wout_ref, o_ref, vpad_ref, *, tm, tiles_per_seq):
    d = x_ref.shape[1]

    @pl.when(pl.program_id(0) % tiles_per_seq == 0)
    def _():
        vpad_ref[0:HALO, :] = jnp.zeros((HALO, d), F32)

    x = x_ref[...]
    h = _rms(x, nw_ref[...], 1e-6).astype(BF16)
    bgate = _dot(h, win_ref[:, 0:d])
    v = _dot(h, win_ref[:, d:2 * d]) * _dot(h, win_ref[:, 2 * d:3 * d])
    vpad_ref[HALO:HALO + tm, :] = v
    conv = cw_ref[SC_CONV - 1:SC_CONV, :] * v
    for kk in range(SC_CONV - 1):
        o = HALO - (SC_CONV - 1) + kk
        conv = conv + cw_ref[kk:kk + 1, :] * vpad_ref[o:o + tm, :]
    vpad_ref[0:HALO, :] = vpad_ref[tm:tm + HALO, :]
    o_ref[...] = x + _dot((bgate * conv).astype(BF16), wout_ref[...])


def _sc_call(x, nw, w_in, conv_w, w_out, *, batch, tm=PROJ_TM):
    T, D = x.shape
    S = T // batch
    assert S % tm == 0
    row = lambda i: (i, 0)
    nbytes = (w_in.size + w_out.size) * 2 + 4 * tm * D * 4 + 8 * tm * D * 4
    return pl.pallas_call(
        functools.partial(_sc_kernel, tm=tm, tiles_per_seq=S // tm),
        grid=(T // tm,),
        in_specs=[pl.BlockSpec((tm, D), row), _resident((1, D)), _resident(w_in.shape),
                  _resident(conv_w.shape), _resident(w_out.shape)],
        out_specs=pl.BlockSpec((tm, D), row),
        out_shape=jax.ShapeDtypeStruct((T, D), F32),
        scratch_shapes=[pltpu.VMEM((tm + HALO, D), F32)],
        compiler_params=pltpu.CompilerParams(
            dimension_semantics=("arbitrary",), vmem_limit_bytes=_vmem_limit(nbytes)),
        name="short_conv",
    )(x, nw.reshape(1, D), w_in, conv_w.astype(F32), w_out)


def kernel(x, rel_bias, final_norm_w, ffn1_norm, ffn1_wg, ffn1_wu, ffn1_wd, mix_norm, ffn2_norm, ffn2_wg, ffn2_wu, ffn2_wd, hyb_w_in, hyb_w_out, diff_lq1, diff_lk1, diff_lq2, diff_lk2, diff_subln_w, ssm_conv_w, ssm_conv_b, ssm_dt_bias, ssm_a_log, ssm_d, ssm_norm_w, sc_w_in, sc_conv_w, sc_w_out):
    B, S, D = x.shape
    T = B * S
    xt = x.reshape(T, D)
    bf = lambda w: w.astype(BF16)

    lam_inits = [0.8 - 0.6 * math.exp(-0.3 * i) for i in range(0, DEPTH, 2)]
    tiles, lams = _prep_call(rel_bias, diff_lq1, diff_lk1, diff_lq2, diff_lk2, lam_inits)

    mix = None
    for i in range(DEPTH):
        j = i // 2
        xt = _ffn_call(xt, ffn1_norm[i], bf(ffn1_wg[i]), bf(ffn1_wu[i]), bf(ffn1_wd[i]))
        if i % 2 == 0:
            w_in = hyb_w_in[j]
            qkv_end = 3 * DIFF_WIDTH
            w_pad = bf(jnp.pad(w_in, ((0, 0), (0, LANES - SSM_HEADS))))
            wvt = bf(w_in[:, 2 * DIFF_WIDTH:qkv_end].T)
            q, k, vt, z, xbc, dt = _hyb_in_call(xt, mix_norm[i], w_pad, wvt)
            o = _attn_call(q, k, vt, tiles, lams[j * SUBLANES:(j + 1) * SUBLANES], diff_subln_w[j],
                           batch=B, lam_init=lam_inits[j])
            y = _ssd_call(xbc, z, dt, ssm_conv_w[j], ssm_conv_b[j], ssm_dt_bias[j], ssm_a_log[j],
                          ssm_d[j], ssm_norm_w[j], batch=B)
            mix = (o, y, bf(hyb_w_out[j]))
        else:
            xt = _sc_call(xt, mix_norm[i], bf(sc_w_in[j]), sc_conv_w[j], bf(sc_w_out[j]), batch=B)
            mix = None
        xt = _ffn_call(xt, ffn2_norm[i], bf(ffn2_wg[i]), bf(ffn2_wu[i]), bf(ffn2_wd[i]), mix=mix,
                       final_w=final_norm_w if i == DEPTH - 1 else None)
    return xt.reshape(B, S, D)
```

```python
import functools
import math

import jax
import jax.numpy as jnp
from jax import lax
from jax.experimental import pallas as pl
from jax.experimental.pallas import tpu as pltpu

F32 = jnp.float32
BF16 = jnp.bfloat16

DEPTH = 4
N_DIFF_HEADS = 4
DIFF_HEAD_DIM = 64
DIFF_WIDTH = N_DIFF_HEADS * 2 * DIFF_HEAD_DIM
NUM_BUCKETS = 32
MAX_DISTANCE = 128
SSM_HEADS = 8
SSM_HEAD_DIM = 64
SSM_WIDTH = SSM_HEADS * SSM_HEAD_DIM
SSM_GROUPS = 2
SSM_STATE = 128
SSM_CONV = 4
SSM_CHUNK = 128
HEADS_PER_GROUP = SSM_HEADS // SSM_GROUPS
SSM_XBC = SSM_WIDTH + 2 * SSM_GROUPS * SSM_STATE
SC_CONV = 3

LANES = 128
SUBLANES = 8
V7X_SCOPED_VMEM_CAP = 60000 * 1024

FFN_TM = 512
FFN_FCHUNK = 256
PROJ_TM = 512
ATT_T = 512
ATT_STRIP = 256
ATT_ONES = 16
ATT_FAR_TILE = 2
SSD_LB = 512
HALO = SUBLANES

NEG_BIG = -1e30
LOG2E = math.log2(math.e)


def _vmem_limit(nbytes):
    return int(min(V7X_SCOPED_VMEM_CAP, nbytes * 5 // 4 + (8 << 20)))


def _dot(a, b):
    return jnp.dot(a, b, preferred_element_type=F32)


def _dot_nt(a, b):
    return lax.dot_general(a, b, (((1,), (1,)), ((), ())), preferred_element_type=F32)


def _rms(x, w, eps):
    ms = jnp.mean(x * x, axis=-1, keepdims=True)
    return x * lax.rsqrt(ms + eps) * w


def _silu(x):
    return x * jax.nn.sigmoid(x)


def _resident(shape):
    nd = len(shape)
    return pl.BlockSpec(shape, lambda *_: (0,) * nd, pipeline_mode=pl.Buffered(1))


def _ffn_kernel(*refs, pre_mix, final_norm, f_total):
    refs = list(refs)
    x_ref = refs.pop(0)
    if pre_mix:
        o_ref, y_ref, wo_ref = refs.pop(0), refs.pop(0), refs.pop(0)
    nw_ref, wg_ref, wu_ref, wd_ref = refs.pop(0), refs.pop(0), refs.pop(0), refs.pop(0)
    if final_norm:
        fw_ref = refs.pop(0)
    out_ref = refs.pop(0)

    x = x_ref[...]
    if pre_mix:
        half = o_ref.shape[1]
        x = x + _dot(o_ref[...], wo_ref[0:half, :]) + _dot(y_ref[...], wo_ref[half:, :])
    h = _rms(x, nw_ref[...], 1e-6).astype(BF16)
    acc = None
    for c0 in range(0, f_total, FFN_FCHUNK):
        c1 = c0 + FFN_FCHUNK
        g = _dot(h, wg_ref[:, c0:c1])
        u = _dot(h, wu_ref[:, c0:c1])
        a = (_silu(g) * u).astype(BF16)
        d = _dot(a, wd_ref[c0:c1, :])
        acc = d if acc is None else acc + d
    y = x + 0.5 * acc
    if final_norm:
        y = _rms(y, fw_ref[...], 1e-6)
    out_ref[...] = y


def _ffn_call(x, nw, wg, wu, wd, *, mix=None, final_w=None, tm=FFN_TM):
    T, D = x.shape
    F = wg.shape[1]
    assert T % tm == 0 and F % FFN_FCHUNK == 0
    row = lambda i: (i, 0)
    args = [x]
    in_specs = [pl.BlockSpec((tm, D), row)]
    nbytes = 4 * tm * D * 4 + 3 * D * F * 2 + 4 * tm * F * 4 // (F // FFN_FCHUNK) + 3 * tm * D * 4
    if mix is not None:
        o, y, wo = mix
        args += [o, y, wo]
        in_specs += [pl.BlockSpec((tm, o.shape[1]), row), pl.BlockSpec((tm, y.shape[1]), row),
                     _resident(wo.shape)]
        nbytes += wo.size * 2 + 4 * tm * o.shape[1] * 2 * 2
    args += [nw.reshape(1, D), wg, wu, wd]
    in_specs += [_resident((1, D)), _resident(wg.shape), _resident(wu.shape), _resident(wd.shape)]
    if final_w is not None:
        args.append(final_w.reshape(1, D))
        in_specs.append(_resident((1, D)))
    kern = functools.partial(_ffn_kernel, pre_mix=mix is not None, final_norm=final_w is not None,
                             f_total=F)
    return pl.pallas_call(
        kern,
        grid=(T // tm,),
        in_specs=in_specs,
        out_specs=pl.BlockSpec((tm, D), row),
        out_shape=jax.ShapeDtypeStruct((T, D), F32),
        compiler_params=pltpu.CompilerParams(
            dimension_semantics=("arbitrary",), vmem_limit_bytes=_vmem_limit(nbytes)),
        name="ffn",
    )(*args)


def _hyb_in_kernel(x_ref, nw_ref, w_ref, wvt_ref, q_ref, k_ref, vt_ref, z_ref, xbc_ref, dt_ref):
    h = _rms(x_ref[...], nw_ref[...], 1e-6).astype(BF16)
    qw = DIFF_WIDTH
    scale = DIFF_HEAD_DIM ** -0.5 * LOG2E
    q_ref[...] = (_dot(h, w_ref[:, 0:qw]) * scale).astype(BF16)
    k_ref[...] = _dot(h, w_ref[:, qw:2 * qw]).astype(BF16)
    vt_ref[...] = _dot_nt(wvt_ref[...], h).astype(BF16)
    c = 2 * qw + DIFF_WIDTH
    z_ref[...] = _dot(h, w_ref[:, c:c + SSM_WIDTH])
    c += SSM_WIDTH
    xbc_ref[...] = _dot(h, w_ref[:, c:c + SSM_XBC])
    c += SSM_XBC
    dt_ref[...] = _dot(h, w_ref[:, c:c + LANES])


def _hyb_in_call(x, nw, w_pad, wvt, *, tm=PROJ_TM):
    T, D = x.shape
    row = lambda i: (i, 0)
    outs = [
        jax.ShapeDtypeStruct((T, DIFF_WIDTH), BF16),
        jax.ShapeDtypeStruct((T, DIFF_WIDTH), BF16),
        jax.ShapeDtypeStruct((DIFF_WIDTH, T), BF16),
        jax.ShapeDtypeStruct((T, SSM_WIDTH), F32),
        jax.ShapeDtypeStruct((T, SSM_XBC), F32),
        jax.ShapeDtypeStruct((T, LANES), F32),
    ]
    out_specs = [
        pl.BlockSpec((tm, DIFF_WIDTH), row), pl.BlockSpec((tm, DIFF_WIDTH), row),
        pl.BlockSpec((DIFF_WIDTH, tm), lambda i: (0, i)),
        pl.BlockSpec((tm, SSM_WIDTH), row), pl.BlockSpec((tm, SSM_XBC), row),
        pl.BlockSpec((tm, LANES), row),
    ]
    nbytes = (w_pad.size + wvt.size) * 2 + 2 * tm * D * 4 + 3 * tm * 4 * (3 * DIFF_WIDTH + SSM_WIDTH + SSM_XBC + LANES)
    return pl.pallas_call(
        _hyb_in_kernel,
        grid=(T // tm,),
        in_specs=[pl.BlockSpec((tm, D), row), _resident((1, D)), _resident(w_pad.shape),
                  _resident(wvt.shape)],
        out_specs=out_specs,
        out_shape=outs,
        compiler_params=pltpu.CompilerParams(
            dimension_semantics=("arbitrary",), vmem_limit_bytes=_vmem_limit(nbytes)),
        name="hyb_in",
    )(x, nw.reshape(1, D), w_pad, wvt)


def _prep_kernel(tab_ref, lq1_ref, lk1_ref, lq2_ref, lk2_ref, tile_ref, lam_ref, *, t, lam_inits):
    hd = pl.program_id(0)
    key = lax.broadcasted_iota(jnp.int32, (t, t), 0)
    qry = lax.broadcasted_iota(jnp.int32, (t, t), 1)
    max_exact = NUM_BUCKETS // 2
    far = tab_ref[hd, NUM_BUCKETS - 1]
    tile_ref[ATT_FAR_TILE] = jnp.zeros((t, t), F32)
    for off in range(ATT_FAR_TILE):
        dist = off * t + qry - key
        d = jnp.maximum(dist, 0)
        large = max_exact + (
            jnp.log(jnp.maximum(d, 1).astype(F32) / max_exact)
            / math.log(MAX_DISTANCE / max_exact) * (NUM_BUCKETS - max_exact)).astype(jnp.int32)
        bucket = jnp.where(d < max_exact, d, jnp.minimum(large, NUM_BUCKETS - 1))
        bias = jnp.full((t, t), far, F32)
        for b in range(NUM_BUCKETS - 1):
            bias = jnp.where(bucket == b, tab_ref[hd, b], bias)
        val = (bias - far) * LOG2E
        if off == 0:
            val = jnp.where(dist >= 0, val, NEG_BIG)
        tile_ref[off] = val
    for j, lam_init in enumerate(lam_inits):
        s1 = jnp.sum(lq1_ref[j:j + 1, :] * lk1_ref[j:j + 1, :], axis=-1, keepdims=True)
        s2 = jnp.sum(lq2_ref[j:j + 1, :] * lk2_ref[j:j + 1, :], axis=-1, keepdims=True)
        lam = jnp.exp(s1) - jnp.exp(s2) + lam_init
        lam_ref[j * SUBLANES:(j + 1) * SUBLANES, :] = jnp.broadcast_to(lam, (SUBLANES, LANES))


def _prep_call(rel_bias, lq1, lk1, lq2, lk2, lam_inits, *, t=ATT_T):
    ne = lq1.shape[0]
    H = rel_bias.shape[1]
    whole = lambda shape: pl.BlockSpec(shape, lambda h: (0,) * len(shape))
    return pl.pallas_call(
        functools.partial(_prep_kernel, t=t, lam_inits=tuple(lam_inits)),
        grid=(H,),
        in_specs=[pl.BlockSpec(memory_space=pltpu.SMEM)] + [whole(lq1.shape)] * 4,
        out_specs=[pl.BlockSpec((None, ATT_FAR_TILE + 1, t, t), lambda h: (h, 0, 0, 0)),
                   whole((ne * SUBLANES, LANES))],
        out_shape=[jax.ShapeDtypeStruct((H, ATT_FAR_TILE + 1, t, t), F32),
                   jax.ShapeDtypeStruct((ne * SUBLANES, LANES), F32)],
        compiler_params=pltpu.CompilerParams(
            dimension_semantics=("arbitrary",), vmem_limit_bytes=_vmem_limit(12 * t * t * 4)),
        name="attn_prep",
    )(rel_bias.T.astype(F32), lq1, lk1, lq2, lk2)


def _attn_kernel(q_ref, k_ref, vt_ref, tile_ref, lam_ref, sw_ref, o_ref, m_ref, acc_ref, s_ref, smax_ref, *,
                 t, lam_init):
    qi = pl.program_id(2)
    e = 2 * DIFF_HEAD_DIM
    q = q_ref[...]
    first = lax.broadcasted_iota(jnp.int32, (t, LANES), 1) < DIFF_HEAD_DIM
    zero = jnp.zeros_like(q)
    q_parts = (jnp.where(first, q, zero), jnp.where(first, zero, q))

    m_ref[...] = jnp.full(m_ref.shape, NEG_BIG, F32)
    acc_ref[...] = jnp.zeros(acc_ref.shape, F32)
    ones = jnp.ones((ATT_ONES, t), BF16)
    chains = [(w, slice(c0, c0 + ATT_STRIP)) for c0 in range(0, t, ATT_STRIP) for w in range(2)]

    def load_k(j):
        return k_ref[pl.ds(pl.multiple_of(j * t, t), t), :]

    def load_vta(j):
        vt = vt_ref[:, pl.ds(pl.multiple_of(j * t, t), t)]
        return jnp.concatenate([vt, ones], axis=0)

    def score_chain(ci, j, slot, k):
        w, cols = chains[ci]
        near = jnp.minimum(qi - j, ATT_FAR_TILE)
        s = _dot_nt(k, q_parts[w][cols, :]) + tile_ref[near, :, cols]
        s_ref[slot, ci] = s
        smax_ref[slot, ci] = jnp.max(s, axis=0, keepdims=True)

    def value_chain(ci, slot, vta):
        w, cols = chains[ci]
        m_old = m_ref[w, :, cols]
        m_new = jnp.maximum(m_old, smax_ref[slot, ci])
        p = jnp.exp2(s_ref[slot, ci] - m_new).astype(BF16)
        acc_ref[w, :, cols] = jnp.exp2(m_old - m_new) * acc_ref[w, :, cols] + _dot(vta, p)
        m_ref[w, :, cols] = m_new

    def step(j, slot):
        k_next = load_k(j + 1)
        vta = load_vta(j)
        for ci in range(len(chains)):
            score_chain(ci, j + 1, 1 - slot, k_next)
            value_chain(ci, slot, vta)

    def drain(slot):
        vta = load_vta(qi)
        for ci in range(len(chains)):
            value_chain(ci, slot, vta)

    k0 = load_k(0)
    for ci in range(len(chains)):
        score_chain(ci, 0, 0, k0)

    def pair(i, carry):
        step(2 * i, 0)
        step(2 * i + 1, 1)
        return carry

    lax.fori_loop(0, qi // 2, pair, 0)

    @pl.when(qi % 2 == 1)
    def _():
        step(qi - 1, 0)
        drain(1)

    @pl.when(qi % 2 == 0)
    def _():
        drain(0)

    lam = lam_ref[0:1, 0:1]
    o1 = acc_ref[0, 0:e, :] * (1.0 / acc_ref[0, e:e + 1, :])
    o2 = acc_ref[1, 0:e, :] * (1.0 / acc_ref[1, e:e + 1, :])
    out_t = o1 - lam * o2
    ms = jnp.mean(out_t * out_t, axis=0, keepdims=True)
    out_t = out_t * lax.rsqrt(ms + 1e-5) * sw_ref[...] * (1.0 - lam_init)
    o_ref[...] = out_t.T.astype(BF16)


def _attn_call(q, k, vt, tiles, lam, subln_w, *, batch, lam_init, t=ATT_T):
    T, W = q.shape
    S = T // batch
    H = W // LANES
    nq = S // t
    e = 2 * DIFF_HEAD_DIM
    assert e == LANES and S % t == 0
    n_chains = 2 * (t // ATT_STRIP)
    n_tiles = ATT_FAR_TILE + 1
    nbytes = (2 * (S * LANES * 2) * 2 + 2 * n_tiles * t * t * 4 + 2 * n_chains * t * ATT_STRIP * 4
              + 4 * t * t * 4 + 4 * t * LANES * 4)
    return pl.pallas_call(
        functools.partial(_attn_kernel, t=t, lam_init=lam_init),
        grid=(batch, H, nq),
        in_specs=[
            pl.BlockSpec((t, LANES), lambda b, h, i: (b * nq + i, h)),
            pl.BlockSpec((S, LANES), lambda b, h, i: (b, h)),
            pl.BlockSpec((LANES, S), lambda b, h, i: (h, b)),
            pl.BlockSpec((None, n_tiles, t, t), lambda b, h, i: (h, 0, 0, 0)),
            pl.BlockSpec((SUBLANES, LANES), lambda b, h, i: (0, 0)),
            pl.BlockSpec((e, 1), lambda b, h, i: (0, 0)),
        ],
        out_specs=pl.BlockSpec((t, LANES), lambda b, h, i: (b * nq + i, h)),
        out_shape=jax.ShapeDtypeStruct((T, W), BF16),
        scratch_shapes=[pltpu.VMEM((2, 1, t), F32), pltpu.VMEM((2, e + ATT_ONES, t), F32),
                        pltpu.VMEM((2, n_chains, t, ATT_STRIP), F32),
                        pltpu.VMEM((2, n_chains, 1, ATT_STRIP), F32)],
        compiler_params=pltpu.CompilerParams(
            dimension_semantics=("arbitrary", "arbitrary", "arbitrary"),
            vmem_limit_bytes=_vmem_limit(nbytes)),
        name="diff_attn",
    )(q, k, vt, tiles, lam, subln_w.reshape(e, 1))


def _split3(a):
    hi = a.astype(BF16)
    r1 = a - hi.astype(F32)
    mid = r1.astype(BF16)
    lo = (r1 - mid.astype(F32)).astype(BF16)
    return hi, mid, lo


def _ssd_kernel(xbc_ref, z_ref, dt_ref, cw_ref, cb_ref, dtb_ref, alog_ref, dsk_ref, nw_ref, y_ref,
                xpad_ref, st_ref, *, lb):
    L = SSM_CHUNK
    gw = HEADS_PER_GROUP * SSM_HEAD_DIM

    @pl.when(pl.program_id(1) == 0)
    def _():
        xpad_ref[0:HALO, :] = jnp.zeros((HALO, SSM_XBC), F32)
        st_ref[...] = jnp.zeros(st_ref.shape, F32)

    xpad_ref[HALO:HALO + lb, :] = xbc_ref[...]
    conv = cb_ref[...]
    for kk in range(SSM_CONV):
        o = HALO - (SSM_CONV - 1) + kk
        conv = conv + cw_ref[kk:kk + 1, :] * xpad_ref[o:o + lb, :]
    xc = _silu(conv)
    xpad_ref[0:HALO, :] = xpad_ref[lb:lb + HALO, :]

    dt_raw = dt_ref[...] + dtb_ref[...]
    dt = jnp.maximum(dt_raw, 0.0) + jnp.log1p(jnp.exp(-jnp.abs(dt_raw)))
    a_dt = -jnp.exp(alog_ref[...]) * dt

    row = lax.broadcasted_iota(jnp.int32, (L, L), 0)
    col = lax.broadcasted_iota(jnp.int32, (L, L), 1)
    causal = row >= col
    tri = jnp.where(causal, 1.0, 0.0).astype(BF16)
    lane_g = lax.broadcasted_iota(jnp.int32, (L, gw), 1) // SSM_HEAD_DIM
    lane_lo = lax.broadcasted_iota(jnp.int32, (L, LANES), 1) < SSM_HEAD_DIM
    lane_lo1 = lax.broadcasted_iota(jnp.int32, (1, LANES), 1) < SSM_HEAD_DIM

    for c in range(lb // L):
        r0 = c * L
        hi, mid, lo = _split3(a_dt[r0:r0 + L, :])
        cs = _dot(tri, hi) + _dot(tri, mid) + _dot(tri, lo)
        cs_t = cs.T
        dt_t = dt[r0:r0 + L, :].T
        xs = xc[r0:r0 + L, 0:SSM_WIDTH]
        ys = []
        for g in range(SSM_GROUPS):
            b0 = SSM_WIDTH + g * SSM_STATE
            c0 = SSM_WIDTH + SSM_GROUPS * SSM_STATE + g * SSM_STATE
            bg = xc[r0:r0 + L, b0:b0 + SSM_STATE]
            cg = xc[r0:r0 + L, c0:c0 + SSM_STATE].astype(BF16)
            gmat = _dot_nt(cg, bg.astype(BF16))
            bg_t = bg.T
            xg = xs[:, g * gw:(g + 1) * gw]
            ydiag = None
            st_new = None
            ecols, decs = [], []
            for hh in range(HEADS_PER_GROUP):
                h = g * HEADS_PER_GROUP + hh
                colb = jnp.broadcast_to(cs[:, h:h + 1], (L, L))
                rowb = cs_t[h:h + 1, :]
                dtrow = dt_t[h:h + 1, :]
                decay = jnp.exp(jnp.where(causal, colb - rowb, NEG_BIG))
                mh = (gmat * decay * dtrow).astype(BF16)
                xm = jnp.where(lane_g == hh, xg, 0.0).astype(BF16)
                last = colb[L - 1:L, :]
                wrow = dtrow * jnp.exp(last - rowb)
                btw = (bg_t * wrow).astype(BF16)
                d1 = _dot(mh, xm)
                d2 = _dot(btw, xm)
                ydiag = d1 if ydiag is None else ydiag + d1
                st_new = d2 if st_new is None else st_new + d2
                ecols.append(jnp.exp(colb))
                decs.append(jnp.exp(last))
            e_mat = jnp.concatenate([jnp.where(lane_lo, ecols[0], ecols[1]),
                                     jnp.where(lane_lo, ecols[2], ecols[3])], axis=1)
            dec = jnp.concatenate([jnp.where(lane_lo1, decs[0], decs[1]),
                                   jnp.where(lane_lo1, decs[2], decs[3])], axis=1)
            st_old = st_ref[g]
            ys.append(ydiag + _dot(cg, st_old.astype(BF16)) * e_mat)
            st_ref[g] = st_old * dec + st_new
        y = jnp.concatenate(ys, axis=1) + dsk_ref[...] * xs
        gated = y * _silu(z_ref[r0:r0 + L, :])
        outs = []
        for g in range(SSM_GROUPS):
            part = gated[:, g * gw:(g + 1) * gw]
            ms = jnp.mean(part * part, axis=-1, keepdims=True)
            outs.append(part * lax.rsqrt(ms + 1e-5))
        y_ref[r0:r0 + L, :] = (jnp.concatenate(outs, axis=1) * nw_ref[...]).astype(BF16)


def _ssd_call(xbc, z, dt, conv_w, conv_b, dt_bias, a_log, d_skip, norm_w, *, batch, lb=SSD_LB):
    T = xbc.shape[0]
    S = T // batch
    nb = S // lb
    assert S % lb == 0 and lb % SSM_CHUNK == 0
    pad = lambda v: jnp.pad(v.astype(F32), (0, LANES - v.shape[0])).reshape(1, LANES)
    dsk = jnp.repeat(d_skip.astype(F32), SSM_HEAD_DIM).reshape(1, SSM_WIDTH)
    row = lambda b, j: (b * nb + j, 0)
    const = lambda shape: pl.BlockSpec(shape, lambda b, j: (0, 0))
    nbytes = 3 * lb * (SSM_XBC + SSM_WIDTH + LANES) * 4 + 12 * lb * SSM_XBC * 4
    return pl.pallas_call(
        functools.partial(_ssd_kernel, lb=lb),
        grid=(batch, nb),
        in_specs=[pl.BlockSpec((lb, SSM_XBC), row), pl.BlockSpec((lb, SSM_WIDTH), row),
                  pl.BlockSpec((lb, LANES), row),
                  const((SSM_CONV, SSM_XBC)), const((1, SSM_XBC)), const((1, LANES)), const((1, LANES)),
                  const((1, SSM_WIDTH)), const((1, SSM_WIDTH))],
        out_specs=pl.BlockSpec((lb, SSM_WIDTH), row),
        out_shape=jax.ShapeDtypeStruct((T, SSM_WIDTH), BF16),
        scratch_shapes=[pltpu.VMEM((lb + HALO, SSM_XBC), F32),
                        pltpu.VMEM((SSM_GROUPS, SSM_STATE, HEADS_PER_GROUP * SSM_HEAD_DIM), F32)],
        compiler_params=pltpu.CompilerParams(
            dimension_semantics=("arbitrary", "arbitrary"), vmem_limit_bytes=_vmem_limit(nbytes)),
        name="ssd",
    )(xbc, z, dt, conv_w.astype(F32), conv_b.reshape(1, SSM_XBC).astype(F32), pad(dt_bias), pad(a_log),
      dsk, norm_w.reshape(1, SSM_WIDTH).astype(F32))


def _sc_kernel(x_ref, nw_ref, win_ref, cw_ref, wout_ref, o_ref, vpad_ref, *, tm, tiles_per_seq):
    d = x_ref.shape[1]

    @pl.when(pl.program_id(0) % tiles_per_seq == 0)
    def _():
        vpad_ref[0:HALO, :] = jnp.zeros((HALO, d), F32)

    x = x_ref[...]
    h = _rms(x, nw_ref[...], 1e-6).astype(BF16)
    bgate = _dot(h, win_ref[:, 0:d])
    v = _dot(h, win_ref[:, d:2 * d]) * _dot(h, win_ref[:, 2 * d:3 * d])
    vpad_ref[HALO:HALO + tm, :] = v
    conv = cw_ref[SC_CONV - 1:SC_CONV, :] * v
    for kk in range(SC_CONV - 1):
        o = HALO - (SC_CONV - 1) + kk
        conv = conv + cw_ref[kk:kk + 1, :] * vpad_ref[o:o + tm, :]
    vpad_ref[0:HALO, :] = vpad_ref[tm:tm + HALO, :]
    o_ref[...] = x + _dot((bgate * conv).astype(BF16), wout_ref[...])


def _sc_call(x, nw, w_in, conv_w, w_out, *, batch, tm=PROJ_TM):
    T, D = x.shape
    S = T // batch
    assert S % tm == 0
    row = lambda i: (i, 0)
    nbytes = (w_in.size + w_out.size) * 2 + 4 * tm * D * 4 + 8 * tm * D * 4
    return pl.pallas_call(
        functools.partial(_sc_kernel, tm=tm, tiles_per_seq=S // tm),
        grid=(T // tm,),
        in_specs=[pl.BlockSpec((tm, D), row), _resident((1, D)), _resident(w_in.shape),
                  _resident(conv_w.shape), _resident(w_out.shape)],
        out_specs=pl.BlockSpec((tm, D), row),
        out_shape=jax.ShapeDtypeStruct((T, D), F32),
        scratch_shapes=[pltpu.VMEM((tm + HALO, D), F32)],
        compiler_params=pltpu.CompilerParams(
            dimension_semantics=("arbitrary",), vmem_limit_bytes=_vmem_limit(nbytes)),
        name="short_conv",
    )(x, nw.reshape(1, D), w_in, conv_w.astype(F32), w_out)


def kernel(x, rel_bias, final_norm_w, ffn1_norm, ffn1_wg, ffn1_wu, ffn1_wd, mix_norm, ffn2_norm, ffn2_wg, ffn2_wu, ffn2_wd, hyb_w_in, hyb_w_out, diff_lq1, diff_lk1, diff_lq2, diff_lk2, diff_subln_w, ssm_conv_w, ssm_conv_b, ssm_dt_bias, ssm_a_log, ssm_d, ssm_norm_w, sc_w_in, sc_conv_w, sc_w_out):
    B, S, D = x.shape
    T = B * S
    xt = x.reshape(T, D)
    bf = lambda w: w.astype(BF16)

    lam_inits = [0.8 - 0.6 * math.exp(-0.3 * i) for i in range(0, DEPTH, 2)]
    tiles, lams = _prep_call(rel_bias, diff_lq1, diff_lk1, diff_lq2, diff_lk2, lam_inits)

    mix = None
    for i in range(DEPTH):
        j = i // 2
        xt = _ffn_call(xt, ffn1_norm[i], bf(ffn1_wg[i]), bf(ffn1_wu[i]), bf(ffn1_wd[i]))
        if i % 2 == 0:
            w_in = hyb_w_in[j]
            qkv_end = 3 * DIFF_WIDTH
            w_pad = bf(jnp.pad(w_in, ((0, 0), (0, LANES - SSM_HEADS))))
            wvt = bf(w_in[:, 2 * DIFF_WIDTH:qkv_end].T)
            q, k, vt, z, xbc, dt = _hyb_in_call(xt, mix_norm[i], w_pad, wvt)
            o = _attn_call(q, k, vt, tiles, lams[j * SUBLANES:(j + 1) * SUBLANES], diff_subln_w[j],
                           batch=B, lam_init=lam_inits[j])
            y = _ssd_call(xbc, z, dt, ssm_conv_w[j], ssm_conv_b[j], ssm_dt_bias[j], ssm_a_log[j],
                          ssm_d[j], ssm_norm_w[j], batch=B)
            mix = (o, y, bf(hyb_w_out[j]))
        else:
            xt = _sc_call(xt, mix_norm[i], bf(sc_w_in[j]), sc_conv_w[j], bf(sc_w_out[j]), batch=B)
            mix = None
        xt = _ffn_call(xt, ffn2_norm[i], bf(ffn2_wg[i]), bf(ffn2_wu[i]), bf(ffn2_wd[i]), mix=mix,
                       final_w=final_norm_w if i == DEPTH - 1 else None)
    return xt.reshape(B, S, D)
```

```python
import functools
import math

import jax
import jax.numpy as jnp
from jax import lax
from jax.experimental import pallas as pl
from jax.experimental.pallas import tpu as pltpu

F32 = jnp.float32
BF16 = jnp.bfloat16

DEPTH = 4
N_DIFF_HEADS = 4
DIFF_HEAD_DIM = 64
DIFF_WIDTH = N_DIFF_HEADS * 2 * DIFF_HEAD_DIM
NUM_BUCKETS = 32
MAX_DISTANCE = 128
SSM_HEADS = 8
SSM_HEAD_DIM = 64
SSM_WIDTH = SSM_HEADS * SSM_HEAD_DIM
SSM_GROUPS = 2
SSM_STATE = 128
SSM_CONV = 4
SSM_CHUNK = 128
HEADS_PER_GROUP = SSM_HEADS // SSM_GROUPS
SSM_XBC = SSM_WIDTH + 2 * SSM_GROUPS * SSM_STATE
SC_CONV = 3

LANES = 128
SUBLANES = 8
V7X_SCOPED_VMEM_CAP = 60000 * 1024

FFN_TM = 512
FFN_FCHUNK = 256
PROJ_TM = 512
ATT_T = 512
ATT_STRIP = 256
ATT_ONES = 16
ATT_FAR_TILE = 2
SSD_LB = 512
HALO = SUBLANES

NEG_BIG = -1e30
LOG2E = math.log2(math.e)


def _vmem_limit(nbytes):
    return int(min(V7X_SCOPED_VMEM_CAP, nbytes * 5 // 4 + (8 << 20)))


def _dot(a, b):
    return jnp.dot(a, b, preferred_element_type=F32)


def _dot_nt(a, b):
    return lax.dot_general(a, b, (((1,), (1,)), ((), ())), preferred_element_type=F32)


def _rms(x, w, eps):
    ms = jnp.mean(x * x, axis=-1, keepdims=True)
    return x * lax.rsqrt(ms + eps) * w


def _silu(x):
    return x * jax.nn.sigmoid(x)


def _resident(shape):
    nd = len(shape)
    return pl.BlockSpec(shape, lambda *_: (0,) * nd, pipeline_mode=pl.Buffered(1))


def _ffn_kernel(*refs, pre_mix, final_norm, f_total):
    refs = list(refs)
    x_ref = refs.pop(0)
    if pre_mix:
        o_ref, y_ref, wo_ref = refs.pop(0), refs.pop(0), refs.pop(0)
    nw_ref, wg_ref, wu_ref, wd_ref = refs.pop(0), refs.pop(0), refs.pop(0), refs.pop(0)
    if final_norm:
        fw_ref = refs.pop(0)
    out_ref = refs.pop(0)

    x = x_ref[...]
    if pre_mix:
        half = o_ref.shape[1]
        x = (x + _dot(o_ref[...], wo_ref[0:half, :].astype(BF16))
             + _dot(y_ref[...], wo_ref[half:, :].astype(BF16)))
    h = _rms(x, nw_ref[...], 1e-6).astype(BF16)
    acc = None
    for c0 in range(0, f_total, FFN_FCHUNK):
        c1 = c0 + FFN_FCHUNK
        g = _dot(h, wg_ref[:, c0:c1].astype(BF16))
        u = _dot(h, wu_ref[:, c0:c1].astype(BF16))
        a = (_silu(g) * u).astype(BF16)
        d = _dot(a, wd_ref[c0:c1, :].astype(BF16))
        acc = d if acc is None else acc + d
    y = x + 0.5 * acc
    if final_norm:
        y = _rms(y, fw_ref[...], 1e-6)
    out_ref[...] = y


def _layer_resident(stacked, layer):
    _, r, c = stacked.shape
    return pl.BlockSpec((None, r, c), lambda *_: (layer, 0, 0), pipeline_mode=pl.Buffered(1))


def _ffn_call(x, nw, wg, wu, wd, layer, *, mix=None, final_w=None, tm=FFN_TM):
    T, D = x.shape
    F = wg.shape[2]
    assert T % tm == 0 and F % FFN_FCHUNK == 0
    row = lambda i: (i, 0)
    args = [x]
    in_specs = [pl.BlockSpec((tm, D), row)]
    nbytes = 4 * tm * D * 4 + 3 * D * F * 4 + 6 * tm * FFN_FCHUNK * 4 + 3 * tm * D * 4
    if mix is not None:
        o, y, wo, mix_layer = mix
        args += [o, y, wo]
        in_specs += [pl.BlockSpec((tm, o.shape[1]), row), pl.BlockSpec((tm, y.shape[1]), row),
                     _layer_resident(wo, mix_layer)]
        nbytes += wo.shape[1] * wo.shape[2] * 4 + 4 * tm * o.shape[1] * 2 * 2
    args += [nw.reshape(1, D), wg, wu, wd]
    in_specs += [_resident((1, D)), _layer_resident(wg, layer), _layer_resident(wu, layer),
                 _layer_resident(wd, layer)]
    if final_w is not None:
        args.append(final_w.reshape(1, D))
        in_specs.append(_resident((1, D)))
    kern = functools.partial(_ffn_kernel, pre_mix=mix is not None, final_norm=final_w is not None,
                             f_total=F)
    return pl.pallas_call(
        kern,
        grid=(T // tm,),
        in_specs=in_specs,
        out_specs=pl.BlockSpec((tm, D), row),
        out_shape=jax.ShapeDtypeStruct((T, D), F32),
        compiler_params=pltpu.CompilerParams(
            dimension_semantics=("arbitrary",), vmem_limit_bytes=_vmem_limit(nbytes)),
        name="ffn",
    )(*args)


def _hyb_in_kernel(x_ref, nw_ref, w_ref, wvt_ref, q_ref, k_ref, vt_ref, z_ref, xbc_ref, dt_ref):
    h = _rms(x_ref[...], nw_ref[...], 1e-6).astype(BF16)
    qw = DIFF_WIDTH
    scale = DIFF_HEAD_DIM ** -0.5 * LOG2E
    q_ref[...] = (_dot(h, w_ref[:, 0:qw]) * scale).astype(BF16)
    k_ref[...] = _dot(h, w_ref[:, qw:2 * qw]).astype(BF16)
    vt_ref[...] = _dot_nt(wvt_ref[...], h).astype(BF16)
    c = 2 * qw + DIFF_WIDTH
    z_ref[...] = _dot(h, w_ref[:, c:c + SSM_WIDTH])
    c += SSM_WIDTH
    xbc_ref[...] = _dot(h, w_ref[:, c:c + SSM_XBC])
    c += SSM_XBC
    dt_ref[...] = _dot(h, w_ref[:, c:c + LANES])


def _hyb_in_call(x, nw, w_pad, wvt, *, tm=PROJ_TM):
    T, D = x.shape
    row = lambda i: (i, 0)
    outs = [
        jax.ShapeDtypeStruct((T, DIFF_WIDTH), BF16),
        jax.ShapeDtypeStruct((T, DIFF_WIDTH), BF16),
        jax.ShapeDtypeStruct((DIFF_WIDTH, T), BF16),
        jax.ShapeDtypeStruct((T, SSM_WIDTH), F32),
        jax.ShapeDtypeStruct((T, SSM_XBC), F32),
        jax.ShapeDtypeStruct((T, LANES), F32),
    ]
    out_specs = [
        pl.BlockSpec((tm, DIFF_WIDTH), row), pl.BlockSpec((tm, DIFF_WIDTH), row),
        pl.BlockSpec((DIFF_WIDTH, tm), lambda i: (0, i)),
        pl.BlockSpec((tm, SSM_WIDTH), row), pl.BlockSpec((tm, SSM_XBC), row),
        pl.BlockSpec((tm, LANES), row),
    ]
    nbytes = (w_pad.size + wvt.size) * 2 + 2 * tm * D * 4 + 3 * tm * 4 * (3 * DIFF_WIDTH + SSM_WIDTH + SSM_XBC + LANES)
    return pl.pallas_call(
        _hyb_in_kernel,
        grid=(T // tm,),
        in_specs=[pl.BlockSpec((tm, D), row), _resident((1, D)), _resident(w_pad.shape),
                  _resident(wvt.shape)],
        out_specs=out_specs,
        out_shape=outs,
        compiler_params=pltpu.CompilerParams(
            dimension_semantics=("arbitrary",), vmem_limit_bytes=_vmem_limit(nbytes)),
        name="hyb_in",
    )(x, nw.reshape(1, D), w_pad, wvt)


def _prep_kernel(tab_ref, lq1_ref, lk1_ref, lq2_ref, lk2_ref, tile_ref, lam_ref, *, t, lam_inits):
    hd = pl.program_id(0)
    key = lax.broadcasted_iota(jnp.int32, (t, t), 0)
    qry = lax.broadcasted_iota(jnp.int32, (t, t), 1)
    max_exact = NUM_BUCKETS // 2
    far = tab_ref[hd, NUM_BUCKETS - 1]
    tile_ref[ATT_FAR_TILE] = jnp.zeros((t, t), F32)
    for off in range(ATT_FAR_TILE):
        dist = off * t + qry - key
        d = jnp.maximum(dist, 0)
        large = max_exact + (
            jnp.log(jnp.maximum(d, 1).astype(F32) / max_exact)
            / math.log(MAX_DISTANCE / max_exact) * (NUM_BUCKETS - max_exact)).astype(jnp.int32)
        bucket = jnp.where(d < max_exact, d, jnp.minimum(large, NUM_BUCKETS - 1))
        bias = jnp.full((t, t), far, F32)
        for b in range(NUM_BUCKETS - 1):
            bias = jnp.where(bucket == b, tab_ref[hd, b], bias)
        val = (bias - far) * LOG2E
        if off == 0:
            val = jnp.where(dist >= 0, val, NEG_BIG)
        tile_ref[off] = val
    for j, lam_init in enumerate(lam_inits):
        s1 = jnp.sum(lq1_ref[j:j + 1, :] * lk1_ref[j:j + 1, :], axis=-1, keepdims=True)
        s2 = jnp.sum(lq2_ref[j:j + 1, :] * lk2_ref[j:j + 1, :], axis=-1, keepdims=True)
        lam = jnp.exp(s1) - jnp.exp(s2) + lam_init
        lam_ref[j * SUBLANES:(j + 1) * SUBLANES, :] = jnp.broadcast_to(lam, (SUBLANES, LANES))


def _prep_call(rel_bias, lq1, lk1, lq2, lk2, lam_inits, *, t=ATT_T):
    ne = lq1.shape[0]
    H = rel_bias.shape[1]
    whole = lambda shape: pl.BlockSpec(shape, lambda h: (0,) * len(shape))
    return pl.pallas_call(
        functools.partial(_prep_kernel, t=t, lam_inits=tuple(lam_inits)),
        grid=(H,),
        in_specs=[pl.BlockSpec(memory_space=pltpu.SMEM)] + [whole(lq1.shape)] * 4,
        out_specs=[pl.BlockSpec((None, ATT_FAR_TILE + 1, t, t), lambda h: (h, 0, 0, 0)),
                   whole((ne * SUBLANES, LANES))],
        out_shape=[jax.ShapeDtypeStruct((H, ATT_FAR_TILE + 1, t, t), F32),
                   jax.ShapeDtypeStruct((ne * SUBLANES, LANES), F32)],
        compiler_params=pltpu.CompilerParams(
            dimension_semantics=("arbitrary",), vmem_limit_bytes=_vmem_limit(12 * t * t * 4)),
        name="attn_prep",
    )(rel_bias.T.astype(F32), lq1, lk1, lq2, lk2)


def _attn_kernel(q_ref, k_ref, vt_ref, tile_ref, lam_ref, sw_ref, o_ref, m_ref, acc_ref, s_ref, smax_ref, *,
                 t, lam_init):
    qi = pl.program_id(2)
    e = 2 * DIFF_HEAD_DIM
    q = q_ref[...]
    first = lax.broadcasted_iota(jnp.int32, (t, LANES), 1) < DIFF_HEAD_DIM
    zero = jnp.zeros_like(q)
    q_parts = (jnp.where(first, q, zero), jnp.where(first, zero, q))

    m_ref[...] = jnp.full(m_ref.shape, NEG_BIG, F32)
    acc_ref[...] = jnp.zeros(acc_ref.shape, F32)
    ones = jnp.ones((ATT_ONES, t), BF16)
    chains = [(w, slice(c0, c0 + ATT_STRIP)) for c0 in range(0, t, ATT_STRIP) for w in range(2)]

    def load_k(j):
        return k_ref[pl.ds(pl.multiple_of(j * t, t), t), :]

    def load_vta(j):
        vt = vt_ref[:, pl.ds(pl.multiple_of(j * t, t), t)]
        return jnp.concatenate([vt, ones], axis=0)

    def score_chain(ci, j, slot, k):
        w, cols = chains[ci]
        near = jnp.minimum(qi - j, ATT_FAR_TILE)
        s = _dot_nt(k, q_parts[w][cols, :]) + tile_ref[near, :, cols]
        s_ref[slot, ci] = s
        smax_ref[slot, ci] = jnp.max(s, axis=0, keepdims=True)

    def value_chain(ci, slot, vta):
        w, cols = chains[ci]
        m_old = m_ref[w, :, cols]
        m_new = jnp.maximum(m_old, smax_ref[slot, ci])
        p = jnp.exp2(s_ref[slot, ci] - m_new).astype(BF16)
        acc_ref[w, :, cols] = jnp.exp2(m_old - m_new) * acc_ref[w, :, cols] + _dot(vta, p)
        m_ref[w, :, cols] = m_new

    def step(j, slot):
        k_next = load_k(j + 1)
        vta = load_vta(j)
        for ci in range(len(chains)):
            score_chain(ci, j + 1, 1 - slot, k_next)
            value_chain(ci, slot, vta)

    def drain(slot):
        vta = load_vta(qi)
        for ci in range(len(chains)):
            value_chain(ci, slot, vta)

    k0 = load_k(0)
    for ci in range(len(chains)):
        score_chain(ci, 0, 0, k0)

    def pair(i, carry):
        step(2 * i, 0)
        step(2 * i + 1, 1)
        return carry

    lax.fori_loop(0, qi // 2, pair, 0)

    @pl.when(qi % 2 == 1)
    def _():
        step(qi - 1, 0)
        drain(1)

    @pl.when(qi % 2 == 0)
    def _():
        drain(0)

    lam = lam_ref[0:1, 0:1]
    o1 = acc_ref[0, 0:e, :] * (1.0 / acc_ref[0, e:e + 1, :])
    o2 = acc_ref[1, 0:e, :] * (1.0 / acc_ref[1, e:e + 1, :])
    out_t = o1 - lam * o2
    ms = jnp.mean(out_t * out_t, axis=0, keepdims=True)
    out_t = out_t * lax.rsqrt(ms + 1e-5) * sw_ref[...] * (1.0 - lam_init)
    o_ref[...] = out_t.T.astype(BF16)


def _attn_call(q, k, vt, tiles, lam, subln_w, *, batch, lam_init, t=ATT_T):
    T, W = q.shape
    S = T // batch
    H = W // LANES
    nq = S // t
    e = 2 * DIFF_HEAD_DIM
    assert e == LANES and S % t == 0
    n_chains = 2 * (t // ATT_STRIP)
    n_tiles = ATT_FAR_TILE + 1
    nbytes = (2 * (S * LANES * 2) * 2 + 2 * n_tiles * t * t * 4 + 2 * n_chains * t * ATT_STRIP * 4
              + 4 * t * t * 4 + 4 * t * LANES * 4)
    return pl.pallas_call(
        functools.partial(_attn_kernel, t=t, lam_init=lam_init),
        grid=(batch, H, nq),
        in_specs=[
            pl.BlockSpec((t, LANES), lambda b, h, i: (b * nq + i, h)),
            pl.BlockSpec((S, LANES), lambda b, h, i: (b, h)),
            pl.BlockSpec((LANES, S), lambda b, h, i: (h, b)),
            pl.BlockSpec((None, n_tiles, t, t), lambda b, h, i: (h, 0, 0, 0)),
            pl.BlockSpec((SUBLANES, LANES), lambda b, h, i: (0, 0)),
            pl.BlockSpec((e, 1), lambda b, h, i: (0, 0)),
        ],
        out_specs=pl.BlockSpec((t, LANES), lambda b, h, i: (b * nq + i, h)),
        out_shape=jax.ShapeDtypeStruct((T, W), BF16),
        scratch_shapes=[pltpu.VMEM((2, 1, t), F32), pltpu.VMEM((2, e + ATT_ONES, t), F32),
                        pltpu.VMEM((2, n_chains, t, ATT_STRIP), F32),
                        pltpu.VMEM((2, n_chains, 1, ATT_STRIP), F32)],
        compiler_params=pltpu.CompilerParams(
            dimension_semantics=("arbitrary", "arbitrary", "arbitrary"),
            vmem_limit_bytes=_vmem_limit(nbytes)),
        name="diff_attn",
    )(q, k, vt, tiles, lam, subln_w.reshape(e, 1))


def _split3(a):
    hi = a.astype(BF16)
    r1 = a - hi.astype(F32)
    mid = r1.astype(BF16)
    lo = (r1 - mid.astype(F32)).astype(BF16)
    return hi, mid, lo


def _ssd_kernel(xbc_ref, z_ref, dt_ref, cw_ref, cb_ref, dtb_ref, alog_ref, dsk_ref, nw_ref, y_ref,
                xpad_ref, st_ref, *, lb):
    L = SSM_CHUNK
    gw = HEADS_PER_GROUP * SSM_HEAD_DIM

    @pl.when(pl.program_id(1) == 0)
    def _():
        xpad_ref[0:HALO, :] = jnp.zeros((HALO, SSM_XBC), F32)
        st_ref[...] = jnp.zeros(st_ref.shape, F32)

    xpad_ref[HALO:HALO + lb, :] = xbc_ref[...]
    conv = cb_ref[...]
    for kk in range(SSM_CONV):
        o = HALO - (SSM_CONV - 1) + kk
        conv = conv + cw_ref[kk:kk + 1, :] * xpad_ref[o:o + lb, :]
    xc = _silu(conv)
    xpad_ref[0:HALO, :] = xpad_ref[lb:lb + HALO, :]

    dt_raw = dt_ref[...] + dtb_ref[...]
    dt = jnp.maximum(dt_raw, 0.0) + jnp.log1p(jnp.exp(-jnp.abs(dt_raw)))
    a_dt = -jnp.exp(alog_ref[...]) * dt

    row = lax.broadcasted_iota(jnp.int32, (L, L), 0)
    col = lax.broadcasted_iota(jnp.int32, (L, L), 1)
    causal = row >= col
    tri = jnp.where(causal, 1.0, 0.0).astype(BF16)
    lane_g = lax.broadcasted_iota(jnp.int32, (L, gw), 1) // SSM_HEAD_DIM
    lane_lo = lax.broadcasted_iota(jnp.int32, (L, LANES), 1) < SSM_HEAD_DIM
    lane_lo1 = lax.broadcasted_iota(jnp.int32, (1, LANES), 1) < SSM_HEAD_DIM

    for c in range(lb // L):
        r0 = c * L
        hi, mid, lo = _split3(a_dt[r0:r0 + L, :])
        cs = _dot(tri, hi) + _dot(tri, mid) + _dot(tri, lo)
        cs_t = cs.T
        dt_t = dt[r0:r0 + L, :].T
        xs = xc[r0:r0 + L, 0:SSM_WIDTH]
        ys = []
        for g in range(SSM_GROUPS):
            b0 = SSM_WIDTH + g * SSM_STATE
            c0 = SSM_WIDTH + SSM_GROUPS * SSM_STATE + g * SSM_STATE
            bg = xc[r0:r0 + L, b0:b0 + SSM_STATE]
            cg = xc[r0:r0 + L, c0:c0 + SSM_STATE].astype(BF16)
            gmat = _dot_nt(cg, bg.astype(BF16))
            bg_t = bg.T
            xg = xs[:, g * gw:(g + 1) * gw]
            ydiag = None
            st_new = None
            ecols, decs = [], []
            for hh in range(HEADS_PER_GROUP):
                h = g * HEADS_PER_GROUP + hh
                colb = jnp.broadcast_to(cs[:, h:h + 1], (L, L))
                rowb = cs_t[h:h + 1, :]
                dtrow = dt_t[h:h + 1, :]
                decay = jnp.exp(jnp.where(causal, colb - rowb, NEG_BIG))
                mh = (gmat * decay * dtrow).astype(BF16)
                xm = jnp.where(lane_g == hh, xg, 0.0).astype(BF16)
                last = colb[L - 1:L, :]
                wrow = dtrow * jnp.exp(last - rowb)
                btw = (bg_t * wrow).astype(BF16)
                d1 = _dot(mh, xm)
                d2 = _dot(btw, xm)
                ydiag = d1 if ydiag is None else ydiag + d1
                st_new = d2 if st_new is None else st_new + d2
                ecols.append(jnp.exp(colb))
                decs.append(jnp.exp(last))
            e_mat = jnp.concatenate([jnp.where(lane_lo, ecols[0], ecols[1]),
                                     jnp.where(lane_lo, ecols[2], ecols[3])], axis=1)
            dec = jnp.concatenate([jnp.where(lane_lo1, decs[0], decs[1]),
                                   jnp.where(lane_lo1, decs[2], decs[3])], axis=1)
            st_old = st_ref[g]
            ys.append(ydiag + _dot(cg, st_old.astype(BF16)) * e_mat)
            st_ref[g] = st_old * dec + st_new
        y = jnp.concatenate(ys, axis=1) + dsk_ref[...] * xs
        gated = y * _silu(z_ref[r0:r0 + L, :])
        outs = []
        for g in range(SSM_GROUPS):
            part = gated[:, g * gw:(g + 1) * gw]
            ms = jnp.mean(part * part, axis=-1, keepdims=True)
            outs.append(part * lax.rsqrt(ms + 1e-5))
        y_ref[r0:r0 + L, :] = (jnp.concatenate(outs, axis=1) * nw_ref[...]).astype(BF16)


def _ssd_call(xbc, z, dt, conv_w, conv_b, dt_bias, a_log, d_skip, norm_w, *, batch, lb=SSD_LB):
    T = xbc.shape[0]
    S = T // batch
    nb = S // lb
    assert S % lb == 0 and lb % SSM_CHUNK == 0
    pad = lambda v: jnp.pad(v.astype(F32), (0, LANES - v.shape[0])).reshape(1, LANES)
    dsk = jnp.repeat(d_skip.astype(F32), SSM_HEAD_DIM).reshape(1, SSM_WIDTH)
    row = lambda b, j: (b * nb + j, 0)
    const = lambda shape: pl.BlockSpec(shape, lambda b, j: (0, 0))
    nbytes = 3 * lb * (SSM_XBC + SSM_WIDTH + LANES) * 4 + 12 * lb * SSM_XBC * 4
    return pl.pallas_call(
        functools.partial(_ssd_kernel, lb=lb),
        grid=(batch, nb),
        in_specs=[pl.BlockSpec((lb, SSM_XBC), row), pl.BlockSpec((lb, SSM_WIDTH), row),
                  pl.BlockSpec((lb, LANES), row),
                  const((SSM_CONV, SSM_XBC)), const((1, SSM_XBC)), const((1, LANES)), const((1, LANES)),
                  const((1, SSM_WIDTH)), const((1, SSM_WIDTH))],
        out_specs=pl.BlockSpec((lb, SSM_WIDTH), row),
        out_shape=jax.ShapeDtypeStruct((T, SSM_WIDTH), BF16),
        scratch_shapes=[pltpu.VMEM((lb + HALO, SSM_XBC), F32),
                        pltpu.VMEM((SSM_GROUPS, SSM_STATE, HEADS_PER_GROUP * SSM_HEAD_DIM), F32)],
        compiler_params=pltpu.CompilerParams(
            dimension_semantics=("arbitrary", "arbitrary"), vmem_limit_bytes=_vmem_limit(nbytes)),
        name="ssd",
    )(xbc, z, dt, conv_w.astype(F32), conv_b.reshape(1, SSM_XBC).astype(F32), pad(dt_bias), pad(a_log),
      dsk, norm_w.reshape(1, SSM_WIDTH).astype(F32))


def _sc_kernel(x_ref, nw_ref, win_ref, cw_ref, wout_ref, o_ref, vpad_ref, *, tm, tiles_per_seq):
    d = x_ref.shape[1]

    @pl.when(pl.program_id(0) % tiles_per_seq == 0)
    def _():
        vpad_ref[0:HALO, :] = jnp.zeros((HALO, d), F32)

    x = x_ref[...]
    h = _rms(x, nw_ref[...], 1e-6).astype(BF16)
    bgate = _dot(h, win_ref[:, 0:d])
    v = _dot(h, win_ref[:, d:2 * d]) * _dot(h, win_ref[:, 2 * d:3 * d])
    vpad_ref[HALO:HALO + tm, :] = v
    conv = cw_ref[SC_CONV - 1:SC_CONV, :] * v
    for kk in range(SC_CONV - 1):
        o = HALO - (SC_CONV - 1) + kk
        conv = conv + cw_ref[kk:kk + 1, :] * vpad_ref[o:o + tm, :]
    vpad_ref[0:HALO, :] = vpad_ref[tm:tm + HALO, :]
    o_ref[...] = x + _dot((bgate * conv).astype(BF16), wout_ref[...])


def _sc_call(x, nw, w_in, conv_w, w_out, *, batch, tm=PROJ_TM):
    T, D = x.shape
    S = T // batch
    assert S % tm == 0
    row = lambda i: (i, 0)
    nbytes = (w_in.size + w_out.size) * 2 + 4 * tm * D * 4 + 8 * tm * D * 4
    return pl.pallas_call(
        functools.partial(_sc_kernel, tm=tm, tiles_per_seq=S // tm),
        grid=(T // tm,),
        in_specs=[pl.BlockSpec((tm, D), row), _resident((1, D)), _resident(w_in.shape),
                  _resident(conv_w.shape), _resident(w_out.shape)],
        out_specs=pl.BlockSpec((tm, D), row),
        out_shape=jax.ShapeDtypeStruct((T, D), F32),
        scratch_shapes=[pltpu.VMEM((tm + HALO, D), F32)],
        compiler_params=pltpu.CompilerParams(
            dimension_semantics=("arbitrary",), vmem_limit_bytes=_vmem_limit(nbytes)),
        name="short_conv",
    )(x, nw.reshape(1, D), w_in, conv_w.astype(F32), w_out)


def kernel(x, rel_bias, final_norm_w, ffn1_norm, ffn1_wg, ffn1_wu, ffn1_wd, mix_norm, ffn2_norm, ffn2_wg, ffn2_wu, ffn2_wd, hyb_w_in, hyb_w_out, diff_lq1, diff_lk1, diff_lq2, diff_lk2, diff_subln_w, ssm_conv_w, ssm_conv_b, ssm_dt_bias, ssm_a_log, ssm_d, ssm_norm_w, sc_w_in, sc_conv_w, sc_w_out):
    B, S, D = x.shape
    T = B * S
    xt = x.reshape(T, D)
    bf = lambda w: w.astype(BF16)

    lam_inits = [0.8 - 0.6 * math.exp(-0.3 * i) for i in range(0, DEPTH, 2)]
    tiles, lams = _prep_call(rel_bias, diff_lq1, diff_lk1, diff_lq2, diff_lk2, lam_inits)

    mix = None
    for i in range(DEPTH):
        j = i // 2
        xt = _ffn_call(xt, ffn1_norm[i], ffn1_wg, ffn1_wu, ffn1_wd, i)
        if i % 2 == 0:
            w_in = hyb_w_in[j]
            qkv_end = 3 * DIFF_WIDTH
            w_pad = bf(jnp.pad(w_in, ((0, 0), (0, LANES - SSM_HEADS))))
            wvt = bf(w_in[:, 2 * DIFF_WIDTH:qkv_end].T)
            q, k, vt, z, xbc, dt = _hyb_in_call(xt, mix_norm[i], w_pad, wvt)
            o = _attn_call(q, k, vt, tiles, lams[j * SUBLANES:(j + 1) * SUBLANES], diff_subln_w[j],
                           batch=B, lam_init=lam_inits[j])
            y = _ssd_call(xbc, z, dt, ssm_conv_w[j], ssm_conv_b[j], ssm_dt_bias[j], ssm_a_log[j],
                          ssm_d[j], ssm_norm_w[j], batch=B)
            mix = (o, y, hyb_w_out, j)
        else:
            xt = _sc_call(xt, mix_norm[i], bf(sc_w_in[j]), sc_conv_w[j], bf(sc_w_out[j]), batch=B)
            mix = None
        xt = _ffn_call(xt, ffn2_norm[i], ffn2_wg, ffn2_wu, ffn2_wd, i, mix=mix,
                       final_w=final_norm_w if i == DEPTH - 1 else None)
    return xt.reshape(B, S, D)
```

```python
import functools
import math

import jax
import jax.numpy as jnp
from jax import lax
from jax.experimental import pallas as pl
from jax.experimental.pallas import tpu as pltpu

F32 = jnp.float32
BF16 = jnp.bfloat16

DEPTH = 4
N_DIFF_HEADS = 4
DIFF_HEAD_DIM = 64
DIFF_WIDTH = N_DIFF_HEADS * 2 * DIFF_HEAD_DIM
NUM_BUCKETS = 32
MAX_DISTANCE = 128
SSM_HEADS = 8
SSM_HEAD_DIM = 64
SSM_WIDTH = SSM_HEADS * SSM_HEAD_DIM
SSM_GROUPS = 2
SSM_STATE = 128
SSM_CONV = 4
SSM_CHUNK = 128
HEADS_PER_GROUP = SSM_HEADS // SSM_GROUPS
SSM_XBC = SSM_WIDTH + 2 * SSM_GROUPS * SSM_STATE
SC_CONV = 3

LANES = 128
SUBLANES = 8
V7X_SCOPED_VMEM_CAP = 60000 * 1024

FFN_TM = 512
FFN_FCHUNK = 256
PROJ_TM = 512
ATT_T = 512
ATT_STRIP = 256
ATT_ONES = 16
ATT_FAR_TILE = 2
SSD_LB = 512
HALO = SUBLANES

NEG_BIG = -1e30
LOG2E = math.log2(math.e)


def _vmem_limit(nbytes):
    return int(min(V7X_SCOPED_VMEM_CAP, nbytes * 5 // 4 + (8 << 20)))


def _dot(a, b):
    return jnp.dot(a, b, preferred_element_type=F32)


def _dot_nt(a, b):
    return lax.dot_general(a, b, (((1,), (1,)), ((), ())), preferred_element_type=F32)


def _rms(x, w, eps):
    ms = jnp.mean(x * x, axis=-1, keepdims=True)
    return x * lax.rsqrt(ms + eps) * w


def _silu(x):
    return x * jax.nn.sigmoid(x)


def _resident(shape):
    nd = len(shape)
    return pl.BlockSpec(shape, lambda *_: (0,) * nd, pipeline_mode=pl.Buffered(1))


def _ffn_kernel(*refs, pre_mix, final_norm, f_total):
    refs = list(refs)
    x_ref = refs.pop(0)
    if pre_mix:
        o_ref, y_ref, wo_ref = refs.pop(0), refs.pop(0), refs.pop(0)
    nw_ref, wg_ref, wu_ref, wd_ref = refs.pop(0), refs.pop(0), refs.pop(0), refs.pop(0)
    if final_norm:
        fw_ref = refs.pop(0)
    out_ref = refs.pop(0)

    x = x_ref[...]
    if pre_mix:
        half = o_ref.shape[1]
        x = (x + _dot(o_ref[...], wo_ref[0:half, :].astype(BF16))
             + _dot(y_ref[...], wo_ref[half:, :].astype(BF16)))
    h = _rms(x, nw_ref[...], 1e-6).astype(BF16)
    acc = None
    for c0 in range(0, f_total, FFN_FCHUNK):
        c1 = c0 + FFN_FCHUNK
        g = _dot(h, wg_ref[:, c0:c1].astype(BF16))
        u = _dot(h, wu_ref[:, c0:c1].astype(BF16))
        a = (_silu(g) * u).astype(BF16)
        d = _dot(a, wd_ref[c0:c1, :].astype(BF16))
        acc = d if acc is None else acc + d
    y = x + 0.5 * acc
    if final_norm:
        y = _rms(y, fw_ref[...], 1e-6)
    out_ref[...] = y


def _layer_resident(stacked, layer):
    _, r, c = stacked.shape
    return pl.BlockSpec((None, r, c), lambda *_: (layer, 0, 0), pipeline_mode=pl.Buffered(1))


def _ffn_call(x, nw, wg, wu, wd, layer, *, mix=None, final_w=None, tm=FFN_TM):
    T, D = x.shape
    F = wg.shape[2]
    assert T % tm == 0 and F % FFN_FCHUNK == 0
    row = lambda i: (i, 0)
    args = [x]
    in_specs = [pl.BlockSpec((tm, D), row)]
    nbytes = 4 * tm * D * 4 + 3 * D * F * 4 + 6 * tm * FFN_FCHUNK * 4 + 3 * tm * D * 4
    if mix is not None:
        o, y, wo, mix_layer = mix
        args += [o, y, wo]
        in_specs += [pl.BlockSpec((tm, o.shape[1]), row), pl.BlockSpec((tm, y.shape[1]), row),
                     _layer_resident(wo, mix_layer)]
        nbytes += wo.shape[1] * wo.shape[2] * 4 + 4 * tm * o.shape[1] * 2 * 2
    args += [nw.reshape(1, D), wg, wu, wd]
    in_specs += [_resident((1, D)), _layer_resident(wg, layer), _layer_resident(wu, layer),
                 _layer_resident(wd, layer)]
    if final_w is not None:
        args.append(final_w.reshape(1, D))
        in_specs.append(_resident((1, D)))
    kern = functools.partial(_ffn_kernel, pre_mix=mix is not None, final_norm=final_w is not None,
                             f_total=F)
    return pl.pallas_call(
        kern,
        grid=(T // tm,),
        in_specs=in_specs,
        out_specs=pl.BlockSpec((tm, D), row),
        out_shape=jax.ShapeDtypeStruct((T, D), F32),
        compiler_params=pltpu.CompilerParams(
            dimension_semantics=("arbitrary",), vmem_limit_bytes=_vmem_limit(nbytes)),
        name="ffn",
    )(*args)


def _hyb_in_kernel(x_ref, nw_ref, w_ref, wvt_ref, q_ref, k_ref, vt_ref, z_ref, xbc_ref, dt_ref):
    h = _rms(x_ref[...], nw_ref[...], 1e-6).astype(BF16)
    qw = DIFF_WIDTH
    scale = DIFF_HEAD_DIM ** -0.5 * LOG2E
    q_ref[...] = (_dot(h, w_ref[:, 0:qw]) * scale).astype(BF16)
    k_ref[...] = _dot(h, w_ref[:, qw:2 * qw]).astype(BF16)
    vt_ref[...] = _dot_nt(wvt_ref[...], h).astype(BF16)
    c = 2 * qw + DIFF_WIDTH
    z_ref[...] = _dot(h, w_ref[:, c:c + SSM_WIDTH])
    c += SSM_WIDTH
    xbc_ref[...] = _dot(h, w_ref[:, c:c + SSM_XBC])
    c += SSM_XBC
    dt_ref[...] = _dot(h, w_ref[:, c:c + LANES])


def _hyb_in_call(x, nw, w_pad, wvt, *, tm=PROJ_TM):
    T, D = x.shape
    row = lambda i: (i, 0)
    outs = [
        jax.ShapeDtypeStruct((T, DIFF_WIDTH), BF16),
        jax.ShapeDtypeStruct((T, DIFF_WIDTH), BF16),
        jax.ShapeDtypeStruct((DIFF_WIDTH, T), BF16),
        jax.ShapeDtypeStruct((T, SSM_WIDTH), F32),
        jax.ShapeDtypeStruct((T, SSM_XBC), F32),
        jax.ShapeDtypeStruct((T, LANES), F32),
    ]
    out_specs = [
        pl.BlockSpec((tm, DIFF_WIDTH), row), pl.BlockSpec((tm, DIFF_WIDTH), row),
        pl.BlockSpec((DIFF_WIDTH, tm), lambda i: (0, i)),
        pl.BlockSpec((tm, SSM_WIDTH), row), pl.BlockSpec((tm, SSM_XBC), row),
        pl.BlockSpec((tm, LANES), row),
    ]
    nbytes = (w_pad.size + wvt.size) * 2 + 2 * tm * D * 4 + 3 * tm * 4 * (3 * DIFF_WIDTH + SSM_WIDTH + SSM_XBC + LANES)
    return pl.pallas_call(
        _hyb_in_kernel,
        grid=(T // tm,),
        in_specs=[pl.BlockSpec((tm, D), row), _resident((1, D)), _resident(w_pad.shape),
                  _resident(wvt.shape)],
        out_specs=out_specs,
        out_shape=outs,
        compiler_params=pltpu.CompilerParams(
            dimension_semantics=("arbitrary",), vmem_limit_bytes=_vmem_limit(nbytes)),
        name="hyb_in",
    )(x, nw.reshape(1, D), w_pad, wvt)


def _prep_kernel(tab_ref, lq1_ref, lk1_ref, lq2_ref, lk2_ref, tile_ref, lam_ref, *, t, lam_inits):
    hd = pl.program_id(0)
    key = lax.broadcasted_iota(jnp.int32, (t, t), 0)
    qry = lax.broadcasted_iota(jnp.int32, (t, t), 1)
    max_exact = NUM_BUCKETS // 2
    far = tab_ref[hd, NUM_BUCKETS - 1]
    tile_ref[ATT_FAR_TILE] = jnp.zeros((t, t), F32)
    for off in range(ATT_FAR_TILE):
        dist = off * t + qry - key
        d = jnp.maximum(dist, 0)
        large = max_exact + (
            jnp.log(jnp.maximum(d, 1).astype(F32) / max_exact)
            / math.log(MAX_DISTANCE / max_exact) * (NUM_BUCKETS - max_exact)).astype(jnp.int32)
        bucket = jnp.where(d < max_exact, d, jnp.minimum(large, NUM_BUCKETS - 1))
        bias = jnp.full((t, t), far, F32)
        for b in range(NUM_BUCKETS - 1):
            bias = jnp.where(bucket == b, tab_ref[hd, b], bias)
        val = (bias - far) * LOG2E
        if off == 0:
            val = jnp.where(dist >= 0, val, NEG_BIG)
        tile_ref[off] = val
    for j, lam_init in enumerate(lam_inits):
        s1 = jnp.sum(lq1_ref[j:j + 1, :] * lk1_ref[j:j + 1, :], axis=-1, keepdims=True)
        s2 = jnp.sum(lq2_ref[j:j + 1, :] * lk2_ref[j:j + 1, :], axis=-1, keepdims=True)
        lam = jnp.exp(s1) - jnp.exp(s2) + lam_init
        lam_ref[j * SUBLANES:(j + 1) * SUBLANES, :] = jnp.broadcast_to(lam, (SUBLANES, LANES))


def _prep_call(rel_bias, lq1, lk1, lq2, lk2, lam_inits, *, t=ATT_T):
    ne = lq1.shape[0]
    H = rel_bias.shape[1]
    whole = lambda shape: pl.BlockSpec(shape, lambda h: (0,) * len(shape))
    return pl.pallas_call(
        functools.partial(_prep_kernel, t=t, lam_inits=tuple(lam_inits)),
        grid=(H,),
        in_specs=[pl.BlockSpec(memory_space=pltpu.SMEM)] + [whole(lq1.shape)] * 4,
        out_specs=[pl.BlockSpec((None, ATT_FAR_TILE + 1, t, t), lambda h: (h, 0, 0, 0)),
                   whole((ne * SUBLANES, LANES))],
        out_shape=[jax.ShapeDtypeStruct((H, ATT_FAR_TILE + 1, t, t), F32),
                   jax.ShapeDtypeStruct((ne * SUBLANES, LANES), F32)],
        compiler_params=pltpu.CompilerParams(
            dimension_semantics=("arbitrary",), vmem_limit_bytes=_vmem_limit(12 * t * t * 4)),
        name="attn_prep",
    )(rel_bias.T.astype(F32), lq1, lk1, lq2, lk2)


def _attn_kernel(q_ref, k_ref, vt_ref, tile_ref, lam_ref, sw_ref, o_ref,
                 qm_ref, m_ref, acc_ref, s_ref, smax_ref, *, t, nq, unroll, lam_init):
    e = 2 * DIFF_HEAD_DIM
    first = lax.broadcasted_iota(jnp.int32, (t, LANES), 1) < DIFF_HEAD_DIM
    ones = jnp.ones((ATT_ONES, t), BF16)
    chains = [(w, slice(c0, c0 + ATT_STRIP)) for c0 in range(0, t, ATT_STRIP) for w in range(2)]

    def rows(blk):
        return pl.ds(pl.multiple_of(blk * t, t), t)

    def prepare(qi, carry):
        q = q_ref[rows(qi), :]
        zero = jnp.zeros_like(q)
        qm_ref[0, rows(qi), :] = jnp.where(first, q, zero)
        qm_ref[1, rows(qi), :] = jnp.where(first, zero, q)
        m_ref[qi] = jnp.full(m_ref.shape[1:], NEG_BIG, F32)
        acc_ref[qi] = jnp.zeros(acc_ref.shape[1:], F32)
        return carry

    def score_chain(ci, qi, j, slot, k):
        w, cols = chains[ci]
        near = jnp.minimum(qi - j, ATT_FAR_TILE)
        q_rows = pl.ds(pl.multiple_of(qi * t + cols.start, ATT_STRIP), ATT_STRIP)
        s = _dot_nt(k, qm_ref[w, q_rows, :]) + tile_ref[near, :, cols]
        s_ref[slot, ci] = s
        smax_ref[slot, ci] = jnp.max(s, axis=0, keepdims=True)

    def value_chain(ci, qi, slot, vta):
        w, cols = chains[ci]
        m_old = m_ref[qi, w, :, cols]
        m_new = jnp.maximum(m_old, smax_ref[slot, ci])
        p = jnp.exp2(s_ref[slot, ci] - m_new).astype(BF16)
        acc_ref[qi, w, :, cols] = jnp.exp2(m_old - m_new) * acc_ref[qi, w, :, cols] + _dot(vta, p)
        m_ref[qi, w, :, cols] = m_new

    def finalize(qi, carry):
        lam = lam_ref[0:1, 0:1]
        o1 = acc_ref[qi, 0, 0:e, :] * (1.0 / acc_ref[qi, 0, e:e + 1, :])
        o2 = acc_ref[qi, 1, 0:e, :] * (1.0 / acc_ref[qi, 1, e:e + 1, :])
        out_t = o1 - lam * o2
        ms = jnp.mean(out_t * out_t, axis=0, keepdims=True)
        out_t = out_t * lax.rsqrt(ms + 1e-5) * sw_ref[...] * (1.0 - lam_init)
        o_ref[rows(qi), :] = out_t.T.astype(BF16)
        return carry

    def step(qv, jv, slot):
        wrap = jv == qv
        qs = jnp.where(wrap, qv + 1, qv)
        js = jnp.where(wrap, 0, jv + 1)
        past_end = qs >= nq
        qs = jnp.where(past_end, nq - 1, qs)
        js = jnp.where(past_end, nq - 1, js)
        k_next = k_ref[rows(js), :]
        vta = jnp.concatenate([vt_ref[:, rows(jv)], ones], axis=0)
        for ci in range(len(chains)):
            score_chain(ci, qs, js, 1 - slot, k_next)
            value_chain(ci, qv, slot, vta)
        return qs, js

    lax.fori_loop(0, nq, prepare, 0)
    k0 = k_ref[rows(0), :]
    for ci in range(len(chains)):
        score_chain(ci, 0, 0, 0, k0)

    def body(_, carry):
        qv, jv = carry
        for u in range(unroll):
            qv, jv = step(qv, jv, u % 2)
        return qv, jv

    n_pairs = nq * (nq + 1) // 2
    lax.fori_loop(0, n_pairs // unroll, body, (jnp.int32(0), jnp.int32(0)))
    lax.fori_loop(0, nq, finalize, 0)


def _attn_call(q, k, vt, tiles, lam, subln_w, *, batch, lam_init, t=ATT_T):
    T, W = q.shape
    S = T // batch
    H = W // LANES
    nq = S // t
    e = 2 * DIFF_HEAD_DIM
    assert e == LANES and S % t == 0
    n_pairs = nq * (nq + 1) // 2
    unroll = max(u for u in (8, 4, 2) if n_pairs % u == 0)
    n_chains = 2 * (t // ATT_STRIP)
    n_tiles = ATT_FAR_TILE + 1
    nbytes = (2 * 4 * (S * LANES * 2) + 2 * n_tiles * t * t * 4 + 2 * S * LANES * 2
              + nq * 2 * (e + ATT_ONES + SUBLANES) * t * 4 + 2 * n_chains * t * ATT_STRIP * 4 + 2 * t * t * 4)
    return pl.pallas_call(
        functools.partial(_attn_kernel, t=t, nq=nq, unroll=unroll, lam_init=lam_init),
        grid=(batch, H),
        in_specs=[
            pl.BlockSpec((S, LANES), lambda b, h: (b, h)),
            pl.BlockSpec((S, LANES), lambda b, h: (b, h)),
            pl.BlockSpec((LANES, S), lambda b, h: (h, b)),
            pl.BlockSpec((None, n_tiles, t, t), lambda b, h: (h, 0, 0, 0)),
            pl.BlockSpec((SUBLANES, LANES), lambda b, h: (0, 0)),
            pl.BlockSpec((e, 1), lambda b, h: (0, 0)),
        ],
        out_specs=pl.BlockSpec((S, LANES), lambda b, h: (b, h)),
        out_shape=jax.ShapeDtypeStruct((T, W), BF16),
        scratch_shapes=[pltpu.VMEM((2, S, LANES), BF16),
                        pltpu.VMEM((nq, 2, 1, t), F32), pltpu.VMEM((nq, 2, e + ATT_ONES, t), F32),
                        pltpu.VMEM((2, n_chains, t, ATT_STRIP), F32),
                        pltpu.VMEM((2, n_chains, 1, ATT_STRIP), F32)],
        compiler_params=pltpu.CompilerParams(
            dimension_semantics=("arbitrary", "arbitrary"), vmem_limit_bytes=_vmem_limit(nbytes)),
        name="diff_attn",
    )(q, k, vt, tiles, lam, subln_w.reshape(e, 1))


def _split3(a):
    hi = a.astype(BF16)
    r1 = a - hi.astype(F32)
    mid = r1.astype(BF16)
    lo = (r1 - mid.astype(F32)).astype(BF16)
    return hi, mid, lo


def _ssd_kernel(xbc_ref, z_ref, dt_ref, cw_ref, cb_ref, dtb_ref, alog_ref, dsk_ref, nw_ref, y_ref,
                xpad_ref, st_ref, *, lb):
    L = SSM_CHUNK
    gw = HEADS_PER_GROUP * SSM_HEAD_DIM

    @pl.when(pl.program_id(1) == 0)
    def _():
        xpad_ref[0:HALO, :] = jnp.zeros((HALO, SSM_XBC), F32)
        st_ref[...] = jnp.zeros(st_ref.shape, F32)

    xpad_ref[HALO:HALO + lb, :] = xbc_ref[...]
    conv = cb_ref[...]
    for kk in range(SSM_CONV):
        o = HALO - (SSM_CONV - 1) + kk
        conv = conv + cw_ref[kk:kk + 1, :] * xpad_ref[o:o + lb, :]
    xc = _silu(conv)
    xpad_ref[0:HALO, :] = xpad_ref[lb:lb + HALO, :]

    dt_raw = dt_ref[...] + dtb_ref[...]
    dt = jnp.maximum(dt_raw, 0.0) + jnp.log1p(jnp.exp(-jnp.abs(dt_raw)))
    a_dt = -jnp.exp(alog_ref[...]) * dt

    row = lax.broadcasted_iota(jnp.int32, (L, L), 0)
    col = lax.broadcasted_iota(jnp.int32, (L, L), 1)
    causal = row >= col
    tri = jnp.where(causal, 1.0, 0.0).astype(BF16)
    lane_g = lax.broadcasted_iota(jnp.int32, (L, gw), 1) // SSM_HEAD_DIM
    lane_lo = lax.broadcasted_iota(jnp.int32, (L, LANES), 1) < SSM_HEAD_DIM
    lane_lo1 = lax.broadcasted_iota(jnp.int32, (1, LANES), 1) < SSM_HEAD_DIM

    for c in range(lb // L):
        r0 = c * L
        hi, mid, lo = _split3(a_dt[r0:r0 + L, :])
        cs = _dot(tri, hi) + _dot(tri, mid) + _dot(tri, lo)
        cs_t = cs.T
        dt_t = dt[r0:r0 + L, :].T
        xs = xc[r0:r0 + L, 0:SSM_WIDTH]
        ys = []
        for g in range(SSM_GROUPS):
            b0 = SSM_WIDTH + g * SSM_STATE
            c0 = SSM_WIDTH + SSM_GROUPS * SSM_STATE + g * SSM_STATE
            bg = xc[r0:r0 + L, b0:b0 + SSM_STATE]
            cg = xc[r0:r0 + L, c0:c0 + SSM_STATE].astype(BF16)
            gmat = _dot_nt(cg, bg.astype(BF16))
            bg_t = bg.T
            xg = xs[:, g * gw:(g + 1) * gw]
            ydiag = None
            st_new = None
            ecols, decs = [], []
            for hh in range(HEADS_PER_GROUP):
                h = g * HEADS_PER_GROUP + hh
                colb = jnp.broadcast_to(cs[:, h:h + 1], (L, L))
                rowb = cs_t[h:h + 1, :]
                dtrow = dt_t[h:h + 1, :]
                decay = jnp.exp(jnp.where(causal, colb - rowb, NEG_BIG))
                mh = (gmat * decay * dtrow).astype(BF16)
                xm = jnp.where(lane_g == hh, xg, 0.0).astype(BF16)
                last = colb[L - 1:L, :]
                wrow = dtrow * jnp.exp(last - rowb)
                btw = (bg_t * wrow).astype(BF16)
                d1 = _dot(mh, xm)
                d2 = _dot(btw, xm)
                ydiag = d1 if ydiag is None else ydiag + d1
                st_new = d2 if st_new is None else st_new + d2
                ecols.append(jnp.exp(colb))
                decs.append(jnp.exp(last))
            e_mat = jnp.concatenate([jnp.where(lane_lo, ecols[0], ecols[1]),
                                     jnp.where(lane_lo, ecols[2], ecols[3])], axis=1)
            dec = jnp.concatenate([jnp.where(lane_lo1, decs[0], decs[1]),
                                   jnp.where(lane_lo1, decs[2], decs[3])], axis=1)
            st_old = st_ref[g]
            ys.append(ydiag + _dot(cg, st_old.astype(BF16)) * e_mat)
            st_ref[g] = st_old * dec + st_new
        y = jnp.concatenate(ys, axis=1) + dsk_ref[...] * xs
        gated = y * _silu(z_ref[r0:r0 + L, :])
        outs = []
        for g in range(SSM_GROUPS):
            part = gated[:, g * gw:(g + 1) * gw]
            ms = jnp.mean(part * part, axis=-1, keepdims=True)
            outs.append(part * lax.rsqrt(ms + 1e-5))
        y_ref[r0:r0 + L, :] = (jnp.concatenate(outs, axis=1) * nw_ref[...]).astype(BF16)


def _ssd_call(xbc, z, dt, conv_w, conv_b, dt_bias, a_log, d_skip, norm_w, *, batch, lb=SSD_LB):
    T = xbc.shape[0]
    S = T // batch
    nb = S // lb
    assert S % lb == 0 and lb % SSM_CHUNK == 0
    pad = lambda v: jnp.pad(v.astype(F32), (0, LANES - v.shape[0])).reshape(1, LANES)
    dsk = jnp.repeat(d_skip.astype(F32), SSM_HEAD_DIM).reshape(1, SSM_WIDTH)
    row = lambda b, j: (b * nb + j, 0)
    const = lambda shape: pl.BlockSpec(shape, lambda b, j: (0, 0))
    nbytes = 3 * lb * (SSM_XBC + SSM_WIDTH + LANES) * 4 + 12 * lb * SSM_XBC * 4
    return pl.pallas_call(
        functools.partial(_ssd_kernel, lb=lb),
        grid=(batch, nb),
        in_specs=[pl.BlockSpec((lb, SSM_XBC), row), pl.BlockSpec((lb, SSM_WIDTH), row),
                  pl.BlockSpec((lb, LANES), row),
                  const((SSM_CONV, SSM_XBC)), const((1, SSM_XBC)), const((1, LANES)), const((1, LANES)),
                  const((1, SSM_WIDTH)), const((1, SSM_WIDTH))],
        out_specs=pl.BlockSpec((lb, SSM_WIDTH), row),
        out_shape=jax.ShapeDtypeStruct((T, SSM_WIDTH), BF16),
        scratch_shapes=[pltpu.VMEM((lb + HALO, SSM_XBC), F32),
                        pltpu.VMEM((SSM_GROUPS, SSM_STATE, HEADS_PER_GROUP * SSM_HEAD_DIM), F32)],
        compiler_params=pltpu.CompilerParams(
            dimension_semantics=("arbitrary", "arbitrary"), vmem_limit_bytes=_vmem_limit(nbytes)),
        name="ssd",
    )(xbc, z, dt, conv_w.astype(F32), conv_b.reshape(1, SSM_XBC).astype(F32), pad(dt_bias), pad(a_log),
      dsk, norm_w.reshape(1, SSM_WIDTH).astype(F32))


def _sc_kernel(x_ref, nw_ref, win_ref, cw_ref, wout_ref, o_ref, vpad_ref, *, tm, tiles_per_seq):
    d = x_ref.shape[1]

    @pl.when(pl.program_id(0) % tiles_per_seq == 0)
    def _():
        vpad_ref[0:HALO, :] = jnp.zeros((HALO, d), F32)

    x = x_ref[...]
    h = _rms(x, nw_ref[...], 1e-6).astype(BF16)
    bgate = _dot(h, win_ref[:, 0:d])
    v = _dot(h, win_ref[:, d:2 * d]) * _dot(h, win_ref[:, 2 * d:3 * d])
    vpad_ref[HALO:HALO + tm, :] = v
    conv = cw_ref[SC_CONV - 1:SC_CONV, :] * v
    for kk in range(SC_CONV - 1):
        o = HALO - (SC_CONV - 1) + kk
        conv = conv + cw_ref[kk:kk + 1, :] * vpad_ref[o:o + tm, :]
    vpad_ref[0:HALO, :] = vpad_ref[tm:tm + HALO, :]
    o_ref[...] = x + _dot((bgate * conv).astype(BF16), wout_ref[...])


def _sc_call(x, nw, w_in, conv_w, w_out, *, batch, tm=PROJ_TM):
    T, D = x.shape
    S = T // batch
    assert S % tm == 0
    row = lambda i: (i, 0)
    nbytes = (w_in.size + w_out.size) * 2 + 4 * tm * D * 4 + 8 * tm * D * 4
    return pl.pallas_call(
        functools.partial(_sc_kernel, tm=tm, tiles_per_seq=S // tm),
        grid=(T // tm,),
        in_specs=[pl.BlockSpec((tm, D), row), _resident((1, D)), _resident(w_in.shape),
                  _resident(conv_w.shape), _resident(w_out.shape)],
        out_specs=pl.BlockSpec((tm, D), row),
        out_shape=jax.ShapeDtypeStruct((T, D), F32),
        scratch_shapes=[pltpu.VMEM((tm + HALO, D), F32)],
        compiler_params=pltpu.CompilerParams(
            dimension_semantics=("arbitrary",), vmem_limit_bytes=_vmem_limit(nbytes)),
        name="short_conv",
    )(x, nw.reshape(1, D), w_in, conv_w.astype(F32), w_out)


def kernel(x, rel_bias, final_norm_w, ffn1_norm, ffn1_wg, ffn1_wu, ffn1_wd, mix_norm, ffn2_norm, ffn2_wg, ffn2_wu, ffn2_wd, hyb_w_in, hyb_w_out, diff_lq1, diff_lk1, diff_lq2, diff_lk2, diff_subln_w, ssm_conv_w, ssm_conv_b, ssm_dt_bias, ssm_a_log, ssm_d, ssm_norm_w, sc_w_in, sc_conv_w, sc_w_out):
    B, S, D = x.shape
    T = B * S
    xt = x.reshape(T, D)
    bf = lambda w: w.astype(BF16)

    lam_inits = [0.8 - 0.6 * math.exp(-0.3 * i) for i in range(0, DEPTH, 2)]
    tiles, lams = _prep_call(rel_bias, diff_lq1, diff_lk1, diff_lq2, diff_lk2, lam_inits)

    mix = None
    for i in range(DEPTH):
        j = i // 2
        xt = _ffn_call(xt, ffn1_norm[i], ffn1_wg, ffn1_wu, ffn1_wd, i)
        if i % 2 == 0:
            w_in = hyb_w_in[j]
            qkv_end = 3 * DIFF_WIDTH
            w_pad = bf(jnp.pad(w_in, ((0, 0), (0, LANES - SSM_HEADS))))
            wvt = bf(w_in[:, 2 * DIFF_WIDTH:qkv_end].T)
            q, k, vt, z, xbc, dt = _hyb_in_call(xt, mix_norm[i], w_pad, wvt)
            o = _attn_call(q, k, vt, tiles, lams[j * SUBLANES:(j + 1) * SUBLANES], diff_subln_w[j],
                           batch=B, lam_init=lam_inits[j])
            y = _ssd_call(xbc, z, dt, ssm_conv_w[j], ssm_conv_b[j], ssm_dt_bias[j], ssm_a_log[j],
                          ssm_d[j], ssm_norm_w[j], batch=B)
            mix = (o, y, hyb_w_out, j)
        else:
            xt = _sc_call(xt, mix_norm[i], bf(sc_w_in[j]), sc_conv_w[j], bf(sc_w_out[j]), batch=B)
            mix = None
        xt = _ffn_call(xt, ffn2_norm[i], ffn2_wg, ffn2_wu, ffn2_wd, i, mix=mix,
                       final_w=final_norm_w if i == DEPTH - 1 else None)
    return xt.reshape(B, S, D)
```

```python
import functools
import math

import jax
import jax.numpy as jnp
from jax import lax
from jax.experimental import pallas as pl
from jax.experimental.pallas import tpu as pltpu

F32 = jnp.float32
BF16 = jnp.bfloat16

DEPTH = 4
N_DIFF_HEADS = 4
DIFF_HEAD_DIM = 64
DIFF_WIDTH = N_DIFF_HEADS * 2 * DIFF_HEAD_DIM
NUM_BUCKETS = 32
MAX_DISTANCE = 128
SSM_HEADS = 8
SSM_HEAD_DIM = 64
SSM_WIDTH = SSM_HEADS * SSM_HEAD_DIM
SSM_GROUPS = 2
SSM_STATE = 128
SSM_CONV = 4
SSM_CHUNK = 128
HEADS_PER_GROUP = SSM_HEADS // SSM_GROUPS
SSM_XBC = SSM_WIDTH + 2 * SSM_GROUPS * SSM_STATE
SC_CONV = 3

LANES = 128
SUBLANES = 8
V7X_SCOPED_VMEM_CAP = 60000 * 1024

FFN_TM = 512
FFN_FCHUNK = 256
PROJ_TM = 512
ATT_T = 512
ATT_STRIP = 256
ATT_ONES = 16
ATT_FAR_TILE = 2
SSD_LB = 512
HALO = SUBLANES

NEG_BIG = -1e30
LOG2E = math.log2(math.e)


def _vmem_limit(nbytes):
    return int(min(V7X_SCOPED_VMEM_CAP, nbytes * 5 // 4 + (8 << 20)))


def _dot(a, b):
    return jnp.dot(a, b, preferred_element_type=F32)


def _dot_nt(a, b):
    return lax.dot_general(a, b, (((1,), (1,)), ((), ())), preferred_element_type=F32)


def _rms(x, w, eps):
    ms = jnp.mean(x * x, axis=-1, keepdims=True)
    return x * lax.rsqrt(ms + eps) * w


def _silu(x):
    return x * jax.nn.sigmoid(x)


def _resident(shape):
    nd = len(shape)
    return pl.BlockSpec(shape, lambda *_: (0,) * nd, pipeline_mode=pl.Buffered(1))


def _ffn_kernel(*refs, pre_mix, final_norm, f_total):
    refs = list(refs)
    x_ref = refs.pop(0)
    if pre_mix:
        o_ref, y_ref, wo_ref = refs.pop(0), refs.pop(0), refs.pop(0)
    nw_ref, wg_ref, wu_ref, wd_ref = refs.pop(0), refs.pop(0), refs.pop(0), refs.pop(0)
    if final_norm:
        fw_ref = refs.pop(0)
    out_ref = refs.pop(0)

    x = x_ref[...]
    if pre_mix:
        half = o_ref.shape[1]
        x = (x + _dot(o_ref[...], wo_ref[0:half, :].astype(BF16))
             + _dot(y_ref[...], wo_ref[half:, :].astype(BF16)))
    h = _rms(x, nw_ref[...], 1e-6).astype(BF16)
    acc = None
    for c0 in range(0, f_total, FFN_FCHUNK):
        c1 = c0 + FFN_FCHUNK
        g = _dot(h, wg_ref[:, c0:c1].astype(BF16))
        u = _dot(h, wu_ref[:, c0:c1].astype(BF16))
        a = (_silu(g) * u).astype(BF16)
        d = _dot(a, wd_ref[c0:c1, :].astype(BF16))
        acc = d if acc is None else acc + d
    y = x + 0.5 * acc
    if final_norm:
        y = _rms(y, fw_ref[...], 1e-6)
    out_ref[...] = y


def _layer_resident(stacked, layer):
    _, r, c = stacked.shape
    return pl.BlockSpec((None, r, c), lambda *_: (layer, 0, 0), pipeline_mode=pl.Buffered(1))


def _ffn_call(x, nw, wg, wu, wd, layer, *, mix=None, final_w=None, tm=FFN_TM):
    T, D = x.shape
    F = wg.shape[2]
    assert T % tm == 0 and F % FFN_FCHUNK == 0
    row = lambda i: (i, 0)
    args = [x]
    in_specs = [pl.BlockSpec((tm, D), row)]
    nbytes = 4 * tm * D * 4 + 3 * D * F * 4 + 6 * tm * FFN_FCHUNK * 4 + 3 * tm * D * 4
    if mix is not None:
        o, y, wo, mix_layer = mix
        args += [o, y, wo]
        in_specs += [pl.BlockSpec((tm, o.shape[1]), row), pl.BlockSpec((tm, y.shape[1]), row),
                     _layer_resident(wo, mix_layer)]
        nbytes += wo.shape[1] * wo.shape[2] * 4 + 4 * tm * o.shape[1] * 2 * 2
    args += [nw.reshape(1, D), wg, wu, wd]
    in_specs += [_resident((1, D)), _layer_resident(wg, layer), _layer_resident(wu, layer),
                 _layer_resident(wd, layer)]
    if final_w is not None:
        args.append(final_w.reshape(1, D))
        in_specs.append(_resident((1, D)))
    kern = functools.partial(_ffn_kernel, pre_mix=mix is not None, final_norm=final_w is not None,
                             f_total=F)
    return pl.pallas_call(
        kern,
        grid=(T // tm,),
        in_specs=in_specs,
        out_specs=pl.BlockSpec((tm, D), row),
        out_shape=jax.ShapeDtypeStruct((T, D), F32),
        compiler_params=pltpu.CompilerParams(
            dimension_semantics=("arbitrary",), vmem_limit_bytes=_vmem_limit(nbytes)),
        name="ffn",
    )(*args)


def _hyb_in_kernel(x_ref, nw_ref, w_ref, wvt_ref, cw_ref, cb_ref, dtb_ref,
                   q_ref, k_ref, vt_ref, z_ref, xc_ref, dt_ref, xpad_ref, *, tm, tiles_per_seq):
    @pl.when(pl.program_id(0) % tiles_per_seq == 0)
    def _():
        xpad_ref[0:HALO, :] = jnp.zeros((HALO, SSM_XBC), F32)

    h = _rms(x_ref[...], nw_ref[...], 1e-6).astype(BF16)
    qw = DIFF_WIDTH
    c_z = 2 * qw + DIFF_WIDTH
    c_xbc = c_z + SSM_WIDTH
    c_dt = c_xbc + SSM_XBC
    xpad_ref[HALO:HALO + tm, :] = _dot(h, w_ref[:, c_xbc:c_xbc + SSM_XBC])
    scale = DIFF_HEAD_DIM ** -0.5 * LOG2E
    q_ref[...] = (_dot(h, w_ref[:, 0:qw]) * scale).astype(BF16)
    k_ref[...] = _dot(h, w_ref[:, qw:2 * qw]).astype(BF16)
    vt_ref[...] = _dot_nt(wvt_ref[...], h).astype(BF16)
    z_ref[...] = _silu(_dot(h, w_ref[:, c_z:c_z + SSM_WIDTH]))
    dt_raw = _dot(h, w_ref[:, c_dt:c_dt + LANES]) + dtb_ref[...]
    dt_ref[...] = jnp.maximum(dt_raw, 0.0) + jnp.log1p(jnp.exp(-jnp.abs(dt_raw)))

    conv = cb_ref[...]
    for kk in range(SSM_CONV):
        o = HALO - (SSM_CONV - 1) + kk
        conv = conv + cw_ref[kk:kk + 1, :] * xpad_ref[o:o + tm, :]
    xc_ref[...] = _silu(conv)
    xpad_ref[0:HALO, :] = xpad_ref[tm:tm + HALO, :]


def _hyb_in_call(x, nw, w_pad, wvt, conv_w, conv_b, dt_bias, *, batch, tm=PROJ_TM):
    T, D = x.shape
    S = T // batch
    assert S % tm == 0
    row = lambda i: (i, 0)
    outs = [
        jax.ShapeDtypeStruct((T, DIFF_WIDTH), BF16),
        jax.ShapeDtypeStruct((T, DIFF_WIDTH), BF16),
        jax.ShapeDtypeStruct((DIFF_WIDTH, T), BF16),
        jax.ShapeDtypeStruct((T, SSM_WIDTH), F32),
        jax.ShapeDtypeStruct((T, SSM_XBC), F32),
        jax.ShapeDtypeStruct((T, LANES), F32),
    ]
    out_specs = [
        pl.BlockSpec((tm, DIFF_WIDTH), row), pl.BlockSpec((tm, DIFF_WIDTH), row),
        pl.BlockSpec((DIFF_WIDTH, tm), lambda i: (0, i)),
        pl.BlockSpec((tm, SSM_WIDTH), row), pl.BlockSpec((tm, SSM_XBC), row),
        pl.BlockSpec((tm, LANES), row),
    ]
    nbytes = ((w_pad.size + wvt.size) * 2 + 2 * tm * D * 4
              + 3 * tm * 4 * (3 * DIFF_WIDTH + SSM_WIDTH + SSM_XBC + LANES) + 3 * tm * SSM_XBC * 4)
    pad = jnp.pad(dt_bias.astype(F32), (0, LANES - dt_bias.shape[0])).reshape(1, LANES)
    return pl.pallas_call(
        functools.partial(_hyb_in_kernel, tm=tm, tiles_per_seq=S // tm),
        grid=(T // tm,),
        in_specs=[pl.BlockSpec((tm, D), row), _resident((1, D)), _resident(w_pad.shape),
                  _resident(wvt.shape), _resident((SSM_CONV, SSM_XBC)), _resident((1, SSM_XBC)),
                  _resident((1, LANES))],
        out_specs=out_specs,
        out_shape=outs,
        scratch_shapes=[pltpu.VMEM((tm + HALO, SSM_XBC), F32)],
        compiler_params=pltpu.CompilerParams(
            dimension_semantics=("arbitrary",), vmem_limit_bytes=_vmem_limit(nbytes)),
        name="hyb_in",
    )(x, nw.reshape(1, D), w_pad, wvt, conv_w.astype(F32), conv_b.reshape(1, SSM_XBC).astype(F32), pad)


def _prep_kernel(tab_ref, lq1_ref, lk1_ref, lq2_ref, lk2_ref, tile_ref, lam_ref, *, t, lam_inits):
    hd = pl.program_id(0)
    key = lax.broadcasted_iota(jnp.int32, (t, t), 0)
    qry = lax.broadcasted_iota(jnp.int32, (t, t), 1)
    max_exact = NUM_BUCKETS // 2
    far = tab_ref[hd, NUM_BUCKETS - 1]
    tile_ref[ATT_FAR_TILE] = jnp.zeros((t, t), F32)
    for off in range(ATT_FAR_TILE):
        dist = off * t + qry - key
        d = jnp.maximum(dist, 0)
        large = max_exact + (
            jnp.log(jnp.maximum(d, 1).astype(F32) / max_exact)
            / math.log(MAX_DISTANCE / max_exact) * (NUM_BUCKETS - max_exact)).astype(jnp.int32)
        bucket = jnp.where(d < max_exact, d, jnp.minimum(large, NUM_BUCKETS - 1))
        bias = jnp.full((t, t), far, F32)
        for b in range(NUM_BUCKETS - 1):
            bias = jnp.where(bucket == b, tab_ref[hd, b], bias)
        val = (bias - far) * LOG2E
        if off == 0:
            val = jnp.where(dist >= 0, val, NEG_BIG)
        tile_ref[off] = val
    for j, lam_init in enumerate(lam_inits):
        s1 = jnp.sum(lq1_ref[j:j + 1, :] * lk1_ref[j:j + 1, :], axis=-1, keepdims=True)
        s2 = jnp.sum(lq2_ref[j:j + 1, :] * lk2_ref[j:j + 1, :], axis=-1, keepdims=True)
        lam = jnp.exp(s1) - jnp.exp(s2) + lam_init
        lam_ref[j * SUBLANES:(j + 1) * SUBLANES, :] = jnp.broadcast_to(lam, (SUBLANES, LANES))


def _prep_call(rel_bias, lq1, lk1, lq2, lk2, lam_inits, *, t=ATT_T):
    ne = lq1.shape[0]
    H = rel_bias.shape[1]
    whole = lambda shape: pl.BlockSpec(shape, lambda h: (0,) * len(shape))
    return pl.pallas_call(
        functools.partial(_prep_kernel, t=t, lam_inits=tuple(lam_inits)),
        grid=(H,),
        in_specs=[pl.BlockSpec(memory_space=pltpu.SMEM)] + [whole(lq1.shape)] * 4,
        out_specs=[pl.BlockSpec((None, ATT_FAR_TILE + 1, t, t), lambda h: (h, 0, 0, 0)),
                   whole((ne * SUBLANES, LANES))],
        out_shape=[jax.ShapeDtypeStruct((H, ATT_FAR_TILE + 1, t, t), F32),
                   jax.ShapeDtypeStruct((ne * SUBLANES, LANES), F32)],
        compiler_params=pltpu.CompilerParams(
            dimension_semantics=("arbitrary",), vmem_limit_bytes=_vmem_limit(12 * t * t * 4)),
        name="attn_prep",
    )(rel_bias.T.astype(F32), lq1, lk1, lq2, lk2)


def _attn_kernel(q_ref, k_ref, vt_ref, tile_ref, lam_ref, sw_ref, o_ref,
                 qm_ref, m_ref, acc_ref, s_ref, smax_ref, *, t, nq, unroll, lam_init):
    e = 2 * DIFF_HEAD_DIM
    first = lax.broadcasted_iota(jnp.int32, (t, LANES), 1) < DIFF_HEAD_DIM
    ones = jnp.ones((ATT_ONES, t), BF16)
    chains = [(w, slice(c0, c0 + ATT_STRIP)) for c0 in range(0, t, ATT_STRIP) for w in range(2)]

    def rows(blk):
        return pl.ds(pl.multiple_of(blk * t, t), t)

    def prepare(qi, carry):
        q = q_ref[rows(qi), :]
        zero = jnp.zeros_like(q)
        qm_ref[0, rows(qi), :] = jnp.where(first, q, zero)
        qm_ref[1, rows(qi), :] = jnp.where(first, zero, q)
        m_ref[qi] = jnp.full(m_ref.shape[1:], NEG_BIG, F32)
        acc_ref[qi] = jnp.zeros(acc_ref.shape[1:], F32)
        return carry

    def score_chain(ci, qi, j, slot, k):
        w, cols = chains[ci]
        near = jnp.minimum(qi - j, ATT_FAR_TILE)
        q_rows = pl.ds(pl.multiple_of(qi * t + cols.start, ATT_STRIP), ATT_STRIP)
        s = _dot_nt(k, qm_ref[w, q_rows, :]) + tile_ref[near, :, cols]
        s_ref[slot, ci] = s
        smax_ref[slot, ci] = jnp.max(s, axis=0, keepdims=True)

    def value_chain(ci, qi, slot, vta):
        w, cols = chains[ci]
        m_old = m_ref[qi, w, :, cols]
        m_new = jnp.maximum(m_old, smax_ref[slot, ci])
        p = jnp.exp2(s_ref[slot, ci] - m_new).astype(BF16)
        acc_ref[qi, w, :, cols] = jnp.exp2(m_old - m_new) * acc_ref[qi, w, :, cols] + _dot(vta, p)
        m_ref[qi, w, :, cols] = m_new

    def finalize(qi, carry):
        lam = lam_ref[0:1, 0:1]
        o1 = acc_ref[qi, 0, 0:e, :] * (1.0 / acc_ref[qi, 0, e:e + 1, :])
        o2 = acc_ref[qi, 1, 0:e, :] * (1.0 / acc_ref[qi, 1, e:e + 1, :])
        out_t = o1 - lam * o2
        ms = jnp.mean(out_t * out_t, axis=0, keepdims=True)
        out_t = out_t * lax.rsqrt(ms + 1e-5) * sw_ref[...] * (1.0 - lam_init)
        o_ref[rows(qi), :] = out_t.T.astype(BF16)
        return carry

    def step(qv, jv, slot):
        wrap = jv == qv
        qs = jnp.where(wrap, qv + 1, qv)
        js = jnp.where(wrap, 0, jv + 1)
        past_end = qs >= nq
        qs = jnp.where(past_end, nq - 1, qs)
        js = jnp.where(past_end, nq - 1, js)
        k_next = k_ref[rows(js), :]
        vta = jnp.concatenate([vt_ref[:, rows(jv)], ones], axis=0)
        for ci in range(len(chains)):
            score_chain(ci, qs, js, 1 - slot, k_next)
            value_chain(ci, qv, slot, vta)
        return qs, js

    lax.fori_loop(0, nq, prepare, 0)
    k0 = k_ref[rows(0), :]
    for ci in range(len(chains)):
        score_chain(ci, 0, 0, 0, k0)

    def body(_, carry):
        qv, jv = carry
        for u in range(unroll):
            qv, jv = step(qv, jv, u % 2)
        return qv, jv

    n_pairs = nq * (nq + 1) // 2
    lax.fori_loop(0, n_pairs // unroll, body, (jnp.int32(0), jnp.int32(0)))
    lax.fori_loop(0, nq, finalize, 0)


def _attn_call(q, k, vt, tiles, lam, subln_w, *, batch, lam_init, t=ATT_T):
    T, W = q.shape
    S = T // batch
    H = W // LANES
    nq = S // t
    e = 2 * DIFF_HEAD_DIM
    assert e == LANES and S % t == 0
    n_pairs = nq * (nq + 1) // 2
    unroll = max(u for u in (8, 4, 2) if n_pairs % u == 0)
    n_chains = 2 * (t // ATT_STRIP)
    n_tiles = ATT_FAR_TILE + 1
    nbytes = (2 * 4 * (S * LANES * 2) + 2 * n_tiles * t * t * 4 + 2 * S * LANES * 2
              + nq * 2 * (e + ATT_ONES + SUBLANES) * t * 4 + 2 * n_chains * t * ATT_STRIP * 4 + 2 * t * t * 4)
    return pl.pallas_call(
        functools.partial(_attn_kernel, t=t, nq=nq, unroll=unroll, lam_init=lam_init),
        grid=(batch, H),
        in_specs=[
            pl.BlockSpec((S, LANES), lambda b, h: (b, h)),
            pl.BlockSpec((S, LANES), lambda b, h: (b, h)),
            pl.BlockSpec((LANES, S), lambda b, h: (h, b)),
            pl.BlockSpec((None, n_tiles, t, t), lambda b, h: (h, 0, 0, 0)),
            pl.BlockSpec((SUBLANES, LANES), lambda b, h: (0, 0)),
            pl.BlockSpec((e, 1), lambda b, h: (0, 0)),
        ],
        out_specs=pl.BlockSpec((S, LANES), lambda b, h: (b, h)),
        out_shape=jax.ShapeDtypeStruct((T, W), BF16),
        scratch_shapes=[pltpu.VMEM((2, S, LANES), BF16),
                        pltpu.VMEM((nq, 2, 1, t), F32), pltpu.VMEM((nq, 2, e + ATT_ONES, t), F32),
                        pltpu.VMEM((2, n_chains, t, ATT_STRIP), F32),
                        pltpu.VMEM((2, n_chains, 1, ATT_STRIP), F32)],
        compiler_params=pltpu.CompilerParams(
            dimension_semantics=("arbitrary", "arbitrary"), vmem_limit_bytes=_vmem_limit(nbytes)),
        name="diff_attn",
    )(q, k, vt, tiles, lam, subln_w.reshape(e, 1))


def _split3(a):
    hi = a.astype(BF16)
    r1 = a - hi.astype(F32)
    mid = r1.astype(BF16)
    lo = (r1 - mid.astype(F32)).astype(BF16)
    return hi, mid, lo


def _ssd_kernel(xc_ref, z_ref, dt_ref, alog_ref, dsk_ref, nw_ref, y_ref, st_ref, *, lb):
    L = SSM_CHUNK
    gw = HEADS_PER_GROUP * SSM_HEAD_DIM

    @pl.when(pl.program_id(1) == 0)
    def _():
        st_ref[...] = jnp.zeros(st_ref.shape, F32)

    xc = xc_ref[...]
    dt = dt_ref[...]
    a_dt = -jnp.exp(alog_ref[...]) * dt

    row = lax.broadcasted_iota(jnp.int32, (L, L), 0)
    col = lax.broadcasted_iota(jnp.int32, (L, L), 1)
    causal = row >= col
    tri = jnp.where(causal, 1.0, 0.0).astype(BF16)
    lane_g = lax.broadcasted_iota(jnp.int32, (L, gw), 1) // SSM_HEAD_DIM
    lane_lo = lax.broadcasted_iota(jnp.int32, (L, LANES), 1) < SSM_HEAD_DIM
    lane_lo1 = lax.broadcasted_iota(jnp.int32, (1, LANES), 1) < SSM_HEAD_DIM

    for c in range(lb // L):
        r0 = c * L
        hi, mid, lo = _split3(a_dt[r0:r0 + L, :])
        cs = _dot(tri, hi) + _dot(tri, mid) + _dot(tri, lo)
        cs_t = cs.T
        dt_t = dt[r0:r0 + L, :].T
        xs = xc[r0:r0 + L, 0:SSM_WIDTH]
        ys = []
        for g in range(SSM_GROUPS):
            b0 = SSM_WIDTH + g * SSM_STATE
            c0 = SSM_WIDTH + SSM_GROUPS * SSM_STATE + g * SSM_STATE
            bg = xc[r0:r0 + L, b0:b0 + SSM_STATE]
            cg = xc[r0:r0 + L, c0:c0 + SSM_STATE].astype(BF16)
            gmat = _dot_nt(cg, bg.astype(BF16))
            bg_t = bg.T
            xg = xs[:, g * gw:(g + 1) * gw]
            ydiag = None
            st_new = None
            ecols, decs = [], []
            for hh in range(HEADS_PER_GROUP):
                h = g * HEADS_PER_GROUP + hh
                colb = jnp.broadcast_to(cs[:, h:h + 1], (L, L))
                rowb = cs_t[h:h + 1, :]
                dtrow = dt_t[h:h + 1, :]
                decay = jnp.exp(jnp.where(causal, colb - rowb, NEG_BIG))
                mh = (gmat * decay * dtrow).astype(BF16)
                xm = jnp.where(lane_g == hh, xg, 0.0).astype(BF16)
                last = colb[L - 1:L, :]
                wrow = dtrow * jnp.exp(last - rowb)
                btw = (bg_t * wrow).astype(BF16)
                d1 = _dot(mh, xm)
                d2 = _dot(btw, xm)
                ydiag = d1 if ydiag is None else ydiag + d1
                st_new = d2 if st_new is None else st_new + d2
                ecols.append(jnp.exp(colb))
                decs.append(jnp.exp(last))
            e_mat = jnp.concatenate([jnp.where(lane_lo, ecols[0], ecols[1]),
                                     jnp.where(lane_lo, ecols[2], ecols[3])], axis=1)
            dec = jnp.concatenate([jnp.where(lane_lo1, decs[0], decs[1]),
                                   jnp.where(lane_lo1, decs[2], decs[3])], axis=1)
            st_old = st_ref[g]
            ys.append(ydiag + _dot(cg, st_old.astype(BF16)) * e_mat)
            st_ref[g] = st_old * dec + st_new
        y = jnp.concatenate(ys, axis=1) + dsk_ref[...] * xs
        gated = y * z_ref[r0:r0 + L, :]
        outs = []
        for g in range(SSM_GROUPS):
            part = gated[:, g * gw:(g + 1) * gw]
            ms = jnp.mean(part * part, axis=-1, keepdims=True)
            outs.append(part * lax.rsqrt(ms + 1e-5))
        y_ref[r0:r0 + L, :] = (jnp.concatenate(outs, axis=1) * nw_ref[...]).astype(BF16)


def _ssd_call(xc, z, dt, a_log, d_skip, norm_w, *, batch, lb=SSD_LB):
    T = xc.shape[0]
    S = T // batch
    nb = S // lb
    assert S % lb == 0 and lb % SSM_CHUNK == 0
    a_pad = jnp.pad(a_log.astype(F32), (0, LANES - a_log.shape[0])).reshape(1, LANES)
    dsk = jnp.repeat(d_skip.astype(F32), SSM_HEAD_DIM).reshape(1, SSM_WIDTH)
    row = lambda b, j: (b * nb + j, 0)
    const = lambda shape: pl.BlockSpec(shape, lambda b, j: (0, 0))
    nbytes = 3 * lb * (SSM_XBC + SSM_WIDTH + LANES) * 4 + 8 * lb * SSM_XBC * 4
    return pl.pallas_call(
        functools.partial(_ssd_kernel, lb=lb),
        grid=(batch, nb),
        in_specs=[pl.BlockSpec((lb, SSM_XBC), row), pl.BlockSpec((lb, SSM_WIDTH), row),
                  pl.BlockSpec((lb, LANES), row),
                  const((1, LANES)), const((1, SSM_WIDTH)), const((1, SSM_WIDTH))],
        out_specs=pl.BlockSpec((lb, SSM_WIDTH), row),
        out_shape=jax.ShapeDtypeStruct((T, SSM_WIDTH), BF16),
        scratch_shapes=[pltpu.VMEM((SSM_GROUPS, SSM_STATE, HEADS_PER_GROUP * SSM_HEAD_DIM), F32)],
        compiler_params=pltpu.CompilerParams(
            dimension_semantics=("arbitrary", "arbitrary"), vmem_limit_bytes=_vmem_limit(nbytes)),
        name="ssd",
    )(xc, z, dt, a_pad, dsk, norm_w.reshape(1, SSM_WIDTH).astype(F32))


def _sc_kernel(x_ref, nw_ref, win_ref, cw_ref, wout_ref, o_ref, vpad_ref, *, tm, tiles_per_seq):
    d = x_ref.shape[1]

    @pl.when(pl.program_id(0) % tiles_per_seq == 0)
    def _():
        vpad_ref[0:HALO, :] = jnp.zeros((HALO, d), F32)

    x = x_ref[...]
    h = _rms(x, nw_ref[...], 1e-6).astype(BF16)
    bgate = _dot(h, win_ref[:, 0:d])
    v = _dot(h, win_ref[:, d:2 * d]) * _dot(h, win_ref[:, 2 * d:3 * d])
    vpad_ref[HALO:HALO + tm, :] = v
    conv = cw_ref[SC_CONV - 1:SC_CONV, :] * v
    for kk in range(SC_CONV - 1):
        o = HALO - (SC_CONV - 1) + kk
        conv = conv + cw_ref[kk:kk + 1, :] * vpad_ref[o:o + tm, :]
    vpad_ref[0:HALO, :] = vpad_ref[tm:tm + HALO, :]
    o_ref[...] = x + _dot((bgate * conv).astype(BF16), wout_ref[...])


def _sc_call(x, nw, w_in, conv_w, w_out, *, batch, tm=PROJ_TM):
    T, D = x.shape
    S = T // batch
    assert S % tm == 0
    row = lambda i: (i, 0)
    nbytes = (w_in.size + w_out.size) * 2 + 4 * tm * D * 4 + 8 * tm * D * 4
    return pl.pallas_call(
        functools.partial(_sc_kernel, tm=tm, tiles_per_seq=S // tm),
        grid=(T // tm,),
        in_specs=[pl.BlockSpec((tm, D), row), _resident((1, D)), _resident(w_in.shape),
                  _resident(conv_w.shape), _resident(w_out.shape)],
        out_specs=pl.BlockSpec((tm, D), row),
        out_shape=jax.ShapeDtypeStruct((T, D), F32),
        scratch_shapes=[pltpu.VMEM((tm + HALO, D), F32)],
        compiler_params=pltpu.CompilerParams(
            dimension_semantics=("arbitrary",), vmem_limit_bytes=_vmem_limit(nbytes)),
        name="short_conv",
    )(x, nw.reshape(1, D), w_in, conv_w.astype(F32), w_out)


def kernel(x, rel_bias, final_norm_w, ffn1_norm, ffn1_wg, ffn1_wu, ffn1_wd, mix_norm, ffn2_norm, ffn2_wg, ffn2_wu, ffn2_wd, hyb_w_in, hyb_w_out, diff_lq1, diff_lk1, diff_lq2, diff_lk2, diff_subln_w, ssm_conv_w, ssm_conv_b, ssm_dt_bias, ssm_a_log, ssm_d, ssm_norm_w, sc_w_in, sc_conv_w, sc_w_out):
    B, S, D = x.shape
    T = B * S
    xt = x.reshape(T, D)
    bf = lambda w: w.astype(BF16)

    lam_inits = [0.8 - 0.6 * math.exp(-0.3 * i) for i in range(0, DEPTH, 2)]
    tiles, lams = _prep_call(rel_bias, diff_lq1, diff_lk1, diff_lq2, diff_lk2, lam_inits)

    mix = None
    for i in range(DEPTH):
        j = i // 2
        xt = _ffn_call(xt, ffn1_norm[i], ffn1_wg, ffn1_wu, ffn1_wd, i)
        if i % 2 == 0:
            w_in = hyb_w_in[j]
            qkv_end = 3 * DIFF_WIDTH
            w_pad = bf(jnp.pad(w_in, ((0, 0), (0, LANES - SSM_HEADS))))
            wvt = bf(w_in[:, 2 * DIFF_WIDTH:qkv_end].T)
            q, k, vt, z, xc, dt = _hyb_in_call(xt, mix_norm[i], w_pad, wvt, ssm_conv_w[j], ssm_conv_b[j],
                                               ssm_dt_bias[j], batch=B)
            o = _attn_call(q, k, vt, tiles, lams[j * SUBLANES:(j + 1) * SUBLANES], diff_subln_w[j],
                           batch=B, lam_init=lam_inits[j])
            y = _ssd_call(xc, z, dt, ssm_a_log[j], ssm_d[j], ssm_norm_w[j], batch=B)
            mix = (o, y, hyb_w_out, j)
        else:
            xt = _sc_call(xt, mix_norm[i], bf(sc_w_in[j]), sc_conv_w[j], bf(sc_w_out[j]), batch=B)
            mix = None
        xt = _ffn_call(xt, ffn2_norm[i], ffn2_wg, ffn2_wu, ffn2_wd, i, mix=mix,
                       final_w=final_norm_w if i == DEPTH - 1 else None)
    return xt.reshape(B, S, D)
```

```python
import functools
import math

import jax
import jax.numpy as jnp
from jax import lax
from jax.experimental import pallas as pl
from jax.experimental.pallas import tpu as pltpu

F32 = jnp.float32
BF16 = jnp.bfloat16

DEPTH = 4
N_DIFF_HEADS = 4
DIFF_HEAD_DIM = 64
DIFF_WIDTH = N_DIFF_HEADS * 2 * DIFF_HEAD_DIM
NUM_BUCKETS = 32
MAX_DISTANCE = 128
SSM_HEADS = 8
SSM_HEAD_DIM = 64
SSM_WIDTH = SSM_HEADS * SSM_HEAD_DIM
SSM_GROUPS = 2
SSM_STATE = 128
SSM_CONV = 4
SSM_CHUNK = 128
HEADS_PER_GROUP = SSM_HEADS // SSM_GROUPS
SSM_XBC = SSM_WIDTH + 2 * SSM_GROUPS * SSM_STATE
SC_CONV = 3

LANES = 128
SUBLANES = 8
V7X_SCOPED_VMEM_CAP = 60000 * 1024

FFN_TM = 1024
FFN_FCHUNK = 256
PROJ_TM = 512
ATT_T = 512
ATT_STRIP = 256
ATT_ONES = 16
ATT_FAR_TILE = 2
SSD_LB = 512
HALO = SUBLANES

NEG_BIG = -1e30
LOG2E = math.log2(math.e)


def _vmem_limit(nbytes):
    return int(min(V7X_SCOPED_VMEM_CAP, nbytes * 5 // 4 + (8 << 20)))


def _dot(a, b):
    return jnp.dot(a, b, preferred_element_type=F32)


def _dot_nt(a, b):
    return lax.dot_general(a, b, (((1,), (1,)), ((), ())), preferred_element_type=F32)


def _rms(x, w, eps):
    ms = jnp.mean(x * x, axis=-1, keepdims=True)
    return x * lax.rsqrt(ms + eps) * w


def _silu(x):
    return x * jax.nn.sigmoid(x)


def _resident(shape):
    nd = len(shape)
    return pl.BlockSpec(shape, lambda *_: (0,) * nd, pipeline_mode=pl.Buffered(1))


def _ffn_kernel(*refs, pre_mix, final_norm, f_total, layer):
    refs = list(refs)
    x_ref = refs.pop(0)
    if pre_mix:
        o_ref, y_ref, wo_ref = refs.pop(0), refs.pop(0), refs.pop(0)
    nw_ref, wg_hbm, wu_hbm, wd_hbm = refs.pop(0), refs.pop(0), refs.pop(0), refs.pop(0)
    if final_norm:
        fw_ref = refs.pop(0)
    out_ref = refs.pop(0)
    wg_ref, wu_ref, wd_ref, stage_g, stage_u, stage_d, sems, h_ref, acc_ref = refs

    chunks = list(range(0, f_total, FFN_FCHUNK))

    def copies(ci):
        cols = pl.ds(chunks[ci], FFN_FCHUNK)
        slot = ci % 2
        return (pltpu.make_async_copy(wg_hbm.at[layer, :, cols], stage_g.at[slot], sems.at[0, slot]),
                pltpu.make_async_copy(wu_hbm.at[layer, :, cols], stage_u.at[slot], sems.at[1, slot]),
                pltpu.make_async_copy(wd_hbm.at[layer, cols, :], stage_d.at[slot], sems.at[2, slot]))

    def body(load_weights):
        if load_weights:
            for ci in range(min(2, len(chunks))):
                for cp in copies(ci):
                    cp.start()
        x = x_ref[...]
        if pre_mix:
            half = o_ref.shape[1]
            x = (x + _dot(o_ref[...], wo_ref[0:half, :].astype(BF16))
                 + _dot(y_ref[...], wo_ref[half:, :].astype(BF16)))
        out_ref[...] = x
        h_ref[...] = _rms(x, nw_ref[...], 1e-6).astype(BF16)
        for ci, c0 in enumerate(chunks):
            c1 = c0 + FFN_FCHUNK
            if load_weights:
                slot = ci % 2
                for cp in copies(ci):
                    cp.wait()
                wg_ref[:, c0:c1] = stage_g[slot].astype(BF16)
                wu_ref[:, c0:c1] = stage_u[slot].astype(BF16)
                wd_ref[c0:c1, :] = stage_d[slot].astype(BF16)
                if ci + 2 < len(chunks):
                    for cp in copies(ci + 2):
                        cp.start()
            h = h_ref[...]
            g = _dot(h, wg_ref[:, c0:c1])
            u = _dot(h, wu_ref[:, c0:c1])
            a = (_silu(g) * u).astype(BF16)
            d = _dot(a, wd_ref[c0:c1, :])
            acc_ref[...] = d if ci == 0 else acc_ref[...] + d
        y = out_ref[...] + 0.5 * acc_ref[...]
        if final_norm:
            y = _rms(y, fw_ref[...], 1e-6)
        out_ref[...] = y

    first = pl.program_id(0) == 0
    pl.when(first)(functools.partial(body, True))
    pl.when(jnp.logical_not(first))(functools.partial(body, False))


def _layer_resident(stacked, layer):
    _, r, c = stacked.shape
    return pl.BlockSpec((None, r, c), lambda *_: (layer, 0, 0), pipeline_mode=pl.Buffered(1))


def _ffn_call(x, nw, wg, wu, wd, layer, *, mix=None, final_w=None, tm=FFN_TM):
    T, D = x.shape
    F = wg.shape[2]
    assert T % tm == 0 and F % FFN_FCHUNK == 0
    row = lambda i: (i, 0)
    args = [x]
    in_specs = [pl.BlockSpec((tm, D), row)]
    nbytes = (4 * tm * D * 4 + 3 * D * F * 2 + 6 * D * FFN_FCHUNK * 4 + 6 * tm * FFN_FCHUNK * 4
              + 2 * tm * D * 4)
    if mix is not None:
        o, y, wo, mix_layer = mix
        args += [o, y, wo]
        in_specs += [pl.BlockSpec((tm, o.shape[1]), row), pl.BlockSpec((tm, y.shape[1]), row),
                     _layer_resident(wo, mix_layer)]
        nbytes += wo.shape[1] * wo.shape[2] * 4 + 4 * tm * o.shape[1] * 2 * 2
    args += [nw.reshape(1, D), wg, wu, wd]
    hbm = pl.BlockSpec(memory_space=pl.ANY)
    in_specs += [_resident((1, D)), hbm, hbm, hbm]
    if final_w is not None:
        args.append(final_w.reshape(1, D))
        in_specs.append(_resident((1, D)))
    kern = functools.partial(_ffn_kernel, pre_mix=mix is not None, final_norm=final_w is not None,
                             f_total=F, layer=layer)
    return pl.pallas_call(
        kern,
        grid=(T // tm,),
        in_specs=in_specs,
        out_specs=pl.BlockSpec((tm, D), row),
        out_shape=jax.ShapeDtypeStruct((T, D), F32),
        scratch_shapes=[pltpu.VMEM((D, F), BF16), pltpu.VMEM((D, F), BF16), pltpu.VMEM((F, D), BF16),
                        pltpu.VMEM((2, D, FFN_FCHUNK), F32), pltpu.VMEM((2, D, FFN_FCHUNK), F32),
                        pltpu.VMEM((2, FFN_FCHUNK, D), F32), pltpu.SemaphoreType.DMA((3, 2)),
                        pltpu.VMEM((tm, D), BF16), pltpu.VMEM((tm, D), F32)],
        compiler_params=pltpu.CompilerParams(
            dimension_semantics=("arbitrary",), vmem_limit_bytes=_vmem_limit(nbytes)),
        name="ffn",
    )(*args)


def _hyb_in_kernel(x_ref, nw_ref, w_ref, wvt_ref, cw_ref, cb_ref, dtb_ref,
                   q_ref, k_ref, vt_ref, z_ref, xc_ref, dt_ref, xpad_ref, *, tm, tiles_per_seq):
    @pl.when(pl.program_id(0) % tiles_per_seq == 0)
    def _():
        xpad_ref[0:HALO, :] = jnp.zeros((HALO, SSM_XBC), F32)

    h = _rms(x_ref[...], nw_ref[...], 1e-6).astype(BF16)
    qw = DIFF_WIDTH
    c_z = 2 * qw + DIFF_WIDTH
    c_xbc = c_z + SSM_WIDTH
    c_dt = c_xbc + SSM_XBC
    xpad_ref[HALO:HALO + tm, :] = _dot(h, w_ref[:, c_xbc:c_xbc + SSM_XBC])
    scale = DIFF_HEAD_DIM ** -0.5 * LOG2E
    q_ref[...] = (_dot(h, w_ref[:, 0:qw]) * scale).astype(BF16)
    k_ref[...] = _dot(h, w_ref[:, qw:2 * qw]).astype(BF16)
    vt_ref[...] = _dot_nt(wvt_ref[...], h).astype(BF16)
    z_ref[...] = _silu(_dot(h, w_ref[:, c_z:c_z + SSM_WIDTH]))
    dt_raw = _dot(h, w_ref[:, c_dt:c_dt + LANES]) + dtb_ref[...]
    dt_ref[...] = jnp.maximum(dt_raw, 0.0) + jnp.log1p(jnp.exp(-jnp.abs(dt_raw)))

    conv = cb_ref[...]
    for kk in range(SSM_CONV):
        o = HALO - (SSM_CONV - 1) + kk
        conv = conv + cw_ref[kk:kk + 1, :] * xpad_ref[o:o + tm, :]
    xc_ref[...] = _silu(conv)
    xpad_ref[0:HALO, :] = xpad_ref[tm:tm + HALO, :]


def _hyb_in_call(x, nw, w_pad, wvt, conv_w, conv_b, dt_bias, *, batch, tm=PROJ_TM):
    T, D = x.shape
    S = T // batch
    assert S % tm == 0
    row = lambda i: (i, 0)
    outs = [
        jax.ShapeDtypeStruct((T, DIFF_WIDTH), BF16),
        jax.ShapeDtypeStruct((T, DIFF_WIDTH), BF16),
        jax.ShapeDtypeStruct((DIFF_WIDTH, T), BF16),
        jax.ShapeDtypeStruct((T, SSM_WIDTH), F32),
        jax.ShapeDtypeStruct((T, SSM_XBC), F32),
        jax.ShapeDtypeStruct((T, LANES), F32),
    ]
    out_specs = [
        pl.BlockSpec((tm, DIFF_WIDTH), row), pl.BlockSpec((tm, DIFF_WIDTH), row),
        pl.BlockSpec((DIFF_WIDTH, tm), lambda i: (0, i)),
        pl.BlockSpec((tm, SSM_WIDTH), row), pl.BlockSpec((tm, SSM_XBC), row),
        pl.BlockSpec((tm, LANES), row),
    ]
    nbytes = ((w_pad.size + wvt.size) * 2 + 2 * tm * D * 4
              + 3 * tm * 4 * (3 * DIFF_WIDTH + SSM_WIDTH + SSM_XBC + LANES) + 3 * tm * SSM_XBC * 4)
    pad = jnp.pad(dt_bias.astype(F32), (0, LANES - dt_bias.shape[0])).reshape(1, LANES)
    return pl.pallas_call(
        functools.partial(_hyb_in_kernel, tm=tm, tiles_per_seq=S // tm),
        grid=(T // tm,),
        in_specs=[pl.BlockSpec((tm, D), row), _resident((1, D)), _resident(w_pad.shape),
                  _resident(wvt.shape), _resident((SSM_CONV, SSM_XBC)), _resident((1, SSM_XBC)),
                  _resident((1, LANES))],
        out_specs=out_specs,
        out_shape=outs,
        scratch_shapes=[pltpu.VMEM((tm + HALO, SSM_XBC), F32)],
        compiler_params=pltpu.CompilerParams(
            dimension_semantics=("arbitrary",), vmem_limit_bytes=_vmem_limit(nbytes)),
        name="hyb_in",
    )(x, nw.reshape(1, D), w_pad, wvt, conv_w.astype(F32), conv_b.reshape(1, SSM_XBC).astype(F32), pad)


def _prep_kernel(tab_ref, lq1_ref, lk1_ref, lq2_ref, lk2_ref, tile_ref, lam_ref, *, t, lam_inits):
    hd = pl.program_id(0)
    key = lax.broadcasted_iota(jnp.int32, (t, t), 0)
    qry = lax.broadcasted_iota(jnp.int32, (t, t), 1)
    max_exact = NUM_BUCKETS // 2
    far = tab_ref[hd, NUM_BUCKETS - 1]
    tile_ref[ATT_FAR_TILE] = jnp.zeros((t, t), F32)
    for off in range(ATT_FAR_TILE):
        dist = off * t + qry - key
        d = jnp.maximum(dist, 0)
        large = max_exact + (
            jnp.log(jnp.maximum(d, 1).astype(F32) / max_exact)
            / math.log(MAX_DISTANCE / max_exact) * (NUM_BUCKETS - max_exact)).astype(jnp.int32)
        bucket = jnp.where(d < max_exact, d, jnp.minimum(large, NUM_BUCKETS - 1))
        bias = jnp.full((t, t), far, F32)
        for b in range(NUM_BUCKETS - 1):
            bias = jnp.where(bucket == b, tab_ref[hd, b], bias)
        val = (bias - far) * LOG2E
        if off == 0:
            val = jnp.where(dist >= 0, val, NEG_BIG)
        tile_ref[off] = val
    for j, lam_init in enumerate(lam_inits):
        s1 = jnp.sum(lq1_ref[j:j + 1, :] * lk1_ref[j:j + 1, :], axis=-1, keepdims=True)
        s2 = jnp.sum(lq2_ref[j:j + 1, :] * lk2_ref[j:j + 1, :], axis=-1, keepdims=True)
        lam = jnp.exp(s1) - jnp.exp(s2) + lam_init
        lam_ref[j * SUBLANES:(j + 1) * SUBLANES, :] = jnp.broadcast_to(lam, (SUBLANES, LANES))


def _prep_call(rel_bias, lq1, lk1, lq2, lk2, lam_inits, *, t=ATT_T):
    ne = lq1.shape[0]
    H = rel_bias.shape[1]
    whole = lambda shape: pl.BlockSpec(shape, lambda h: (0,) * len(shape))
    return pl.pallas_call(
        functools.partial(_prep_kernel, t=t, lam_inits=tuple(lam_inits)),
        grid=(H,),
        in_specs=[pl.BlockSpec(memory_space=pltpu.SMEM)] + [whole(lq1.shape)] * 4,
        out_specs=[pl.BlockSpec((None, ATT_FAR_TILE + 1, t, t), lambda h: (h, 0, 0, 0)),
                   whole((ne * SUBLANES, LANES))],
        out_shape=[jax.ShapeDtypeStruct((H, ATT_FAR_TILE + 1, t, t), F32),
                   jax.ShapeDtypeStruct((ne * SUBLANES, LANES), F32)],
        compiler_params=pltpu.CompilerParams(
            dimension_semantics=("arbitrary",), vmem_limit_bytes=_vmem_limit(12 * t * t * 4)),
        name="attn_prep",
    )(rel_bias.T.astype(F32), lq1, lk1, lq2, lk2)


def _attn_kernel(q_ref, k_ref, vt_ref, tile_ref, lam_ref, sw_ref, o_ref,
                 qm_ref, m_ref, acc_ref, s_ref, smax_ref, *, t, nq, unroll, lam_init):
    e = 2 * DIFF_HEAD_DIM
    first = lax.broadcasted_iota(jnp.int32, (t, LANES), 1) < DIFF_HEAD_DIM
    ones = jnp.ones((ATT_ONES, t), BF16)
    chains = [(w, slice(c0, c0 + ATT_STRIP)) for c0 in range(0, t, ATT_STRIP) for w in range(2)]

    def rows(blk):
        return pl.ds(pl.multiple_of(blk * t, t), t)

    def prepare(qi, carry):
        q = q_ref[rows(qi), :]
        zero = jnp.zeros_like(q)
        qm_ref[0, rows(qi), :] = jnp.where(first, q, zero)
        qm_ref[1, rows(qi), :] = jnp.where(first, zero, q)
        m_ref[qi] = jnp.full(m_ref.shape[1:], NEG_BIG, F32)
        acc_ref[qi] = jnp.zeros(acc_ref.shape[1:], F32)
        return carry

    def score_chain(ci, qi, j, slot, k):
        w, cols = chains[ci]
        near = jnp.minimum(qi - j, ATT_FAR_TILE)
        q_rows = pl.ds(pl.multiple_of(qi * t + cols.start, ATT_STRIP), ATT_STRIP)
        s = _dot_nt(k, qm_ref[w, q_rows, :]) + tile_ref[near, :, cols]
        s_ref[slot, ci] = s
        smax_ref[slot, ci] = jnp.max(s, axis=0, keepdims=True)

    def value_chain(ci, qi, slot, vta):
        w, cols = chains[ci]
        m_old = m_ref[qi, w, :, cols]
        m_new = jnp.maximum(m_old, smax_ref[slot, ci])
        p = jnp.exp2(s_ref[slot, ci] - m_new).astype(BF16)
        acc_ref[qi, w, :, cols] = jnp.exp2(m_old - m_new) * acc_ref[qi, w, :, cols] + _dot(vta, p)
        m_ref[qi, w, :, cols] = m_new

    def finalize(qi, carry):
        lam = lam_ref[0:1, 0:1]
        o1 = acc_ref[qi, 0, 0:e, :] * (1.0 / acc_ref[qi, 0, e:e + 1, :])
        o2 = acc_ref[qi, 1, 0:e, :] * (1.0 / acc_ref[qi, 1, e:e + 1, :])
        out_t = o1 - lam * o2
        ms = jnp.mean(out_t * out_t, axis=0, keepdims=True)
        out_t = out_t * lax.rsqrt(ms + 1e-5) * sw_ref[...] * (1.0 - lam_init)
        o_ref[rows(qi), :] = out_t.T.astype(BF16)
        return carry

    def step(qv, jv, slot):
        wrap = jv == qv
        qs = jnp.where(wrap, qv + 1, qv)
        js = jnp.where(wrap, 0, jv + 1)
        past_end = qs >= nq
        qs = jnp.where(past_end, nq - 1, qs)
        js = jnp.where(past_end, nq - 1, js)
        k_next = k_ref[rows(js), :]
        vta = jnp.concatenate([vt_ref[:, rows(jv)], ones], axis=0)
        for ci in range(len(chains)):
            score_chain(ci, qs, js, 1 - slot, k_next)
            value_chain(ci, qv, slot, vta)
        return qs, js

    lax.fori_loop(0, nq, prepare, 0)
    k0 = k_ref[rows(0), :]
    for ci in range(len(chains)):
        score_chain(ci, 0, 0, 0, k0)

    def body(_, carry):
        qv, jv = carry
        for u in range(unroll):
            qv, jv = step(qv, jv, u % 2)
        return qv, jv

    n_pairs = nq * (nq + 1) // 2
    lax.fori_loop(0, n_pairs // unroll, body, (jnp.int32(0), jnp.int32(0)))
    lax.fori_loop(0, nq, finalize, 0)


def _attn_call(q, k, vt, tiles, lam, subln_w, *, batch, lam_init, t=ATT_T):
    T, W = q.shape
    S = T // batch
    H = W // LANES
    nq = S // t
    e = 2 * DIFF_HEAD_DIM
    assert e == LANES and S % t == 0
    n_pairs = nq * (nq + 1) // 2
    unroll = max(u for u in (8, 4, 2) if n_pairs % u == 0)
    n_chains = 2 * (t // ATT_STRIP)
    n_tiles = ATT_FAR_TILE + 1
    nbytes = (2 * 4 * (S * LANES * 2) + 2 * n_tiles * t * t * 4 + 2 * S * LANES * 2
              + nq * 2 * (e + ATT_ONES + SUBLANES) * t * 4 + 2 * n_chains * t * ATT_STRIP * 4 + 2 * t * t * 4)
    return pl.pallas_call(
        functools.partial(_attn_kernel, t=t, nq=nq, unroll=unroll, lam_init=lam_init),
        grid=(batch, H),
        in_specs=[
            pl.BlockSpec((S, LANES), lambda b, h: (b, h)),
            pl.BlockSpec((S, LANES), lambda b, h: (b, h)),
            pl.BlockSpec((LANES, S), lambda b, h: (h, b)),
            pl.BlockSpec((None, n_tiles, t, t), lambda b, h: (h, 0, 0, 0)),
            pl.BlockSpec((SUBLANES, LANES), lambda b, h: (0, 0)),
            pl.BlockSpec((e, 1), lambda b, h: (0, 0)),
        ],
        out_specs=pl.BlockSpec((S, LANES), lambda b, h: (b, h)),
        out_shape=jax.ShapeDtypeStruct((T, W), BF16),
        scratch_shapes=[pltpu.VMEM((2, S, LANES), BF16),
                        pltpu.VMEM((nq, 2, 1, t), F32), pltpu.VMEM((nq, 2, e + ATT_ONES, t), F32),
                        pltpu.VMEM((2, n_chains, t, ATT_STRIP), F32),
                        pltpu.VMEM((2, n_chains, 1, ATT_STRIP), F32)],
        compiler_params=pltpu.CompilerParams(
            dimension_semantics=("arbitrary", "arbitrary"), vmem_limit_bytes=_vmem_limit(nbytes)),
        name="diff_attn",
    )(q, k, vt, tiles, lam, subln_w.reshape(e, 1))


def _split3(a):
    hi = a.astype(BF16)
    r1 = a - hi.astype(F32)
    mid = r1.astype(BF16)
    lo = (r1 - mid.astype(F32)).astype(BF16)
    return hi, mid, lo


def _ssd_kernel(xc_ref, z_ref, dt_ref, alog_ref, dsk_ref, nw_ref, y_ref, st_ref, *, lb):
    L = SSM_CHUNK
    gw = HEADS_PER_GROUP * SSM_HEAD_DIM

    @pl.when(pl.program_id(1) == 0)
    def _():
        st_ref[...] = jnp.zeros(st_ref.shape, F32)

    xc = xc_ref[...]
    dt = dt_ref[...]
    a_dt = -jnp.exp(alog_ref[...]) * dt

    row = lax.broadcasted_iota(jnp.int32, (L, L), 0)
    col = lax.broadcasted_iota(jnp.int32, (L, L), 1)
    causal = row >= col
    tri = jnp.where(causal, 1.0, 0.0).astype(BF16)
    lane_g = lax.broadcasted_iota(jnp.int32, (L, gw), 1) // SSM_HEAD_DIM
    lane_lo = lax.broadcasted_iota(jnp.int32, (L, LANES), 1) < SSM_HEAD_DIM
    lane_lo1 = lax.broadcasted_iota(jnp.int32, (1, LANES), 1) < SSM_HEAD_DIM

    for c in range(lb // L):
        r0 = c * L
        hi, mid, lo = _split3(a_dt[r0:r0 + L, :])
        cs = _dot(tri, hi) + _dot(tri, mid) + _dot(tri, lo)
        cs_t = cs.T
        dt_t = dt[r0:r0 + L, :].T
        xs = xc[r0:r0 + L, 0:SSM_WIDTH]
        ys = []
        for g in range(SSM_GROUPS):
            b0 = SSM_WIDTH + g * SSM_STATE
            c0 = SSM_WIDTH + SSM_GROUPS * SSM_STATE + g * SSM_STATE
            bg = xc[r0:r0 + L, b0:b0 + SSM_STATE]
            cg = xc[r0:r0 + L, c0:c0 + SSM_STATE].astype(BF16)
            gmat = _dot_nt(cg, bg.astype(BF16))
            bg_t = bg.T
            xg = xs[:, g * gw:(g + 1) * gw]
            ydiag = None
            st_new = None
            ecols, decs = [], []
            for hh in range(HEADS_PER_GROUP):
                h = g * HEADS_PER_GROUP + hh
                colb = jnp.broadcast_to(cs[:, h:h + 1], (L, L))
                rowb = cs_t[h:h + 1, :]
                dtrow = dt_t[h:h + 1, :]
                decay = jnp.exp(jnp.where(causal, colb - rowb, NEG_BIG))
                mh = (gmat * decay * dtrow).astype(BF16)
                xm = jnp.where(lane_g == hh, xg, 0.0).astype(BF16)
                last = colb[L - 1:L, :]
                wrow = dtrow * jnp.exp(last - rowb)
                btw = (bg_t * wrow).astype(BF16)
                d1 = _dot(mh, xm)
                d2 = _dot(btw, xm)
                ydiag = d1 if ydiag is None else ydiag + d1
                st_new = d2 if st_new is None else st_new + d2
                ecols.append(jnp.exp(colb))
                decs.append(jnp.exp(last))
            e_mat = jnp.concatenate([jnp.where(lane_lo, ecols[0], ecols[1]),
                                     jnp.where(lane_lo, ecols[2], ecols[3])], axis=1)
            dec = jnp.concatenate([jnp.where(lane_lo1, decs[0], decs[1]),
                                   jnp.where(lane_lo1, decs[2], decs[3])], axis=1)
            st_old = st_ref[g]
            ys.append(ydiag + _dot(cg, st_old.astype(BF16)) * e_mat)
            st_ref[g] = st_old * dec + st_new
        y = jnp.concatenate(ys, axis=1) + dsk_ref[...] * xs
        gated = y * z_ref[r0:r0 + L, :]
        outs = []
        for g in range(SSM_GROUPS):
            part = gated[:, g * gw:(g + 1) * gw]
            ms = jnp.mean(part * part, axis=-1, keepdims=True)
            outs.append(part * lax.rsqrt(ms + 1e-5))
        y_ref[r0:r0 + L, :] = (jnp.concatenate(outs, axis=1) * nw_ref[...]).astype(BF16)


def _ssd_call(xc, z, dt, a_log, d_skip, norm_w, *, batch, lb=SSD_LB):
    T = xc.shape[0]
    S = T // batch
    nb = S // lb
    assert S % lb == 0 and lb % SSM_CHUNK == 0
    a_pad = jnp.pad(a_log.astype(F32), (0, LANES - a_log.shape[0])).reshape(1, LANES)
    dsk = jnp.repeat(d_skip.astype(F32), SSM_HEAD_DIM).reshape(1, SSM_WIDTH)
    row = lambda b, j: (b * nb + j, 0)
    const = lambda shape: pl.BlockSpec(shape, lambda b, j: (0, 0))
    nbytes = 3 * lb * (SSM_XBC + SSM_WIDTH + LANES) * 4 + 8 * lb * SSM_XBC * 4
    return pl.pallas_call(
        functools.partial(_ssd_kernel, lb=lb),
        grid=(batch, nb),
        in_specs=[pl.BlockSpec((lb, SSM_XBC), row), pl.BlockSpec((lb, SSM_WIDTH), row),
                  pl.BlockSpec((lb, LANES), row),
                  const((1, LANES)), const((1, SSM_WIDTH)), const((1, SSM_WIDTH))],
        out_specs=pl.BlockSpec((lb, SSM_WIDTH), row),
        out_shape=jax.ShapeDtypeStruct((T, SSM_WIDTH), BF16),
        scratch_shapes=[pltpu.VMEM((SSM_GROUPS, SSM_STATE, HEADS_PER_GROUP * SSM_HEAD_DIM), F32)],
        compiler_params=pltpu.CompilerParams(
            dimension_semantics=("arbitrary", "arbitrary"), vmem_limit_bytes=_vmem_limit(nbytes)),
        name="ssd",
    )(xc, z, dt, a_pad, dsk, norm_w.reshape(1, SSM_WIDTH).astype(F32))


def _sc_kernel(x_ref, nw_ref, win_ref, cw_ref, wout_ref, o_ref, vpad_ref, *, tm, tiles_per_seq):
    d = x_ref.shape[1]

    @pl.when(pl.program_id(0) % tiles_per_seq == 0)
    def _():
        vpad_ref[0:HALO, :] = jnp.zeros((HALO, d), F32)

    x = x_ref[...]
    h = _rms(x, nw_ref[...], 1e-6).astype(BF16)
    bgate = _dot(h, win_ref[:, 0:d])
    v = _dot(h, win_ref[:, d:2 * d]) * _dot(h, win_ref[:, 2 * d:3 * d])
    vpad_ref[HALO:HALO + tm, :] = v
    conv = cw_ref[SC_CONV - 1:SC_CONV, :] * v
    for kk in range(SC_CONV - 1):
        o = HALO - (SC_CONV - 1) + kk
        conv = conv + cw_ref[kk:kk + 1, :] * vpad_ref[o:o + tm, :]
    vpad_ref[0:HALO, :] = vpad_ref[tm:tm + HALO, :]
    o_ref[...] = x + _dot((bgate * conv).astype(BF16), wout_ref[...])


def _sc_call(x, nw, w_in, conv_w, w_out, *, batch, tm=PROJ_TM):
    T, D = x.shape
    S = T // batch
    assert S % tm == 0
    row = lambda i: (i, 0)
    nbytes = (w_in.size + w_out.size) * 2 + 4 * tm * D * 4 + 8 * tm * D * 4
    return pl.pallas_call(
        functools.partial(_sc_kernel, tm=tm, tiles_per_seq=S // tm),
        grid=(T // tm,),
        in_specs=[pl.BlockSpec((tm, D), row), _resident((1, D)), _resident(w_in.shape),
                  _resident(conv_w.shape), _resident(w_out.shape)],
        out_specs=pl.BlockSpec((tm, D), row),
        out_shape=jax.ShapeDtypeStruct((T, D), F32),
        scratch_shapes=[pltpu.VMEM((tm + HALO, D), F32)],
        compiler_params=pltpu.CompilerParams(
            dimension_semantics=("arbitrary",), vmem_limit_bytes=_vmem_limit(nbytes)),
        name="short_conv",
    )(x, nw.reshape(1, D), w_in, conv_w.astype(F32), w_out)


def kernel(x, rel_bias, final_norm_w, ffn1_norm, ffn1_wg, ffn1_wu, ffn1_wd, mix_norm, ffn2_norm, ffn2_wg, ffn2_wu, ffn2_wd, hyb_w_in, hyb_w_out, diff_lq1, diff_lk1, diff_lq2, diff_lk2, diff_subln_w, ssm_conv_w, ssm_conv_b, ssm_dt_bias, ssm_a_log, ssm_d, ssm_norm_w, sc_w_in, sc_conv_w, sc_w_out):
    B, S, D = x.shape
    T = B * S
    xt = x.reshape(T, D)
    bf = lambda w: w.astype(BF16)

    lam_inits = [0.8 - 0.6 * math.exp(-0.3 * i) for i in range(0, DEPTH, 2)]
    tiles, lams = _prep_call(rel_bias, diff_lq1, diff_lk1, diff_lq2, diff_lk2, lam_inits)

    mix = None
    for i in range(DEPTH):
        j = i // 2
        xt = _ffn_call(xt, ffn1_norm[i], ffn1_wg, ffn1_wu, ffn1_wd, i)
        if i % 2 == 0:
            w_in = hyb_w_in[j]
            qkv_end = 3 * DIFF_WIDTH
            w_pad = bf(jnp.pad(w_in, ((0, 0), (0, LANES - SSM_HEADS))))
            wvt = bf(w_in[:, 2 * DIFF_WIDTH:qkv_end].T)
            q, k, vt, z, xc, dt = _hyb_in_call(xt, mix_norm[i], w_pad, wvt, ssm_conv_w[j], ssm_conv_b[j],
                                               ssm_dt_bias[j], batch=B)
            o = _attn_call(q, k, vt, tiles, lams[j * SUBLANES:(j + 1) * SUBLANES], diff_subln_w[j],
                           batch=B, lam_init=lam_inits[j])
            y = _ssd_call(xc, z, dt, ssm_a_log[j], ssm_d[j], ssm_norm_w[j], batch=B)
            mix = (o, y, hyb_w_out, j)
        else:
            xt = _sc_call(xt, mix_norm[i], bf(sc_w_in[j]), sc_conv_w[j], bf(sc_w_out[j]), batch=B)
            mix = None
        xt = _ffn_call(xt, ffn2_norm[i], ffn2_wg, ffn2_wu, ffn2_wd, i, mix=mix,
                       final_w=final_norm_w if i == DEPTH - 1 else None)
    return xt.reshape(B, S, D)
```

```python
import functools
import math

import jax
import jax.numpy as jnp
from jax import lax
from jax.experimental import pallas as pl
from jax.experimental.pallas import tpu as pltpu

F32 = jnp.float32
BF16 = jnp.bfloat16

DEPTH = 4
N_DIFF_HEADS = 4
DIFF_HEAD_DIM = 64
DIFF_WIDTH = N_DIFF_HEADS * 2 * DIFF_HEAD_DIM
NUM_BUCKETS = 32
MAX_DISTANCE = 128
SSM_HEADS = 8
SSM_HEAD_DIM = 64
SSM_WIDTH = SSM_HEADS * SSM_HEAD_DIM
SSM_GROUPS = 2
SSM_STATE = 128
SSM_CONV = 4
SSM_CHUNK = 128
HEADS_PER_GROUP = SSM_HEADS // SSM_GROUPS
SSM_XBC = SSM_WIDTH + 2 * SSM_GROUPS * SSM_STATE
SC_CONV = 3

LANES = 128
SUBLANES = 8
V7X_SCOPED_VMEM_CAP = 60000 * 1024

FFN_TM = 512
FFN_FCHUNK = 256
PROJ_TM = 512
ATT_T = 512
ATT_STRIP = 256
ATT_ONES = 16
ATT_FAR_TILE = 2
SSD_LB = 512
HALO = SUBLANES

NEG_BIG = -1e30
LOG2E = math.log2(math.e)


def _vmem_limit(nbytes):
    return int(min(V7X_SCOPED_VMEM_CAP, nbytes * 5 // 4 + (8 << 20)))


def _dot(a, b):
    return jnp.dot(a, b, preferred_element_type=F32)


def _dot_nt(a, b):
    return lax.dot_general(a, b, (((1,), (1,)), ((), ())), preferred_element_type=F32)


def _rms(x, w, eps):
    ms = jnp.mean(x * x, axis=-1, keepdims=True)
    return x * lax.rsqrt(ms + eps) * w


def _silu(x):
    return x * jax.nn.sigmoid(x)


def _resident(shape):
    nd = len(shape)
    return pl.BlockSpec(shape, lambda *_: (0,) * nd, pipeline_mode=pl.Buffered(1))


def _ffn_kernel(*refs, pre_mix, final_norm, f_total, layer):
    refs = list(refs)
    x_ref = refs.pop(0)
    if pre_mix:
        o_ref, y_ref, wo_ref = refs.pop(0), refs.pop(0), refs.pop(0)
    nw_ref, wg_hbm, wu_hbm, wd_hbm = refs.pop(0), refs.pop(0), refs.pop(0), refs.pop(0)
    if final_norm:
        fw_ref = refs.pop(0)
    out_ref = refs.pop(0)
    wg_ref, wu_ref, wd_ref, stage_g, stage_u, stage_d, sems, h_ref, acc_ref = refs

    chunks = list(range(0, f_total, FFN_FCHUNK))

    def copies(ci):
        cols = pl.ds(chunks[ci], FFN_FCHUNK)
        slot = ci % 2
        return (pltpu.make_async_copy(wg_hbm.at[layer, :, cols], stage_g.at[slot], sems.at[0, slot]),
                pltpu.make_async_copy(wu_hbm.at[layer, :, cols], stage_u.at[slot], sems.at[1, slot]),
                pltpu.make_async_copy(wd_hbm.at[layer, cols, :], stage_d.at[slot], sems.at[2, slot]))

    def body(load_weights):
        if load_weights:
            for ci in range(min(2, len(chunks))):
                for cp in copies(ci):
                    cp.start()
        x = x_ref[...]
        if pre_mix:
            half = o_ref.shape[1]
            x = (x + _dot(o_ref[...], wo_ref[0:half, :].astype(BF16))
                 + _dot(y_ref[...], wo_ref[half:, :].astype(BF16)))
        out_ref[...] = x
        h_ref[...] = _rms(x, nw_ref[...], 1e-6).astype(BF16)
        for ci, c0 in enumerate(chunks):
            c1 = c0 + FFN_FCHUNK
            if load_weights:
                slot = ci % 2
                for cp in copies(ci):
                    cp.wait()
                wg_ref[:, c0:c1] = stage_g[slot].astype(BF16)
                wu_ref[:, c0:c1] = stage_u[slot].astype(BF16)
                wd_ref[c0:c1, :] = stage_d[slot].astype(BF16)
                if ci + 2 < len(chunks):
                    for cp in copies(ci + 2):
                        cp.start()
            h = h_ref[...]
            g = _dot(h, wg_ref[:, c0:c1])
            u = _dot(h, wu_ref[:, c0:c1])
            a = (_silu(g) * u).astype(BF16)
            d = _dot(a, wd_ref[c0:c1, :])
            acc_ref[...] = d if ci == 0 else acc_ref[...] + d
        y = out_ref[...] + 0.5 * acc_ref[...]
        if final_norm:
            y = _rms(y, fw_ref[...], 1e-6)
        out_ref[...] = y

    first = pl.program_id(0) == 0
    pl.when(first)(functools.partial(body, True))
    pl.when(jnp.logical_not(first))(functools.partial(body, False))


def _layer_resident(stacked, layer):
    _, r, c = stacked.shape
    return pl.BlockSpec((None, r, c), lambda *_: (layer, 0, 0), pipeline_mode=pl.Buffered(1))


def _ffn_call(x, nw, wg, wu, wd, layer, *, mix=None, final_w=None, tm=FFN_TM):
    T, D = x.shape
    F = wg.shape[2]
    assert T % tm == 0 and F % FFN_FCHUNK == 0
    row = lambda i: (i, 0)
    args = [x]
    in_specs = [pl.BlockSpec((tm, D), row)]
    nbytes = (4 * tm * D * 4 + 3 * D * F * 2 + 6 * D * FFN_FCHUNK * 4 + 6 * tm * FFN_FCHUNK * 4
              + 2 * tm * D * 4)
    if mix is not None:
        o, y, wo, mix_layer = mix
        args += [o, y, wo]
        in_specs += [pl.BlockSpec((tm, o.shape[1]), row), pl.BlockSpec((tm, y.shape[1]), row),
                     _layer_resident(wo, mix_layer)]
        nbytes += wo.shape[1] * wo.shape[2] * 4 + 4 * tm * o.shape[1] * 2 * 2
    args += [nw.reshape(1, D), wg, wu, wd]
    hbm = pl.BlockSpec(memory_space=pl.ANY)
    in_specs += [_resident((1, D)), hbm, hbm, hbm]
    if final_w is not None:
        args.append(final_w.reshape(1, D))
        in_specs.append(_resident((1, D)))
    kern = functools.partial(_ffn_kernel, pre_mix=mix is not None, final_norm=final_w is not None,
                             f_total=F, layer=layer)
    return pl.pallas_call(
        kern,
        grid=(T // tm,),
        in_specs=in_specs,
        out_specs=pl.BlockSpec((tm, D), row),
        out_shape=jax.ShapeDtypeStruct((T, D), F32),
        scratch_shapes=[pltpu.VMEM((D, F), BF16), pltpu.VMEM((D, F), BF16), pltpu.VMEM((F, D), BF16),
                        pltpu.VMEM((2, D, FFN_FCHUNK), F32), pltpu.VMEM((2, D, FFN_FCHUNK), F32),
                        pltpu.VMEM((2, FFN_FCHUNK, D), F32), pltpu.SemaphoreType.DMA((3, 2)),
                        pltpu.VMEM((tm, D), BF16), pltpu.VMEM((tm, D), F32)],
        compiler_params=pltpu.CompilerParams(
            dimension_semantics=("arbitrary",), vmem_limit_bytes=_vmem_limit(nbytes)),
        name="ffn",
    )(*args)


def _hyb_in_kernel(x_ref, nw_ref, w_ref, wvt_ref, cw_ref, cb_ref, dtb_ref,
                   q_ref, k_ref, vt_ref, z_ref, xc_ref, dt_ref, xpad_ref, *, tm, tiles_per_seq):
    @pl.when(pl.program_id(0) % tiles_per_seq == 0)
    def _():
        xpad_ref[0:HALO, :] = jnp.zeros((HALO, SSM_XBC), F32)

    h = _rms(x_ref[...], nw_ref[...], 1e-6).astype(BF16)
    qw = DIFF_WIDTH
    c_z = 2 * qw + DIFF_WIDTH
    c_xbc = c_z + SSM_WIDTH
    c_dt = c_xbc + SSM_XBC
    xpad_ref[HALO:HALO + tm, :] = _dot(h, w_ref[:, c_xbc:c_xbc + SSM_XBC])
    scale = DIFF_HEAD_DIM ** -0.5 * LOG2E
    q_ref[...] = (_dot(h, w_ref[:, 0:qw]) * scale).astype(BF16)
    k_ref[...] = _dot(h, w_ref[:, qw:2 * qw]).astype(BF16)
    vt_ref[...] = _dot_nt(wvt_ref[...], h).astype(BF16)
    z_ref[...] = _silu(_dot(h, w_ref[:, c_z:c_z + SSM_WIDTH]))
    dt_raw = _dot(h, w_ref[:, c_dt:c_dt + LANES]) + dtb_ref[...]
    dt_ref[...] = jnp.maximum(dt_raw, 0.0) + jnp.log1p(jnp.exp(-jnp.abs(dt_raw)))

    conv = cb_ref[...]
    for kk in range(SSM_CONV):
        o = HALO - (SSM_CONV - 1) + kk
        conv = conv + cw_ref[kk:kk + 1, :] * xpad_ref[o:o + tm, :]
    xc_ref[...] = _silu(conv)
    xpad_ref[0:HALO, :] = xpad_ref[tm:tm + HALO, :]


def _hyb_in_call(x, nw, w_pad, wvt, conv_w, conv_b, dt_bias, *, batch, tm=PROJ_TM):
    T, D = x.shape
    S = T // batch
    assert S % tm == 0
    row = lambda i: (i, 0)
    outs = [
        jax.ShapeDtypeStruct((T, DIFF_WIDTH), BF16),
        jax.ShapeDtypeStruct((T, DIFF_WIDTH), BF16),
        jax.ShapeDtypeStruct((DIFF_WIDTH, T), BF16),
        jax.ShapeDtypeStruct((T, SSM_WIDTH), F32),
        jax.ShapeDtypeStruct((T, SSM_XBC), F32),
        jax.ShapeDtypeStruct((T, LANES), F32),
    ]
    out_specs = [
        pl.BlockSpec((tm, DIFF_WIDTH), row), pl.BlockSpec((tm, DIFF_WIDTH), row),
        pl.BlockSpec((DIFF_WIDTH, tm), lambda i: (0, i)),
        pl.BlockSpec((tm, SSM_WIDTH), row), pl.BlockSpec((tm, SSM_XBC), row),
        pl.BlockSpec((tm, LANES), row),
    ]
    nbytes = ((w_pad.size + wvt.size) * 2 + 2 * tm * D * 4
              + 3 * tm * 4 * (3 * DIFF_WIDTH + SSM_WIDTH + SSM_XBC + LANES) + 3 * tm * SSM_XBC * 4)
    pad = jnp.pad(dt_bias.astype(F32), (0, LANES - dt_bias.shape[0])).reshape(1, LANES)
    return pl.pallas_call(
        functools.partial(_hyb_in_kernel, tm=tm, tiles_per_seq=S // tm),
        grid=(T // tm,),
        in_specs=[pl.BlockSpec((tm, D), row), _resident((1, D)), _resident(w_pad.shape),
                  _resident(wvt.shape), _resident((SSM_CONV, SSM_XBC)), _resident((1, SSM_XBC)),
                  _resident((1, LANES))],
        out_specs=out_specs,
        out_shape=outs,
        scratch_shapes=[pltpu.VMEM((tm + HALO, SSM_XBC), F32)],
        compiler_params=pltpu.CompilerParams(
            dimension_semantics=("arbitrary",), vmem_limit_bytes=_vmem_limit(nbytes)),
        name="hyb_in",
    )(x, nw.reshape(1, D), w_pad, wvt, conv_w.astype(F32), conv_b.reshape(1, SSM_XBC).astype(F32), pad)


def _prep_kernel(tab_ref, lq1_ref, lk1_ref, lq2_ref, lk2_ref, tile_ref, lam_ref, *, t, lam_inits):
    hd = pl.program_id(0)
    key = lax.broadcasted_iota(jnp.int32, (t, t), 0)
    qry = lax.broadcasted_iota(jnp.int32, (t, t), 1)
    max_exact = NUM_BUCKETS // 2
    far = tab_ref[hd, NUM_BUCKETS - 1]
    tile_ref[ATT_FAR_TILE] = jnp.zeros((t, t), F32)
    for off in range(ATT_FAR_TILE):
        dist = off * t + qry - key
        d = jnp.maximum(dist, 0)
        large = max_exact + (
            jnp.log(jnp.maximum(d, 1).astype(F32) / max_exact)
            / math.log(MAX_DISTANCE / max_exact) * (NUM_BUCKETS - max_exact)).astype(jnp.int32)
        bucket = jnp.where(d < max_exact, d, jnp.minimum(large, NUM_BUCKETS - 1))
        bias = jnp.full((t, t), far, F32)
        for b in range(NUM_BUCKETS - 1):
            bias = jnp.where(bucket == b, tab_ref[hd, b], bias)
        val = (bias - far) * LOG2E
        if off == 0:
            val = jnp.where(dist >= 0, val, NEG_BIG)
        tile_ref[off] = val
    for j, lam_init in enumerate(lam_inits):
        s1 = jnp.sum(lq1_ref[j:j + 1, :] * lk1_ref[j:j + 1, :], axis=-1, keepdims=True)
        s2 = jnp.sum(lq2_ref[j:j + 1, :] * lk2_ref[j:j + 1, :], axis=-1, keepdims=True)
        lam = jnp.exp(s1) - jnp.exp(s2) + lam_init
        lam_ref[j * SUBLANES:(j + 1) * SUBLANES, :] = jnp.broadcast_to(lam, (SUBLANES, LANES))


def _prep_call(rel_bias, lq1, lk1, lq2, lk2, lam_inits, *, t=ATT_T):
    ne = lq1.shape[0]
    H = rel_bias.shape[1]
    whole = lambda shape: pl.BlockSpec(shape, lambda h: (0,) * len(shape))
    return pl.pallas_call(
        functools.partial(_prep_kernel, t=t, lam_inits=tuple(lam_inits)),
        grid=(H,),
        in_specs=[pl.BlockSpec(memory_space=pltpu.SMEM)] + [whole(lq1.shape)] * 4,
        out_specs=[pl.BlockSpec((None, ATT_FAR_TILE + 1, t, t), lambda h: (h, 0, 0, 0)),
                   whole((ne * SUBLANES, LANES))],
        out_shape=[jax.ShapeDtypeStruct((H, ATT_FAR_TILE + 1, t, t), F32),
                   jax.ShapeDtypeStruct((ne * SUBLANES, LANES), F32)],
        compiler_params=pltpu.CompilerParams(
            dimension_semantics=("arbitrary",), vmem_limit_bytes=_vmem_limit(12 * t * t * 4)),
        name="attn_prep",
    )(rel_bias.T.astype(F32), lq1, lk1, lq2, lk2)


def _attn_kernel(q_ref, k_ref, vt_ref, tile_ref, lam_ref, sw_ref, o_ref,
                 qm_ref, m_ref, acc_ref, s_ref, smax_ref, *, t, nq, unroll, lam_init):
    e = 2 * DIFF_HEAD_DIM
    first = lax.broadcasted_iota(jnp.int32, (t, LANES), 1) < DIFF_HEAD_DIM
    ones = jnp.ones((ATT_ONES, t), BF16)
    chains = [(w, slice(c0, c0 + ATT_STRIP)) for c0 in range(0, t, ATT_STRIP) for w in range(2)]

    def rows(blk):
        return pl.ds(pl.multiple_of(blk * t, t), t)

    def prepare(qi, carry):
        q = q_ref[rows(qi), :]
        zero = jnp.zeros_like(q)
        qm_ref[0, rows(qi), :] = jnp.where(first, q, zero)
        qm_ref[1, rows(qi), :] = jnp.where(first, zero, q)
        m_ref[qi] = jnp.full(m_ref.shape[1:], NEG_BIG, F32)
        acc_ref[qi] = jnp.zeros(acc_ref.shape[1:], F32)
        return carry

    def score_chain(ci, qi, j, slot, k):
        w, cols = chains[ci]
        near = jnp.minimum(qi - j, ATT_FAR_TILE)
        q_rows = pl.ds(pl.multiple_of(qi * t + cols.start, ATT_STRIP), ATT_STRIP)
        s = _dot_nt(k, qm_ref[w, q_rows, :]) + tile_ref[near, :, cols]
        s_ref[slot, ci] = s
        smax_ref[slot, ci] = jnp.max(s, axis=0, keepdims=True)

    def value_chain(ci, qi, slot, vta):
        w, cols = chains[ci]
        m_old = m_ref[qi, w, :, cols]
        m_new = jnp.maximum(m_old, smax_ref[slot, ci])
        p = jnp.exp2(s_ref[slot, ci] - m_new).astype(BF16)
        acc_ref[qi, w, :, cols] = jnp.exp2(m_old - m_new) * acc_ref[qi, w, :, cols] + _dot(vta, p)
        m_ref[qi, w, :, cols] = m_new

    def finalize(qi, carry):
        lam = lam_ref[0:1, 0:1]
        o1 = acc_ref[qi, 0, 0:e, :] * (1.0 / acc_ref[qi, 0, e:e + 1, :])
        o2 = acc_ref[qi, 1, 0:e, :] * (1.0 / acc_ref[qi, 1, e:e + 1, :])
        out_t = o1 - lam * o2
        ms = jnp.mean(out_t * out_t, axis=0, keepdims=True)
        out_t = out_t * lax.rsqrt(ms + 1e-5) * sw_ref[...] * (1.0 - lam_init)
        o_ref[rows(qi), :] = out_t.T.astype(BF16)
        return carry

    def step(qv, jv, slot):
        wrap = jv == qv
        qs = jnp.where(wrap, qv + 1, qv)
        js = jnp.where(wrap, 0, jv + 1)
        past_end = qs >= nq
        qs = jnp.where(past_end, nq - 1, qs)
        js = jnp.where(past_end, nq - 1, js)
        k_next = k_ref[rows(js), :]
        vta = jnp.concatenate([vt_ref[:, rows(jv)], ones], axis=0)
        for ci in range(len(chains)):
            score_chain(ci, qs, js, 1 - slot, k_next)
            value_chain(ci, qv, slot, vta)
        return qs, js

    lax.fori_loop(0, nq, prepare, 0)
    k0 = k_ref[rows(0), :]
    for ci in range(len(chains)):
        score_chain(ci, 0, 0, 0, k0)

    def body(_, carry):
        qv, jv = carry
        for u in range(unroll):
            qv, jv = step(qv, jv, u % 2)
        return qv, jv

    n_pairs = nq * (nq + 1) // 2
    lax.fori_loop(0, n_pairs // unroll, body, (jnp.int32(0), jnp.int32(0)))
    lax.fori_loop(0, nq, finalize, 0)


def _attn_call(q, k, vt, tiles, lam, subln_w, *, batch, lam_init, t=ATT_T):
    T, W = q.shape
    S = T // batch
    H = W // LANES
    nq = S // t
    e = 2 * DIFF_HEAD_DIM
    assert e == LANES and S % t == 0
    n_pairs = nq * (nq + 1) // 2
    unroll = max(u for u in (8, 4, 2) if n_pairs % u == 0)
    n_chains = 2 * (t // ATT_STRIP)
    n_tiles = ATT_FAR_TILE + 1
    nbytes = (2 * 4 * (S * LANES * 2) + 2 * n_tiles * t * t * 4 + 2 * S * LANES * 2
              + nq * 2 * (e + ATT_ONES + SUBLANES) * t * 4 + 2 * n_chains * t * ATT_STRIP * 4 + 2 * t * t * 4)
    return pl.pallas_call(
        functools.partial(_attn_kernel, t=t, nq=nq, unroll=unroll, lam_init=lam_init),
        grid=(batch, H),
        in_specs=[
            pl.BlockSpec((S, LANES), lambda b, h: (b, h)),
            pl.BlockSpec((S, LANES), lambda b, h: (b, h)),
            pl.BlockSpec((LANES, S), lambda b, h: (h, b)),
            pl.BlockSpec((None, n_tiles, t, t), lambda b, h: (h, 0, 0, 0)),
            pl.BlockSpec((SUBLANES, LANES), lambda b, h: (0, 0)),
            pl.BlockSpec((e, 1), lambda b, h: (0, 0)),
        ],
        out_specs=pl.BlockSpec((S, LANES), lambda b, h: (b, h)),
        out_shape=jax.ShapeDtypeStruct((T, W), BF16),
        scratch_shapes=[pltpu.VMEM((2, S, LANES), BF16),
                        pltpu.VMEM((nq, 2, 1, t), F32), pltpu.VMEM((nq, 2, e + ATT_ONES, t), F32),
                        pltpu.VMEM((2, n_chains, t, ATT_STRIP), F32),
                        pltpu.VMEM((2, n_chains, 1, ATT_STRIP), F32)],
        compiler_params=pltpu.CompilerParams(
            dimension_semantics=("arbitrary", "arbitrary"), vmem_limit_bytes=_vmem_limit(nbytes)),
        name="diff_attn",
    )(q, k, vt, tiles, lam, subln_w.reshape(e, 1))


def _split3(a):
    hi = a.astype(BF16)
    r1 = a - hi.astype(F32)
    mid = r1.astype(BF16)
    lo = (r1 - mid.astype(F32)).astype(BF16)
    return hi, mid, lo


def _ssd_kernel(xc_ref, z_ref, dt_ref, alog_ref, dsk_ref, nw_ref, y_ref, st_ref, *, lb):
    L = SSM_CHUNK
    gw = HEADS_PER_GROUP * SSM_HEAD_DIM

    @pl.when(pl.program_id(1) == 0)
    def _():
        st_ref[...] = jnp.zeros(st_ref.shape, F32)

    xc = xc_ref[...]
    dt = dt_ref[...]
    a_dt = -jnp.exp(alog_ref[...]) * dt

    row = lax.broadcasted_iota(jnp.int32, (L, L), 0)
    col = lax.broadcasted_iota(jnp.int32, (L, L), 1)
    causal = row >= col
    tri = jnp.where(causal, 1.0, 0.0).astype(BF16)
    lane_g = lax.broadcasted_iota(jnp.int32, (L, gw), 1) // SSM_HEAD_DIM
    lane_lo = lax.broadcasted_iota(jnp.int32, (L, LANES), 1) < SSM_HEAD_DIM
    lane_lo1 = lax.broadcasted_iota(jnp.int32, (1, LANES), 1) < SSM_HEAD_DIM

    for c in range(lb // L):
        r0 = c * L
        hi, mid, lo = _split3(a_dt[r0:r0 + L, :])
        cs = _dot(tri, hi) + _dot(tri, mid) + _dot(tri, lo)
        cs_t = cs.T
        dt_t = dt[r0:r0 + L, :].T
        xs = xc[r0:r0 + L, 0:SSM_WIDTH]
        ys = []
        for g in range(SSM_GROUPS):
            b0 = SSM_WIDTH + g * SSM_STATE
            c0 = SSM_WIDTH + SSM_GROUPS * SSM_STATE + g * SSM_STATE
            bg = xc[r0:r0 + L, b0:b0 + SSM_STATE]
            cg = xc[r0:r0 + L, c0:c0 + SSM_STATE].astype(BF16)
            gmat = _dot_nt(cg, bg.astype(BF16))
            bg_t = bg.T
            xg = xs[:, g * gw:(g + 1) * gw]
            ydiag = None
            st_new = None
            ecols, decs = [], []
            for hh in range(HEADS_PER_GROUP):
                h = g * HEADS_PER_GROUP + hh
                colb = jnp.broadcast_to(cs[:, h:h + 1], (L, L))
                rowb = cs_t[h:h + 1, :]
                dtrow = dt_t[h:h + 1, :]
                decay = jnp.exp(jnp.where(causal, colb - rowb, NEG_BIG))
                mh = (gmat * decay * dtrow).astype(BF16)
                xm = jnp.where(lane_g == hh, xg, 0.0).astype(BF16)
                last = colb[L - 1:L, :]
                wrow = dtrow * jnp.exp(last - rowb)
                btw = (bg_t * wrow).astype(BF16)
                d1 = _dot(mh, xm)
                d2 = _dot(btw, xm)
                ydiag = d1 if ydiag is None else ydiag + d1
                st_new = d2 if st_new is None else st_new + d2
                ecols.append(jnp.exp(colb))
                decs.append(jnp.exp(last))
            e_mat = jnp.concatenate([jnp.where(lane_lo, ecols[0], ecols[1]),
                                     jnp.where(lane_lo, ecols[2], ecols[3])], axis=1)
            dec = jnp.concatenate([jnp.where(lane_lo1, decs[0], decs[1]),
                                   jnp.where(lane_lo1, decs[2], decs[3])], axis=1)
            st_old = st_ref[g]
            ys.append(ydiag + _dot(cg, st_old.astype(BF16)) * e_mat)
            st_ref[g] = st_old * dec + st_new
        y = jnp.concatenate(ys, axis=1) + dsk_ref[...] * xs
        gated = y * z_ref[r0:r0 + L, :]
        outs = []
        for g in range(SSM_GROUPS):
            part = gated[:, g * gw:(g + 1) * gw]
            ms = jnp.mean(part * part, axis=-1, keepdims=True)
            outs.append(part * lax.rsqrt(ms + 1e-5))
        y_ref[r0:r0 + L, :] = (jnp.concatenate(outs, axis=1) * nw_ref[...]).astype(BF16)


def _ssd_call(xc, z, dt, a_log, d_skip, norm_w, *, batch, lb=SSD_LB):
    T = xc.shape[0]
    S = T // batch
    nb = S // lb
    assert S % lb == 0 and lb % SSM_CHUNK == 0
    a_pad = jnp.pad(a_log.astype(F32), (0, LANES - a_log.shape[0])).reshape(1, LANES)
    dsk = jnp.repeat(d_skip.astype(F32), SSM_HEAD_DIM).reshape(1, SSM_WIDTH)
    row = lambda b, j: (b * nb + j, 0)
    const = lambda shape: pl.BlockSpec(shape, lambda b, j: (0, 0))
    nbytes = 3 * lb * (SSM_XBC + SSM_WIDTH + LANES) * 4 + 8 * lb * SSM_XBC * 4
    return pl.pallas_call(
        functools.partial(_ssd_kernel, lb=lb),
        grid=(batch, nb),
        in_specs=[pl.BlockSpec((lb, SSM_XBC), row), pl.BlockSpec((lb, SSM_WIDTH), row),
                  pl.BlockSpec((lb, LANES), row),
                  const((1, LANES)), const((1, SSM_WIDTH)), const((1, SSM_WIDTH))],
        out_specs=pl.BlockSpec((lb, SSM_WIDTH), row),
        out_shape=jax.ShapeDtypeStruct((T, SSM_WIDTH), BF16),
        scratch_shapes=[pltpu.VMEM((SSM_GROUPS, SSM_STATE, HEADS_PER_GROUP * SSM_HEAD_DIM), F32)],
        compiler_params=pltpu.CompilerParams(
            dimension_semantics=("arbitrary", "arbitrary"), vmem_limit_bytes=_vmem_limit(nbytes)),
        name="ssd",
    )(xc, z, dt, a_pad, dsk, norm_w.reshape(1, SSM_WIDTH).astype(F32))


def _sc_kernel(x_ref, nw_ref, win_ref, cw_ref, wout_ref, o_ref, vpad_ref, *, tm, tiles_per_seq):
    d = x_ref.shape[1]

    @pl.when(pl.program_id(0) % tiles_per_seq == 0)
    def _():
        vpad_ref[0:HALO, :] = jnp.zeros((HALO, d), F32)

    x = x_ref[...]
    h = _rms(x, nw_ref[...], 1e-6).astype(BF16)
    bgate = _dot(h, win_ref[:, 0:d])
    v = _dot(h, win_ref[:, d:2 * d]) * _dot(h, win_ref[:, 2 * d:3 * d])
    vpad_ref[HALO:HALO + tm, :] = v
    conv = cw_ref[SC_CONV - 1:SC_CONV, :] * v
    for kk in range(SC_CONV - 1):
        o = HALO - (SC_CONV - 1) + kk
        conv = conv + cw_ref[kk:kk + 1, :] * vpad_ref[o:o + tm, :]
    vpad_ref[0:HALO, :] = vpad_ref[tm:tm + HALO, :]
    o_ref[...] = x + _dot((bgate * conv).astype(BF16), wout_ref[...])


def _sc_call(x, nw, w_in, conv_w, w_out, *, batch, tm=PROJ_TM):
    T, D = x.shape
    S = T // batch
    assert S % tm == 0
    row = lambda i: (i, 0)
    nbytes = (w_in.size + w_out.size) * 2 + 4 * tm * D * 4 + 8 * tm * D * 4
    return pl.pallas_call(
        functools.partial(_sc_kernel, tm=tm, tiles_per_seq=S // tm),
        grid=(T // tm,),
        in_specs=[pl.BlockSpec((tm, D), row), _resident((1, D)), _resident(w_in.shape),
                  _resident(conv_w.shape), _resident(w_out.shape)],
        out_specs=pl.BlockSpec((tm, D), row),
        out_shape=jax.ShapeDtypeStruct((T, D), F32),
        scratch_shapes=[pltpu.VMEM((tm + HALO, D), F32)],
        compiler_params=pltpu.CompilerParams(
            dimension_semantics=("arbitrary",), vmem_limit_bytes=_vmem_limit(nbytes)),
        name="short_conv",
    )(x, nw.reshape(1, D), w_in, conv_w.astype(F32), w_out)


def kernel(x, rel_bias, final_norm_w, ffn1_norm, ffn1_wg, ffn1_wu, ffn1_wd, mix_norm, ffn2_norm, ffn2_wg, ffn2_wu, ffn2_wd, hyb_w_in, hyb_w_out, diff_lq1, diff_lk1, diff_lq2, diff_lk2, diff_subln_w, ssm_conv_w, ssm_conv_b, ssm_dt_bias, ssm_a_log, ssm_d, ssm_norm_w, sc_w_in, sc_conv_w, sc_w_out):
    B, S, D = x.shape
    T = B * S
    xt = x.reshape(T, D)
    bf = lambda w: w.astype(BF16)

    lam_inits = [0.8 - 0.6 * math.exp(-0.3 * i) for i in range(0, DEPTH, 2)]
    tiles, lams = _prep_call(rel_bias, diff_lq1, diff_lk1, diff_lq2, diff_lk2, lam_inits)

    mix = None
    for i in range(DEPTH):
        j = i // 2
        xt = _ffn_call(xt, ffn1_norm[i], ffn1_wg, ffn1_wu, ffn1_wd, i)
        if i % 2 == 0:
            w_in = hyb_w_in[j]
            qkv_end = 3 * DIFF_WIDTH
            w_pad = bf(jnp.pad(w_in, ((0, 0), (0, LANES - SSM_HEADS))))
            wvt = bf(w_in[:, 2 * DIFF_WIDTH:qkv_end].T)
            q, k, vt, z, xc, dt = _hyb_in_call(xt, mix_norm[i], w_pad, wvt, ssm_conv_w[j], ssm_conv_b[j],
                                               ssm_dt_bias[j], batch=B)
            o = _attn_call(q, k, vt, tiles, lams[j * SUBLANES:(j + 1) * SUBLANES], diff_subln_w[j],
                           batch=B, lam_init=lam_inits[j])
            y = _ssd_call(xc, z, dt, ssm_a_log[j], ssm_d[j], ssm_norm_w[j], batch=B)
            mix = (o, y, hyb_w_out, j)
        else:
            xt = _sc_call(xt, mix_norm[i], bf(sc_w_in[j]), sc_conv_w[j], bf(sc_w_out[j]), batch=B)
            mix = None
        xt = _ffn_call(xt, ffn2_norm[i], ffn2_wg, ffn2_wu, ffn2_wd, i, mix=mix,
                       final_w=final_norm_w if i == DEPTH - 1 else None)
    return xt.reshape(B, S, D)
```

```python
import functools
import math

import jax
import jax.numpy as jnp
from jax import lax
from jax.experimental import pallas as pl
from jax.experimental.pallas import tpu as pltpu

F32 = jnp.float32
BF16 = jnp.bfloat16

DEPTH = 4
N_DIFF_HEADS = 4
DIFF_HEAD_DIM = 64
DIFF_WIDTH = N_DIFF_HEADS * 2 * DIFF_HEAD_DIM
NUM_BUCKETS = 32
MAX_DISTANCE = 128
SSM_HEADS = 8
SSM_HEAD_DIM = 64
SSM_WIDTH = SSM_HEADS * SSM_HEAD_DIM
SSM_GROUPS = 2
SSM_STATE = 128
SSM_CONV = 4
SSM_CHUNK = 128
HEADS_PER_GROUP = SSM_HEADS // SSM_GROUPS
SSM_XBC = SSM_WIDTH + 2 * SSM_GROUPS * SSM_STATE
SC_CONV = 3

LANES = 128
SUBLANES = 8
V7X_SCOPED_VMEM_CAP = 60000 * 1024

FFN_TM = 512
FFN_FCHUNK = 256
PROJ_TM = 512
ATT_T = 512
ATT_STRIP = 256
ATT_ONES = 16
ATT_FAR_TILE = 2
SSD_LB = 512
HALO = SUBLANES

NEG_BIG = -1e30
LOG2E = math.log2(math.e)


def _vmem_limit(nbytes):
    return int(min(V7X_SCOPED_VMEM_CAP, nbytes * 5 // 4 + (8 << 20)))


def _dot(a, b):
    return jnp.dot(a, b, preferred_element_type=F32)


def _dot_nt(a, b):
    return lax.dot_general(a, b, (((1,), (1,)), ((), ())), preferred_element_type=F32)


def _rms(x, w, eps):
    ms = jnp.mean(x * x, axis=-1, keepdims=True)
    return x * lax.rsqrt(ms + eps) * w


def _silu(x):
    return x * jax.nn.sigmoid(x)


def _resident(shape):
    nd = len(shape)
    return pl.BlockSpec(shape, lambda *_: (0,) * nd, pipeline_mode=pl.Buffered(1))


def _ffn_kernel(*refs, pre_mix, final_norm, f_total):
    refs = list(refs)
    x_ref = refs.pop(0)
    if pre_mix:
        o_ref, y_ref, wo_ref = refs.pop(0), refs.pop(0), refs.pop(0)
    nw_ref, wg_ref, wu_ref, wd_ref = refs.pop(0), refs.pop(0), refs.pop(0), refs.pop(0)
    if final_norm:
        fw_ref = refs.pop(0)
    out_ref = refs.pop(0)

    x = x_ref[...]
    if pre_mix:
        half = o_ref.shape[1]
        x = (x + _dot(o_ref[...], wo_ref[0:half, :].astype(BF16))
             + _dot(y_ref[...], wo_ref[half:, :].astype(BF16)))
    h = _rms(x, nw_ref[...], 1e-6).astype(BF16)
    acc = None
    for c0 in range(0, f_total, FFN_FCHUNK):
        c1 = c0 + FFN_FCHUNK
        g = _dot(h, wg_ref[:, c0:c1].astype(BF16))
        u = _dot(h, wu_ref[:, c0:c1].astype(BF16))
        a = (_silu(g) * u).astype(BF16)
        d = _dot(a, wd_ref[c0:c1, :].astype(BF16))
        acc = d if acc is None else acc + d
    y = x + 0.5 * acc
    if final_norm:
        y = _rms(y, fw_ref[...], 1e-6)
    out_ref[...] = y


def _layer_resident(stacked, layer):
    _, r, c = stacked.shape
    return pl.BlockSpec((None, r, c), lambda *_: (layer, 0, 0), pipeline_mode=pl.Buffered(1))


def _ffn_call(x, nw, wg, wu, wd, layer, *, mix=None, final_w=None, tm=FFN_TM):
    T, D = x.shape
    F = wg.shape[2]
    assert T % tm == 0 and F % FFN_FCHUNK == 0
    row = lambda i: (i, 0)
    args = [x]
    in_specs = [pl.BlockSpec((tm, D), row)]
    nbytes = 4 * tm * D * 4 + 3 * D * F * 4 + 6 * tm * FFN_FCHUNK * 4 + 3 * tm * D * 4
    if mix is not None:
        o, y, wo, mix_layer = mix
        args += [o, y, wo]
        in_specs += [pl.BlockSpec((tm, o.shape[1]), row), pl.BlockSpec((tm, y.shape[1]), row),
                     _layer_resident(wo, mix_layer)]
        nbytes += wo.shape[1] * wo.shape[2] * 4 + 4 * tm * o.shape[1] * 2 * 2
    args += [nw.reshape(1, D), wg, wu, wd]
    in_specs += [_resident((1, D)), _layer_resident(wg, layer), _layer_resident(wu, layer),
                 _layer_resident(wd, layer)]
    if final_w is not None:
        args.append(final_w.reshape(1, D))
        in_specs.append(_resident((1, D)))
    kern = functools.partial(_ffn_kernel, pre_mix=mix is not None, final_norm=final_w is not None,
                             f_total=F)
    return pl.pallas_call(
        kern,
        grid=(T // tm,),
        in_specs=in_specs,
        out_specs=pl.BlockSpec((tm, D), row),
        out_shape=jax.ShapeDtypeStruct((T, D), F32),
        compiler_params=pltpu.CompilerParams(
            dimension_semantics=("arbitrary",), vmem_limit_bytes=_vmem_limit(nbytes)),
        name="ffn",
    )(*args)


def _hyb_in_kernel(x_ref, nw_ref, w_ref, wvt_ref, cw_ref, cb_ref, dtb_ref,
                   q_ref, k_ref, vt_ref, z_ref, xc_ref, dt_ref, xpad_ref, *, tm, tiles_per_seq):
    @pl.when(pl.program_id(0) % tiles_per_seq == 0)
    def _():
        xpad_ref[0:HALO, :] = jnp.zeros((HALO, SSM_XBC), F32)

    h = _rms(x_ref[...], nw_ref[...], 1e-6).astype(BF16)
    qw = DIFF_WIDTH
    c_z = 2 * qw + DIFF_WIDTH
    c_xbc = c_z + SSM_WIDTH
    c_dt = c_xbc + SSM_XBC
    xpad_ref[HALO:HALO + tm, :] = _dot(h, w_ref[:, c_xbc:c_xbc + SSM_XBC])
    scale = DIFF_HEAD_DIM ** -0.5 * LOG2E
    q_ref[...] = (_dot_nt(wvt_ref[0:qw, :], h) * scale).astype(BF16)
    k_ref[...] = _dot(h, w_ref[:, qw:2 * qw]).astype(BF16)
    vt_ref[...] = _dot_nt(wvt_ref[qw:, :], h).astype(BF16)
    z_ref[...] = _silu(_dot(h, w_ref[:, c_z:c_z + SSM_WIDTH]))
    dt_raw = _dot(h, w_ref[:, c_dt:c_dt + LANES]) + dtb_ref[...]
    dt_ref[...] = jnp.maximum(dt_raw, 0.0) + jnp.log1p(jnp.exp(-jnp.abs(dt_raw)))

    conv = cb_ref[...]
    for kk in range(SSM_CONV):
        o = HALO - (SSM_CONV - 1) + kk
        conv = conv + cw_ref[kk:kk + 1, :] * xpad_ref[o:o + tm, :]
    xc_ref[...] = _silu(conv)
    xpad_ref[0:HALO, :] = xpad_ref[tm:tm + HALO, :]


def _hyb_in_call(x, nw, w_pad, wvt, conv_w, conv_b, dt_bias, *, batch, tm=PROJ_TM):
    T, D = x.shape
    S = T // batch
    assert S % tm == 0
    row = lambda i: (i, 0)
    outs = [
        jax.ShapeDtypeStruct((DIFF_WIDTH, T), BF16),
        jax.ShapeDtypeStruct((T, DIFF_WIDTH), BF16),
        jax.ShapeDtypeStruct((DIFF_WIDTH, T), BF16),
        jax.ShapeDtypeStruct((T, SSM_WIDTH), F32),
        jax.ShapeDtypeStruct((T, SSM_XBC), F32),
        jax.ShapeDtypeStruct((T, LANES), F32),
    ]
    out_specs = [
        pl.BlockSpec((DIFF_WIDTH, tm), lambda i: (0, i)), pl.BlockSpec((tm, DIFF_WIDTH), row),
        pl.BlockSpec((DIFF_WIDTH, tm), lambda i: (0, i)),
        pl.BlockSpec((tm, SSM_WIDTH), row), pl.BlockSpec((tm, SSM_XBC), row),
        pl.BlockSpec((tm, LANES), row),
    ]
    nbytes = ((w_pad.size + wvt.size) * 2 + 2 * tm * D * 4
              + 3 * tm * 4 * (3 * DIFF_WIDTH + SSM_WIDTH + SSM_XBC + LANES) + 3 * tm * SSM_XBC * 4)
    pad = jnp.pad(dt_bias.astype(F32), (0, LANES - dt_bias.shape[0])).reshape(1, LANES)
    return pl.pallas_call(
        functools.partial(_hyb_in_kernel, tm=tm, tiles_per_seq=S // tm),
        grid=(T // tm,),
        in_specs=[pl.BlockSpec((tm, D), row), _resident((1, D)), _resident(w_pad.shape),
                  _resident(wvt.shape), _resident((SSM_CONV, SSM_XBC)), _resident((1, SSM_XBC)),
                  _resident((1, LANES))],
        out_specs=out_specs,
        out_shape=outs,
        scratch_shapes=[pltpu.VMEM((tm + HALO, SSM_XBC), F32)],
        compiler_params=pltpu.CompilerParams(
            dimension_semantics=("arbitrary",), vmem_limit_bytes=_vmem_limit(nbytes)),
        name="hyb_in",
    )(x, nw.reshape(1, D), w_pad, wvt, conv_w.astype(F32), conv_b.reshape(1, SSM_XBC).astype(F32), pad)


def _prep_kernel(tab_ref, lq1_ref, lk1_ref, lq2_ref, lk2_ref, tile_ref, lam_ref, *, t, lam_inits):
    hd = pl.program_id(0)
    key = lax.broadcasted_iota(jnp.int32, (t, t), 0)
    qry = lax.broadcasted_iota(jnp.int32, (t, t), 1)
    max_exact = NUM_BUCKETS // 2
    far = tab_ref[hd, NUM_BUCKETS - 1]
    tile_ref[ATT_FAR_TILE] = jnp.zeros((t, t), F32)
    for off in range(ATT_FAR_TILE):
        dist = off * t + qry - key
        d = jnp.maximum(dist, 0)
        large = max_exact + (
            jnp.log(jnp.maximum(d, 1).astype(F32) / max_exact)
            / math.log(MAX_DISTANCE / max_exact) * (NUM_BUCKETS - max_exact)).astype(jnp.int32)
        bucket = jnp.where(d < max_exact, d, jnp.minimum(large, NUM_BUCKETS - 1))
        bias = jnp.full((t, t), far, F32)
        for b in range(NUM_BUCKETS - 1):
            bias = jnp.where(bucket == b, tab_ref[hd, b], bias)
        val = (bias - far) * LOG2E
        if off == 0:
            val = jnp.where(dist >= 0, val, NEG_BIG)
        tile_ref[off] = val
    for j, lam_init in enumerate(lam_inits):
        s1 = jnp.sum(lq1_ref[j:j + 1, :] * lk1_ref[j:j + 1, :], axis=-1, keepdims=True)
        s2 = jnp.sum(lq2_ref[j:j + 1, :] * lk2_ref[j:j + 1, :], axis=-1, keepdims=True)
        lam = jnp.exp(s1) - jnp.exp(s2) + lam_init
        lam_ref[j * SUBLANES:(j + 1) * SUBLANES, :] = jnp.broadcast_to(lam, (SUBLANES, LANES))


def _prep_call(rel_bias, lq1, lk1, lq2, lk2, lam_inits, *, t=ATT_T):
    ne = lq1.shape[0]
    H = rel_bias.shape[1]
    whole = lambda shape: pl.BlockSpec(shape, lambda h: (0,) * len(shape))
    return pl.pallas_call(
        functools.partial(_prep_kernel, t=t, lam_inits=tuple(lam_inits)),
        grid=(H,),
        in_specs=[pl.BlockSpec(memory_space=pltpu.SMEM)] + [whole(lq1.shape)] * 4,
        out_specs=[pl.BlockSpec((None, ATT_FAR_TILE + 1, t, t), lambda h: (h, 0, 0, 0)),
                   whole((ne * SUBLANES, LANES))],
        out_shape=[jax.ShapeDtypeStruct((H, ATT_FAR_TILE + 1, t, t), F32),
                   jax.ShapeDtypeStruct((ne * SUBLANES, LANES), F32)],
        compiler_params=pltpu.CompilerParams(
            dimension_semantics=("arbitrary",), vmem_limit_bytes=_vmem_limit(12 * t * t * 4)),
        name="attn_prep",
    )(rel_bias.T.astype(F32), lq1, lk1, lq2, lk2)


def _attn_kernel(qt_ref, k_ref, vt_ref, tile_ref, lam_ref, sw_ref, o_ref,
                 qm_ref, m_ref, acc_ref, s_ref, smax_ref, *, t, nq, unroll, lam_init):
    e = 2 * DIFF_HEAD_DIM
    first = lax.broadcasted_iota(jnp.int32, (LANES, t), 0) < DIFF_HEAD_DIM
    ones = jnp.ones((ATT_ONES, t), BF16)
    chains = [(w, slice(c0, c0 + ATT_STRIP)) for c0 in range(0, t, ATT_STRIP) for w in range(2)]

    def rows(blk):
        return pl.ds(pl.multiple_of(blk * t, t), t)

    def prepare(qi, carry):
        q = qt_ref[:, rows(qi)]
        zero = jnp.zeros_like(q)
        qm_ref[0, :, rows(qi)] = jnp.where(first, q, zero)
        qm_ref[1, :, rows(qi)] = jnp.where(first, zero, q)
        m_ref[qi] = jnp.full(m_ref.shape[1:], NEG_BIG, F32)
        acc_ref[qi] = jnp.zeros(acc_ref.shape[1:], F32)
        return carry

    def score_chain(ci, qi, j, slot, k):
        w, cols = chains[ci]
        near = jnp.minimum(qi - j, ATT_FAR_TILE)
        q_cols = pl.ds(pl.multiple_of(qi * t + cols.start, ATT_STRIP), ATT_STRIP)
        s = _dot(k, qm_ref[w, :, q_cols]) + tile_ref[near, :, cols]
        s_ref[slot, ci] = s
        smax_ref[slot, ci] = jnp.max(s, axis=0, keepdims=True)

    def value_chain(ci, qi, slot, vta):
        w, cols = chains[ci]
        m_old = m_ref[qi, w, :, cols]
        m_new = jnp.maximum(m_old, smax_ref[slot, ci])
        p = jnp.exp2(s_ref[slot, ci] - m_new).astype(BF16)
        acc_ref[qi, w, :, cols] = jnp.exp2(m_old - m_new) * acc_ref[qi, w, :, cols] + _dot(vta, p)
        m_ref[qi, w, :, cols] = m_new

    def finalize(qi, carry):
        lam = lam_ref[0:1, 0:1]
        o1 = acc_ref[qi, 0, 0:e, :] * (1.0 / acc_ref[qi, 0, e:e + 1, :])
        o2 = acc_ref[qi, 1, 0:e, :] * (1.0 / acc_ref[qi, 1, e:e + 1, :])
        out_t = o1 - lam * o2
        ms = jnp.mean(out_t * out_t, axis=0, keepdims=True)
        out_t = out_t * lax.rsqrt(ms + 1e-5) * sw_ref[...] * (1.0 - lam_init)
        o_ref[rows(qi), :] = out_t.T.astype(BF16)
        return carry

    def step(qv, jv, slot):
        wrap = jv == qv
        qs = jnp.where(wrap, qv + 1, qv)
        js = jnp.where(wrap, 0, jv + 1)
        past_end = qs >= nq
        qs = jnp.where(past_end, nq - 1, qs)
        js = jnp.where(past_end, nq - 1, js)
        k_next = k_ref[rows(js), :]
        vta = jnp.concatenate([vt_ref[:, rows(jv)], ones], axis=0)
        for ci in range(len(chains)):
            score_chain(ci, qs, js, 1 - slot, k_next)
            value_chain(ci, qv, slot, vta)
        return qs, js

    lax.fori_loop(0, nq, prepare, 0)
    k0 = k_ref[rows(0), :]
    for ci in range(len(chains)):
        score_chain(ci, 0, 0, 0, k0)

    def body(_, carry):
        qv, jv = carry
        for u in range(unroll):
            qv, jv = step(qv, jv, u % 2)
        return qv, jv

    n_pairs = nq * (nq + 1) // 2
    lax.fori_loop(0, n_pairs // unroll, body, (jnp.int32(0), jnp.int32(0)))
    lax.fori_loop(0, nq, finalize, 0)


def _attn_call(qt, k, vt, tiles, lam, subln_w, *, batch, lam_init, t=ATT_T):
    T, W = k.shape
    S = T // batch
    H = W // LANES
    nq = S // t
    e = 2 * DIFF_HEAD_DIM
    assert e == LANES and S % t == 0
    n_pairs = nq * (nq + 1) // 2
    unroll = max(u for u in (8, 4, 2) if n_pairs % u == 0)
    n_chains = 2 * (t // ATT_STRIP)
    n_tiles = ATT_FAR_TILE + 1
    nbytes = (2 * 4 * (S * LANES * 2) + 2 * n_tiles * t * t * 4 + 2 * S * LANES * 2
              + nq * 2 * (e + ATT_ONES + SUBLANES) * t * 4 + 2 * n_chains * t * ATT_STRIP * 4 + 2 * t * t * 4)
    return pl.pallas_call(
        functools.partial(_attn_kernel, t=t, nq=nq, unroll=unroll, lam_init=lam_init),
        grid=(batch, H),
        in_specs=[
            pl.BlockSpec((LANES, S), lambda b, h: (h, b)),
            pl.BlockSpec((S, LANES), lambda b, h: (b, h)),
            pl.BlockSpec((LANES, S), lambda b, h: (h, b)),
            pl.BlockSpec((None, n_tiles, t, t), lambda b, h: (h, 0, 0, 0)),
            pl.BlockSpec((SUBLANES, LANES), lambda b, h: (0, 0)),
            pl.BlockSpec((e, 1), lambda b, h: (0, 0)),
        ],
        out_specs=pl.BlockSpec((S, LANES), lambda b, h: (b, h)),
        out_shape=jax.ShapeDtypeStruct((T, W), BF16),
        scratch_shapes=[pltpu.VMEM((2, LANES, S), BF16),
                        pltpu.VMEM((nq, 2, 1, t), F32), pltpu.VMEM((nq, 2, e + ATT_ONES, t), F32),
                        pltpu.VMEM((2, n_chains, t, ATT_STRIP), F32),
                        pltpu.VMEM((2, n_chains, 1, ATT_STRIP), F32)],
        compiler_params=pltpu.CompilerParams(
            dimension_semantics=("arbitrary", "arbitrary"), vmem_limit_bytes=_vmem_limit(nbytes)),
        name="diff_attn",
    )(qt, k, vt, tiles, lam, subln_w.reshape(e, 1))


def _split3(a):
    hi = a.astype(BF16)
    r1 = a - hi.astype(F32)
    mid = r1.astype(BF16)
    lo = (r1 - mid.astype(F32)).astype(BF16)
    return hi, mid, lo


def _ssd_kernel(xc_ref, z_ref, dt_ref, alog_ref, dsk_ref, nw_ref, y_ref, st_ref, *, lb):
    L = SSM_CHUNK
    gw = HEADS_PER_GROUP * SSM_HEAD_DIM

    @pl.when(pl.program_id(1) == 0)
    def _():
        st_ref[...] = jnp.zeros(st_ref.shape, F32)

    xc = xc_ref[...]
    dt = dt_ref[...]
    a_dt = -jnp.exp(alog_ref[...]) * dt

    row = lax.broadcasted_iota(jnp.int32, (L, L), 0)
    col = lax.broadcasted_iota(jnp.int32, (L, L), 1)
    causal = row >= col
    tri = jnp.where(causal, 1.0, 0.0).astype(BF16)
    lane_g = lax.broadcasted_iota(jnp.int32, (L, gw), 1) // SSM_HEAD_DIM
    lane_lo = lax.broadcasted_iota(jnp.int32, (L, LANES), 1) < SSM_HEAD_DIM
    lane_lo1 = lax.broadcasted_iota(jnp.int32, (1, LANES), 1) < SSM_HEAD_DIM

    for c in range(lb // L):
        r0 = c * L
        hi, mid, lo = _split3(a_dt[r0:r0 + L, :])
        cs = _dot(tri, hi) + _dot(tri, mid) + _dot(tri, lo)
        cs_t = cs.T
        dt_t = dt[r0:r0 + L, :].T
        xs = xc[r0:r0 + L, 0:SSM_WIDTH]
        ys = []
        for g in range(SSM_GROUPS):
            b0 = SSM_WIDTH + g * SSM_STATE
            c0 = SSM_WIDTH + SSM_GROUPS * SSM_STATE + g * SSM_STATE
            bg = xc[r0:r0 + L, b0:b0 + SSM_STATE]
            cg = xc[r0:r0 + L, c0:c0 + SSM_STATE].astype(BF16)
            gmat = _dot_nt(cg, bg.astype(BF16))
            bg_t = bg.T
            xg = xs[:, g * gw:(g + 1) * gw]
            ydiag = None
            st_new = None
            ecols, decs = [], []
            for hh in range(HEADS_PER_GROUP):
                h = g * HEADS_PER_GROUP + hh
                colb = jnp.broadcast_to(cs[:, h:h + 1], (L, L))
                rowb = cs_t[h:h + 1, :]
                dtrow = dt_t[h:h + 1, :]
                decay = jnp.exp(jnp.where(causal, colb - rowb, NEG_BIG))
                mh = (gmat * decay * dtrow).astype(BF16)
                xm = jnp.where(lane_g == hh, xg, 0.0).astype(BF16)
                last = colb[L - 1:L, :]
                wrow = dtrow * jnp.exp(last - rowb)
                btw = (bg_t * wrow).astype(BF16)
                d1 = _dot(mh, xm)
                d2 = _dot(btw, xm)
                ydiag = d1 if ydiag is None else ydiag + d1
                st_new = d2 if st_new is None else st_new + d2
                ecols.append(jnp.exp(colb))
                decs.append(jnp.exp(last))
            e_mat = jnp.concatenate([jnp.where(lane_lo, ecols[0], ecols[1]),
                                     jnp.where(lane_lo, ecols[2], ecols[3])], axis=1)
            dec = jnp.concatenate([jnp.where(lane_lo1, decs[0], decs[1]),
                                   jnp.where(lane_lo1, decs[2], decs[3])], axis=1)
            st_old = st_ref[g]
            ys.append(ydiag + _dot(cg, st_old.astype(BF16)) * e_mat)
            st_ref[g] = st_old * dec + st_new
        y = jnp.concatenate(ys, axis=1) + dsk_ref[...] * xs
        gated = y * z_ref[r0:r0 + L, :]
        outs = []
        for g in range(SSM_GROUPS):
            part = gated[:, g * gw:(g + 1) * gw]
            ms = jnp.mean(part * part, axis=-1, keepdims=True)
            outs.append(part * lax.rsqrt(ms + 1e-5))
        y_ref[r0:r0 + L, :] = (jnp.concatenate(outs, axis=1) * nw_ref[...]).astype(BF16)


def _ssd_call(xc, z, dt, a_log, d_skip, norm_w, *, batch, lb=SSD_LB):
    T = xc.shape[0]
    S = T // batch
    nb = S // lb
    assert S % lb == 0 and lb % SSM_CHUNK == 0
    a_pad = jnp.pad(a_log.astype(F32), (0, LANES - a_log.shape[0])).reshape(1, LANES)
    dsk = jnp.repeat(d_skip.astype(F32), SSM_HEAD_DIM).reshape(1, SSM_WIDTH)
    row = lambda b, j: (b * nb + j, 0)
    const = lambda shape: pl.BlockSpec(shape, lambda b, j: (0, 0))
    nbytes = 3 * lb * (SSM_XBC + SSM_WIDTH + LANES) * 4 + 8 * lb * SSM_XBC * 4
    return pl.pallas_call(
        functools.partial(_ssd_kernel, lb=lb),
        grid=(batch, nb),
        in_specs=[pl.BlockSpec((lb, SSM_XBC), row), pl.BlockSpec((lb, SSM_WIDTH), row),
                  pl.BlockSpec((lb, LANES), row),
                  const((1, LANES)), const((1, SSM_WIDTH)), const((1, SSM_WIDTH))],
        out_specs=pl.BlockSpec((lb, SSM_WIDTH), row),
        out_shape=jax.ShapeDtypeStruct((T, SSM_WIDTH), BF16),
        scratch_shapes=[pltpu.VMEM((SSM_GROUPS, SSM_STATE, HEADS_PER_GROUP * SSM_HEAD_DIM), F32)],
        compiler_params=pltpu.CompilerParams(
            dimension_semantics=("arbitrary", "arbitrary"), vmem_limit_bytes=_vmem_limit(nbytes)),
        name="ssd",
    )(xc, z, dt, a_pad, dsk, norm_w.reshape(1, SSM_WIDTH).astype(F32))


def _sc_kernel(x_ref, nw_ref, win_ref, cw_ref, wout_ref, o_ref, vpad_ref, *, tm, tiles_per_seq):
    d = x_ref.shape[1]

    @pl.when(pl.program_id(0) % tiles_per_seq == 0)
    def _():
        vpad_ref[0:HALO, :] = jnp.zeros((HALO, d), F32)

    x = x_ref[...]
    h = _rms(x, nw_ref[...], 1e-6).astype(BF16)
    bgate = _dot(h, win_ref[:, 0:d])
    v = _dot(h, win_ref[:, d:2 * d]) * _dot(h, win_ref[:, 2 * d:3 * d])
    vpad_ref[HALO:HALO + tm, :] = v
    conv = cw_ref[SC_CONV - 1:SC_CONV, :] * v
    for kk in range(SC_CONV - 1):
        o = HALO - (SC_CONV - 1) + kk
        conv = conv + cw_ref[kk:kk + 1, :] * vpad_ref[o:o + tm, :]
    vpad_ref[0:HALO, :] = vpad_ref[tm:tm + HALO, :]
    o_ref[...] = x + _dot((bgate * conv).astype(BF16), wout_ref[...])


def _sc_call(x, nw, w_in, conv_w, w_out, *, batch, tm=PROJ_TM):
    T, D = x.shape
    S = T // batch
    assert S % tm == 0
    row = lambda i: (i, 0)
    nbytes = (w_in.size + w_out.size) * 2 + 4 * tm * D * 4 + 8 * tm * D * 4
    return pl.pallas_call(
        functools.partial(_sc_kernel, tm=tm, tiles_per_seq=S // tm),
        grid=(T // tm,),
        in_specs=[pl.BlockSpec((tm, D), row), _resident((1, D)), _resident(w_in.shape),
                  _resident(conv_w.shape), _resident(w_out.shape)],
        out_specs=pl.BlockSpec((tm, D), row),
        out_shape=jax.ShapeDtypeStruct((T, D), F32),
        scratch_shapes=[pltpu.VMEM((tm + HALO, D), F32)],
        compiler_params=pltpu.CompilerParams(
            dimension_semantics=("arbitrary",), vmem_limit_bytes=_vmem_limit(nbytes)),
        name="short_conv",
    )(x, nw.reshape(1, D), w_in, conv_w.astype(F32), w_out)


def kernel(x, rel_bias, final_norm_w, ffn1_norm, ffn1_wg, ffn1_wu, ffn1_wd, mix_norm, ffn2_norm, ffn2_wg, ffn2_wu, ffn2_wd, hyb_w_in, hyb_w_out, diff_lq1, diff_lk1, diff_lq2, diff_lk2, diff_subln_w, ssm_conv_w, ssm_conv_b, ssm_dt_bias, ssm_a_log, ssm_d, ssm_norm_w, sc_w_in, sc_conv_w, sc_w_out):
    B, S, D = x.shape
    T = B * S
    xt = x.reshape(T, D)
    bf = lambda w: w.astype(BF16)

    lam_inits = [0.8 - 0.6 * math.exp(-0.3 * i) for i in range(0, DEPTH, 2)]
    tiles, lams = _prep_call(rel_bias, diff_lq1, diff_lk1, diff_lq2, diff_lk2, lam_inits)

    mix = None
    for i in range(DEPTH):
        j = i // 2
        xt = _ffn_call(xt, ffn1_norm[i], ffn1_wg, ffn1_wu, ffn1_wd, i)
        if i % 2 == 0:
            w_in = hyb_w_in[j]
            qkv_end = 3 * DIFF_WIDTH
            w_pad = bf(jnp.pad(w_in, ((0, 0), (0, LANES - SSM_HEADS))))
            wvt = bf(jnp.concatenate([w_in[:, 0:DIFF_WIDTH], w_in[:, 2 * DIFF_WIDTH:qkv_end]], axis=1).T)
            q, k, vt, z, xc, dt = _hyb_in_call(xt, mix_norm[i], w_pad, wvt, ssm_conv_w[j], ssm_conv_b[j],
                                               ssm_dt_bias[j], batch=B)
            o = _attn_call(q, k, vt, tiles, lams[j * SUBLANES:(j + 1) * SUBLANES], diff_subln_w[j],
                           batch=B, lam_init=lam_inits[j])
            y = _ssd_call(xc, z, dt, ssm_a_log[j], ssm_d[j], ssm_norm_w[j], batch=B)
            mix = (o, y, hyb_w_out, j)
        else:
            xt = _sc_call(xt, mix_norm[i], bf(sc_w_in[j]), sc_conv_w[j], bf(sc_w_out[j]), batch=B)
            mix = None
        xt = _ffn_call(xt, ffn2_norm[i], ffn2_wg, ffn2_wu, ffn2_wd, i, mix=mix,
                       final_w=final_norm_w if i == DEPTH - 1 else None)
    return xt.reshape(B, S, D)
```

```python
import functools
import math

import jax
import jax.numpy as jnp
from jax import lax
from jax.experimental import pallas as pl
from jax.experimental.pallas import tpu as pltpu

F32 = jnp.float32
BF16 = jnp.bfloat16

DEPTH = 4
N_DIFF_HEADS = 4
DIFF_HEAD_DIM = 64
DIFF_WIDTH = N_DIFF_HEADS * 2 * DIFF_HEAD_DIM
NUM_BUCKETS = 32
MAX_DISTANCE = 128
SSM_HEADS = 8
SSM_HEAD_DIM = 64
SSM_WIDTH = SSM_HEADS * SSM_HEAD_DIM
SSM_GROUPS = 2
SSM_STATE = 128
SSM_CONV = 4
SSM_CHUNK = 128
HEADS_PER_GROUP = SSM_HEADS // SSM_GROUPS
SSM_XBC = SSM_WIDTH + 2 * SSM_GROUPS * SSM_STATE
SC_CONV = 3

LANES = 128
SUBLANES = 8
V7X_SCOPED_VMEM_CAP = 60000 * 1024

FFN_TM = 512
FFN_FCHUNK = 256
PROJ_TM = 512
ATT_T = 512
ATT_STRIP = 256
ATT_ONES = 16
ATT_FAR_TILE = 2
SSD_LB = 512
HALO = SUBLANES

NEG_BIG = -1e30
LOG2E = math.log2(math.e)


def _vmem_limit(nbytes):
    return int(min(V7X_SCOPED_VMEM_CAP, nbytes * 5 // 4 + (8 << 20)))


def _dot(a, b):
    return jnp.dot(a, b, preferred_element_type=F32)


def _dot_nt(a, b):
    return lax.dot_general(a, b, (((1,), (1,)), ((), ())), preferred_element_type=F32)


def _rms(x, w, eps):
    ms = jnp.mean(x * x, axis=-1, keepdims=True)
    return x * lax.rsqrt(ms + eps) * w


def _silu(x):
    return x * jax.nn.sigmoid(x)


def _resident(shape):
    nd = len(shape)
    return pl.BlockSpec(shape, lambda *_: (0,) * nd, pipeline_mode=pl.Buffered(1))


def _ffn_kernel(*refs, pre_mix, final_norm, f_total):
    refs = list(refs)
    x_ref = refs.pop(0)
    if pre_mix:
        o_ref, y_ref, wo_ref = refs.pop(0), refs.pop(0), refs.pop(0)
    nw_ref, wg_ref, wu_ref, wd_ref = refs.pop(0), refs.pop(0), refs.pop(0), refs.pop(0)
    if final_norm:
        fw_ref = refs.pop(0)
    out_ref = refs.pop(0)

    x = x_ref[...]
    if pre_mix:
        half = o_ref.shape[1]
        x = (x + _dot(o_ref[...], wo_ref[0:half, :].astype(BF16))
             + _dot(y_ref[...], wo_ref[half:, :].astype(BF16)))
    h = _rms(x, nw_ref[...], 1e-6).astype(BF16)
    acc = None
    for c0 in range(0, f_total, FFN_FCHUNK):
        c1 = c0 + FFN_FCHUNK
        g = _dot(h, wg_ref[:, c0:c1].astype(BF16))
        u = _dot(h, wu_ref[:, c0:c1].astype(BF16))
        a = (_silu(g) * u).astype(BF16)
        d = _dot(a, wd_ref[c0:c1, :].astype(BF16))
        acc = d if acc is None else acc + d
    y = x + 0.5 * acc
    if final_norm:
        y = _rms(y, fw_ref[...], 1e-6)
    out_ref[...] = y


def _layer_resident(stacked, layer):
    _, r, c = stacked.shape
    return pl.BlockSpec((None, r, c), lambda *_: (layer, 0, 0), pipeline_mode=pl.Buffered(1))


def _ffn_call(x, nw, wg, wu, wd, layer, *, mix=None, final_w=None, tm=FFN_TM):
    T, D = x.shape
    F = wg.shape[2]
    assert T % tm == 0 and F % FFN_FCHUNK == 0
    row = lambda i: (i, 0)
    args = [x]
    in_specs = [pl.BlockSpec((tm, D), row)]
    nbytes = 4 * tm * D * 4 + 3 * D * F * 4 + 6 * tm * FFN_FCHUNK * 4 + 3 * tm * D * 4
    if mix is not None:
        o, y, wo, mix_layer = mix
        args += [o, y, wo]
        in_specs += [pl.BlockSpec((tm, o.shape[1]), row), pl.BlockSpec((tm, y.shape[1]), row),
                     _layer_resident(wo, mix_layer)]
        nbytes += wo.shape[1] * wo.shape[2] * 4 + 4 * tm * o.shape[1] * 2 * 2
    args += [nw.reshape(1, D), wg, wu, wd]
    in_specs += [_resident((1, D)), _layer_resident(wg, layer), _layer_resident(wu, layer),
                 _layer_resident(wd, layer)]
    if final_w is not None:
        args.append(final_w.reshape(1, D))
        in_specs.append(_resident((1, D)))
    kern = functools.partial(_ffn_kernel, pre_mix=mix is not None, final_norm=final_w is not None,
                             f_total=F)
    return pl.pallas_call(
        kern,
        grid=(T // tm,),
        in_specs=in_specs,
        out_specs=pl.BlockSpec((tm, D), row),
        out_shape=jax.ShapeDtypeStruct((T, D), F32),
        compiler_params=pltpu.CompilerParams(
            dimension_semantics=("arbitrary",), vmem_limit_bytes=_vmem_limit(nbytes)),
        name="ffn",
    )(*args)


def _hyb_in_kernel(x_ref, nw_ref, w_ref, wvt_ref, cw_ref, cb_ref, dtb_ref,
                   q_ref, k_ref, vt_ref, z_ref, xc_ref, dt_ref, xpad_ref, *, tm, tiles_per_seq):
    @pl.when(pl.program_id(0) % tiles_per_seq == 0)
    def _():
        xpad_ref[0:HALO, :] = jnp.zeros((HALO, SSM_XBC), F32)

    h = _rms(x_ref[...], nw_ref[...], 1e-6).astype(BF16)
    qw = DIFF_WIDTH
    c_z = 2 * qw + DIFF_WIDTH
    c_xbc = c_z + SSM_WIDTH
    c_dt = c_xbc + SSM_XBC
    xpad_ref[HALO:HALO + tm, :] = _dot(h, w_ref[:, c_xbc:c_xbc + SSM_XBC])
    scale = DIFF_HEAD_DIM ** -0.5 * LOG2E
    q_ref[...] = (_dot_nt(wvt_ref[0:qw, :], h) * scale).astype(BF16)
    k_ref[...] = _dot(h, w_ref[:, qw:2 * qw]).astype(BF16)
    vt_ref[...] = _dot_nt(wvt_ref[qw:, :], h).astype(BF16)
    z_ref[...] = _silu(_dot(h, w_ref[:, c_z:c_z + SSM_WIDTH]))
    dt_raw = _dot(h, w_ref[:, c_dt:c_dt + LANES]) + dtb_ref[...]
    dt_ref[...] = jnp.maximum(dt_raw, 0.0) + jnp.log1p(jnp.exp(-jnp.abs(dt_raw)))

    conv = cb_ref[...]
    for kk in range(SSM_CONV):
        o = HALO - (SSM_CONV - 1) + kk
        conv = conv + cw_ref[kk:kk + 1, :] * xpad_ref[o:o + tm, :]
    xc_ref[...] = _silu(conv)
    xpad_ref[0:HALO, :] = xpad_ref[tm:tm + HALO, :]


def _hyb_in_call(x, nw, w_pad, wvt, conv_w, conv_b, dt_bias, *, batch, tm=PROJ_TM):
    T, D = x.shape
    S = T // batch
    assert S % tm == 0
    row = lambda i: (i, 0)
    outs = [
        jax.ShapeDtypeStruct((DIFF_WIDTH, T), BF16),
        jax.ShapeDtypeStruct((T, DIFF_WIDTH), BF16),
        jax.ShapeDtypeStruct((DIFF_WIDTH, T), BF16),
        jax.ShapeDtypeStruct((T, SSM_WIDTH), F32),
        jax.ShapeDtypeStruct((T, SSM_XBC), F32),
        jax.ShapeDtypeStruct((T, LANES), F32),
    ]
    out_specs = [
        pl.BlockSpec((DIFF_WIDTH, tm), lambda i: (0, i)), pl.BlockSpec((tm, DIFF_WIDTH), row),
        pl.BlockSpec((DIFF_WIDTH, tm), lambda i: (0, i)),
        pl.BlockSpec((tm, SSM_WIDTH), row), pl.BlockSpec((tm, SSM_XBC), row),
        pl.BlockSpec((tm, LANES), row),
    ]
    nbytes = ((w_pad.size + wvt.size) * 2 + 2 * tm * D * 4
              + 3 * tm * 4 * (3 * DIFF_WIDTH + SSM_WIDTH + SSM_XBC + LANES) + 3 * tm * SSM_XBC * 4)
    pad = jnp.pad(dt_bias.astype(F32), (0, LANES - dt_bias.shape[0])).reshape(1, LANES)
    return pl.pallas_call(
        functools.partial(_hyb_in_kernel, tm=tm, tiles_per_seq=S // tm),
        grid=(T // tm,),
        in_specs=[pl.BlockSpec((tm, D), row), _resident((1, D)), _resident(w_pad.shape),
                  _resident(wvt.shape), _resident((SSM_CONV, SSM_XBC)), _resident((1, SSM_XBC)),
                  _resident((1, LANES))],
        out_specs=out_specs,
        out_shape=outs,
        scratch_shapes=[pltpu.VMEM((tm + HALO, SSM_XBC), F32)],
        compiler_params=pltpu.CompilerParams(
            dimension_semantics=("arbitrary",), vmem_limit_bytes=_vmem_limit(nbytes)),
        name="hyb_in",
    )(x, nw.reshape(1, D), w_pad, wvt, conv_w.astype(F32), conv_b.reshape(1, SSM_XBC).astype(F32), pad)


def _prep_kernel(tab_ref, lq1_ref, lk1_ref, lq2_ref, lk2_ref, tile_ref, lam_ref, *, t, lam_inits):
    hd = pl.program_id(0)
    key = lax.broadcasted_iota(jnp.int32, (t, t), 0)
    qry = lax.broadcasted_iota(jnp.int32, (t, t), 1)
    max_exact = NUM_BUCKETS // 2
    far = tab_ref[hd, NUM_BUCKETS - 1]
    tile_ref[ATT_FAR_TILE] = jnp.zeros((t, t), F32)
    for off in range(ATT_FAR_TILE):
        dist = off * t + qry - key
        d = jnp.maximum(dist, 0)
        large = max_exact + (
            jnp.log(jnp.maximum(d, 1).astype(F32) / max_exact)
            / math.log(MAX_DISTANCE / max_exact) * (NUM_BUCKETS - max_exact)).astype(jnp.int32)
        bucket = jnp.where(d < max_exact, d, jnp.minimum(large, NUM_BUCKETS - 1))
        bias = jnp.full((t, t), far, F32)
        for b in range(NUM_BUCKETS - 1):
            bias = jnp.where(bucket == b, tab_ref[hd, b], bias)
        val = (bias - far) * LOG2E
        if off == 0:
            val = jnp.where(dist >= 0, val, NEG_BIG)
        tile_ref[off] = val
    for j, lam_init in enumerate(lam_inits):
        s1 = jnp.sum(lq1_ref[j:j + 1, :] * lk1_ref[j:j + 1, :], axis=-1, keepdims=True)
        s2 = jnp.sum(lq2_ref[j:j + 1, :] * lk2_ref[j:j + 1, :], axis=-1, keepdims=True)
        lam = jnp.exp(s1) - jnp.exp(s2) + lam_init
        lam_ref[j * SUBLANES:(j + 1) * SUBLANES, :] = jnp.broadcast_to(lam, (SUBLANES, LANES))


def _prep_call(rel_bias, lq1, lk1, lq2, lk2, lam_inits, *, t=ATT_T):
    ne = lq1.shape[0]
    H = rel_bias.shape[1]
    whole = lambda shape: pl.BlockSpec(shape, lambda h: (0,) * len(shape))
    return pl.pallas_call(
        functools.partial(_prep_kernel, t=t, lam_inits=tuple(lam_inits)),
        grid=(H,),
        in_specs=[pl.BlockSpec(memory_space=pltpu.SMEM)] + [whole(lq1.shape)] * 4,
        out_specs=[pl.BlockSpec((None, ATT_FAR_TILE + 1, t, t), lambda h: (h, 0, 0, 0)),
                   whole((ne * SUBLANES, LANES))],
        out_shape=[jax.ShapeDtypeStruct((H, ATT_FAR_TILE + 1, t, t), F32),
                   jax.ShapeDtypeStruct((ne * SUBLANES, LANES), F32)],
        compiler_params=pltpu.CompilerParams(
            dimension_semantics=("arbitrary",), vmem_limit_bytes=_vmem_limit(12 * t * t * 4)),
        name="attn_prep",
    )(rel_bias.T.astype(F32), lq1, lk1, lq2, lk2)


def _attn_kernel(qt_ref, k_ref, vt_ref, tile_ref, lam_ref, sw_ref, o_ref,
                 qm_ref, m_ref, acc_ref, s_ref, smax_ref, *, t, nq, unroll, q_split, lam_init):
    e = 2 * DIFF_HEAD_DIM
    first = lax.broadcasted_iota(jnp.int32, (LANES, t), 0) < DIFF_HEAD_DIM
    ones = jnp.ones((ATT_ONES, t), BF16)
    chains = [(w, slice(c0, c0 + ATT_STRIP)) for c0 in range(0, t, ATT_STRIP) for w in range(2)]

    def rows(blk):
        return pl.ds(pl.multiple_of(blk * t, t), t)

    def prepare(qi, carry):
        q = qt_ref[:, rows(qi)]
        zero = jnp.zeros_like(q)
        qm_ref[0, :, rows(qi)] = jnp.where(first, q, zero)
        qm_ref[1, :, rows(qi)] = jnp.where(first, zero, q)
        m_ref[qi] = jnp.full(m_ref.shape[1:], NEG_BIG, F32)
        acc_ref[qi] = jnp.zeros(acc_ref.shape[1:], F32)
        return carry

    def score_chain(ci, qi, j, slot, k, biased):
        w, cols = chains[ci]
        q_cols = pl.ds(pl.multiple_of(qi * t + cols.start, ATT_STRIP), ATT_STRIP)
        s = _dot(k, qm_ref[w, :, q_cols])
        if biased:
            near = jnp.minimum(qi - j, ATT_FAR_TILE)
            s = s + tile_ref[near, :, cols]
        s_ref[slot, ci] = s
        smax_ref[slot, ci] = jnp.max(s, axis=0, keepdims=True)

    def value_chain(ci, qi, slot, vta):
        w, cols = chains[ci]
        m_old = m_ref[qi, w, :, cols]
        m_new = jnp.maximum(m_old, smax_ref[slot, ci])
        p = jnp.exp2(s_ref[slot, ci] - m_new).astype(BF16)
        acc_ref[qi, w, :, cols] = jnp.exp2(m_old - m_new) * acc_ref[qi, w, :, cols] + _dot(vta, p)
        m_ref[qi, w, :, cols] = m_new

    def finalize(qi, carry):
        lam = lam_ref[0:1, 0:1]
        o1 = acc_ref[qi, 0, 0:e, :] * (1.0 / acc_ref[qi, 0, e:e + 1, :])
        o2 = acc_ref[qi, 1, 0:e, :] * (1.0 / acc_ref[qi, 1, e:e + 1, :])
        out_t = o1 - lam * o2
        ms = jnp.mean(out_t * out_t, axis=0, keepdims=True)
        out_t = out_t * lax.rsqrt(ms + 1e-5) * sw_ref[...] * (1.0 - lam_init)
        o_ref[rows(qi), :] = out_t.T.astype(BF16)
        return carry

    def sweep(first_pair, next_pair, last_pair, count, biased):
        def step(qv, jv, slot):
            qs, js = next_pair(qv, jv)
            past_end = qs >= nq
            qs = jnp.where(past_end, last_pair[0], qs)
            js = jnp.where(past_end, last_pair[1], js)
            k_next = k_ref[rows(js), :]
            vta = jnp.concatenate([vt_ref[:, rows(jv)], ones], axis=0)
            for ci in range(len(chains)):
                score_chain(ci, qs, js, 1 - slot, k_next, biased)
                value_chain(ci, qv, slot, vta)
            return qs, js

        def body(_, carry):
            qv, jv = carry
            for u in range(unroll):
                qv, jv = step(qv, jv, u % 2)
            return qv, jv

        k0 = k_ref[rows(first_pair[1]), :]
        for ci in range(len(chains)):
            score_chain(ci, first_pair[0], first_pair[1], 0, k0, biased)
        lax.fori_loop(0, count // unroll, body, (jnp.int32(first_pair[0]), jnp.int32(first_pair[1])))

    def next_near(qv, jv):
        wrap = jv == qv
        qn = qv + 1
        return jnp.where(wrap, qn, qv), jnp.where(wrap, jnp.where(qn < q_split, 0, qn - 1), jv + 1)

    def next_far(qv, jv):
        wrap = jv == qv - ATT_FAR_TILE
        return jnp.where(wrap, qv + 1, qv), jnp.where(wrap, 0, jv + 1)

    n_near = q_split * (q_split + 1) // 2 + ATT_FAR_TILE * (nq - q_split)
    n_far = nq * (nq + 1) // 2 - n_near
    lax.fori_loop(0, nq, prepare, 0)
    sweep((0, 0), next_near, (nq - 1, nq - 1), n_near, True)
    if n_far:
        sweep((q_split, 0), next_far, (nq - 1, nq - 1 - ATT_FAR_TILE), n_far, False)
    lax.fori_loop(0, nq, finalize, 0)


def _attn_plan(nq):
    for unroll in (8, 4, 2):
        for q_split in range(ATT_FAR_TILE, nq + 1):
            n_near = q_split * (q_split + 1) // 2 + ATT_FAR_TILE * (nq - q_split)
            n_far = nq * (nq + 1) // 2 - n_near
            if n_near % unroll == 0 and n_far % unroll == 0:
                return unroll, q_split
    raise ValueError(f"unsupported number of query blocks: {nq}")


def _attn_call(qt, k, vt, tiles, lam, subln_w, *, batch, lam_init, t=ATT_T):
    T, W = k.shape
    S = T // batch
    H = W // LANES
    nq = S // t
    e = 2 * DIFF_HEAD_DIM
    assert e == LANES and S % t == 0
    unroll, q_split = _attn_plan(nq)
    n_chains = 2 * (t // ATT_STRIP)
    n_tiles = ATT_FAR_TILE + 1
    nbytes = (2 * 4 * (S * LANES * 2) + 2 * n_tiles * t * t * 4 + 2 * S * LANES * 2
              + nq * 2 * (e + ATT_ONES + SUBLANES) * t * 4 + 2 * n_chains * t * ATT_STRIP * 4 + 2 * t * t * 4)
    return pl.pallas_call(
        functools.partial(_attn_kernel, t=t, nq=nq, unroll=unroll, q_split=q_split, lam_init=lam_init),
        grid=(batch, H),
        in_specs=[
            pl.BlockSpec((LANES, S), lambda b, h: (h, b)),
            pl.BlockSpec((S, LANES), lambda b, h: (b, h)),
            pl.BlockSpec((LANES, S), lambda b, h: (h, b)),
            pl.BlockSpec((None, n_tiles, t, t), lambda b, h: (h, 0, 0, 0)),
            pl.BlockSpec((SUBLANES, LANES), lambda b, h: (0, 0)),
            pl.BlockSpec((e, 1), lambda b, h: (0, 0)),
        ],
        out_specs=pl.BlockSpec((S, LANES), lambda b, h: (b, h)),
        out_shape=jax.ShapeDtypeStruct((T, W), BF16),
        scratch_shapes=[pltpu.VMEM((2, LANES, S), BF16),
                        pltpu.VMEM((nq, 2, 1, t), F32), pltpu.VMEM((nq, 2, e + ATT_ONES, t), F32),
                        pltpu.VMEM((2, n_chains, t, ATT_STRIP), F32),
                        pltpu.VMEM((2, n_chains, 1, ATT_STRIP), F32)],
        compiler_params=pltpu.CompilerParams(
            dimension_semantics=("arbitrary", "arbitrary"), vmem_limit_bytes=_vmem_limit(nbytes)),
        name="diff_attn",
    )(qt, k, vt, tiles, lam, subln_w.reshape(e, 1))


def _split3(a):
    hi = a.astype(BF16)
    r1 = a - hi.astype(F32)
    mid = r1.astype(BF16)
    lo = (r1 - mid.astype(F32)).astype(BF16)
    return hi, mid, lo


def _ssd_kernel(xc_ref, z_ref, dt_ref, alog_ref, dsk_ref, nw_ref, y_ref, st_ref, *, lb):
    L = SSM_CHUNK
    gw = HEADS_PER_GROUP * SSM_HEAD_DIM

    @pl.when(pl.program_id(1) == 0)
    def _():
        st_ref[...] = jnp.zeros(st_ref.shape, F32)

    xc = xc_ref[...]
    dt = dt_ref[...]
    a_dt = -jnp.exp(alog_ref[...]) * dt

    row = lax.broadcasted_iota(jnp.int32, (L, L), 0)
    col = lax.broadcasted_iota(jnp.int32, (L, L), 1)
    causal = row >= col
    tri = jnp.where(causal, 1.0, 0.0).astype(BF16)
    lane_g = lax.broadcasted_iota(jnp.int32, (L, gw), 1) // SSM_HEAD_DIM
    lane_lo = lax.broadcasted_iota(jnp.int32, (L, LANES), 1) < SSM_HEAD_DIM
    lane_lo1 = lax.broadcasted_iota(jnp.int32, (1, LANES), 1) < SSM_HEAD_DIM

    for c in range(lb // L):
        r0 = c * L
        hi, mid, lo = _split3(a_dt[r0:r0 + L, :])
        cs = _dot(tri, hi) + _dot(tri, mid) + _dot(tri, lo)
        cs_t = cs.T
        dt_t = dt[r0:r0 + L, :].T
        xs = xc[r0:r0 + L, 0:SSM_WIDTH]
        ys = []
        for g in range(SSM_GROUPS):
            b0 = SSM_WIDTH + g * SSM_STATE
            c0 = SSM_WIDTH + SSM_GROUPS * SSM_STATE + g * SSM_STATE
            bg = xc[r0:r0 + L, b0:b0 + SSM_STATE]
            cg = xc[r0:r0 + L, c0:c0 + SSM_STATE].astype(BF16)
            gmat = _dot_nt(cg, bg.astype(BF16))
            bg_t = bg.T
            xg = xs[:, g * gw:(g + 1) * gw]
            ydiag = None
            st_new = None
            ecols, decs = [], []
            for hh in range(HEADS_PER_GROUP):
                h = g * HEADS_PER_GROUP + hh
                colb = jnp.broadcast_to(cs[:, h:h + 1], (L, L))
                rowb = cs_t[h:h + 1, :]
                dtrow = dt_t[h:h + 1, :]
                decay = jnp.exp(jnp.where(causal, colb - rowb, NEG_BIG))
                mh = (gmat * decay * dtrow).astype(BF16)
                xm = jnp.where(lane_g == hh, xg, 0.0).astype(BF16)
                last = colb[L - 1:L, :]
                wrow = dtrow * jnp.exp(last - rowb)
                btw = (bg_t * wrow).astype(BF16)
                d1 = _dot(mh, xm)
                d2 = _dot(btw, xm)
                ydiag = d1 if ydiag is None else ydiag + d1
                st_new = d2 if st_new is None else st_new + d2
                ecols.append(jnp.exp(colb))
                decs.append(jnp.exp(last))
            e_mat = jnp.concatenate([jnp.where(lane_lo, ecols[0], ecols[1]),
                                     jnp.where(lane_lo, ecols[2], ecols[3])], axis=1)
            dec = jnp.concatenate([jnp.where(lane_lo1, decs[0], decs[1]),
                                   jnp.where(lane_lo1, decs[2], decs[3])], axis=1)
            st_old = st_ref[g]
            ys.append(ydiag + _dot(cg, st_old.astype(BF16)) * e_mat)
            st_ref[g] = st_old * dec + st_new
        y = jnp.concatenate(ys, axis=1) + dsk_ref[...] * xs
        gated = y * z_ref[r0:r0 + L, :]
        outs = []
        for g in range(SSM_GROUPS):
            part = gated[:, g * gw:(g + 1) * gw]
            ms = jnp.mean(part * part, axis=-1, keepdims=True)
            outs.append(part * lax.rsqrt(ms + 1e-5))
        y_ref[r0:r0 + L, :] = (jnp.concatenate(outs, axis=1) * nw_ref[...]).astype(BF16)


def _ssd_call(xc, z, dt, a_log, d_skip, norm_w, *, batch, lb=SSD_LB):
    T = xc.shape[0]
    S = T // batch
    nb = S // lb
    assert S % lb == 0 and lb % SSM_CHUNK == 0
    a_pad = jnp.pad(a_log.astype(F32), (0, LANES - a_log.shape[0])).reshape(1, LANES)
    dsk = jnp.repeat(d_skip.astype(F32), SSM_HEAD_DIM).reshape(1, SSM_WIDTH)
    row = lambda b, j: (b * nb + j, 0)
    const = lambda shape: pl.BlockSpec(shape, lambda b, j: (0, 0))
    nbytes = 3 * lb * (SSM_XBC + SSM_WIDTH + LANES) * 4 + 8 * lb * SSM_XBC * 4
    return pl.pallas_call(
        functools.partial(_ssd_kernel, lb=lb),
        grid=(batch, nb),
        in_specs=[pl.BlockSpec((lb, SSM_XBC), row), pl.BlockSpec((lb, SSM_WIDTH), row),
                  pl.BlockSpec((lb, LANES), row),
                  const((1, LANES)), const((1, SSM_WIDTH)), const((1, SSM_WIDTH))],
        out_specs=pl.BlockSpec((lb, SSM_WIDTH), row),
        out_shape=jax.ShapeDtypeStruct((T, SSM_WIDTH), BF16),
        scratch_shapes=[pltpu.VMEM((SSM_GROUPS, SSM_STATE, HEADS_PER_GROUP * SSM_HEAD_DIM), F32)],
        compiler_params=pltpu.CompilerParams(
            dimension_semantics=("arbitrary", "arbitrary"), vmem_limit_bytes=_vmem_limit(nbytes)),
        name="ssd",
    )(xc, z, dt, a_pad, dsk, norm_w.reshape(1, SSM_WIDTH).astype(F32))


def _sc_kernel(x_ref, nw_ref, win_ref, cw_ref, wout_ref, o_ref, vpad_ref, *, tm, tiles_per_seq):
    d = x_ref.shape[1]

    @pl.when(pl.program_id(0) % tiles_per_seq == 0)
    def _():
        vpad_ref[0:HALO, :] = jnp.zeros((HALO, d), F32)

    x = x_ref[...]
    h = _rms(x, nw_ref[...], 1e-6).astype(BF16)
    bgate = _dot(h, win_ref[:, 0:d])
    v = _dot(h, win_ref[:, d:2 * d]) * _dot(h, win_ref[:, 2 * d:3 * d])
    vpad_ref[HALO:HALO + tm, :] = v
    conv = cw_ref[SC_CONV - 1:SC_CONV, :] * v
    for kk in range(SC_CONV - 1):
        o = HALO - (SC_CONV - 1) + kk
        conv = conv + cw_ref[kk:kk + 1, :] * vpad_ref[o:o + tm, :]
    vpad_ref[0:HALO, :] = vpad_ref[tm:tm + HALO, :]
    o_ref[...] = x + _dot((bgate * conv).astype(BF16), wout_ref[...])


def _sc_call(x, nw, w_in, conv_w, w_out, *, batch, tm=PROJ_TM):
    T, D = x.shape
    S = T // batch
    assert S % tm == 0
    row = lambda i: (i, 0)
    nbytes = (w_in.size + w_out.size) * 2 + 4 * tm * D * 4 + 8 * tm * D * 4
    return pl.pallas_call(
        functools.partial(_sc_kernel, tm=tm, tiles_per_seq=S // tm),
        grid=(T // tm,),
        in_specs=[pl.BlockSpec((tm, D), row), _resident((1, D)), _resident(w_in.shape),
                  _resident(conv_w.shape), _resident(w_out.shape)],
        out_specs=pl.BlockSpec((tm, D), row),
        out_shape=jax.ShapeDtypeStruct((T, D), F32),
        scratch_shapes=[pltpu.VMEM((tm + HALO, D), F32)],
        compiler_params=pltpu.CompilerParams(
            dimension_semantics=("arbitrary",), vmem_limit_bytes=_vmem_limit(nbytes)),
        name="short_conv",
    )(x, nw.reshape(1, D), w_in, conv_w.astype(F32), w_out)


def kernel(x, rel_bias, final_norm_w, ffn1_norm, ffn1_wg, ffn1_wu, ffn1_wd, mix_norm, ffn2_norm, ffn2_wg, ffn2_wu, ffn2_wd, hyb_w_in, hyb_w_out, diff_lq1, diff_lk1, diff_lq2, diff_lk2, diff_subln_w, ssm_conv_w, ssm_conv_b, ssm_dt_bias, ssm_a_log, ssm_d, ssm_norm_w, sc_w_in, sc_conv_w, sc_w_out):
    B, S, D = x.shape
    T = B * S
    xt = x.reshape(T, D)
    bf = lambda w: w.astype(BF16)

    lam_inits = [0.8 - 0.6 * math.exp(-0.3 * i) for i in range(0, DEPTH, 2)]
    tiles, lams = _prep_call(rel_bias, diff_lq1, diff_lk1, diff_lq2, diff_lk2, lam_inits)

    mix = None
    for i in range(DEPTH):
        j = i // 2
        xt = _ffn_call(xt, ffn1_norm[i], ffn1_wg, ffn1_wu, ffn1_wd, i)
        if i % 2 == 0:
            w_in = hyb_w_in[j]
            qkv_end = 3 * DIFF_WIDTH
            w_pad = bf(jnp.pad(w_in, ((0, 0), (0, LANES - SSM_HEADS))))
            wvt = bf(jnp.concatenate([w_in[:, 0:DIFF_WIDTH], w_in[:, 2 * DIFF_WIDTH:qkv_end]], axis=1).T)
            q, k, vt, z, xc, dt = _hyb_in_call(xt, mix_norm[i], w_pad, wvt, ssm_conv_w[j], ssm_conv_b[j],
                                               ssm_dt_bias[j], batch=B)
            o = _attn_call(q, k, vt, tiles, lams[j * SUBLANES:(j + 1) * SUBLANES], diff_subln_w[j],
                           batch=B, lam_init=lam_inits[j])
            y = _ssd_call(xc, z, dt, ssm_a_log[j], ssm_d[j], ssm_norm_w[j], batch=B)
            mix = (o, y, hyb_w_out, j)
        else:
            xt = _sc_call(xt, mix_norm[i], bf(sc_w_in[j]), sc_conv_w[j], bf(sc_w_out[j]), batch=B)
            mix = None
        xt = _ffn_call(xt, ffn2_norm[i], ffn2_wg, ffn2_wu, ffn2_wd, i, mix=mix,
                       final_w=final_norm_w if i == DEPTH - 1 else None)
    return xt.reshape(B, S, D)
```

```python
import functools
import math

import jax
import jax.numpy as jnp
from jax import lax
from jax.experimental import pallas as pl
from jax.experimental.pallas import tpu as pltpu

F32 = jnp.float32
BF16 = jnp.bfloat16

DEPTH = 4
N_DIFF_HEADS = 4
DIFF_HEAD_DIM = 64
DIFF_WIDTH = N_DIFF_HEADS * 2 * DIFF_HEAD_DIM
NUM_BUCKETS = 32
MAX_DISTANCE = 128
SSM_HEADS = 8
SSM_HEAD_DIM = 64
SSM_WIDTH = SSM_HEADS * SSM_HEAD_DIM
SSM_GROUPS = 2
SSM_STATE = 128
SSM_CONV = 4
SSM_CHUNK = 128
HEADS_PER_GROUP = SSM_HEADS // SSM_GROUPS
SSM_XBC = SSM_WIDTH + 2 * SSM_GROUPS * SSM_STATE
SC_CONV = 3

LANES = 128
SUBLANES = 8
V7X_SCOPED_VMEM_CAP = 60000 * 1024

FFN_TM = 512
FFN_FCHUNK = 256
PROJ_TM = 512
ATT_T = 512
ATT_STRIP = 256
ATT_ONES = 16
ATT_FAR_TILE = 2
SSD_LB = 512
HALO = SUBLANES

NEG_BIG = -1e30
LOG2E = math.log2(math.e)


def _vmem_limit(nbytes):
    return int(min(V7X_SCOPED_VMEM_CAP, nbytes * 5 // 4 + (8 << 20)))


def _dot(a, b):
    return jnp.dot(a, b, preferred_element_type=F32)


def _dot_nt(a, b):
    return lax.dot_general(a, b, (((1,), (1,)), ((), ())), preferred_element_type=F32)


def _rms(x, w, eps):
    ms = jnp.mean(x * x, axis=-1, keepdims=True)
    return x * lax.rsqrt(ms + eps) * w


def _silu(x):
    return x * jax.nn.sigmoid(x)


def _resident(shape):
    nd = len(shape)
    return pl.BlockSpec(shape, lambda *_: (0,) * nd, pipeline_mode=pl.Buffered(1))


def _ffn_kernel(*refs, pre_mix, final_norm, f_total):
    refs = list(refs)
    x_ref = refs.pop(0)
    if pre_mix:
        o_ref, y_ref, wo_ref = refs.pop(0), refs.pop(0), refs.pop(0)
    nw_ref, wg_ref, wu_ref, wd_ref = refs.pop(0), refs.pop(0), refs.pop(0), refs.pop(0)
    if final_norm:
        fw_ref = refs.pop(0)
    out_ref = refs.pop(0)

    x = x_ref[...]
    if pre_mix:
        half = o_ref.shape[1]
        x = (x + _dot(o_ref[...], wo_ref[0:half, :].astype(BF16))
             + _dot(y_ref[...], wo_ref[half:, :].astype(BF16)))
    h = _rms(x, nw_ref[...], 1e-6).astype(BF16)
    acc = None
    for c0 in range(0, f_total, FFN_FCHUNK):
        c1 = c0 + FFN_FCHUNK
        g = _dot(h, wg_ref[:, c0:c1].astype(BF16))
        u = _dot(h, wu_ref[:, c0:c1].astype(BF16))
        a = (_silu(g) * u).astype(BF16)
        d = _dot(a, wd_ref[c0:c1, :].astype(BF16))
        acc = d if acc is None else acc + d
    y = x + 0.5 * acc
    if final_norm:
        y = _rms(y, fw_ref[...], 1e-6)
    out_ref[...] = y


def _layer_resident(stacked, layer):
    _, r, c = stacked.shape
    return pl.BlockSpec((None, r, c), lambda *_: (layer, 0, 0), pipeline_mode=pl.Buffered(1))


def _ffn_call(x, nw, wg, wu, wd, layer, *, mix=None, final_w=None, tm=FFN_TM):
    T, D = x.shape
    F = wg.shape[2]
    assert T % tm == 0 and F % FFN_FCHUNK == 0
    row = lambda i: (i, 0)
    args = [x]
    in_specs = [pl.BlockSpec((tm, D), row)]
    nbytes = 4 * tm * D * 4 + 3 * D * F * 4 + 6 * tm * FFN_FCHUNK * 4 + 3 * tm * D * 4
    if mix is not None:
        o, y, wo, mix_layer = mix
        args += [o, y, wo]
        in_specs += [pl.BlockSpec((tm, o.shape[1]), row), pl.BlockSpec((tm, y.shape[1]), row),
                     _layer_resident(wo, mix_layer)]
        nbytes += wo.shape[1] * wo.shape[2] * 4 + 4 * tm * o.shape[1] * 2 * 2
    args += [nw.reshape(1, D), wg, wu, wd]
    in_specs += [_resident((1, D)), _layer_resident(wg, layer), _layer_resident(wu, layer),
                 _layer_resident(wd, layer)]
    if final_w is not None:
        args.append(final_w.reshape(1, D))
        in_specs.append(_resident((1, D)))
    kern = functools.partial(_ffn_kernel, pre_mix=mix is not None, final_norm=final_w is not None,
                             f_total=F)
    return pl.pallas_call(
        kern,
        grid=(T // tm,),
        in_specs=in_specs,
        out_specs=pl.BlockSpec((tm, D), row),
        out_shape=jax.ShapeDtypeStruct((T, D), F32),
        compiler_params=pltpu.CompilerParams(
            dimension_semantics=("arbitrary",), vmem_limit_bytes=_vmem_limit(nbytes)),
        name="ffn",
    )(*args)


def _hyb_in_kernel(x_ref, nw_ref, w_ref, wvt_ref, cw_ref, cb_ref, dtb_ref,
                   q_ref, k_ref, vt_ref, z_ref, xc_ref, dt_ref, xpad_ref, *, tm, tiles_per_seq):
    @pl.when(pl.program_id(0) % tiles_per_seq == 0)
    def _():
        xpad_ref[0:HALO, :] = jnp.zeros((HALO, SSM_XBC), F32)

    h = _rms(x_ref[...], nw_ref[...], 1e-6).astype(BF16)
    qw = DIFF_WIDTH
    c_z = 2 * qw + DIFF_WIDTH
    c_xbc = c_z + SSM_WIDTH
    c_dt = c_xbc + SSM_XBC
    xpad_ref[HALO:HALO + tm, :] = _dot(h, w_ref[:, c_xbc:c_xbc + SSM_XBC])
    scale = DIFF_HEAD_DIM ** -0.5 * LOG2E
    q_ref[...] = (_dot_nt(wvt_ref[0:qw, :], h) * scale).astype(BF16)
    k_ref[...] = _dot(h, w_ref[:, qw:2 * qw]).astype(BF16)
    vt_ref[...] = _dot_nt(wvt_ref[qw:, :], h).astype(BF16)
    z_ref[...] = _silu(_dot(h, w_ref[:, c_z:c_z + SSM_WIDTH]))
    dt_raw = _dot(h, w_ref[:, c_dt:c_dt + LANES]) + dtb_ref[...]
    dt_ref[...] = jnp.maximum(dt_raw, 0.0) + jnp.log1p(jnp.exp(-jnp.abs(dt_raw)))

    conv = cb_ref[...]
    for kk in range(SSM_CONV):
        o = HALO - (SSM_CONV - 1) + kk
        conv = conv + cw_ref[kk:kk + 1, :] * xpad_ref[o:o + tm, :]
    xc_ref[...] = _silu(conv)
    xpad_ref[0:HALO, :] = xpad_ref[tm:tm + HALO, :]


def _hyb_in_call(x, nw, w_pad, wvt, conv_w, conv_b, dt_bias, *, batch, tm=PROJ_TM):
    T, D = x.shape
    S = T // batch
    assert S % tm == 0
    row = lambda i: (i, 0)
    outs = [
        jax.ShapeDtypeStruct((DIFF_WIDTH, T), BF16),
        jax.ShapeDtypeStruct((T, DIFF_WIDTH), BF16),
        jax.ShapeDtypeStruct((DIFF_WIDTH, T), BF16),
        jax.ShapeDtypeStruct((T, SSM_WIDTH), F32),
        jax.ShapeDtypeStruct((T, SSM_XBC), F32),
        jax.ShapeDtypeStruct((T, LANES), F32),
    ]
    out_specs = [
        pl.BlockSpec((DIFF_WIDTH, tm), lambda i: (0, i)), pl.BlockSpec((tm, DIFF_WIDTH), row),
        pl.BlockSpec((DIFF_WIDTH, tm), lambda i: (0, i)),
        pl.BlockSpec((tm, SSM_WIDTH), row), pl.BlockSpec((tm, SSM_XBC), row),
        pl.BlockSpec((tm, LANES), row),
    ]
    nbytes = ((w_pad.size + wvt.size) * 2 + 2 * tm * D * 4
              + 3 * tm * 4 * (3 * DIFF_WIDTH + SSM_WIDTH + SSM_XBC + LANES) + 3 * tm * SSM_XBC * 4)
    pad = jnp.pad(dt_bias.astype(F32), (0, LANES - dt_bias.shape[0])).reshape(1, LANES)
    return pl.pallas_call(
        functools.partial(_hyb_in_kernel, tm=tm, tiles_per_seq=S // tm),
        grid=(T // tm,),
        in_specs=[pl.BlockSpec((tm, D), row), _resident((1, D)), _resident(w_pad.shape),
                  _resident(wvt.shape), _resident((SSM_CONV, SSM_XBC)), _resident((1, SSM_XBC)),
                  _resident((1, LANES))],
        out_specs=out_specs,
        out_shape=outs,
        scratch_shapes=[pltpu.VMEM((tm + HALO, SSM_XBC), F32)],
        compiler_params=pltpu.CompilerParams(
            dimension_semantics=("arbitrary",), vmem_limit_bytes=_vmem_limit(nbytes)),
        name="hyb_in",
    )(x, nw.reshape(1, D), w_pad, wvt, conv_w.astype(F32), conv_b.reshape(1, SSM_XBC).astype(F32), pad)


def _prep_kernel(tab_ref, lq1_ref, lk1_ref, lq2_ref, lk2_ref, tile_ref, lam_ref, *, t, lam_inits):
    hd = pl.program_id(0)
    key = lax.broadcasted_iota(jnp.int32, (t, t), 0)
    qry = lax.broadcasted_iota(jnp.int32, (t, t), 1)
    max_exact = NUM_BUCKETS // 2
    far = tab_ref[hd, NUM_BUCKETS - 1]
    tile_ref[ATT_FAR_TILE] = jnp.zeros((t, t), F32)
    for off in range(ATT_FAR_TILE):
        dist = off * t + qry - key
        d = jnp.maximum(dist, 0)
        large = max_exact + (
            jnp.log(jnp.maximum(d, 1).astype(F32) / max_exact)
            / math.log(MAX_DISTANCE / max_exact) * (NUM_BUCKETS - max_exact)).astype(jnp.int32)
        bucket = jnp.where(d < max_exact, d, jnp.minimum(large, NUM_BUCKETS - 1))
        bias = jnp.full((t, t), far, F32)
        for b in range(NUM_BUCKETS - 1):
            bias = jnp.where(bucket == b, tab_ref[hd, b], bias)
        val = (bias - far) * LOG2E
        if off == 0:
            val = jnp.where(dist >= 0, val, NEG_BIG)
        tile_ref[off] = val
    for j, lam_init in enumerate(lam_inits):
        s1 = jnp.sum(lq1_ref[j:j + 1, :] * lk1_ref[j:j + 1, :], axis=-1, keepdims=True)
        s2 = jnp.sum(lq2_ref[j:j + 1, :] * lk2_ref[j:j + 1, :], axis=-1, keepdims=True)
        lam = jnp.exp(s1) - jnp.exp(s2) + lam_init
        lam_ref[j * SUBLANES:(j + 1) * SUBLANES, :] = jnp.broadcast_to(lam, (SUBLANES, LANES))


def _prep_call(rel_bias, lq1, lk1, lq2, lk2, lam_inits, *, t=ATT_T):
    ne = lq1.shape[0]
    H = rel_bias.shape[1]
    whole = lambda shape: pl.BlockSpec(shape, lambda h: (0,) * len(shape))
    return pl.pallas_call(
        functools.partial(_prep_kernel, t=t, lam_inits=tuple(lam_inits)),
        grid=(H,),
        in_specs=[pl.BlockSpec(memory_space=pltpu.SMEM)] + [whole(lq1.shape)] * 4,
        out_specs=[pl.BlockSpec((None, ATT_FAR_TILE + 1, t, t), lambda h: (h, 0, 0, 0)),
                   whole((ne * SUBLANES, LANES))],
        out_shape=[jax.ShapeDtypeStruct((H, ATT_FAR_TILE + 1, t, t), F32),
                   jax.ShapeDtypeStruct((ne * SUBLANES, LANES), F32)],
        compiler_params=pltpu.CompilerParams(
            dimension_semantics=("arbitrary",), vmem_limit_bytes=_vmem_limit(12 * t * t * 4)),
        name="attn_prep",
    )(rel_bias.T.astype(F32), lq1, lk1, lq2, lk2)


def _attn_kernel(qt_ref, k_ref, vt_ref, tile_ref, lam_ref, sw_ref, o_ref,
                 qm_ref, m_ref, acc_ref, s_ref, smax_ref, *, t, nq, unroll, q_split, lam_init):
    e = 2 * DIFF_HEAD_DIM
    first = lax.broadcasted_iota(jnp.int32, (LANES, t), 0) < DIFF_HEAD_DIM
    ones = jnp.ones((ATT_ONES, t), BF16)
    chains = [(w, slice(c0, c0 + ATT_STRIP)) for c0 in range(0, t, ATT_STRIP) for w in range(2)]

    def rows(blk):
        return pl.ds(pl.multiple_of(blk * t, t), t)

    def prepare(qi, carry):
        q = qt_ref[:, rows(qi)]
        zero = jnp.zeros_like(q)
        qm_ref[0, :, rows(qi)] = jnp.where(first, q, zero)
        qm_ref[1, :, rows(qi)] = jnp.where(first, zero, q)
        m_ref[qi] = jnp.full(m_ref.shape[1:], NEG_BIG, F32)
        acc_ref[qi] = jnp.zeros(acc_ref.shape[1:], F32)
        return carry

    def score_chain(ci, qi, j, slot, k, biased):
        w, cols = chains[ci]
        q_cols = pl.ds(pl.multiple_of(qi * t + cols.start, ATT_STRIP), ATT_STRIP)
        s = _dot(k, qm_ref[w, :, q_cols])
        if biased:
            near = jnp.minimum(qi - j, ATT_FAR_TILE)
            s = s + tile_ref[near, :, cols]
        s_ref[slot, ci] = s
        smax_ref[slot, ci] = jnp.max(s, axis=0, keepdims=True)

    def value_chain(ci, qi, slot, vta):
        w, cols = chains[ci]
        m_old = m_ref[qi, w, :, cols]
        m_new = jnp.maximum(m_old, smax_ref[slot, ci])
        p = jnp.exp2(s_ref[slot, ci] - m_new).astype(BF16)
        acc_ref[qi, w, :, cols] = jnp.exp2(m_old - m_new) * acc_ref[qi, w, :, cols] + _dot(vta, p)
        m_ref[qi, w, :, cols] = m_new

    def finalize(qi, carry):
        lam = lam_ref[0:1, 0:1]
        o1 = acc_ref[qi, 0, 0:e, :] * (1.0 / acc_ref[qi, 0, e:e + 1, :])
        o2 = acc_ref[qi, 1, 0:e, :] * (1.0 / acc_ref[qi, 1, e:e + 1, :])
        out_t = o1 - lam * o2
        ms = jnp.mean(out_t * out_t, axis=0, keepdims=True)
        out_t = out_t * lax.rsqrt(ms + 1e-5) * sw_ref[...] * (1.0 - lam_init)
        o_ref[rows(qi), :] = out_t.T.astype(BF16)
        return carry

    def sweep(first_pair, next_pair, last_pair, count, biased):
        def step(qv, jv, slot):
            qs, js = next_pair(qv, jv)
            past_end = qs >= nq
            qs = jnp.where(past_end, last_pair[0], qs)
            js = jnp.where(past_end, last_pair[1], js)
            k_next = k_ref[rows(js), :]
            vta = jnp.concatenate([vt_ref[:, rows(jv)], ones], axis=0)
            for ci in range(len(chains)):
                score_chain(ci, qs, js, 1 - slot, k_next, biased)
                value_chain(ci, qv, slot, vta)
            return qs, js

        def body(_, carry):
            qv, jv = carry
            for u in range(unroll):
                qv, jv = step(qv, jv, u % 2)
            return qv, jv

        k0 = k_ref[rows(first_pair[1]), :]
        for ci in range(len(chains)):
            score_chain(ci, first_pair[0], first_pair[1], 0, k0, biased)
        lax.fori_loop(0, count // unroll, body, (jnp.int32(first_pair[0]), jnp.int32(first_pair[1])))

    def next_near(qv, jv):
        wrap = jv == qv
        qn = qv + 1
        return jnp.where(wrap, qn, qv), jnp.where(wrap, jnp.where(qn < q_split, 0, qn - 1), jv + 1)

    def next_far(qv, jv):
        wrap = jv == qv - ATT_FAR_TILE
        return jnp.where(wrap, qv + 1, qv), jnp.where(wrap, 0, jv + 1)

    n_near = q_split * (q_split + 1) // 2 + ATT_FAR_TILE * (nq - q_split)
    n_far = nq * (nq + 1) // 2 - n_near
    lax.fori_loop(0, nq, prepare, 0)
    sweep((0, 0), next_near, (nq - 1, nq - 1), n_near, True)
    if n_far:
        sweep((q_split, 0), next_far, (nq - 1, nq - 1 - ATT_FAR_TILE), n_far, False)
    lax.fori_loop(0, nq, finalize, 0)


def _attn_plan(nq):
    for unroll in (8, 4, 2):
        for q_split in range(ATT_FAR_TILE, nq + 1):
            n_near = q_split * (q_split + 1) // 2 + ATT_FAR_TILE * (nq - q_split)
            n_far = nq * (nq + 1) // 2 - n_near
            if n_near % unroll == 0 and n_far % unroll == 0:
                return unroll, q_split
    raise ValueError(f"unsupported number of query blocks: {nq}")


def _attn_call(qt, k, vt, tiles, lam, subln_w, *, batch, lam_init, t=ATT_T):
    T, W = k.shape
    S = T // batch
    H = W // LANES
    nq = S // t
    e = 2 * DIFF_HEAD_DIM
    assert e == LANES and S % t == 0
    unroll, q_split = _attn_plan(nq)
    n_chains = 2 * (t // ATT_STRIP)
    n_tiles = ATT_FAR_TILE + 1
    nbytes = (2 * 4 * (S * LANES * 2) + 2 * n_tiles * t * t * 4 + 2 * S * LANES * 2
              + nq * 2 * (e + ATT_ONES + SUBLANES) * t * 4 + 2 * n_chains * t * ATT_STRIP * 4 + 2 * t * t * 4)
    return pl.pallas_call(
        functools.partial(_attn_kernel, t=t, nq=nq, unroll=unroll, q_split=q_split, lam_init=lam_init),
        grid=(batch, H),
        in_specs=[
            pl.BlockSpec((LANES, S), lambda b, h: (h, b)),
            pl.BlockSpec((S, LANES), lambda b, h: (b, h)),
            pl.BlockSpec((LANES, S), lambda b, h: (h, b)),
            pl.BlockSpec((None, n_tiles, t, t), lambda b, h: (h, 0, 0, 0)),
            pl.BlockSpec((SUBLANES, LANES), lambda b, h: (0, 0)),
            pl.BlockSpec((e, 1), lambda b, h: (0, 0)),
        ],
        out_specs=pl.BlockSpec((S, LANES), lambda b, h: (b, h)),
        out_shape=jax.ShapeDtypeStruct((T, W), BF16),
        scratch_shapes=[pltpu.VMEM((2, LANES, S), BF16),
                        pltpu.VMEM((nq, 2, 1, t), F32), pltpu.VMEM((nq, 2, e + ATT_ONES, t), F32),
                        pltpu.VMEM((2, n_chains, t, ATT_STRIP), F32),
                        pltpu.VMEM((2, n_chains, 1, ATT_STRIP), F32)],
        compiler_params=pltpu.CompilerParams(
            dimension_semantics=("arbitrary", "arbitrary"), vmem_limit_bytes=_vmem_limit(nbytes)),
        name="diff_attn",
    )(qt, k, vt, tiles, lam, subln_w.reshape(e, 1))


def _split3(a):
    hi = a.astype(BF16)
    r1 = a - hi.astype(F32)
    mid = r1.astype(BF16)
    lo = (r1 - mid.astype(F32)).astype(BF16)
    return hi, mid, lo


def _ssd_kernel(xc_ref, z_ref, dt_ref, alog_ref, dsk_ref, nw_ref, y_ref, st_ref, *, lb):
    L = SSM_CHUNK
    gw = HEADS_PER_GROUP * SSM_HEAD_DIM

    @pl.when(pl.program_id(1) == 0)
    def _():
        st_ref[...] = jnp.zeros(st_ref.shape, F32)

    xc = xc_ref[...]
    dt = dt_ref[...]
    a_dt = -jnp.exp(alog_ref[...]) * dt

    row = lax.broadcasted_iota(jnp.int32, (L, L), 0)
    col = lax.broadcasted_iota(jnp.int32, (L, L), 1)
    causal = row >= col
    tri = jnp.where(causal, 1.0, 0.0).astype(BF16)
    lane_g = lax.broadcasted_iota(jnp.int32, (L, gw), 1) // SSM_HEAD_DIM
    lane_lo = lax.broadcasted_iota(jnp.int32, (L, LANES), 1) < SSM_HEAD_DIM
    lane_lo1 = lax.broadcasted_iota(jnp.int32, (1, LANES), 1) < SSM_HEAD_DIM

    for c in range(lb // L):
        r0 = c * L
        hi, mid, lo = _split3(a_dt[r0:r0 + L, :])
        cs = _dot(tri, hi) + _dot(tri, mid) + _dot(tri, lo)
        cs_t = cs.T
        dt_t = dt[r0:r0 + L, :].T
        xs = xc[r0:r0 + L, 0:SSM_WIDTH]
        ys = []
        for g in range(SSM_GROUPS):
            b0 = SSM_WIDTH + g * SSM_STATE
            c0 = SSM_WIDTH + SSM_GROUPS * SSM_STATE + g * SSM_STATE
            bg = xc[r0:r0 + L, b0:b0 + SSM_STATE]
            cg = xc[r0:r0 + L, c0:c0 + SSM_STATE].astype(BF16)
            gmat = _dot_nt(cg, bg.astype(BF16))
            bg_t = bg.T
            xg = xs[:, g * gw:(g + 1) * gw]
            ydiag = None
            st_new = None
            ecols, decs = [], []
            for hh in range(HEADS_PER_GROUP):
                h = g * HEADS_PER_GROUP + hh
                colb = jnp.broadcast_to(cs[:, h:h + 1], (L, L))
                rowb = cs_t[h:h + 1, :]
                dtrow = dt_t[h:h + 1, :]
                decay = jnp.exp(jnp.where(causal, colb - rowb, NEG_BIG))
                mh = (gmat * decay * dtrow).astype(BF16)
                xm = jnp.where(lane_g == hh, xg, 0.0).astype(BF16)
                last = colb[L - 1:L, :]
                wrow = dtrow * jnp.exp(last - rowb)
                btw = (bg_t * wrow).astype(BF16)
                d1 = _dot(mh, xm)
                d2 = _dot(btw, xm)
                ydiag = d1 if ydiag is None else ydiag + d1
                st_new = d2 if st_new is None else st_new + d2
                ecols.append(jnp.exp(colb))
                decs.append(jnp.exp(last))
            e_mat = jnp.concatenate([jnp.where(lane_lo, ecols[0], ecols[1]),
                                     jnp.where(lane_lo, ecols[2], ecols[3])], axis=1)
            dec = jnp.concatenate([jnp.where(lane_lo1, decs[0], decs[1]),
                                   jnp.where(lane_lo1, decs[2], decs[3])], axis=1)
            st_old = st_ref[g]
            ys.append(ydiag + _dot(cg, st_old.astype(BF16)) * e_mat)
            st_ref[g] = st_old * dec + st_new
        y = jnp.concatenate(ys, axis=1) + dsk_ref[...] * xs
        gated = y * z_ref[r0:r0 + L, :]
        outs = []
        for g in range(SSM_GROUPS):
            part = gated[:, g * gw:(g + 1) * gw]
            ms = jnp.mean(part * part, axis=-1, keepdims=True)
            outs.append(part * lax.rsqrt(ms + 1e-5))
        y_ref[r0:r0 + L, :] = (jnp.concatenate(outs, axis=1) * nw_ref[...]).astype(BF16)


def _ssd_call(xc, z, dt, a_log, d_skip, norm_w, *, batch, lb=SSD_LB):
    T = xc.shape[0]
    S = T // batch
    nb = S // lb
    assert S % lb == 0 and lb % SSM_CHUNK == 0
    a_pad = jnp.pad(a_log.astype(F32), (0, LANES - a_log.shape[0])).reshape(1, LANES)
    dsk = jnp.repeat(d_skip.astype(F32), SSM_HEAD_DIM).reshape(1, SSM_WIDTH)
    row = lambda b, j: (b * nb + j, 0)
    const = lambda shape: pl.BlockSpec(shape, lambda b, j: (0, 0))
    nbytes = 3 * lb * (SSM_XBC + SSM_WIDTH + LANES) * 4 + 8 * lb * SSM_XBC * 4
    return pl.pallas_call(
        functools.partial(_ssd_kernel, lb=lb),
        grid=(batch, nb),
        in_specs=[pl.BlockSpec((lb, SSM_XBC), row), pl.BlockSpec((lb, SSM_WIDTH), row),
                  pl.BlockSpec((lb, LANES), row),
                  const((1, LANES)), const((1, SSM_WIDTH)), const((1, SSM_WIDTH))],
        out_specs=pl.BlockSpec((lb, SSM_WIDTH), row),
        out_shape=jax.ShapeDtypeStruct((T, SSM_WIDTH), BF16),
        scratch_shapes=[pltpu.VMEM((SSM_GROUPS, SSM_STATE, HEADS_PER_GROUP * SSM_HEAD_DIM), F32)],
        compiler_params=pltpu.CompilerParams(
            dimension_semantics=("arbitrary", "arbitrary"), vmem_limit_bytes=_vmem_limit(nbytes)),
        name="ssd",
    )(xc, z, dt, a_pad, dsk, norm_w.reshape(1, SSM_WIDTH).astype(F32))


def _sc_kernel(x_ref, xprev_ref, nw_ref, win_ref, cw_ref, wout_ref, o_ref, vpad_ref, bg_ref, *,
               tm, tiles_per_seq):
    i = pl.program_id(0)
    d = x_ref.shape[1]

    @pl.when(i == 0)
    def _():
        vpad_ref[...] = jnp.zeros(vpad_ref.shape, F32)
        bg_ref[...] = jnp.zeros(bg_ref.shape, F32)

    def finish_chunk(prev, c0, c1):
        conv = cw_ref[SC_CONV - 1:SC_CONV, c0:c1] * vpad_ref[prev, HALO:HALO + tm, c0:c1]
        for kk in range(SC_CONV - 1):
            o = HALO - (SC_CONV - 1) + kk
            conv = conv + cw_ref[kk:kk + 1, c0:c1] * vpad_ref[prev, o:o + tm, c0:c1]
        return _dot((bg_ref[prev, :, c0:c1] * conv).astype(BF16), wout_ref[c0:c1, :])

    def stage(cur, prev):
        n_chunks = 4
        cw = d // n_chunks
        h = _rms(x_ref[...], nw_ref[...], 1e-6).astype(BF16)
        bgate = _dot(h, win_ref[:, 0:d])
        out = finish_chunk(prev, 0, cw)
        cgate = _dot(h, win_ref[:, d:2 * d])
        out = out + finish_chunk(prev, cw, 2 * cw)
        u = _dot(h, win_ref[:, 2 * d:3 * d])
        out = out + finish_chunk(prev, 2 * cw, 3 * cw)
        out = out + finish_chunk(prev, 3 * cw, d)
        o_ref[...] = xprev_ref[...] + out

        tail = vpad_ref[prev, tm:tm + HALO, :]
        vpad_ref[cur, 0:HALO, :] = jnp.where(i % tiles_per_seq == 0, jnp.zeros_like(tail), tail)
        vpad_ref[cur, HALO:HALO + tm, :] = cgate * u
        bg_ref[cur] = bgate

    pl.when(i % 2 == 0)(functools.partial(stage, 0, 1))
    pl.when(i % 2 == 1)(functools.partial(stage, 1, 0))


def _sc_call(x, nw, w_in, conv_w, w_out, *, batch, tm=PROJ_TM):
    T, D = x.shape
    S = T // batch
    assert S % tm == 0
    nt = T // tm
    cur = lambda i: (jnp.minimum(i, nt - 1), 0)
    prev = lambda i: (jnp.maximum(i - 1, 0), 0)
    nbytes = (w_in.size + w_out.size) * 2 + 6 * tm * D * 4 + 4 * tm * D * 4 + 8 * tm * D * 4
    return pl.pallas_call(
        functools.partial(_sc_kernel, tm=tm, tiles_per_seq=S // tm),
        grid=(nt + 1,),
        in_specs=[pl.BlockSpec((tm, D), cur), pl.BlockSpec((tm, D), prev), _resident((1, D)),
                  _resident(w_in.shape), _resident(conv_w.shape), _resident(w_out.shape)],
        out_specs=pl.BlockSpec((tm, D), prev),
        out_shape=jax.ShapeDtypeStruct((T, D), F32),
        scratch_shapes=[pltpu.VMEM((2, tm + HALO, D), F32), pltpu.VMEM((2, tm, D), F32)],
        compiler_params=pltpu.CompilerParams(
            dimension_semantics=("arbitrary",), vmem_limit_bytes=_vmem_limit(nbytes)),
        name="short_conv",
    )(x, x, nw.reshape(1, D), w_in, conv_w.astype(F32), w_out)


def kernel(x, rel_bias, final_norm_w, ffn1_norm, ffn1_wg, ffn1_wu, ffn1_wd, mix_norm, ffn2_norm, ffn2_wg, ffn2_wu, ffn2_wd, hyb_w_in, hyb_w_out, diff_lq1, diff_lk1, diff_lq2, diff_lk2, diff_subln_w, ssm_conv_w, ssm_conv_b, ssm_dt_bias, ssm_a_log, ssm_d, ssm_norm_w, sc_w_in, sc_conv_w, sc_w_out):
    B, S, D = x.shape
    T = B * S
    xt = x.reshape(T, D)
    bf = lambda w: w.astype(BF16)

    lam_inits = [0.8 - 0.6 * math.exp(-0.3 * i) for i in range(0, DEPTH, 2)]
    tiles, lams = _prep_call(rel_bias, diff_lq1, diff_lk1, diff_lq2, diff_lk2, lam_inits)

    mix = None
    for i in range(DEPTH):
        j = i // 2
        xt = _ffn_call(xt, ffn1_norm[i], ffn1_wg, ffn1_wu, ffn1_wd, i)
        if i % 2 == 0:
            w_in = hyb_w_in[j]
            qkv_end = 3 * DIFF_WIDTH
            w_pad = bf(jnp.pad(w_in, ((0, 0), (0, LANES - SSM_HEADS))))
            wvt = bf(jnp.concatenate([w_in[:, 0:DIFF_WIDTH], w_in[:, 2 * DIFF_WIDTH:qkv_end]], axis=1).T)
            q, k, vt, z, xc, dt = _hyb_in_call(xt, mix_norm[i], w_pad, wvt, ssm_conv_w[j], ssm_conv_b[j],
                                               ssm_dt_bias[j], batch=B)
            o = _attn_call(q, k, vt, tiles, lams[j * SUBLANES:(j + 1) * SUBLANES], diff_subln_w[j],
                           batch=B, lam_init=lam_inits[j])
            y = _ssd_call(xc, z, dt, ssm_a_log[j], ssm_d[j], ssm_norm_w[j], batch=B)
            mix = (o, y, hyb_w_out, j)
        else:
            xt = _sc_call(xt, mix_norm[i], bf(sc_w_in[j]), sc_conv_w[j], bf(sc_w_out[j]), batch=B)
            mix = None
        xt = _ffn_call(xt, ffn2_norm[i], ffn2_wg, ffn2_wu, ffn2_wd, i, mix=mix,
                       final_w=final_norm_w if i == DEPTH - 1 else None)
    return xt.reshape(B, S, D)
```

```python
import functools
import math

import jax
import jax.numpy as jnp
from jax import lax
from jax.experimental import pallas as pl
from jax.experimental.pallas import tpu as pltpu

F32 = jnp.float32
BF16 = jnp.bfloat16

DEPTH = 4
N_DIFF_HEADS = 4
DIFF_HEAD_DIM = 64
DIFF_WIDTH = N_DIFF_HEADS * 2 * DIFF_HEAD_DIM
NUM_BUCKETS = 32
MAX_DISTANCE = 128
SSM_HEADS = 8
SSM_HEAD_DIM = 64
SSM_WIDTH = SSM_HEADS * SSM_HEAD_DIM
SSM_GROUPS = 2
SSM_STATE = 128
SSM_CONV = 4
SSM_CHUNK = 128
HEADS_PER_GROUP = SSM_HEADS // SSM_GROUPS
SSM_XBC = SSM_WIDTH + 2 * SSM_GROUPS * SSM_STATE
SC_CONV = 3

LANES = 128
SUBLANES = 8
V7X_SCOPED_VMEM_CAP = 60000 * 1024

FFN_TM = 512
FFN_FCHUNK = 256
PROJ_TM = 512
ATT_T = 512
ATT_STRIP = 256
ATT_ONES = 16
ATT_FAR_TILE = 2
SSD_LB = 512
HALO = SUBLANES

NEG_BIG = -1e30
LOG2E = math.log2(math.e)


def _vmem_limit(nbytes):
    return int(min(V7X_SCOPED_VMEM_CAP, nbytes * 5 // 4 + (8 << 20)))


def _dot(a, b):
    return jnp.dot(a, b, preferred_element_type=F32)


def _dot_nt(a, b):
    return lax.dot_general(a, b, (((1,), (1,)), ((), ())), preferred_element_type=F32)


def _rms(x, w, eps):
    ms = jnp.mean(x * x, axis=-1, keepdims=True)
    return x * lax.rsqrt(ms + eps) * w


def _silu(x):
    return x * jax.nn.sigmoid(x)


def _resident(shape):
    nd = len(shape)
    return pl.BlockSpec(shape, lambda *_: (0,) * nd, pipeline_mode=pl.Buffered(1))


def _ffn_kernel(*refs, pre_mix, final_norm, f_total):
    refs = list(refs)
    x_ref = refs.pop(0)
    if pre_mix:
        o_ref, y_ref, wo_ref = refs.pop(0), refs.pop(0), refs.pop(0)
    nw_ref, wg_ref, wu_ref, wd_ref = refs.pop(0), refs.pop(0), refs.pop(0), refs.pop(0)
    if final_norm:
        fw_ref = refs.pop(0)
    out_ref = refs.pop(0)

    x = x_ref[...]
    if pre_mix:
        half = o_ref.shape[1]
        x = (x + _dot(o_ref[...], wo_ref[0:half, :].astype(BF16))
             + _dot(y_ref[...], wo_ref[half:, :].astype(BF16)))
    h = _rms(x, nw_ref[...], 1e-6).astype(BF16)
    acc = None
    for c0 in range(0, f_total, FFN_FCHUNK):
        c1 = c0 + FFN_FCHUNK
        g = _dot(h, wg_ref[:, c0:c1].astype(BF16))
        u = _dot(h, wu_ref[:, c0:c1].astype(BF16))
        a = (_silu(g) * u).astype(BF16)
        d = _dot(a, wd_ref[c0:c1, :].astype(BF16))
        acc = d if acc is None else acc + d
    y = x + 0.5 * acc
    if final_norm:
        y = _rms(y, fw_ref[...], 1e-6)
    out_ref[...] = y


def _layer_resident(stacked, layer):
    _, r, c = stacked.shape
    return pl.BlockSpec((None, r, c), lambda *_: (layer, 0, 0), pipeline_mode=pl.Buffered(1))


def _ffn_call(x, nw, wg, wu, wd, layer, *, mix=None, final_w=None, tm=FFN_TM):
    T, D = x.shape
    F = wg.shape[2]
    assert T % tm == 0 and F % FFN_FCHUNK == 0
    row = lambda i: (i, 0)
    args = [x]
    in_specs = [pl.BlockSpec((tm, D), row)]
    nbytes = 4 * tm * D * 4 + 3 * D * F * 4 + 6 * tm * FFN_FCHUNK * 4 + 3 * tm * D * 4
    if mix is not None:
        o, y, wo, mix_layer = mix
        args += [o, y, wo]
        in_specs += [pl.BlockSpec((tm, o.shape[1]), row), pl.BlockSpec((tm, y.shape[1]), row),
                     _layer_resident(wo, mix_layer)]
        nbytes += wo.shape[1] * wo.shape[2] * 4 + 4 * tm * o.shape[1] * 2 * 2
    args += [nw.reshape(1, D), wg, wu, wd]
    in_specs += [_resident((1, D)), _layer_resident(wg, layer), _layer_resident(wu, layer),
                 _layer_resident(wd, layer)]
    if final_w is not None:
        args.append(final_w.reshape(1, D))
        in_specs.append(_resident((1, D)))
    kern = functools.partial(_ffn_kernel, pre_mix=mix is not None, final_norm=final_w is not None,
                             f_total=F)
    return pl.pallas_call(
        kern,
        grid=(T // tm,),
        in_specs=in_specs,
        out_specs=pl.BlockSpec((tm, D), row),
        out_shape=jax.ShapeDtypeStruct((T, D), F32),
        compiler_params=pltpu.CompilerParams(
            dimension_semantics=("arbitrary",), vmem_limit_bytes=_vmem_limit(nbytes)),
        name="ffn",
    )(*args)


def _hyb_in_kernel(x_ref, nw_ref, w_ref, wvt_ref, cw_ref, cb_ref, dtb_ref,
                   q_ref, k_ref, vt_ref, z_ref, xc_ref, dt_ref, xpad_ref, *, tm, tiles_per_seq):
    @pl.when(pl.program_id(0) % tiles_per_seq == 0)
    def _():
        xpad_ref[0:HALO, :] = jnp.zeros((HALO, SSM_XBC), F32)

    h = _rms(x_ref[...], nw_ref[...], 1e-6).astype(BF16)
    qw = DIFF_WIDTH
    c_z = 2 * qw + DIFF_WIDTH
    c_xbc = c_z + SSM_WIDTH
    c_dt = c_xbc + SSM_XBC
    xpad_ref[HALO:HALO + tm, :] = _dot(h, w_ref[:, c_xbc:c_xbc + SSM_XBC])
    scale = DIFF_HEAD_DIM ** -0.5 * LOG2E
    q_ref[...] = (_dot_nt(wvt_ref[0:qw, :], h) * scale).astype(BF16)
    k_ref[...] = _dot(h, w_ref[:, qw:2 * qw]).astype(BF16)
    vt_ref[...] = _dot_nt(wvt_ref[qw:, :], h).astype(BF16)
    z_ref[...] = _silu(_dot(h, w_ref[:, c_z:c_z + SSM_WIDTH]))
    dt_raw = _dot(h, w_ref[:, c_dt:c_dt + LANES]) + dtb_ref[...]
    dt_ref[...] = jnp.maximum(dt_raw, 0.0) + jnp.log1p(jnp.exp(-jnp.abs(dt_raw)))

    conv = cb_ref[...]
    for kk in range(SSM_CONV):
        o = HALO - (SSM_CONV - 1) + kk
        conv = conv + cw_ref[kk:kk + 1, :] * xpad_ref[o:o + tm, :]
    xc_ref[...] = _silu(conv)
    xpad_ref[0:HALO, :] = xpad_ref[tm:tm + HALO, :]


def _hyb_in_call(x, nw, w_pad, wvt, conv_w, conv_b, dt_bias, *, batch, tm=PROJ_TM):
    T, D = x.shape
    S = T // batch
    assert S % tm == 0
    row = lambda i: (i, 0)
    outs = [
        jax.ShapeDtypeStruct((DIFF_WIDTH, T), BF16),
        jax.ShapeDtypeStruct((T, DIFF_WIDTH), BF16),
        jax.ShapeDtypeStruct((DIFF_WIDTH, T), BF16),
        jax.ShapeDtypeStruct((T, SSM_WIDTH), F32),
        jax.ShapeDtypeStruct((T, SSM_XBC), F32),
        jax.ShapeDtypeStruct((T, LANES), F32),
    ]
    out_specs = [
        pl.BlockSpec((DIFF_WIDTH, tm), lambda i: (0, i)), pl.BlockSpec((tm, DIFF_WIDTH), row),
        pl.BlockSpec((DIFF_WIDTH, tm), lambda i: (0, i)),
        pl.BlockSpec((tm, SSM_WIDTH), row), pl.BlockSpec((tm, SSM_XBC), row),
        pl.BlockSpec((tm, LANES), row),
    ]
    nbytes = ((w_pad.size + wvt.size) * 2 + 2 * tm * D * 4
              + 3 * tm * 4 * (3 * DIFF_WIDTH + SSM_WIDTH + SSM_XBC + LANES) + 3 * tm * SSM_XBC * 4)
    pad = jnp.pad(dt_bias.astype(F32), (0, LANES - dt_bias.shape[0])).reshape(1, LANES)
    return pl.pallas_call(
        functools.partial(_hyb_in_kernel, tm=tm, tiles_per_seq=S // tm),
        grid=(T // tm,),
        in_specs=[pl.BlockSpec((tm, D), row), _resident((1, D)), _resident(w_pad.shape),
                  _resident(wvt.shape), _resident((SSM_CONV, SSM_XBC)), _resident((1, SSM_XBC)),
                  _resident((1, LANES))],
        out_specs=out_specs,
        out_shape=outs,
        scratch_shapes=[pltpu.VMEM((tm + HALO, SSM_XBC), F32)],
        compiler_params=pltpu.CompilerParams(
            dimension_semantics=("arbitrary",), vmem_limit_bytes=_vmem_limit(nbytes)),
        name="hyb_in",
    )(x, nw.reshape(1, D), w_pad, wvt, conv_w.astype(F32), conv_b.reshape(1, SSM_XBC).astype(F32), pad)


def _prep_kernel(tab_ref, lq1_ref, lk1_ref, lq2_ref, lk2_ref, tile_ref, lam_ref, *, t, lam_inits):
    hd = pl.program_id(0)
    key = lax.broadcasted_iota(jnp.int32, (t, t), 0)
    qry = lax.broadcasted_iota(jnp.int32, (t, t), 1)
    max_exact = NUM_BUCKETS // 2
    far = tab_ref[hd, NUM_BUCKETS - 1]
    tile_ref[ATT_FAR_TILE] = jnp.zeros((t, t), F32)
    for off in range(ATT_FAR_TILE):
        dist = off * t + qry - key
        d = jnp.maximum(dist, 0)
        large = max_exact + (
            jnp.log(jnp.maximum(d, 1).astype(F32) / max_exact)
            / math.log(MAX_DISTANCE / max_exact) * (NUM_BUCKETS - max_exact)).astype(jnp.int32)
        bucket = jnp.where(d < max_exact, d, jnp.minimum(large, NUM_BUCKETS - 1))
        bias = jnp.full((t, t), far, F32)
        for b in range(NUM_BUCKETS - 1):
            bias = jnp.where(bucket == b, tab_ref[hd, b], bias)
        val = (bias - far) * LOG2E
        if off == 0:
            val = jnp.where(dist >= 0, val, NEG_BIG)
        tile_ref[off] = val
    for j, lam_init in enumerate(lam_inits):
        s1 = jnp.sum(lq1_ref[j:j + 1, :] * lk1_ref[j:j + 1, :], axis=-1, keepdims=True)
        s2 = jnp.sum(lq2_ref[j:j + 1, :] * lk2_ref[j:j + 1, :], axis=-1, keepdims=True)
        lam = jnp.exp(s1) - jnp.exp(s2) + lam_init
        lam_ref[j * SUBLANES:(j + 1) * SUBLANES, :] = jnp.broadcast_to(lam, (SUBLANES, LANES))


def _prep_call(rel_bias, lq1, lk1, lq2, lk2, lam_inits, *, t=ATT_T):
    ne = lq1.shape[0]
    H = rel_bias.shape[1]
    whole = lambda shape: pl.BlockSpec(shape, lambda h: (0,) * len(shape))
    return pl.pallas_call(
        functools.partial(_prep_kernel, t=t, lam_inits=tuple(lam_inits)),
        grid=(H,),
        in_specs=[pl.BlockSpec(memory_space=pltpu.SMEM)] + [whole(lq1.shape)] * 4,
        out_specs=[pl.BlockSpec((None, ATT_FAR_TILE + 1, t, t), lambda h: (h, 0, 0, 0)),
                   whole((ne * SUBLANES, LANES))],
        out_shape=[jax.ShapeDtypeStruct((H, ATT_FAR_TILE + 1, t, t), F32),
                   jax.ShapeDtypeStruct((ne * SUBLANES, LANES), F32)],
        compiler_params=pltpu.CompilerParams(
            dimension_semantics=("arbitrary",), vmem_limit_bytes=_vmem_limit(12 * t * t * 4)),
        name="attn_prep",
    )(rel_bias.T.astype(F32), lq1, lk1, lq2, lk2)


def _attn_kernel(qt_ref, k_ref, vt_ref, tile_ref, lam_ref, sw_ref, o_ref,
                 qm_ref, m_ref, acc_ref, s_ref, smax_ref, *, t, nq, unroll, q_split, lam_init):
    e = 2 * DIFF_HEAD_DIM
    first = lax.broadcasted_iota(jnp.int32, (LANES, t), 0) < DIFF_HEAD_DIM
    ones = jnp.ones((ATT_ONES, t), BF16)
    chains = [(w, slice(c0, c0 + ATT_STRIP)) for c0 in range(0, t, ATT_STRIP) for w in range(2)]

    def rows(blk):
        return pl.ds(pl.multiple_of(blk * t, t), t)

    def prepare(qi, carry):
        q = qt_ref[:, rows(qi)]
        zero = jnp.zeros_like(q)
        qm_ref[0, :, rows(qi)] = jnp.where(first, q, zero)
        qm_ref[1, :, rows(qi)] = jnp.where(first, zero, q)
        m_ref[qi] = jnp.full(m_ref.shape[1:], NEG_BIG, F32)
        acc_ref[qi] = jnp.zeros(acc_ref.shape[1:], F32)
        return carry

    def score_chain(ci, qi, j, slot, k, biased):
        w, cols = chains[ci]
        q_cols = pl.ds(pl.multiple_of(qi * t + cols.start, ATT_STRIP), ATT_STRIP)
        s = _dot(k, qm_ref[w, :, q_cols])
        if biased:
            near = jnp.minimum(qi - j, ATT_FAR_TILE)
            s = s + tile_ref[near, :, cols]
        s_ref[slot, ci] = s
        smax_ref[slot, ci] = jnp.max(s, axis=0, keepdims=True)

    def value_chain(ci, qi, slot, vta):
        w, cols = chains[ci]
        m_old = m_ref[qi, w, :, cols]
        m_new = jnp.maximum(m_old, smax_ref[slot, ci])
        p = jnp.exp2(s_ref[slot, ci] - m_new).astype(BF16)
        acc_ref[qi, w, :, cols] = jnp.exp2(m_old - m_new) * acc_ref[qi, w, :, cols] + _dot(vta, p)
        m_ref[qi, w, :, cols] = m_new

    def finalize(qi, carry):
        lam = lam_ref[0:1, 0:1]
        o1 = acc_ref[qi, 0, 0:e, :] * (1.0 / acc_ref[qi, 0, e:e + 1, :])
        o2 = acc_ref[qi, 1, 0:e, :] * (1.0 / acc_ref[qi, 1, e:e + 1, :])
        out_t = o1 - lam * o2
        ms = jnp.mean(out_t * out_t, axis=0, keepdims=True)
        out_t = out_t * lax.rsqrt(ms + 1e-5) * sw_ref[...] * (1.0 - lam_init)
        o_ref[rows(qi), :] = out_t.T.astype(BF16)
        return carry

    def sweep(first_pair, next_pair, last_pair, count, biased):
        def step(qv, jv, slot):
            qs, js = next_pair(qv, jv)
            past_end = qs >= nq
            qs = jnp.where(past_end, last_pair[0], qs)
            js = jnp.where(past_end, last_pair[1], js)
            k_next = k_ref[rows(js), :]
            vta = jnp.concatenate([vt_ref[:, rows(jv)], ones], axis=0)
            for ci in range(len(chains)):
                score_chain(ci, qs, js, 1 - slot, k_next, biased)
                value_chain(ci, qv, slot, vta)
            return qs, js

        def body(_, carry):
            qv, jv = carry
            for u in range(unroll):
                qv, jv = step(qv, jv, u % 2)
            return qv, jv

        k0 = k_ref[rows(first_pair[1]), :]
        for ci in range(len(chains)):
            score_chain(ci, first_pair[0], first_pair[1], 0, k0, biased)
        lax.fori_loop(0, count // unroll, body, (jnp.int32(first_pair[0]), jnp.int32(first_pair[1])))

    def next_near(qv, jv):
        wrap = jv == qv
        qn = qv + 1
        return jnp.where(wrap, qn, qv), jnp.where(wrap, jnp.where(qn < q_split, 0, qn - 1), jv + 1)

    def next_far(qv, jv):
        wrap = jv == qv - ATT_FAR_TILE
        return jnp.where(wrap, qv + 1, qv), jnp.where(wrap, 0, jv + 1)

    n_near = q_split * (q_split + 1) // 2 + ATT_FAR_TILE * (nq - q_split)
    n_far = nq * (nq + 1) // 2 - n_near
    lax.fori_loop(0, nq, prepare, 0)
    sweep((0, 0), next_near, (nq - 1, nq - 1), n_near, True)
    if n_far:
        sweep((q_split, 0), next_far, (nq - 1, nq - 1 - ATT_FAR_TILE), n_far, False)
    lax.fori_loop(0, nq, finalize, 0)


def _attn_plan(nq):
    for unroll in (8, 4, 2):
        for q_split in range(ATT_FAR_TILE, nq + 1):
            n_near = q_split * (q_split + 1) // 2 + ATT_FAR_TILE * (nq - q_split)
            n_far = nq * (nq + 1) // 2 - n_near
            if n_near % unroll == 0 and n_far % unroll == 0:
                return unroll, q_split
    raise ValueError(f"unsupported number of query blocks: {nq}")


def _attn_call(qt, k, vt, tiles, lam, subln_w, *, batch, lam_init, t=ATT_T):
    T, W = k.shape
    S = T // batch
    H = W // LANES
    nq = S // t
    e = 2 * DIFF_HEAD_DIM
    assert e == LANES and S % t == 0
    unroll, q_split = _attn_plan(nq)
    n_chains = 2 * (t // ATT_STRIP)
    n_tiles = ATT_FAR_TILE + 1
    nbytes = (2 * 4 * (S * LANES * 2) + 2 * n_tiles * t * t * 4 + 2 * S * LANES * 2
              + nq * 2 * (e + ATT_ONES + SUBLANES) * t * 4 + 2 * n_chains * t * ATT_STRIP * 4 + 2 * t * t * 4)
    return pl.pallas_call(
        functools.partial(_attn_kernel, t=t, nq=nq, unroll=unroll, q_split=q_split, lam_init=lam_init),
        grid=(batch, H),
        in_specs=[
            pl.BlockSpec((LANES, S), lambda b, h: (h, b)),
            pl.BlockSpec((S, LANES), lambda b, h: (b, h)),
            pl.BlockSpec((LANES, S), lambda b, h: (h, b)),
            pl.BlockSpec((None, n_tiles, t, t), lambda b, h: (h, 0, 0, 0)),
            pl.BlockSpec((SUBLANES, LANES), lambda b, h: (0, 0)),
            pl.BlockSpec((e, 1), lambda b, h: (0, 0)),
        ],
        out_specs=pl.BlockSpec((S, LANES), lambda b, h: (b, h)),
        out_shape=jax.ShapeDtypeStruct((T, W), BF16),
        scratch_shapes=[pltpu.VMEM((2, LANES, S), BF16),
                        pltpu.VMEM((nq, 2, 1, t), F32), pltpu.VMEM((nq, 2, e + ATT_ONES, t), F32),
                        pltpu.VMEM((2, n_chains, t, ATT_STRIP), F32),
                        pltpu.VMEM((2, n_chains, 1, ATT_STRIP), F32)],
        compiler_params=pltpu.CompilerParams(
            dimension_semantics=("arbitrary", "arbitrary"), vmem_limit_bytes=_vmem_limit(nbytes)),
        name="diff_attn",
    )(qt, k, vt, tiles, lam, subln_w.reshape(e, 1))


def _split3(a):
    hi = a.astype(BF16)
    r1 = a - hi.astype(F32)
    mid = r1.astype(BF16)
    lo = (r1 - mid.astype(F32)).astype(BF16)
    return hi, mid, lo


def _ssd_kernel(xc_ref, z_ref, dt_ref, alog_ref, dsk_ref, nw_ref, y_ref, st_ref, *, lb):
    L = SSM_CHUNK
    gw = HEADS_PER_GROUP * SSM_HEAD_DIM

    @pl.when(pl.program_id(1) == 0)
    def _():
        st_ref[...] = jnp.zeros(st_ref.shape, F32)

    xc = xc_ref[...]
    dt = dt_ref[...]
    a_dt = -jnp.exp(alog_ref[...]) * dt

    row = lax.broadcasted_iota(jnp.int32, (L, L), 0)
    col = lax.broadcasted_iota(jnp.int32, (L, L), 1)
    causal = row >= col
    tri = jnp.where(causal, 1.0, 0.0).astype(BF16)
    lane_g = lax.broadcasted_iota(jnp.int32, (L, gw), 1) // SSM_HEAD_DIM
    lane_lo = lax.broadcasted_iota(jnp.int32, (L, LANES), 1) < SSM_HEAD_DIM
    lane_lo1 = lax.broadcasted_iota(jnp.int32, (1, LANES), 1) < SSM_HEAD_DIM

    for c in range(lb // L):
        r0 = c * L
        hi, mid, lo = _split3(a_dt[r0:r0 + L, :])
        cs = _dot(tri, hi) + _dot(tri, mid) + _dot(tri, lo)
        cs_t = cs.T
        dt_t = dt[r0:r0 + L, :].T
        xs = xc[r0:r0 + L, 0:SSM_WIDTH]
        ys = []
        for g in range(SSM_GROUPS):
            b0 = SSM_WIDTH + g * SSM_STATE
            c0 = SSM_WIDTH + SSM_GROUPS * SSM_STATE + g * SSM_STATE
            bg = xc[r0:r0 + L, b0:b0 + SSM_STATE]
            cg = xc[r0:r0 + L, c0:c0 + SSM_STATE].astype(BF16)
            gmat = _dot_nt(cg, bg.astype(BF16))
            bg_t = bg.T
            xg = xs[:, g * gw:(g + 1) * gw]
            ydiag = None
            st_new = None
            ecols, decs = [], []
            for hh in range(HEADS_PER_GROUP):
                h = g * HEADS_PER_GROUP + hh
                colb = jnp.broadcast_to(cs[:, h:h + 1], (L, L))
                rowb = cs_t[h:h + 1, :]
                dtrow = dt_t[h:h + 1, :]
                decay = jnp.exp(jnp.where(causal, colb - rowb, NEG_BIG))
                mh = (gmat * decay * dtrow).astype(BF16)
                xm = jnp.where(lane_g == hh, xg, 0.0).astype(BF16)
                last = colb[L - 1:L, :]
                wrow = dtrow * jnp.exp(last - rowb)
                btw = (bg_t * wrow).astype(BF16)
                d1 = _dot(mh, xm)
                d2 = _dot(btw, xm)
                ydiag = d1 if ydiag is None else ydiag + d1
                st_new = d2 if st_new is None else st_new + d2
                ecols.append(jnp.exp(colb))
                decs.append(jnp.exp(last))
            e_mat = jnp.concatenate([jnp.where(lane_lo, ecols[0], ecols[1]),
                                     jnp.where(lane_lo, ecols[2], ecols[3])], axis=1)
            dec = jnp.concatenate([jnp.where(lane_lo1, decs[0], decs[1]),
                                   jnp.where(lane_lo1, decs[2], decs[3])], axis=1)
            st_old = st_ref[g]
            ys.append(ydiag + _dot(cg, st_old.astype(BF16)) * e_mat)
            st_ref[g] = st_old * dec + st_new
        y = jnp.concatenate(ys, axis=1) + dsk_ref[...] * xs
        gated = y * z_ref[r0:r0 + L, :]
        outs = []
        for g in range(SSM_GROUPS):
            part = gated[:, g * gw:(g + 1) * gw]
            ms = jnp.mean(part * part, axis=-1, keepdims=True)
            outs.append(part * lax.rsqrt(ms + 1e-5))
        y_ref[r0:r0 + L, :] = (jnp.concatenate(outs, axis=1) * nw_ref[...]).astype(BF16)


def _ssd_call(xc, z, dt, a_log, d_skip, norm_w, *, batch, lb=SSD_LB):
    T = xc.shape[0]
    S = T // batch
    nb = S // lb
    assert S % lb == 0 and lb % SSM_CHUNK == 0
    a_pad = jnp.pad(a_log.astype(F32), (0, LANES - a_log.shape[0])).reshape(1, LANES)
    dsk = jnp.repeat(d_skip.astype(F32), SSM_HEAD_DIM).reshape(1, SSM_WIDTH)
    row = lambda b, j: (b * nb + j, 0)
    const = lambda shape: pl.BlockSpec(shape, lambda b, j: (0, 0))
    nbytes = 3 * lb * (SSM_XBC + SSM_WIDTH + LANES) * 4 + 8 * lb * SSM_XBC * 4
    return pl.pallas_call(
        functools.partial(_ssd_kernel, lb=lb),
        grid=(batch, nb),
        in_specs=[pl.BlockSpec((lb, SSM_XBC), row), pl.BlockSpec((lb, SSM_WIDTH), row),
                  pl.BlockSpec((lb, LANES), row),
                  const((1, LANES)), const((1, SSM_WIDTH)), const((1, SSM_WIDTH))],
        out_specs=pl.BlockSpec((lb, SSM_WIDTH), row),
        out_shape=jax.ShapeDtypeStruct((T, SSM_WIDTH), BF16),
        scratch_shapes=[pltpu.VMEM((SSM_GROUPS, SSM_STATE, HEADS_PER_GROUP * SSM_HEAD_DIM), F32)],
        compiler_params=pltpu.CompilerParams(
            dimension_semantics=("arbitrary", "arbitrary"), vmem_limit_bytes=_vmem_limit(nbytes)),
        name="ssd",
    )(xc, z, dt, a_pad, dsk, norm_w.reshape(1, SSM_WIDTH).astype(F32))


def _hyb_kernel(x_ref, nw_ref, w_ref, wvt_ref, cw_ref, cb_ref, dtb_ref, alog_ref, dsk_ref, ynw_ref,
                qt_ref, k_ref, vt_ref, y_ref, xpad_ref, z_scr, dt_scr, st_ref, *, tm, tiles_per_seq):
    i = pl.program_id(0)
    L = SSM_CHUNK
    gw = HEADS_PER_GROUP * SSM_HEAD_DIM
    qw = DIFF_WIDTH
    c_z = 2 * qw + DIFF_WIDTH
    c_xbc = c_z + SSM_WIDTH
    c_dt = c_xbc + SSM_XBC

    @pl.when(i == 0)
    def _():
        xpad_ref[...] = jnp.zeros(xpad_ref.shape, F32)
        z_scr[...] = jnp.zeros(z_scr.shape, F32)
        dt_scr[...] = jnp.zeros(dt_scr.shape, F32)
        st_ref[...] = jnp.zeros(st_ref.shape, F32)

    def stage(cur, prev):
        row = lax.broadcasted_iota(jnp.int32, (L, L), 0)
        col = lax.broadcasted_iota(jnp.int32, (L, L), 1)
        causal = row >= col
        tri = jnp.where(causal, 1.0, 0.0).astype(BF16)
        lane_g = lax.broadcasted_iota(jnp.int32, (L, gw), 1) // SSM_HEAD_DIM
        lane_lo = lax.broadcasted_iota(jnp.int32, (L, LANES), 1) < SSM_HEAD_DIM
        lane_lo1 = lax.broadcasted_iota(jnp.int32, (1, LANES), 1) < SSM_HEAD_DIM
        a_neg = -jnp.exp(alog_ref[...])
        keep_state = jnp.where((i - 1) % tiles_per_seq == 0, 0.0, 1.0)

        def ssd_chunk(c):
            r0 = c * L
            dt = dt_scr[prev, r0:r0 + L, :]
            hi, mid, lo = _split3(a_neg * dt)
            cs = _dot(tri, hi) + _dot(tri, mid) + _dot(tri, lo)
            yield

            conv = cb_ref[...]
            for kk in range(SSM_CONV):
                o = HALO - (SSM_CONV - 1) + kk + r0
                conv = conv + cw_ref[kk:kk + 1, :] * xpad_ref[prev, o:o + L, :]
            xc = _silu(conv)
            xs = xc[:, 0:SSM_WIDTH]
            bgs, cgs, gmats = [], [], []
            for g in range(SSM_GROUPS):
                b0 = SSM_WIDTH + g * SSM_STATE
                c0 = SSM_WIDTH + SSM_GROUPS * SSM_STATE + g * SSM_STATE
                bgs.append(xc[:, b0:b0 + SSM_STATE])
                cgs.append(xc[:, c0:c0 + SSM_STATE].astype(BF16))
                gmats.append(_dot_nt(cgs[g], bgs[g].astype(BF16)))
            yield

            cs_t = cs.T
            dt_t = dt.T
            ydiags, st_news, e_mats, decs_g = [], [], [], []
            for g in range(SSM_GROUPS):
                bg_t = bgs[g].T
                xg = xs[:, g * gw:(g + 1) * gw]
                ydiag = None
                st_new = None
                ecols, decs = [], []
                for hh in range(HEADS_PER_GROUP):
                    h = g * HEADS_PER_GROUP + hh
                    colb = jnp.broadcast_to(cs[:, h:h + 1], (L, L))
                    rowb = cs_t[h:h + 1, :]
                    dtrow = dt_t[h:h + 1, :]
                    decay = jnp.exp(jnp.where(causal, colb - rowb, NEG_BIG))
                    mh = (gmats[g] * decay * dtrow).astype(BF16)
                    xm = jnp.where(lane_g == hh, xg, 0.0).astype(BF16)
                    last = colb[L - 1:L, :]
                    wrow = dtrow * jnp.exp(last - rowb)
                    btw = (bg_t * wrow).astype(BF16)
                    d1 = _dot(mh, xm)
                    d2 = _dot(btw, xm)
                    ydiag = d1 if ydiag is None else ydiag + d1
                    st_new = d2 if st_new is None else st_new + d2
                    ecols.append(jnp.exp(colb))
                    decs.append(jnp.exp(last))
                ydiags.append(ydiag)
                st_news.append(st_new)
                e_mats.append(jnp.concatenate([jnp.where(lane_lo, ecols[0], ecols[1]),
                                               jnp.where(lane_lo, ecols[2], ecols[3])], axis=1))
                decs_g.append(jnp.concatenate([jnp.where(lane_lo1, decs[0], decs[1]),
                                               jnp.where(lane_lo1, decs[2], decs[3])], axis=1))
                if g == 0:
                    yield

            ys = []
            for g in range(SSM_GROUPS):
                st_old = st_ref[g] * keep_state if c == 0 else st_ref[g]
                ys.append(ydiags[g] + _dot(cgs[g], st_old.astype(BF16)) * e_mats[g])
                st_ref[g] = st_old * decs_g[g] + st_news[g]
            y = jnp.concatenate(ys, axis=1) + dsk_ref[...] * xs
            gated = y * z_scr[prev, r0:r0 + L, :]
            outs = []
            for g in range(SSM_GROUPS):
                part = gated[:, g * gw:(g + 1) * gw]
                ms = jnp.mean(part * part, axis=-1, keepdims=True)
                outs.append(part * lax.rsqrt(ms + 1e-5))
            y_ref[r0:r0 + L, :] = (jnp.concatenate(outs, axis=1) * ynw_ref[...]).astype(BF16)
            yield

        h = _rms(x_ref[...], nw_ref[...], 1e-6).astype(BF16)
        scale = DIFF_HEAD_DIM ** -0.5 * LOG2E
        piece = 2 * LANES

        def pieces():
            for r in range(0, qw, piece):
                qt_ref[r:r + piece, :] = (_dot_nt(wvt_ref[r:r + piece, :], h) * scale).astype(BF16)
                yield
            for c in range(0, qw, piece):
                k_ref[:, c:c + piece] = _dot(h, w_ref[:, qw + c:qw + c + piece]).astype(BF16)
                yield
            for r in range(0, DIFF_WIDTH, piece):
                vt_ref[r:r + piece, :] = _dot_nt(wvt_ref[qw + r:qw + r + piece, :], h).astype(BF16)
                yield
            for c in range(0, SSM_WIDTH, piece):
                z_scr[cur, :, c:c + piece] = _silu(_dot(h, w_ref[:, c_z + c:c_z + c + piece]))
                yield
            dt_raw = _dot(h, w_ref[:, c_dt:c_dt + LANES]) + dtb_ref[...]
            dt_scr[cur] = jnp.maximum(dt_raw, 0.0) + jnp.log1p(jnp.exp(-jnp.abs(dt_raw)))
            yield
            tail = xpad_ref[prev, tm:tm + HALO, :]
            xpad_ref[cur, 0:HALO, :] = jnp.where(i % tiles_per_seq == 0, jnp.zeros_like(tail), tail)
            for c in range(0, SSM_XBC, piece):
                xpad_ref[cur, HALO:HALO + tm, c:c + piece] = _dot(h, w_ref[:, c_xbc + c:c_xbc + c + piece])
                yield

        proj = pieces()
        for c in range(tm // L):
            for _ in ssd_chunk(c):
                next(proj, None)
        for _ in proj:
            pass

    pl.when(i % 2 == 0)(functools.partial(stage, 0, 1))
    pl.when(i % 2 == 1)(functools.partial(stage, 1, 0))


def _hyb_call(x, nw, w_pad, wvt, conv_w, conv_b, dt_bias, a_log, d_skip, norm_w, *, batch, tm=PROJ_TM):
    T, D = x.shape
    S = T // batch
    assert S % tm == 0 and tm % SSM_CHUNK == 0
    nt = T // tm
    cur = lambda i: (jnp.minimum(i, nt - 1), 0)
    cur_t = lambda i: (0, jnp.minimum(i, nt - 1))
    prev = lambda i: (jnp.maximum(i - 1, 0), 0)
    pad = lambda v: jnp.pad(v.astype(F32), (0, LANES - v.shape[0])).reshape(1, LANES)
    dsk = jnp.repeat(d_skip.astype(F32), SSM_HEAD_DIM).reshape(1, SSM_WIDTH)
    outs = [
        jax.ShapeDtypeStruct((DIFF_WIDTH, T), BF16),
        jax.ShapeDtypeStruct((T, DIFF_WIDTH), BF16),
        jax.ShapeDtypeStruct((DIFF_WIDTH, T), BF16),
        jax.ShapeDtypeStruct((T, SSM_WIDTH), BF16),
    ]
    out_specs = [pl.BlockSpec((DIFF_WIDTH, tm), cur_t), pl.BlockSpec((tm, DIFF_WIDTH), cur),
                 pl.BlockSpec((DIFF_WIDTH, tm), cur_t), pl.BlockSpec((tm, SSM_WIDTH), prev)]
    nbytes = ((w_pad.size + wvt.size) * 2 + 2 * tm * D * 4 + 2 * tm * 2 * 4 * DIFF_WIDTH
              + 2 * (tm + HALO) * SSM_XBC * 4 + 2 * tm * (SSM_WIDTH + LANES) * 4 + 16 * tm * SSM_XBC * 4)
    return pl.pallas_call(
        functools.partial(_hyb_kernel, tm=tm, tiles_per_seq=S // tm),
        grid=(nt + 1,),
        in_specs=[pl.BlockSpec((tm, D), cur), _resident((1, D)), _resident(w_pad.shape),
                  _resident(wvt.shape), _resident((SSM_CONV, SSM_XBC)), _resident((1, SSM_XBC)),
                  _resident((1, LANES)), _resident((1, LANES)), _resident((1, SSM_WIDTH)),
                  _resident((1, SSM_WIDTH))],
        out_specs=out_specs,
        out_shape=outs,
        scratch_shapes=[pltpu.VMEM((2, tm + HALO, SSM_XBC), F32), pltpu.VMEM((2, tm, SSM_WIDTH), F32),
                        pltpu.VMEM((2, tm, LANES), F32),
                        pltpu.VMEM((SSM_GROUPS, SSM_STATE, HEADS_PER_GROUP * SSM_HEAD_DIM), F32)],
        compiler_params=pltpu.CompilerParams(
            dimension_semantics=("arbitrary",), vmem_limit_bytes=_vmem_limit(nbytes)),
        name="hyb_ssd",
    )(x, nw.reshape(1, D), w_pad, wvt, conv_w.astype(F32), conv_b.reshape(1, SSM_XBC).astype(F32),
      pad(dt_bias), pad(a_log), dsk, norm_w.reshape(1, SSM_WIDTH).astype(F32))


def _sc_kernel(x_ref, xprev_ref, nw_ref, win_ref, cw_ref, wout_ref, o_ref, vpad_ref, bg_ref, *,
               tm, tiles_per_seq):
    i = pl.program_id(0)
    d = x_ref.shape[1]

    @pl.when(i == 0)
    def _():
        vpad_ref[...] = jnp.zeros(vpad_ref.shape, F32)
        bg_ref[...] = jnp.zeros(bg_ref.shape, F32)

    def finish_chunk(prev, c0, c1):
        conv = cw_ref[SC_CONV - 1:SC_CONV, c0:c1] * vpad_ref[prev, HALO:HALO + tm, c0:c1]
        for kk in range(SC_CONV - 1):
            o = HALO - (SC_CONV - 1) + kk
            conv = conv + cw_ref[kk:kk + 1, c0:c1] * vpad_ref[prev, o:o + tm, c0:c1]
        return _dot((bg_ref[prev, :, c0:c1] * conv).astype(BF16), wout_ref[c0:c1, :])

    def stage(cur, prev):
        n_chunks = 4
        cw = d // n_chunks
        h = _rms(x_ref[...], nw_ref[...], 1e-6).astype(BF16)
        bgate = _dot(h, win_ref[:, 0:d])
        out = finish_chunk(prev, 0, cw)
        cgate = _dot(h, win_ref[:, d:2 * d])
        out = out + finish_chunk(prev, cw, 2 * cw)
        u = _dot(h, win_ref[:, 2 * d:3 * d])
        out = out + finish_chunk(prev, 2 * cw, 3 * cw)
        out = out + finish_chunk(prev, 3 * cw, d)
        o_ref[...] = xprev_ref[...] + out

        tail = vpad_ref[prev, tm:tm + HALO, :]
        vpad_ref[cur, 0:HALO, :] = jnp.where(i % tiles_per_seq == 0, jnp.zeros_like(tail), tail)
        vpad_ref[cur, HALO:HALO + tm, :] = cgate * u
        bg_ref[cur] = bgate

    pl.when(i % 2 == 0)(functools.partial(stage, 0, 1))
    pl.when(i % 2 == 1)(functools.partial(stage, 1, 0))


def _sc_call(x, nw, w_in, conv_w, w_out, *, batch, tm=PROJ_TM):
    T, D = x.shape
    S = T // batch
    assert S % tm == 0
    nt = T // tm
    cur = lambda i: (jnp.minimum(i, nt - 1), 0)
    prev = lambda i: (jnp.maximum(i - 1, 0), 0)
    nbytes = (w_in.size + w_out.size) * 2 + 6 * tm * D * 4 + 4 * tm * D * 4 + 8 * tm * D * 4
    return pl.pallas_call(
        functools.partial(_sc_kernel, tm=tm, tiles_per_seq=S // tm),
        grid=(nt + 1,),
        in_specs=[pl.BlockSpec((tm, D), cur), pl.BlockSpec((tm, D), prev), _resident((1, D)),
                  _resident(w_in.shape), _resident(conv_w.shape), _resident(w_out.shape)],
        out_specs=pl.BlockSpec((tm, D), prev),
        out_shape=jax.ShapeDtypeStruct((T, D), F32),
        scratch_shapes=[pltpu.VMEM((2, tm + HALO, D), F32), pltpu.VMEM((2, tm, D), F32)],
        compiler_params=pltpu.CompilerParams(
            dimension_semantics=("arbitrary",), vmem_limit_bytes=_vmem_limit(nbytes)),
        name="short_conv",
    )(x, x, nw.reshape(1, D), w_in, conv_w.astype(F32), w_out)


def kernel(x, rel_bias, final_norm_w, ffn1_norm, ffn1_wg, ffn1_wu, ffn1_wd, mix_norm, ffn2_norm, ffn2_wg, ffn2_wu, ffn2_wd, hyb_w_in, hyb_w_out, diff_lq1, diff_lk1, diff_lq2, diff_lk2, diff_subln_w, ssm_conv_w, ssm_conv_b, ssm_dt_bias, ssm_a_log, ssm_d, ssm_norm_w, sc_w_in, sc_conv_w, sc_w_out):
    B, S, D = x.shape
    T = B * S
    xt = x.reshape(T, D)
    bf = lambda w: w.astype(BF16)

    lam_inits = [0.8 - 0.6 * math.exp(-0.3 * i) for i in range(0, DEPTH, 2)]
    tiles, lams = _prep_call(rel_bias, diff_lq1, diff_lk1, diff_lq2, diff_lk2, lam_inits)

    mix = None
    for i in range(DEPTH):
        j = i // 2
        xt = _ffn_call(xt, ffn1_norm[i], ffn1_wg, ffn1_wu, ffn1_wd, i)
        if i % 2 == 0:
            w_in = hyb_w_in[j]
            qkv_end = 3 * DIFF_WIDTH
            w_pad = bf(jnp.pad(w_in, ((0, 0), (0, LANES - SSM_HEADS))))
            wvt = bf(jnp.concatenate([w_in[:, 0:DIFF_WIDTH], w_in[:, 2 * DIFF_WIDTH:qkv_end]], axis=1).T)
            q, k, vt, y = _hyb_call(xt, mix_norm[i], w_pad, wvt, ssm_conv_w[j], ssm_conv_b[j],
                                    ssm_dt_bias[j], ssm_a_log[j], ssm_d[j], ssm_norm_w[j], batch=B)
            o = _attn_call(q, k, vt, tiles, lams[j * SUBLANES:(j + 1) * SUBLANES], diff_subln_w[j],
                           batch=B, lam_init=lam_inits[j])
            mix = (o, y, hyb_w_out, j)
        else:
            xt = _sc_call(xt, mix_norm[i], bf(sc_w_in[j]), sc_conv_w[j], bf(sc_w_out[j]), batch=B)
            mix = None
        xt = _ffn_call(xt, ffn2_norm[i], ffn2_wg, ffn2_wu, ffn2_wd, i, mix=mix,
                       final_w=final_norm_w if i == DEPTH - 1 else None)
    return xt.reshape(B, S, D)
```

```python
import functools
import math

import jax
import jax.numpy as jnp
from jax import lax
from jax.experimental import pallas as pl
from jax.experimental.pallas import tpu as pltpu

F32 = jnp.float32
BF16 = jnp.bfloat16

DEPTH = 4
N_DIFF_HEADS = 4
DIFF_HEAD_DIM = 64
DIFF_WIDTH = N_DIFF_HEADS * 2 * DIFF_HEAD_DIM
NUM_BUCKETS = 32
MAX_DISTANCE = 128
SSM_HEADS = 8
SSM_HEAD_DIM = 64
SSM_WIDTH = SSM_HEADS * SSM_HEAD_DIM
SSM_GROUPS = 2
SSM_STATE = 128
SSM_CONV = 4
SSM_CHUNK = 128
HEADS_PER_GROUP = SSM_HEADS // SSM_GROUPS
SSM_XBC = SSM_WIDTH + 2 * SSM_GROUPS * SSM_STATE
SC_CONV = 3

LANES = 128
SUBLANES = 8
V7X_SCOPED_VMEM_CAP = 60000 * 1024

FFN_TM = 512
FFN_FCHUNK = 256
PROJ_TM = 512
ATT_T = 512
ATT_STRIP = 256
ATT_ONES = 16
ATT_FAR_TILE = 2
SSD_LB = 512
HALO = SUBLANES

NEG_BIG = -1e30
LOG2E = math.log2(math.e)


def _vmem_limit(nbytes):
    return int(min(V7X_SCOPED_VMEM_CAP, nbytes * 5 // 4 + (8 << 20)))


def _dot(a, b):
    return jnp.dot(a, b, preferred_element_type=F32)


def _dot_nt(a, b):
    return lax.dot_general(a, b, (((1,), (1,)), ((), ())), preferred_element_type=F32)


def _rms(x, w, eps):
    ms = jnp.mean(x * x, axis=-1, keepdims=True)
    return x * lax.rsqrt(ms + eps) * w


def _silu(x):
    return x * jax.nn.sigmoid(x)


def _resident(shape):
    nd = len(shape)
    return pl.BlockSpec(shape, lambda *_: (0,) * nd, pipeline_mode=pl.Buffered(1))


def _ffn_kernel(*refs, pre_mix, final_norm, f_total):
    refs = list(refs)
    x_ref = refs.pop(0)
    if pre_mix:
        o_ref, y_ref, wo_ref = refs.pop(0), refs.pop(0), refs.pop(0)
    nw_ref, wg_ref, wu_ref, wd_ref = refs.pop(0), refs.pop(0), refs.pop(0), refs.pop(0)
    if final_norm:
        fw_ref = refs.pop(0)
    out_ref = refs.pop(0)

    x = x_ref[...]
    if pre_mix:
        half = o_ref.shape[1]
        x = (x + _dot(o_ref[...], wo_ref[0:half, :].astype(BF16))
             + _dot(y_ref[...], wo_ref[half:, :].astype(BF16)))
    h = _rms(x, nw_ref[...], 1e-6).astype(BF16)
    acc = None
    for c0 in range(0, f_total, FFN_FCHUNK):
        c1 = c0 + FFN_FCHUNK
        g = _dot(h, wg_ref[:, c0:c1].astype(BF16))
        u = _dot(h, wu_ref[:, c0:c1].astype(BF16))
        a = (_silu(g) * u).astype(BF16)
        d = _dot(a, wd_ref[c0:c1, :].astype(BF16))
        acc = d if acc is None else acc + d
    y = x + 0.5 * acc
    if final_norm:
        y = _rms(y, fw_ref[...], 1e-6)
    out_ref[...] = y


def _layer_resident(stacked, layer):
    _, r, c = stacked.shape
    return pl.BlockSpec((None, r, c), lambda *_: (layer, 0, 0), pipeline_mode=pl.Buffered(1))


def _ffn_call(x, nw, wg, wu, wd, layer, *, mix=None, final_w=None, tm=FFN_TM):
    T, D = x.shape
    F = wg.shape[2]
    assert T % tm == 0 and F % FFN_FCHUNK == 0
    row = lambda i: (i, 0)
    args = [x]
    in_specs = [pl.BlockSpec((tm, D), row)]
    nbytes = 4 * tm * D * 4 + 3 * D * F * 4 + 6 * tm * FFN_FCHUNK * 4 + 3 * tm * D * 4
    if mix is not None:
        o, y, wo, mix_layer = mix
        args += [o, y, wo]
        in_specs += [pl.BlockSpec((tm, o.shape[1]), row), pl.BlockSpec((tm, y.shape[1]), row),
                     _layer_resident(wo, mix_layer)]
        nbytes += wo.shape[1] * wo.shape[2] * 4 + 4 * tm * o.shape[1] * 2 * 2
    args += [nw.reshape(1, D), wg, wu, wd]
    in_specs += [_resident((1, D)), _layer_resident(wg, layer), _layer_resident(wu, layer),
                 _layer_resident(wd, layer)]
    if final_w is not None:
        args.append(final_w.reshape(1, D))
        in_specs.append(_resident((1, D)))
    kern = functools.partial(_ffn_kernel, pre_mix=mix is not None, final_norm=final_w is not None,
                             f_total=F)
    return pl.pallas_call(
        kern,
        grid=(T // tm,),
        in_specs=in_specs,
        out_specs=pl.BlockSpec((tm, D), row),
        out_shape=jax.ShapeDtypeStruct((T, D), F32),
        compiler_params=pltpu.CompilerParams(
            dimension_semantics=("arbitrary",), vmem_limit_bytes=_vmem_limit(nbytes)),
        name="ffn",
    )(*args)


def _hyb_in_kernel(x_ref, nw_ref, w_ref, wvt_ref, cw_ref, cb_ref, dtb_ref,
                   q_ref, k_ref, vt_ref, z_ref, xc_ref, dt_ref, xpad_ref, *, tm, tiles_per_seq):
    @pl.when(pl.program_id(0) % tiles_per_seq == 0)
    def _():
        xpad_ref[0:HALO, :] = jnp.zeros((HALO, SSM_XBC), F32)

    h = _rms(x_ref[...], nw_ref[...], 1e-6).astype(BF16)
    qw = DIFF_WIDTH
    c_z = 2 * qw + DIFF_WIDTH
    c_xbc = c_z + SSM_WIDTH
    c_dt = c_xbc + SSM_XBC
    xpad_ref[HALO:HALO + tm, :] = _dot(h, w_ref[:, c_xbc:c_xbc + SSM_XBC])
    scale = DIFF_HEAD_DIM ** -0.5 * LOG2E
    q_ref[...] = (_dot_nt(wvt_ref[0:qw, :], h) * scale).astype(BF16)
    k_ref[...] = _dot(h, w_ref[:, qw:2 * qw]).astype(BF16)
    vt_ref[...] = _dot_nt(wvt_ref[qw:, :], h).astype(BF16)
    z_ref[...] = _silu(_dot(h, w_ref[:, c_z:c_z + SSM_WIDTH]))
    dt_raw = _dot(h, w_ref[:, c_dt:c_dt + LANES]) + dtb_ref[...]
    dt_ref[...] = jnp.maximum(dt_raw, 0.0) + jnp.log1p(jnp.exp(-jnp.abs(dt_raw)))

    conv = cb_ref[...]
    for kk in range(SSM_CONV):
        o = HALO - (SSM_CONV - 1) + kk
        conv = conv + cw_ref[kk:kk + 1, :] * xpad_ref[o:o + tm, :]
    xc_ref[...] = _silu(conv)
    xpad_ref[0:HALO, :] = xpad_ref[tm:tm + HALO, :]


def _hyb_in_call(x, nw, w_pad, wvt, conv_w, conv_b, dt_bias, *, batch, tm=PROJ_TM):
    T, D = x.shape
    S = T // batch
    assert S % tm == 0
    row = lambda i: (i, 0)
    outs = [
        jax.ShapeDtypeStruct((DIFF_WIDTH, T), BF16),
        jax.ShapeDtypeStruct((T, DIFF_WIDTH), BF16),
        jax.ShapeDtypeStruct((DIFF_WIDTH, T), BF16),
        jax.ShapeDtypeStruct((T, SSM_WIDTH), F32),
        jax.ShapeDtypeStruct((T, SSM_XBC), F32),
        jax.ShapeDtypeStruct((T, LANES), F32),
    ]
    out_specs = [
        pl.BlockSpec((DIFF_WIDTH, tm), lambda i: (0, i)), pl.BlockSpec((tm, DIFF_WIDTH), row),
        pl.BlockSpec((DIFF_WIDTH, tm), lambda i: (0, i)),
        pl.BlockSpec((tm, SSM_WIDTH), row), pl.BlockSpec((tm, SSM_XBC), row),
        pl.BlockSpec((tm, LANES), row),
    ]
    nbytes = ((w_pad.size + wvt.size) * 2 + 2 * tm * D * 4
              + 3 * tm * 4 * (3 * DIFF_WIDTH + SSM_WIDTH + SSM_XBC + LANES) + 3 * tm * SSM_XBC * 4)
    pad = jnp.pad(dt_bias.astype(F32), (0, LANES - dt_bias.shape[0])).reshape(1, LANES)
    return pl.pallas_call(
        functools.partial(_hyb_in_kernel, tm=tm, tiles_per_seq=S // tm),
        grid=(T // tm,),
        in_specs=[pl.BlockSpec((tm, D), row), _resident((1, D)), _resident(w_pad.shape),
                  _resident(wvt.shape), _resident((SSM_CONV, SSM_XBC)), _resident((1, SSM_XBC)),
                  _resident((1, LANES))],
        out_specs=out_specs,
        out_shape=outs,
        scratch_shapes=[pltpu.VMEM((tm + HALO, SSM_XBC), F32)],
        compiler_params=pltpu.CompilerParams(
            dimension_semantics=("arbitrary",), vmem_limit_bytes=_vmem_limit(nbytes)),
        name="hyb_in",
    )(x, nw.reshape(1, D), w_pad, wvt, conv_w.astype(F32), conv_b.reshape(1, SSM_XBC).astype(F32), pad)


def _prep_kernel(tab_ref, lq1_ref, lk1_ref, lq2_ref, lk2_ref, tile_ref, lam_ref, *, t, lam_inits):
    hd = pl.program_id(0)
    key = lax.broadcasted_iota(jnp.int32, (t, t), 0)
    qry = lax.broadcasted_iota(jnp.int32, (t, t), 1)
    max_exact = NUM_BUCKETS // 2
    far = tab_ref[hd, NUM_BUCKETS - 1]
    tile_ref[ATT_FAR_TILE] = jnp.zeros((t, t), F32)
    for off in range(ATT_FAR_TILE):
        dist = off * t + qry - key
        d = jnp.maximum(dist, 0)
        large = max_exact + (
            jnp.log(jnp.maximum(d, 1).astype(F32) / max_exact)
            / math.log(MAX_DISTANCE / max_exact) * (NUM_BUCKETS - max_exact)).astype(jnp.int32)
        bucket = jnp.where(d < max_exact, d, jnp.minimum(large, NUM_BUCKETS - 1))
        bias = jnp.full((t, t), far, F32)
        for b in range(NUM_BUCKETS - 1):
            bias = jnp.where(bucket == b, tab_ref[hd, b], bias)
        val = (bias - far) * LOG2E
        if off == 0:
            val = jnp.where(dist >= 0, val, NEG_BIG)
        tile_ref[off] = val
    for j, lam_init in enumerate(lam_inits):
        s1 = jnp.sum(lq1_ref[j:j + 1, :] * lk1_ref[j:j + 1, :], axis=-1, keepdims=True)
        s2 = jnp.sum(lq2_ref[j:j + 1, :] * lk2_ref[j:j + 1, :], axis=-1, keepdims=True)
        lam = jnp.exp(s1) - jnp.exp(s2) + lam_init
        lam_ref[j * SUBLANES:(j + 1) * SUBLANES, :] = jnp.broadcast_to(lam, (SUBLANES, LANES))


def _prep_call(rel_bias, lq1, lk1, lq2, lk2, lam_inits, *, t=ATT_T):
    ne = lq1.shape[0]
    H = rel_bias.shape[1]
    whole = lambda shape: pl.BlockSpec(shape, lambda h: (0,) * len(shape))
    return pl.pallas_call(
        functools.partial(_prep_kernel, t=t, lam_inits=tuple(lam_inits)),
        grid=(H,),
        in_specs=[pl.BlockSpec(memory_space=pltpu.SMEM)] + [whole(lq1.shape)] * 4,
        out_specs=[pl.BlockSpec((None, ATT_FAR_TILE + 1, t, t), lambda h: (h, 0, 0, 0)),
                   whole((ne * SUBLANES, LANES))],
        out_shape=[jax.ShapeDtypeStruct((H, ATT_FAR_TILE + 1, t, t), F32),
                   jax.ShapeDtypeStruct((ne * SUBLANES, LANES), F32)],
        compiler_params=pltpu.CompilerParams(
            dimension_semantics=("arbitrary",), vmem_limit_bytes=_vmem_limit(12 * t * t * 4)),
        name="attn_prep",
    )(rel_bias.T.astype(F32), lq1, lk1, lq2, lk2)


def _attn_kernel(qt_ref, k_ref, vt_ref, tile_ref, lam_ref, sw_ref, o_ref,
                 qm_ref, m_ref, acc_ref, s_ref, smax_ref, *, t, nq, unroll, q_split, lam_init):
    e = 2 * DIFF_HEAD_DIM
    first = lax.broadcasted_iota(jnp.int32, (LANES, t), 0) < DIFF_HEAD_DIM
    ones = jnp.ones((ATT_ONES, t), BF16)
    chains = [(w, slice(c0, c0 + ATT_STRIP)) for c0 in range(0, t, ATT_STRIP) for w in range(2)]

    def rows(blk):
        return pl.ds(pl.multiple_of(blk * t, t), t)

    def prepare(qi, carry):
        q = qt_ref[:, rows(qi)]
        zero = jnp.zeros_like(q)
        qm_ref[0, :, rows(qi)] = jnp.where(first, q, zero)
        qm_ref[1, :, rows(qi)] = jnp.where(first, zero, q)
        m_ref[qi] = jnp.full(m_ref.shape[1:], NEG_BIG, F32)
        acc_ref[qi] = jnp.zeros(acc_ref.shape[1:], F32)
        return carry

    def score_chain(ci, qi, j, slot, k, biased):
        w, cols = chains[ci]
        q_cols = pl.ds(pl.multiple_of(qi * t + cols.start, ATT_STRIP), ATT_STRIP)
        s = _dot(k, qm_ref[w, :, q_cols])
        if biased:
            near = jnp.minimum(qi - j, ATT_FAR_TILE)
            s = s + tile_ref[near, :, cols]
        s_ref[slot, ci] = s
        smax_ref[slot, ci] = jnp.max(s, axis=0, keepdims=True)

    def value_chain(ci, qi, slot, vta):
        w, cols = chains[ci]
        m_old = m_ref[qi, w, :, cols]
        m_new = jnp.maximum(m_old, smax_ref[slot, ci])
        p = jnp.exp2(s_ref[slot, ci] - m_new).astype(BF16)
        acc_ref[qi, w, :, cols] = jnp.exp2(m_old - m_new) * acc_ref[qi, w, :, cols] + _dot(vta, p)
        m_ref[qi, w, :, cols] = m_new

    def finalize(pair, carry):
        for qi in (2 * pair, 2 * pair + 1):
            lam = lam_ref[0:1, 0:1]
            o1 = acc_ref[qi, 0, 0:e, :] * (1.0 / acc_ref[qi, 0, e:e + 1, :])
            o2 = acc_ref[qi, 1, 0:e, :] * (1.0 / acc_ref[qi, 1, e:e + 1, :])
            out_t = o1 - lam * o2
            ms = jnp.mean(out_t * out_t, axis=0, keepdims=True)
            out_t = out_t * lax.rsqrt(ms + 1e-5) * sw_ref[...] * (1.0 - lam_init)
            o_ref[rows(qi), :] = out_t.T.astype(BF16)
        return carry

    def sweep(first_pair, next_pair, last_pair, count, biased):
        def step(qv, jv, slot):
            qs, js = next_pair(qv, jv)
            past_end = qs >= nq
            qs = jnp.where(past_end, last_pair[0], qs)
            js = jnp.where(past_end, last_pair[1], js)
            k_next = k_ref[rows(js), :]
            vta = jnp.concatenate([vt_ref[:, rows(jv)], ones], axis=0)
            for ci in range(len(chains)):
                score_chain(ci, qs, js, 1 - slot, k_next, biased)
                value_chain(ci, qv, slot, vta)
            return qs, js

        def body(_, carry):
            qv, jv = carry
            for u in range(unroll):
                qv, jv = step(qv, jv, u % 2)
            return qv, jv

        k0 = k_ref[rows(first_pair[1]), :]
        for ci in range(len(chains)):
            score_chain(ci, first_pair[0], first_pair[1], 0, k0, biased)
        lax.fori_loop(0, count // unroll, body, (jnp.int32(first_pair[0]), jnp.int32(first_pair[1])))

    def next_near(qv, jv):
        wrap = jv == qv
        qn = qv + 1
        return jnp.where(wrap, qn, qv), jnp.where(wrap, jnp.where(qn < q_split, 0, qn - 1), jv + 1)

    def next_far(qv, jv):
        wrap = jv == qv - ATT_FAR_TILE
        return jnp.where(wrap, qv + 1, qv), jnp.where(wrap, 0, jv + 1)

    n_near = q_split * (q_split + 1) // 2 + ATT_FAR_TILE * (nq - q_split)
    n_far = nq * (nq + 1) // 2 - n_near
    lax.fori_loop(0, nq, prepare, 0)
    sweep((0, 0), next_near, (nq - 1, nq - 1), n_near, True)
    if n_far:
        sweep((q_split, 0), next_far, (nq - 1, nq - 1 - ATT_FAR_TILE), n_far, False)
    lax.fori_loop(0, nq // 2, finalize, 0)


def _attn_plan(nq):
    for unroll in (8, 4, 2):
        for q_split in range(ATT_FAR_TILE, nq + 1):
            n_near = q_split * (q_split + 1) // 2 + ATT_FAR_TILE * (nq - q_split)
            n_far = nq * (nq + 1) // 2 - n_near
            if n_near % unroll == 0 and n_far % unroll == 0:
                return unroll, q_split
    raise ValueError(f"unsupported number of query blocks: {nq}")


def _attn_call(qt, k, vt, tiles, lam, subln_w, *, batch, lam_init, t=ATT_T):
    T, W = k.shape
    S = T // batch
    H = W // LANES
    nq = S // t
    e = 2 * DIFF_HEAD_DIM
    assert e == LANES and S % t == 0 and nq % 2 == 0
    unroll, q_split = _attn_plan(nq)
    n_chains = 2 * (t // ATT_STRIP)
    n_tiles = ATT_FAR_TILE + 1
    nbytes = (2 * 4 * (S * LANES * 2) + 2 * n_tiles * t * t * 4 + 2 * S * LANES * 2
              + nq * 2 * (e + ATT_ONES + SUBLANES) * t * 4 + 2 * n_chains * t * ATT_STRIP * 4 + 2 * t * t * 4)
    return pl.pallas_call(
        functools.partial(_attn_kernel, t=t, nq=nq, unroll=unroll, q_split=q_split, lam_init=lam_init),
        grid=(batch, H),
        in_specs=[
            pl.BlockSpec((LANES, S), lambda b, h: (h, b)),
            pl.BlockSpec((S, LANES), lambda b, h: (b, h)),
            pl.BlockSpec((LANES, S), lambda b, h: (h, b)),
            pl.BlockSpec((None, n_tiles, t, t), lambda b, h: (h, 0, 0, 0)),
            pl.BlockSpec((SUBLANES, LANES), lambda b, h: (0, 0)),
            pl.BlockSpec((e, 1), lambda b, h: (0, 0)),
        ],
        out_specs=pl.BlockSpec((S, LANES), lambda b, h: (b, h)),
        out_shape=jax.ShapeDtypeStruct((T, W), BF16),
        scratch_shapes=[pltpu.VMEM((2, LANES, S), BF16),
                        pltpu.VMEM((nq, 2, 1, t), F32), pltpu.VMEM((nq, 2, e + ATT_ONES, t), F32),
                        pltpu.VMEM((2, n_chains, t, ATT_STRIP), F32),
                        pltpu.VMEM((2, n_chains, 1, ATT_STRIP), F32)],
        compiler_params=pltpu.CompilerParams(
            dimension_semantics=("arbitrary", "arbitrary"), vmem_limit_bytes=_vmem_limit(nbytes)),
        name="diff_attn",
    )(qt, k, vt, tiles, lam, subln_w.reshape(e, 1))


def _split3(a):
    hi = a.astype(BF16)
    r1 = a - hi.astype(F32)
    mid = r1.astype(BF16)
    lo = (r1 - mid.astype(F32)).astype(BF16)
    return hi, mid, lo


def _ssd_kernel(xc_ref, z_ref, dt_ref, alog_ref, dsk_ref, nw_ref, y_ref, st_ref, *, lb):
    L = SSM_CHUNK
    gw = HEADS_PER_GROUP * SSM_HEAD_DIM

    @pl.when(pl.program_id(1) == 0)
    def _():
        st_ref[...] = jnp.zeros(st_ref.shape, F32)

    xc = xc_ref[...]
    dt = dt_ref[...]
    a_dt = -jnp.exp(alog_ref[...]) * dt

    row = lax.broadcasted_iota(jnp.int32, (L, L), 0)
    col = lax.broadcasted_iota(jnp.int32, (L, L), 1)
    causal = row >= col
    tri = jnp.where(causal, 1.0, 0.0).astype(BF16)
    lane_g = lax.broadcasted_iota(jnp.int32, (L, gw), 1) // SSM_HEAD_DIM
    lane_lo = lax.broadcasted_iota(jnp.int32, (L, LANES), 1) < SSM_HEAD_DIM
    lane_lo1 = lax.broadcasted_iota(jnp.int32, (1, LANES), 1) < SSM_HEAD_DIM

    for c in range(lb // L):
        r0 = c * L
        hi, mid, lo = _split3(a_dt[r0:r0 + L, :])
        cs = _dot(tri, hi) + _dot(tri, mid) + _dot(tri, lo)
        cs_t = cs.T
        dt_t = dt[r0:r0 + L, :].T
        xs = xc[r0:r0 + L, 0:SSM_WIDTH]
        ys = []
        for g in range(SSM_GROUPS):
            b0 = SSM_WIDTH + g * SSM_STATE
            c0 = SSM_WIDTH + SSM_GROUPS * SSM_STATE + g * SSM_STATE
            bg = xc[r0:r0 + L, b0:b0 + SSM_STATE]
            cg = xc[r0:r0 + L, c0:c0 + SSM_STATE].astype(BF16)
            gmat = _dot_nt(cg, bg.astype(BF16))
            bg_t = bg.T
            xg = xs[:, g * gw:(g + 1) * gw]
            ydiag = None
            st_new = None
            ecols, decs = [], []
            for hh in range(HEADS_PER_GROUP):
                h = g * HEADS_PER_GROUP + hh
                colb = jnp.broadcast_to(cs[:, h:h + 1], (L, L))
                rowb = cs_t[h:h + 1, :]
                dtrow = dt_t[h:h + 1, :]
                decay = jnp.exp(jnp.where(causal, colb - rowb, NEG_BIG))
                mh = (gmat * decay * dtrow).astype(BF16)
                xm = jnp.where(lane_g == hh, xg, 0.0).astype(BF16)
                last = colb[L - 1:L, :]
                wrow = dtrow * jnp.exp(last - rowb)
                btw = (bg_t * wrow).astype(BF16)
                d1 = _dot(mh, xm)
                d2 = _dot(btw, xm)
                ydiag = d1 if ydiag is None else ydiag + d1
                st_new = d2 if st_new is None else st_new + d2
                ecols.append(jnp.exp(colb))
                decs.append(jnp.exp(last))
            e_mat = jnp.concatenate([jnp.where(lane_lo, ecols[0], ecols[1]),
                                     jnp.where(lane_lo, ecols[2], ecols[3])], axis=1)
            dec = jnp.concatenate([jnp.where(lane_lo1, decs[0], decs[1]),
                                   jnp.where(lane_lo1, decs[2], decs[3])], axis=1)
            st_old = st_ref[g]
            ys.append(ydiag + _dot(cg, st_old.astype(BF16)) * e_mat)
            st_ref[g] = st_old * dec + st_new
        y = jnp.concatenate(ys, axis=1) + dsk_ref[...] * xs
        gated = y * z_ref[r0:r0 + L, :]
        outs = []
        for g in range(SSM_GROUPS):
            part = gated[:, g * gw:(g + 1) * gw]
            ms = jnp.mean(part * part, axis=-1, keepdims=True)
            outs.append(part * lax.rsqrt(ms + 1e-5))
        y_ref[r0:r0 + L, :] = (jnp.concatenate(outs, axis=1) * nw_ref[...]).astype(BF16)


def _ssd_call(xc, z, dt, a_log, d_skip, norm_w, *, batch, lb=SSD_LB):
    T = xc.shape[0]
    S = T // batch
    nb = S // lb
    assert S % lb == 0 and lb % SSM_CHUNK == 0
    a_pad = jnp.pad(a_log.astype(F32), (0, LANES - a_log.shape[0])).reshape(1, LANES)
    dsk = jnp.repeat(d_skip.astype(F32), SSM_HEAD_DIM).reshape(1, SSM_WIDTH)
    row = lambda b, j: (b * nb + j, 0)
    const = lambda shape: pl.BlockSpec(shape, lambda b, j: (0, 0))
    nbytes = 3 * lb * (SSM_XBC + SSM_WIDTH + LANES) * 4 + 8 * lb * SSM_XBC * 4
    return pl.pallas_call(
        functools.partial(_ssd_kernel, lb=lb),
        grid=(batch, nb),
        in_specs=[pl.BlockSpec((lb, SSM_XBC), row), pl.BlockSpec((lb, SSM_WIDTH), row),
                  pl.BlockSpec((lb, LANES), row),
                  const((1, LANES)), const((1, SSM_WIDTH)), const((1, SSM_WIDTH))],
        out_specs=pl.BlockSpec((lb, SSM_WIDTH), row),
        out_shape=jax.ShapeDtypeStruct((T, SSM_WIDTH), BF16),
        scratch_shapes=[pltpu.VMEM((SSM_GROUPS, SSM_STATE, HEADS_PER_GROUP * SSM_HEAD_DIM), F32)],
        compiler_params=pltpu.CompilerParams(
            dimension_semantics=("arbitrary", "arbitrary"), vmem_limit_bytes=_vmem_limit(nbytes)),
        name="ssd",
    )(xc, z, dt, a_pad, dsk, norm_w.reshape(1, SSM_WIDTH).astype(F32))


def _hyb_kernel(x_ref, nw_ref, w_ref, wvt_ref, cw_ref, cb_ref, dtb_ref, alog_ref, dsk_ref, ynw_ref,
                qt_ref, k_ref, vt_ref, y_ref, xpad_ref, z_scr, dt_scr, st_ref, *, tm, tiles_per_seq):
    i = pl.program_id(0)
    L = SSM_CHUNK
    gw = HEADS_PER_GROUP * SSM_HEAD_DIM
    qw = DIFF_WIDTH
    c_z = 2 * qw + DIFF_WIDTH
    c_xbc = c_z + SSM_WIDTH
    c_dt = c_xbc + SSM_XBC

    @pl.when(i == 0)
    def _():
        xpad_ref[...] = jnp.zeros(xpad_ref.shape, F32)
        z_scr[...] = jnp.zeros(z_scr.shape, F32)
        dt_scr[...] = jnp.zeros(dt_scr.shape, F32)
        st_ref[...] = jnp.zeros(st_ref.shape, F32)

    def stage(cur, prev):
        row = lax.broadcasted_iota(jnp.int32, (L, L), 0)
        col = lax.broadcasted_iota(jnp.int32, (L, L), 1)
        causal = row >= col
        tri = jnp.where(causal, 1.0, 0.0).astype(BF16)
        lane_g = lax.broadcasted_iota(jnp.int32, (L, gw), 1) // SSM_HEAD_DIM
        lane_lo = lax.broadcasted_iota(jnp.int32, (L, LANES), 1) < SSM_HEAD_DIM
        lane_lo1 = lax.broadcasted_iota(jnp.int32, (1, LANES), 1) < SSM_HEAD_DIM
        a_neg = -jnp.exp(alog_ref[...])
        keep_state = jnp.where((i - 1) % tiles_per_seq == 0, 0.0, 1.0)

        def ssd_chunk(c):
            r0 = c * L
            dt = dt_scr[prev, r0:r0 + L, :]
            hi, mid, lo = _split3(a_neg * dt)
            cs = _dot(tri, hi) + _dot(tri, mid) + _dot(tri, lo)
            yield

            conv = cb_ref[...]
            for kk in range(SSM_CONV):
                o = HALO - (SSM_CONV - 1) + kk + r0
                conv = conv + cw_ref[kk:kk + 1, :] * xpad_ref[prev, o:o + L, :]
            xc = _silu(conv)
            xs = xc[:, 0:SSM_WIDTH]
            bgs, cgs, gmats = [], [], []
            for g in range(SSM_GROUPS):
                b0 = SSM_WIDTH + g * SSM_STATE
                c0 = SSM_WIDTH + SSM_GROUPS * SSM_STATE + g * SSM_STATE
                bgs.append(xc[:, b0:b0 + SSM_STATE])
                cgs.append(xc[:, c0:c0 + SSM_STATE].astype(BF16))
                gmats.append(_dot_nt(cgs[g], bgs[g].astype(BF16)))
            yield

            cs_t = cs.T
            dt_t = dt.T
            ydiags, st_news, e_mats, decs_g = [], [], [], []
            for g in range(SSM_GROUPS):
                bg_t = bgs[g].T
                xg = xs[:, g * gw:(g + 1) * gw]
                ydiag = None
                st_new = None
                ecols, decs = [], []
                for hh in range(HEADS_PER_GROUP):
                    h = g * HEADS_PER_GROUP + hh
                    colb = jnp.broadcast_to(cs[:, h:h + 1], (L, L))
                    rowb = cs_t[h:h + 1, :]
                    dtrow = dt_t[h:h + 1, :]
                    decay = jnp.exp(jnp.where(causal, colb - rowb, NEG_BIG))
                    mh = (gmats[g] * decay * dtrow).astype(BF16)
                    xm = jnp.where(lane_g == hh, xg, 0.0).astype(BF16)
                    last = colb[L - 1:L, :]
                    wrow = dtrow * jnp.exp(last - rowb)
                    btw = (bg_t * wrow).astype(BF16)
                    d1 = _dot(mh, xm)
                    d2 = _dot(btw, xm)
                    ydiag = d1 if ydiag is None else ydiag + d1
                    st_new = d2 if st_new is None else st_new + d2
                    ecols.append(jnp.exp(colb))
                    decs.append(jnp.exp(last))
                ydiags.append(ydiag)
                st_news.append(st_new)
                e_mats.append(jnp.concatenate([jnp.where(lane_lo, ecols[0], ecols[1]),
                                               jnp.where(lane_lo, ecols[2], ecols[3])], axis=1))
                decs_g.append(jnp.concatenate([jnp.where(lane_lo1, decs[0], decs[1]),
                                               jnp.where(lane_lo1, decs[2], decs[3])], axis=1))
                if g == 0:
                    yield

            ys = []
            for g in range(SSM_GROUPS):
                st_old = st_ref[g] * keep_state if c == 0 else st_ref[g]
                ys.append(ydiags[g] + _dot(cgs[g], st_old.astype(BF16)) * e_mats[g])
                st_ref[g] = st_old * decs_g[g] + st_news[g]
            y = jnp.concatenate(ys, axis=1) + dsk_ref[...] * xs
            gated = y * z_scr[prev, r0:r0 + L, :]
            outs = []
            for g in range(SSM_GROUPS):
                part = gated[:, g * gw:(g + 1) * gw]
                ms = jnp.mean(part * part, axis=-1, keepdims=True)
                outs.append(part * lax.rsqrt(ms + 1e-5))
            y_ref[r0:r0 + L, :] = (jnp.concatenate(outs, axis=1) * ynw_ref[...]).astype(BF16)
            yield

        h = _rms(x_ref[...], nw_ref[...], 1e-6).astype(BF16)
        scale = DIFF_HEAD_DIM ** -0.5 * LOG2E
        piece = 2 * LANES

        def pieces():
            for r in range(0, qw, piece):
                qt_ref[r:r + piece, :] = (_dot_nt(wvt_ref[r:r + piece, :], h) * scale).astype(BF16)
                yield
            for c in range(0, qw, piece):
                k_ref[:, c:c + piece] = _dot(h, w_ref[:, qw + c:qw + c + piece]).astype(BF16)
                yield
            for r in range(0, DIFF_WIDTH, piece):
                vt_ref[r:r + piece, :] = _dot_nt(wvt_ref[qw + r:qw + r + piece, :], h).astype(BF16)
                yield
            for c in range(0, SSM_WIDTH, piece):
                z_scr[cur, :, c:c + piece] = _silu(_dot(h, w_ref[:, c_z + c:c_z + c + piece]))
                yield
            dt_raw = _dot(h, w_ref[:, c_dt:c_dt + LANES]) + dtb_ref[...]
            dt_scr[cur] = jnp.maximum(dt_raw, 0.0) + jnp.log1p(jnp.exp(-jnp.abs(dt_raw)))
            yield
            tail = xpad_ref[prev, tm:tm + HALO, :]
            xpad_ref[cur, 0:HALO, :] = jnp.where(i % tiles_per_seq == 0, jnp.zeros_like(tail), tail)
            for c in range(0, SSM_XBC, piece):
                xpad_ref[cur, HALO:HALO + tm, c:c + piece] = _dot(h, w_ref[:, c_xbc + c:c_xbc + c + piece])
                yield

        pieces_after_group = (0, 2, 1, 1)
        proj = pieces()
        for c in range(tm // L):
            for group, _ in enumerate(ssd_chunk(c)):
                for _ in range(pieces_after_group[group]):
                    next(proj, None)
        for _ in proj:
            pass

    pl.when(i % 2 == 0)(functools.partial(stage, 0, 1))
    pl.when(i % 2 == 1)(functools.partial(stage, 1, 0))


def _hyb_call(x, nw, w_pad, wvt, conv_w, conv_b, dt_bias, a_log, d_skip, norm_w, *, batch, tm=PROJ_TM):
    T, D = x.shape
    S = T // batch
    assert S % tm == 0 and tm % SSM_CHUNK == 0
    nt = T // tm
    cur = lambda i: (jnp.minimum(i, nt - 1), 0)
    cur_t = lambda i: (0, jnp.minimum(i, nt - 1))
    prev = lambda i: (jnp.maximum(i - 1, 0), 0)
    pad = lambda v: jnp.pad(v.astype(F32), (0, LANES - v.shape[0])).reshape(1, LANES)
    dsk = jnp.repeat(d_skip.astype(F32), SSM_HEAD_DIM).reshape(1, SSM_WIDTH)
    outs = [
        jax.ShapeDtypeStruct((DIFF_WIDTH, T), BF16),
        jax.ShapeDtypeStruct((T, DIFF_WIDTH), BF16),
        jax.ShapeDtypeStruct((DIFF_WIDTH, T), BF16),
        jax.ShapeDtypeStruct((T, SSM_WIDTH), BF16),
    ]
    out_specs = [pl.BlockSpec((DIFF_WIDTH, tm), cur_t), pl.BlockSpec((tm, DIFF_WIDTH), cur),
                 pl.BlockSpec((DIFF_WIDTH, tm), cur_t), pl.BlockSpec((tm, SSM_WIDTH), prev)]
    nbytes = ((w_pad.size + wvt.size) * 2 + 2 * tm * D * 4 + 2 * tm * 2 * 4 * DIFF_WIDTH
              + 2 * (tm + HALO) * SSM_XBC * 4 + 2 * tm * (SSM_WIDTH + LANES) * 4 + 16 * tm * SSM_XBC * 4)
    return pl.pallas_call(
        functools.partial(_hyb_kernel, tm=tm, tiles_per_seq=S // tm),
        grid=(nt + 1,),
        in_specs=[pl.BlockSpec((tm, D), cur), _resident((1, D)), _resident(w_pad.shape),
                  _resident(wvt.shape), _resident((SSM_CONV, SSM_XBC)), _resident((1, SSM_XBC)),
                  _resident((1, LANES)), _resident((1, LANES)), _resident((1, SSM_WIDTH)),
                  _resident((1, SSM_WIDTH))],
        out_specs=out_specs,
        out_shape=outs,
        scratch_shapes=[pltpu.VMEM((2, tm + HALO, SSM_XBC), F32), pltpu.VMEM((2, tm, SSM_WIDTH), F32),
                        pltpu.VMEM((2, tm, LANES), F32),
                        pltpu.VMEM((SSM_GROUPS, SSM_STATE, HEADS_PER_GROUP * SSM_HEAD_DIM), F32)],
        compiler_params=pltpu.CompilerParams(
            dimension_semantics=("arbitrary",), vmem_limit_bytes=_vmem_limit(nbytes)),
        name="hyb_ssd",
    )(x, nw.reshape(1, D), w_pad, wvt, conv_w.astype(F32), conv_b.reshape(1, SSM_XBC).astype(F32),
      pad(dt_bias), pad(a_log), dsk, norm_w.reshape(1, SSM_WIDTH).astype(F32))


def _sc_kernel(x_ref, xprev_ref, nw_ref, win_ref, cw_ref, wout_ref, o_ref, vpad_ref, bg_ref, *,
               tm, tiles_per_seq):
    i = pl.program_id(0)
    d = x_ref.shape[1]

    @pl.when(i == 0)
    def _():
        vpad_ref[...] = jnp.zeros(vpad_ref.shape, F32)
        bg_ref[...] = jnp.zeros(bg_ref.shape, F32)

    def finish_chunk(prev, c0, c1):
        conv = cw_ref[SC_CONV - 1:SC_CONV, c0:c1] * vpad_ref[prev, HALO:HALO + tm, c0:c1]
        for kk in range(SC_CONV - 1):
            o = HALO - (SC_CONV - 1) + kk
            conv = conv + cw_ref[kk:kk + 1, c0:c1] * vpad_ref[prev, o:o + tm, c0:c1]
        return _dot((bg_ref[prev, :, c0:c1] * conv).astype(BF16), wout_ref[c0:c1, :])

    def stage(cur, prev):
        n_chunks = 4
        cw = d // n_chunks
        h = _rms(x_ref[...], nw_ref[...], 1e-6).astype(BF16)
        bgate = _dot(h, win_ref[:, 0:d])
        out = finish_chunk(prev, 0, cw)
        cgate = _dot(h, win_ref[:, d:2 * d])
        out = out + finish_chunk(prev, cw, 2 * cw)
        u = _dot(h, win_ref[:, 2 * d:3 * d])
        out = out + finish_chunk(prev, 2 * cw, 3 * cw)
        out = out + finish_chunk(prev, 3 * cw, d)
        o_ref[...] = xprev_ref[...] + out

        tail = vpad_ref[prev, tm:tm + HALO, :]
        vpad_ref[cur, 0:HALO, :] = jnp.where(i % tiles_per_seq == 0, jnp.zeros_like(tail), tail)
        vpad_ref[cur, HALO:HALO + tm, :] = cgate * u
        bg_ref[cur] = bgate

    pl.when(i % 2 == 0)(functools.partial(stage, 0, 1))
    pl.when(i % 2 == 1)(functools.partial(stage, 1, 0))


def _sc_call(x, nw, w_in, conv_w, w_out, *, batch, tm=PROJ_TM):
    T, D = x.shape
    S = T // batch
    assert S % tm == 0
    nt = T // tm
    cur = lambda i: (jnp.minimum(i, nt - 1), 0)
    prev = lambda i: (jnp.maximum(i - 1, 0), 0)
    nbytes = (w_in.size + w_out.size) * 2 + 6 * tm * D * 4 + 4 * tm * D * 4 + 8 * tm * D * 4
    return pl.pallas_call(
        functools.partial(_sc_kernel, tm=tm, tiles_per_seq=S // tm),
        grid=(nt + 1,),
        in_specs=[pl.BlockSpec((tm, D), cur), pl.BlockSpec((tm, D), prev), _resident((1, D)),
                  _resident(w_in.shape), _resident(conv_w.shape), _resident(w_out.shape)],
        out_specs=pl.BlockSpec((tm, D), prev),
        out_shape=jax.ShapeDtypeStruct((T, D), F32),
        scratch_shapes=[pltpu.VMEM((2, tm + HALO, D), F32), pltpu.VMEM((2, tm, D), F32)],
        compiler_params=pltpu.CompilerParams(
            dimension_semantics=("arbitrary",), vmem_limit_bytes=_vmem_limit(nbytes)),
        name="short_conv",
    )(x, x, nw.reshape(1, D), w_in, conv_w.astype(F32), w_out)


def kernel(x, rel_bias, final_norm_w, ffn1_norm, ffn1_wg, ffn1_wu, ffn1_wd, mix_norm, ffn2_norm, ffn2_wg, ffn2_wu, ffn2_wd, hyb_w_in, hyb_w_out, diff_lq1, diff_lk1, diff_lq2, diff_lk2, diff_subln_w, ssm_conv_w, ssm_conv_b, ssm_dt_bias, ssm_a_log, ssm_d, ssm_norm_w, sc_w_in, sc_conv_w, sc_w_out):
    B, S, D = x.shape
    T = B * S
    xt = x.reshape(T, D)
    bf = lambda w: w.astype(BF16)

    lam_inits = [0.8 - 0.6 * math.exp(-0.3 * i) for i in range(0, DEPTH, 2)]
    tiles, lams = _prep_call(rel_bias, diff_lq1, diff_lk1, diff_lq2, diff_lk2, lam_inits)

    mix = None
    for i in range(DEPTH):
        j = i // 2
        xt = _ffn_call(xt, ffn1_norm[i], ffn1_wg, ffn1_wu, ffn1_wd, i)
        if i % 2 == 0:
            w_in = hyb_w_in[j]
            qkv_end = 3 * DIFF_WIDTH
            w_pad = bf(jnp.pad(w_in, ((0, 0), (0, LANES - SSM_HEADS))))
            wvt = bf(jnp.concatenate([w_in[:, 0:DIFF_WIDTH], w_in[:, 2 * DIFF_WIDTH:qkv_end]], axis=1).T)
            q, k, vt, y = _hyb_call(xt, mix_norm[i], w_pad, wvt, ssm_conv_w[j], ssm_conv_b[j],
                                    ssm_dt_bias[j], ssm_a_log[j], ssm_d[j], ssm_norm_w[j], batch=B)
            o = _attn_call(q, k, vt, tiles, lams[j * SUBLANES:(j + 1) * SUBLANES], diff_subln_w[j],
                           batch=B, lam_init=lam_inits[j])
            mix = (o, y, hyb_w_out, j)
        else:
            xt = _sc_call(xt, mix_norm[i], bf(sc_w_in[j]), sc_conv_w[j], bf(sc_w_out[j]), batch=B)
            mix = None
        xt = _ffn_call(xt, ffn2_norm[i], ffn2_wg, ffn2_wu, ffn2_wd, i, mix=mix,
                       final_w=final_norm_w if i == DEPTH - 1 else None)
    return xt.reshape(B, S, D)
```

```python
import functools
import math

import jax
import jax.numpy as jnp
from jax import lax
from jax.experimental import pallas as pl
from jax.experimental.pallas import tpu as pltpu

F32 = jnp.float32
BF16 = jnp.bfloat16

DEPTH = 4
N_DIFF_HEADS = 4
DIFF_HEAD_DIM = 64
DIFF_WIDTH = N_DIFF_HEADS * 2 * DIFF_HEAD_DIM
NUM_BUCKETS = 32
MAX_DISTANCE = 128
SSM_HEADS = 8
SSM_HEAD_DIM = 64
SSM_WIDTH = SSM_HEADS * SSM_HEAD_DIM
SSM_GROUPS = 2
SSM_STATE = 128
SSM_CONV = 4
SSM_CHUNK = 128
HEADS_PER_GROUP = SSM_HEADS // SSM_GROUPS
SSM_XBC = SSM_WIDTH + 2 * SSM_GROUPS * SSM_STATE
SC_CONV = 3

LANES = 128
SUBLANES = 8
V7X_SCOPED_VMEM_CAP = 60000 * 1024

FFN_TM = 512
FFN_FCHUNK = 256
PROJ_TM = 512
ATT_T = 512
ATT_STRIP = 256
ATT_ONES = 16
ATT_FAR_TILE = 2
SSD_LB = 512
HALO = SUBLANES

NEG_BIG = -1e30
LOG2E = math.log2(math.e)


def _vmem_limit(nbytes):
    return int(min(V7X_SCOPED_VMEM_CAP, nbytes * 5 // 4 + (8 << 20)))


def _dot(a, b):
    return jnp.dot(a, b, preferred_element_type=F32)


def _dot_nt(a, b):
    return lax.dot_general(a, b, (((1,), (1,)), ((), ())), preferred_element_type=F32)


def _rms(x, w, eps):
    ms = jnp.mean(x * x, axis=-1, keepdims=True)
    return x * lax.rsqrt(ms + eps) * w


def _silu(x):
    h = 0.5 * x
    return h + h * jnp.tanh(h)


def _resident(shape):
    nd = len(shape)
    return pl.BlockSpec(shape, lambda *_: (0,) * nd, pipeline_mode=pl.Buffered(1))


def _ffn_kernel(*refs, pre_mix, final_norm, f_total):
    refs = list(refs)
    x_ref = refs.pop(0)
    if pre_mix:
        o_ref, y_ref, wo_ref = refs.pop(0), refs.pop(0), refs.pop(0)
    nw_ref, wg_ref, wu_ref, wd_ref = refs.pop(0), refs.pop(0), refs.pop(0), refs.pop(0)
    if final_norm:
        fw_ref = refs.pop(0)
    out_ref = refs.pop(0)

    x = x_ref[...]
    if pre_mix:
        half = o_ref.shape[1]
        x = (x + _dot(o_ref[...], wo_ref[0:half, :].astype(BF16))
             + _dot(y_ref[...], wo_ref[half:, :].astype(BF16)))
    h = _rms(x, nw_ref[...], 1e-6).astype(BF16)
    acc = None
    for c0 in range(0, f_total, FFN_FCHUNK):
        c1 = c0 + FFN_FCHUNK
        g = _dot(h, wg_ref[:, c0:c1].astype(BF16))
        u = _dot(h, wu_ref[:, c0:c1].astype(BF16))
        a = (_silu(g) * u).astype(BF16)
        d = _dot(a, wd_ref[c0:c1, :].astype(BF16))
        acc = d if acc is None else acc + d
    y = x + 0.5 * acc
    if final_norm:
        y = _rms(y, fw_ref[...], 1e-6)
    out_ref[...] = y


def _layer_resident(stacked, layer):
    _, r, c = stacked.shape
    return pl.BlockSpec((None, r, c), lambda *_: (layer, 0, 0), pipeline_mode=pl.Buffered(1))


def _ffn_call(x, nw, wg, wu, wd, layer, *, mix=None, final_w=None, tm=FFN_TM):
    T, D = x.shape
    F = wg.shape[2]
    assert T % tm == 0 and F % FFN_FCHUNK == 0
    row = lambda i: (i, 0)
    args = [x]
    in_specs = [pl.BlockSpec((tm, D), row)]
    nbytes = 4 * tm * D * 4 + 3 * D * F * 4 + 6 * tm * FFN_FCHUNK * 4 + 3 * tm * D * 4
    if mix is not None:
        o, y, wo, mix_layer = mix
        args += [o, y, wo]
        in_specs += [pl.BlockSpec((tm, o.shape[1]), row), pl.BlockSpec((tm, y.shape[1]), row),
                     _layer_resident(wo, mix_layer)]
        nbytes += wo.shape[1] * wo.shape[2] * 4 + 4 * tm * o.shape[1] * 2 * 2
    args += [nw.reshape(1, D), wg, wu, wd]
    in_specs += [_resident((1, D)), _layer_resident(wg, layer), _layer_resident(wu, layer),
                 _layer_resident(wd, layer)]
    if final_w is not None:
        args.append(final_w.reshape(1, D))
        in_specs.append(_resident((1, D)))
    kern = functools.partial(_ffn_kernel, pre_mix=mix is not None, final_norm=final_w is not None,
                             f_total=F)
    return pl.pallas_call(
        kern,
        grid=(T // tm,),
        in_specs=in_specs,
        out_specs=pl.BlockSpec((tm, D), row),
        out_shape=jax.ShapeDtypeStruct((T, D), F32),
        compiler_params=pltpu.CompilerParams(
            dimension_semantics=("arbitrary",), vmem_limit_bytes=_vmem_limit(nbytes)),
        name="ffn",
    )(*args)


def _hyb_in_kernel(x_ref, nw_ref, w_ref, wvt_ref, cw_ref, cb_ref, dtb_ref,
                   q_ref, k_ref, vt_ref, z_ref, xc_ref, dt_ref, xpad_ref, *, tm, tiles_per_seq):
    @pl.when(pl.program_id(0) % tiles_per_seq == 0)
    def _():
        xpad_ref[0:HALO, :] = jnp.zeros((HALO, SSM_XBC), F32)

    h = _rms(x_ref[...], nw_ref[...], 1e-6).astype(BF16)
    qw = DIFF_WIDTH
    c_z = 2 * qw + DIFF_WIDTH
    c_xbc = c_z + SSM_WIDTH
    c_dt = c_xbc + SSM_XBC
    xpad_ref[HALO:HALO + tm, :] = _dot(h, w_ref[:, c_xbc:c_xbc + SSM_XBC])
    scale = DIFF_HEAD_DIM ** -0.5 * LOG2E
    q_ref[...] = (_dot_nt(wvt_ref[0:qw, :], h) * scale).astype(BF16)
    k_ref[...] = _dot(h, w_ref[:, qw:2 * qw]).astype(BF16)
    vt_ref[...] = _dot_nt(wvt_ref[qw:, :], h).astype(BF16)
    z_ref[...] = _silu(_dot(h, w_ref[:, c_z:c_z + SSM_WIDTH]))
    dt_raw = _dot(h, w_ref[:, c_dt:c_dt + LANES]) + dtb_ref[...]
    dt_ref[...] = jnp.maximum(dt_raw, 0.0) + jnp.log1p(jnp.exp(-jnp.abs(dt_raw)))

    conv = cb_ref[...]
    for kk in range(SSM_CONV):
        o = HALO - (SSM_CONV - 1) + kk
        conv = conv + cw_ref[kk:kk + 1, :] * xpad_ref[o:o + tm, :]
    xc_ref[...] = _silu(conv)
    xpad_ref[0:HALO, :] = xpad_ref[tm:tm + HALO, :]


def _hyb_in_call(x, nw, w_pad, wvt, conv_w, conv_b, dt_bias, *, batch, tm=PROJ_TM):
    T, D = x.shape
    S = T // batch
    assert S % tm == 0
    row = lambda i: (i, 0)
    outs = [
        jax.ShapeDtypeStruct((DIFF_WIDTH, T), BF16),
        jax.ShapeDtypeStruct((T, DIFF_WIDTH), BF16),
        jax.ShapeDtypeStruct((DIFF_WIDTH, T), BF16),
        jax.ShapeDtypeStruct((T, SSM_WIDTH), F32),
        jax.ShapeDtypeStruct((T, SSM_XBC), F32),
        jax.ShapeDtypeStruct((T, LANES), F32),
    ]
    out_specs = [
        pl.BlockSpec((DIFF_WIDTH, tm), lambda i: (0, i)), pl.BlockSpec((tm, DIFF_WIDTH), row),
        pl.BlockSpec((DIFF_WIDTH, tm), lambda i: (0, i)),
        pl.BlockSpec((tm, SSM_WIDTH), row), pl.BlockSpec((tm, SSM_XBC), row),
        pl.BlockSpec((tm, LANES), row),
    ]
    nbytes = ((w_pad.size + wvt.size) * 2 + 2 * tm * D * 4
              + 3 * tm * 4 * (3 * DIFF_WIDTH + SSM_WIDTH + SSM_XBC + LANES) + 3 * tm * SSM_XBC * 4)
    pad = jnp.pad(dt_bias.astype(F32), (0, LANES - dt_bias.shape[0])).reshape(1, LANES)
    return pl.pallas_call(
        functools.partial(_hyb_in_kernel, tm=tm, tiles_per_seq=S // tm),
        grid=(T // tm,),
        in_specs=[pl.BlockSpec((tm, D), row), _resident((1, D)), _resident(w_pad.shape),
                  _resident(wvt.shape), _resident((SSM_CONV, SSM_XBC)), _resident((1, SSM_XBC)),
                  _resident((1, LANES))],
        out_specs=out_specs,
        out_shape=outs,
        scratch_shapes=[pltpu.VMEM((tm + HALO, SSM_XBC), F32)],
        compiler_params=pltpu.CompilerParams(
            dimension_semantics=("arbitrary",), vmem_limit_bytes=_vmem_limit(nbytes)),
        name="hyb_in",
    )(x, nw.reshape(1, D), w_pad, wvt, conv_w.astype(F32), conv_b.reshape(1, SSM_XBC).astype(F32), pad)


def _prep_kernel(tab_ref, lq1_ref, lk1_ref, lq2_ref, lk2_ref, tile_ref, lam_ref, *, t, lam_inits):
    hd = pl.program_id(0)
    key = lax.broadcasted_iota(jnp.int32, (t, t), 0)
    qry = lax.broadcasted_iota(jnp.int32, (t, t), 1)
    max_exact = NUM_BUCKETS // 2
    far = tab_ref[hd, NUM_BUCKETS - 1]
    tile_ref[ATT_FAR_TILE] = jnp.zeros((t, t), F32)
    for off in range(ATT_FAR_TILE):
        dist = off * t + qry - key
        d = jnp.maximum(dist, 0)
        large = max_exact + (
            jnp.log(jnp.maximum(d, 1).astype(F32) / max_exact)
            / math.log(MAX_DISTANCE / max_exact) * (NUM_BUCKETS - max_exact)).astype(jnp.int32)
        bucket = jnp.where(d < max_exact, d, jnp.minimum(large, NUM_BUCKETS - 1))
        bias = jnp.full((t, t), far, F32)
        for b in range(NUM_BUCKETS - 1):
            bias = jnp.where(bucket == b, tab_ref[hd, b], bias)
        val = (bias - far) * LOG2E
        if off == 0:
            val = jnp.where(dist >= 0, val, NEG_BIG)
        tile_ref[off] = val
    for j, lam_init in enumerate(lam_inits):
        s1 = jnp.sum(lq1_ref[j:j + 1, :] * lk1_ref[j:j + 1, :], axis=-1, keepdims=True)
        s2 = jnp.sum(lq2_ref[j:j + 1, :] * lk2_ref[j:j + 1, :], axis=-1, keepdims=True)
        lam = jnp.exp(s1) - jnp.exp(s2) + lam_init
        lam_ref[j * SUBLANES:(j + 1) * SUBLANES, :] = jnp.broadcast_to(lam, (SUBLANES, LANES))


def _prep_call(rel_bias, lq1, lk1, lq2, lk2, lam_inits, *, t=ATT_T):
    ne = lq1.shape[0]
    H = rel_bias.shape[1]
    whole = lambda shape: pl.BlockSpec(shape, lambda h: (0,) * len(shape))
    return pl.pallas_call(
        functools.partial(_prep_kernel, t=t, lam_inits=tuple(lam_inits)),
        grid=(H,),
        in_specs=[pl.BlockSpec(memory_space=pltpu.SMEM)] + [whole(lq1.shape)] * 4,
        out_specs=[pl.BlockSpec((None, ATT_FAR_TILE + 1, t, t), lambda h: (h, 0, 0, 0)),
                   whole((ne * SUBLANES, LANES))],
        out_shape=[jax.ShapeDtypeStruct((H, ATT_FAR_TILE + 1, t, t), F32),
                   jax.ShapeDtypeStruct((ne * SUBLANES, LANES), F32)],
        compiler_params=pltpu.CompilerParams(
            dimension_semantics=("arbitrary",), vmem_limit_bytes=_vmem_limit(12 * t * t * 4)),
        name="attn_prep",
    )(rel_bias.T.astype(F32), lq1, lk1, lq2, lk2)


def _attn_kernel(qt_ref, k_ref, vt_ref, tile_ref, lam_ref, sw_ref, o_ref,
                 qm_ref, m_ref, acc_ref, s_ref, smax_ref, *, t, nq, unroll, q_split, lam_init):
    e = 2 * DIFF_HEAD_DIM
    first = lax.broadcasted_iota(jnp.int32, (LANES, t), 0) < DIFF_HEAD_DIM
    ones = jnp.ones((ATT_ONES, t), BF16)
    chains = [(w, slice(c0, c0 + ATT_STRIP)) for c0 in range(0, t, ATT_STRIP) for w in range(2)]

    def rows(blk):
        return pl.ds(pl.multiple_of(blk * t, t), t)

    def prepare(qi, carry):
        q = qt_ref[:, rows(qi)]
        zero = jnp.zeros_like(q)
        qm_ref[0, :, rows(qi)] = jnp.where(first, q, zero)
        qm_ref[1, :, rows(qi)] = jnp.where(first, zero, q)
        m_ref[qi] = jnp.full(m_ref.shape[1:], NEG_BIG, F32)
        acc_ref[qi] = jnp.zeros(acc_ref.shape[1:], F32)
        return carry

    def score_chain(ci, qi, j, slot, k, biased):
        w, cols = chains[ci]
        q_cols = pl.ds(pl.multiple_of(qi * t + cols.start, ATT_STRIP), ATT_STRIP)
        s = _dot(k, qm_ref[w, :, q_cols])
        if biased:
            near = jnp.minimum(qi - j, ATT_FAR_TILE)
            s = s + tile_ref[near, :, cols]
        s_ref[slot, ci] = s
        smax_ref[slot, ci] = jnp.max(s, axis=0, keepdims=True)

    def value_chain(ci, qi, slot, vta):
        w, cols = chains[ci]
        m_old = m_ref[qi, w, :, cols]
        m_new = jnp.maximum(m_old, smax_ref[slot, ci])
        p = jnp.exp2(s_ref[slot, ci] - m_new).astype(BF16)
        acc_ref[qi, w, :, cols] = jnp.exp2(m_old - m_new) * acc_ref[qi, w, :, cols] + _dot(vta, p)
        m_ref[qi, w, :, cols] = m_new

    def finalize(pair, carry):
        for qi in (2 * pair, 2 * pair + 1):
            lam = lam_ref[0:1, 0:1]
            o1 = acc_ref[qi, 0, 0:e, :] * (1.0 / acc_ref[qi, 0, e:e + 1, :])
            o2 = acc_ref[qi, 1, 0:e, :] * (1.0 / acc_ref[qi, 1, e:e + 1, :])
            out_t = o1 - lam * o2
            ms = jnp.mean(out_t * out_t, axis=0, keepdims=True)
            out_t = out_t * lax.rsqrt(ms + 1e-5) * sw_ref[...] * (1.0 - lam_init)
            o_ref[rows(qi), :] = out_t.T.astype(BF16)
        return carry

    def sweep(first_pair, next_pair, last_pair, count, biased):
        def step(qv, jv, slot):
            qs, js = next_pair(qv, jv)
            past_end = qs >= nq
            qs = jnp.where(past_end, last_pair[0], qs)
            js = jnp.where(past_end, last_pair[1], js)
            k_next = k_ref[rows(js), :]
            vta = jnp.concatenate([vt_ref[:, rows(jv)], ones], axis=0)
            for ci in range(len(chains)):
                score_chain(ci, qs, js, 1 - slot, k_next, biased)
                value_chain(ci, qv, slot, vta)
            return qs, js

        def body(_, carry):
            qv, jv = carry
            for u in range(unroll):
                qv, jv = step(qv, jv, u % 2)
            return qv, jv

        k0 = k_ref[rows(first_pair[1]), :]
        for ci in range(len(chains)):
            score_chain(ci, first_pair[0], first_pair[1], 0, k0, biased)
        lax.fori_loop(0, count // unroll, body, (jnp.int32(first_pair[0]), jnp.int32(first_pair[1])))

    def next_near(qv, jv):
        wrap = jv == qv
        qn = qv + 1
        return jnp.where(wrap, qn, qv), jnp.where(wrap, jnp.where(qn < q_split, 0, qn - 1), jv + 1)

    def next_far(qv, jv):
        wrap = jv == qv - ATT_FAR_TILE
        return jnp.where(wrap, qv + 1, qv), jnp.where(wrap, 0, jv + 1)

    n_near = q_split * (q_split + 1) // 2 + ATT_FAR_TILE * (nq - q_split)
    n_far = nq * (nq + 1) // 2 - n_near
    lax.fori_loop(0, nq, prepare, 0)
    sweep((0, 0), next_near, (nq - 1, nq - 1), n_near, True)
    if n_far:
        sweep((q_split, 0), next_far, (nq - 1, nq - 1 - ATT_FAR_TILE), n_far, False)
    lax.fori_loop(0, nq // 2, finalize, 0)


def _attn_plan(nq):
    for unroll in (8, 4, 2):
        for q_split in range(ATT_FAR_TILE, nq + 1):
            n_near = q_split * (q_split + 1) // 2 + ATT_FAR_TILE * (nq - q_split)
            n_far = nq * (nq + 1) // 2 - n_near
            if n_near % unroll == 0 and n_far % unroll == 0:
                return unroll, q_split
    raise ValueError(f"unsupported number of query blocks: {nq}")


def _attn_call(qt, k, vt, tiles, lam, subln_w, *, batch, lam_init, t=ATT_T):
    T, W = k.shape
    S = T // batch
    H = W // LANES
    nq = S // t
    e = 2 * DIFF_HEAD_DIM
    assert e == LANES and S % t == 0 and nq % 2 == 0
    unroll, q_split = _attn_plan(nq)
    n_chains = 2 * (t // ATT_STRIP)
    n_tiles = ATT_FAR_TILE + 1
    nbytes = (2 * 4 * (S * LANES * 2) + 2 * n_tiles * t * t * 4 + 2 * S * LANES * 2
              + nq * 2 * (e + ATT_ONES + SUBLANES) * t * 4 + 2 * n_chains * t * ATT_STRIP * 4 + 2 * t * t * 4)
    return pl.pallas_call(
        functools.partial(_attn_kernel, t=t, nq=nq, unroll=unroll, q_split=q_split, lam_init=lam_init),
        grid=(batch, H),
        in_specs=[
            pl.BlockSpec((LANES, S), lambda b, h: (h, b)),
            pl.BlockSpec((S, LANES), lambda b, h: (b, h)),
            pl.BlockSpec((LANES, S), lambda b, h: (h, b)),
            pl.BlockSpec((None, n_tiles, t, t), lambda b, h: (h, 0, 0, 0)),
            pl.BlockSpec((SUBLANES, LANES), lambda b, h: (0, 0)),
            pl.BlockSpec((e, 1), lambda b, h: (0, 0)),
        ],
        out_specs=pl.BlockSpec((S, LANES), lambda b, h: (b, h)),
        out_shape=jax.ShapeDtypeStruct((T, W), BF16),
        scratch_shapes=[pltpu.VMEM((2, LANES, S), BF16),
                        pltpu.VMEM((nq, 2, 1, t), F32), pltpu.VMEM((nq, 2, e + ATT_ONES, t), F32),
                        pltpu.VMEM((2, n_chains, t, ATT_STRIP), F32),
                        pltpu.VMEM((2, n_chains, 1, ATT_STRIP), F32)],
        compiler_params=pltpu.CompilerParams(
            dimension_semantics=("arbitrary", "arbitrary"), vmem_limit_bytes=_vmem_limit(nbytes)),
        name="diff_attn",
    )(qt, k, vt, tiles, lam, subln_w.reshape(e, 1))


def _split3(a):
    hi = a.astype(BF16)
    r1 = a - hi.astype(F32)
    mid = r1.astype(BF16)
    lo = (r1 - mid.astype(F32)).astype(BF16)
    return hi, mid, lo


def _ssd_kernel(xc_ref, z_ref, dt_ref, alog_ref, dsk_ref, nw_ref, y_ref, st_ref, *, lb):
    L = SSM_CHUNK
    gw = HEADS_PER_GROUP * SSM_HEAD_DIM

    @pl.when(pl.program_id(1) == 0)
    def _():
        st_ref[...] = jnp.zeros(st_ref.shape, F32)

    xc = xc_ref[...]
    dt = dt_ref[...]
    a_dt = -jnp.exp(alog_ref[...]) * dt

    row = lax.broadcasted_iota(jnp.int32, (L, L), 0)
    col = lax.broadcasted_iota(jnp.int32, (L, L), 1)
    causal = row >= col
    tri = jnp.where(causal, 1.0, 0.0).astype(BF16)
    lane_g = lax.broadcasted_iota(jnp.int32, (L, gw), 1) // SSM_HEAD_DIM
    lane_lo = lax.broadcasted_iota(jnp.int32, (L, LANES), 1) < SSM_HEAD_DIM
    lane_lo1 = lax.broadcasted_iota(jnp.int32, (1, LANES), 1) < SSM_HEAD_DIM

    for c in range(lb // L):
        r0 = c * L
        hi, mid, lo = _split3(a_dt[r0:r0 + L, :])
        cs = _dot(tri, hi) + _dot(tri, mid) + _dot(tri, lo)
        cs_t = cs.T
        dt_t = dt[r0:r0 + L, :].T
        xs = xc[r0:r0 + L, 0:SSM_WIDTH]
        ys = []
        for g in range(SSM_GROUPS):
            b0 = SSM_WIDTH + g * SSM_STATE
            c0 = SSM_WIDTH + SSM_GROUPS * SSM_STATE + g * SSM_STATE
            bg = xc[r0:r0 + L, b0:b0 + SSM_STATE]
            cg = xc[r0:r0 + L, c0:c0 + SSM_STATE].astype(BF16)
            gmat = _dot_nt(cg, bg.astype(BF16))
            bg_t = bg.T
            xg = xs[:, g * gw:(g + 1) * gw]
            ydiag = None
            st_new = None
            ecols, decs = [], []
            for hh in range(HEADS_PER_GROUP):
                h = g * HEADS_PER_GROUP + hh
                colb = jnp.broadcast_to(cs[:, h:h + 1], (L, L))
                rowb = cs_t[h:h + 1, :]
                dtrow = dt_t[h:h + 1, :]
                decay = jnp.exp(jnp.where(causal, colb - rowb, NEG_BIG))
                mh = (gmat * decay * dtrow).astype(BF16)
                xm = jnp.where(lane_g == hh, xg, 0.0).astype(BF16)
                last = colb[L - 1:L, :]
                wrow = dtrow * jnp.exp(last - rowb)
                btw = (bg_t * wrow).astype(BF16)
                d1 = _dot(mh, xm)
                d2 = _dot(btw, xm)
                ydiag = d1 if ydiag is None else ydiag + d1
                st_new = d2 if st_new is None else st_new + d2
                ecols.append(jnp.exp(colb))
                decs.append(jnp.exp(last))
            e_mat = jnp.concatenate([jnp.where(lane_lo, ecols[0], ecols[1]),
                                     jnp.where(lane_lo, ecols[2], ecols[3])], axis=1)
            dec = jnp.concatenate([jnp.where(lane_lo1, decs[0], decs[1]),
                                   jnp.where(lane_lo1, decs[2], decs[3])], axis=1)
            st_old = st_ref[g]
            ys.append(ydiag + _dot(cg, st_old.astype(BF16)) * e_mat)
            st_ref[g] = st_old * dec + st_new
        y = jnp.concatenate(ys, axis=1) + dsk_ref[...] * xs
        gated = y * z_ref[r0:r0 + L, :]
        outs = []
        for g in range(SSM_GROUPS):
            part = gated[:, g * gw:(g + 1) * gw]
            ms = jnp.mean(part * part, axis=-1, keepdims=True)
            outs.append(part * lax.rsqrt(ms + 1e-5))
        y_ref[r0:r0 + L, :] = (jnp.concatenate(outs, axis=1) * nw_ref[...]).astype(BF16)


def _ssd_call(xc, z, dt, a_log, d_skip, norm_w, *, batch, lb=SSD_LB):
    T = xc.shape[0]
    S = T // batch
    nb = S // lb
    assert S % lb == 0 and lb % SSM_CHUNK == 0
    a_pad = jnp.pad(a_log.astype(F32), (0, LANES - a_log.shape[0])).reshape(1, LANES)
    dsk = jnp.repeat(d_skip.astype(F32), SSM_HEAD_DIM).reshape(1, SSM_WIDTH)
    row = lambda b, j: (b * nb + j, 0)
    const = lambda shape: pl.BlockSpec(shape, lambda b, j: (0, 0))
    nbytes = 3 * lb * (SSM_XBC + SSM_WIDTH + LANES) * 4 + 8 * lb * SSM_XBC * 4
    return pl.pallas_call(
        functools.partial(_ssd_kernel, lb=lb),
        grid=(batch, nb),
        in_specs=[pl.BlockSpec((lb, SSM_XBC), row), pl.BlockSpec((lb, SSM_WIDTH), row),
                  pl.BlockSpec((lb, LANES), row),
                  const((1, LANES)), const((1, SSM_WIDTH)), const((1, SSM_WIDTH))],
        out_specs=pl.BlockSpec((lb, SSM_WIDTH), row),
        out_shape=jax.ShapeDtypeStruct((T, SSM_WIDTH), BF16),
        scratch_shapes=[pltpu.VMEM((SSM_GROUPS, SSM_STATE, HEADS_PER_GROUP * SSM_HEAD_DIM), F32)],
        compiler_params=pltpu.CompilerParams(
            dimension_semantics=("arbitrary", "arbitrary"), vmem_limit_bytes=_vmem_limit(nbytes)),
        name="ssd",
    )(xc, z, dt, a_pad, dsk, norm_w.reshape(1, SSM_WIDTH).astype(F32))


def _hyb_kernel(x_ref, nw_ref, w_ref, wvt_ref, cw_ref, cb_ref, dtb_ref, alog_ref, dsk_ref, ynw_ref,
                qt_ref, k_ref, vt_ref, y_ref, xpad_ref, z_scr, dt_scr, st_ref, *, tm, tiles_per_seq):
    i = pl.program_id(0)
    L = SSM_CHUNK
    gw = HEADS_PER_GROUP * SSM_HEAD_DIM
    qw = DIFF_WIDTH
    c_z = 2 * qw + DIFF_WIDTH
    c_xbc = c_z + SSM_WIDTH
    c_dt = c_xbc + SSM_XBC

    @pl.when(i == 0)
    def _():
        xpad_ref[...] = jnp.zeros(xpad_ref.shape, F32)
        z_scr[...] = jnp.zeros(z_scr.shape, F32)
        dt_scr[...] = jnp.zeros(dt_scr.shape, F32)
        st_ref[...] = jnp.zeros(st_ref.shape, F32)

    def stage(cur, prev):
        row = lax.broadcasted_iota(jnp.int32, (L, L), 0)
        col = lax.broadcasted_iota(jnp.int32, (L, L), 1)
        causal = row >= col
        tri = jnp.where(causal, 1.0, 0.0).astype(BF16)
        lane_g = lax.broadcasted_iota(jnp.int32, (L, gw), 1) // SSM_HEAD_DIM
        lane_lo = lax.broadcasted_iota(jnp.int32, (L, LANES), 1) < SSM_HEAD_DIM
        lane_lo1 = lax.broadcasted_iota(jnp.int32, (1, LANES), 1) < SSM_HEAD_DIM
        a_neg = -jnp.exp(alog_ref[...])
        keep_state = jnp.where((i - 1) % tiles_per_seq == 0, 0.0, 1.0)

        def ssd_chunk(c):
            r0 = c * L
            dt = dt_scr[prev, r0:r0 + L, :]
            hi, mid, lo = _split3(a_neg * dt)
            cs = _dot(tri, hi) + _dot(tri, mid) + _dot(tri, lo)
            yield

            conv = cb_ref[...]
            for kk in range(SSM_CONV):
                o = HALO - (SSM_CONV - 1) + kk + r0
                conv = conv + cw_ref[kk:kk + 1, :] * xpad_ref[prev, o:o + L, :]
            xc = _silu(conv)
            xs = xc[:, 0:SSM_WIDTH]
            bgs, cgs, gmats = [], [], []
            for g in range(SSM_GROUPS):
                b0 = SSM_WIDTH + g * SSM_STATE
                c0 = SSM_WIDTH + SSM_GROUPS * SSM_STATE + g * SSM_STATE
                bgs.append(xc[:, b0:b0 + SSM_STATE])
                cgs.append(xc[:, c0:c0 + SSM_STATE].astype(BF16))
                gmats.append(_dot_nt(cgs[g], bgs[g].astype(BF16)))
            yield

            cs_t = cs.T
            dt_t = dt.T
            ydiags, st_news, e_mats, decs_g = [], [], [], []
            for g in range(SSM_GROUPS):
                bg_t = bgs[g].T
                xg = xs[:, g * gw:(g + 1) * gw]
                ydiag = None
                st_new = None
                ecols, decs = [], []
                for hh in range(HEADS_PER_GROUP):
                    h = g * HEADS_PER_GROUP + hh
                    colb = jnp.broadcast_to(cs[:, h:h + 1], (L, L))
                    rowb = cs_t[h:h + 1, :]
                    dtrow = dt_t[h:h + 1, :]
                    decay = jnp.exp(jnp.where(causal, colb - rowb, NEG_BIG))
                    mh = (gmats[g] * decay * dtrow).astype(BF16)
                    xm = jnp.where(lane_g == hh, xg, 0.0).astype(BF16)
                    last = colb[L - 1:L, :]
                    wrow = dtrow * jnp.exp(last - rowb)
                    btw = (bg_t * wrow).astype(BF16)
                    d1 = _dot(mh, xm)
                    d2 = _dot(btw, xm)
                    ydiag = d1 if ydiag is None else ydiag + d1
                    st_new = d2 if st_new is None else st_new + d2
                    ecols.append(jnp.exp(colb))
                    decs.append(jnp.exp(last))
                ydiags.append(ydiag)
                st_news.append(st_new)
                e_mats.append(jnp.concatenate([jnp.where(lane_lo, ecols[0], ecols[1]),
                                               jnp.where(lane_lo, ecols[2], ecols[3])], axis=1))
                decs_g.append(jnp.concatenate([jnp.where(lane_lo1, decs[0], decs[1]),
                                               jnp.where(lane_lo1, decs[2], decs[3])], axis=1))
                if g == 0:
                    yield

            ys = []
            for g in range(SSM_GROUPS):
                st_old = st_ref[g] * keep_state if c == 0 else st_ref[g]
                ys.append(ydiags[g] + _dot(cgs[g], st_old.astype(BF16)) * e_mats[g])
                st_ref[g] = st_old * decs_g[g] + st_news[g]
            y = jnp.concatenate(ys, axis=1) + dsk_ref[...] * xs
            gated = y * z_scr[prev, r0:r0 + L, :]
            outs = []
            for g in range(SSM_GROUPS):
                part = gated[:, g * gw:(g + 1) * gw]
                ms = jnp.mean(part * part, axis=-1, keepdims=True)
                outs.append(part * lax.rsqrt(ms + 1e-5))
            y_ref[r0:r0 + L, :] = (jnp.concatenate(outs, axis=1) * ynw_ref[...]).astype(BF16)
            yield

        h = _rms(x_ref[...], nw_ref[...], 1e-6).astype(BF16)
        scale = DIFF_HEAD_DIM ** -0.5 * LOG2E
        piece = 2 * LANES

        def pieces():
            for r in range(0, qw, piece):
                qt_ref[r:r + piece, :] = (_dot_nt(wvt_ref[r:r + piece, :], h) * scale).astype(BF16)
                yield
            for c in range(0, qw, piece):
                k_ref[:, c:c + piece] = _dot(h, w_ref[:, qw + c:qw + c + piece]).astype(BF16)
                yield
            for r in range(0, DIFF_WIDTH, piece):
                vt_ref[r:r + piece, :] = _dot_nt(wvt_ref[qw + r:qw + r + piece, :], h).astype(BF16)
                yield
            for c in range(0, SSM_WIDTH, piece):
                z_scr[cur, :, c:c + piece] = _silu(_dot(h, w_ref[:, c_z + c:c_z + c + piece]))
                yield
            dt_raw = _dot(h, w_ref[:, c_dt:c_dt + LANES]) + dtb_ref[...]
            dt_scr[cur] = jnp.maximum(dt_raw, 0.0) + jnp.log1p(jnp.exp(-jnp.abs(dt_raw)))
            yield
            tail = xpad_ref[prev, tm:tm + HALO, :]
            xpad_ref[cur, 0:HALO, :] = jnp.where(i % tiles_per_seq == 0, jnp.zeros_like(tail), tail)
            for c in range(0, SSM_XBC, piece):
                xpad_ref[cur, HALO:HALO + tm, c:c + piece] = _dot(h, w_ref[:, c_xbc + c:c_xbc + c + piece])
                yield

        pieces_after_group = (0, 2, 1, 0)
        proj = pieces()
        for c in range(tm // L):
            for group, _ in enumerate(ssd_chunk(c)):
                for _ in range(pieces_after_group[group]):
                    next(proj, None)
        for _ in proj:
            pass

    pl.when(i % 2 == 0)(functools.partial(stage, 0, 1))
    pl.when(i % 2 == 1)(functools.partial(stage, 1, 0))


def _hyb_call(x, nw, w_pad, wvt, conv_w, conv_b, dt_bias, a_log, d_skip, norm_w, *, batch, tm=PROJ_TM):
    T, D = x.shape
    S = T // batch
    assert S % tm == 0 and tm % SSM_CHUNK == 0
    nt = T // tm
    cur = lambda i: (jnp.minimum(i, nt - 1), 0)
    cur_t = lambda i: (0, jnp.minimum(i, nt - 1))
    prev = lambda i: (jnp.maximum(i - 1, 0), 0)
    pad = lambda v: jnp.pad(v.astype(F32), (0, LANES - v.shape[0])).reshape(1, LANES)
    dsk = jnp.repeat(d_skip.astype(F32), SSM_HEAD_DIM).reshape(1, SSM_WIDTH)
    outs = [
        jax.ShapeDtypeStruct((DIFF_WIDTH, T), BF16),
        jax.ShapeDtypeStruct((T, DIFF_WIDTH), BF16),
        jax.ShapeDtypeStruct((DIFF_WIDTH, T), BF16),
        jax.ShapeDtypeStruct((T, SSM_WIDTH), BF16),
    ]
    out_specs = [pl.BlockSpec((DIFF_WIDTH, tm), cur_t), pl.BlockSpec((tm, DIFF_WIDTH), cur),
                 pl.BlockSpec((DIFF_WIDTH, tm), cur_t), pl.BlockSpec((tm, SSM_WIDTH), prev)]
    nbytes = ((w_pad.size + wvt.size) * 2 + 2 * tm * D * 4 + 2 * tm * 2 * 4 * DIFF_WIDTH
              + 2 * (tm + HALO) * SSM_XBC * 4 + 2 * tm * (SSM_WIDTH + LANES) * 4 + 16 * tm * SSM_XBC * 4)
    return pl.pallas_call(
        functools.partial(_hyb_kernel, tm=tm, tiles_per_seq=S // tm),
        grid=(nt + 1,),
        in_specs=[pl.BlockSpec((tm, D), cur), _resident((1, D)), _resident(w_pad.shape),
                  _resident(wvt.shape), _resident((SSM_CONV, SSM_XBC)), _resident((1, SSM_XBC)),
                  _resident((1, LANES)), _resident((1, LANES)), _resident((1, SSM_WIDTH)),
                  _resident((1, SSM_WIDTH))],
        out_specs=out_specs,
        out_shape=outs,
        scratch_shapes=[pltpu.VMEM((2, tm + HALO, SSM_XBC), F32), pltpu.VMEM((2, tm, SSM_WIDTH), F32),
                        pltpu.VMEM((2, tm, LANES), F32),
                        pltpu.VMEM((SSM_GROUPS, SSM_STATE, HEADS_PER_GROUP * SSM_HEAD_DIM), F32)],
        compiler_params=pltpu.CompilerParams(
            dimension_semantics=("arbitrary",), vmem_limit_bytes=_vmem_limit(nbytes)),
        name="hyb_ssd",
    )(x, nw.reshape(1, D), w_pad, wvt, conv_w.astype(F32), conv_b.reshape(1, SSM_XBC).astype(F32),
      pad(dt_bias), pad(a_log), dsk, norm_w.reshape(1, SSM_WIDTH).astype(F32))


def _sc_kernel(x_ref, xprev_ref, nw_ref, win_ref, cw_ref, wout_ref, o_ref, vpad_ref, bg_ref, *,
               tm, tiles_per_seq):
    i = pl.program_id(0)
    d = x_ref.shape[1]

    @pl.when(i == 0)
    def _():
        vpad_ref[...] = jnp.zeros(vpad_ref.shape, F32)
        bg_ref[...] = jnp.zeros(bg_ref.shape, F32)

    def finish_chunk(prev, c0, c1):
        conv = cw_ref[SC_CONV - 1:SC_CONV, c0:c1] * vpad_ref[prev, HALO:HALO + tm, c0:c1]
        for kk in range(SC_CONV - 1):
            o = HALO - (SC_CONV - 1) + kk
            conv = conv + cw_ref[kk:kk + 1, c0:c1] * vpad_ref[prev, o:o + tm, c0:c1]
        return _dot((bg_ref[prev, :, c0:c1] * conv).astype(BF16), wout_ref[c0:c1, :])

    def stage(cur, prev):
        n_chunks = 4
        cw = d // n_chunks
        h = _rms(x_ref[...], nw_ref[...], 1e-6).astype(BF16)
        bgate = _dot(h, win_ref[:, 0:d])
        out = finish_chunk(prev, 0, cw)
        cgate = _dot(h, win_ref[:, d:2 * d])
        out = out + finish_chunk(prev, cw, 2 * cw)
        u = _dot(h, win_ref[:, 2 * d:3 * d])
        out = out + finish_chunk(prev, 2 * cw, 3 * cw)
        out = out + finish_chunk(prev, 3 * cw, d)
        o_ref[...] = xprev_ref[...] + out

        tail = vpad_ref[prev, tm:tm + HALO, :]
        vpad_ref[cur, 0:HALO, :] = jnp.where(i % tiles_per_seq == 0, jnp.zeros_like(tail), tail)
        vpad_ref[cur, HALO:HALO + tm, :] = cgate * u
        bg_ref[cur] = bgate

    pl.when(i % 2 == 0)(functools.partial(stage, 0, 1))
    pl.when(i % 2 == 1)(functools.partial(stage, 1, 0))


def _sc_call(x, nw, w_in, conv_w, w_out, *, batch, tm=PROJ_TM):
    T, D = x.shape
    S = T // batch
    assert S % tm == 0
    nt = T // tm
    cur = lambda i: (jnp.minimum(i, nt - 1), 0)
    prev = lambda i: (jnp.maximum(i - 1, 0), 0)
    nbytes = (w_in.size + w_out.size) * 2 + 6 * tm * D * 4 + 4 * tm * D * 4 + 8 * tm * D * 4
    return pl.pallas_call(
        functools.partial(_sc_kernel, tm=tm, tiles_per_seq=S // tm),
        grid=(nt + 1,),
        in_specs=[pl.BlockSpec((tm, D), cur), pl.BlockSpec((tm, D), prev), _resident((1, D)),
                  _resident(w_in.shape), _resident(conv_w.shape), _resident(w_out.shape)],
        out_specs=pl.BlockSpec((tm, D), prev),
        out_shape=jax.ShapeDtypeStruct((T, D), F32),
        scratch_shapes=[pltpu.VMEM((2, tm + HALO, D), F32), pltpu.VMEM((2, tm, D), F32)],
        compiler_params=pltpu.CompilerParams(
            dimension_semantics=("arbitrary",), vmem_limit_bytes=_vmem_limit(nbytes)),
        name="short_conv",
    )(x, x, nw.reshape(1, D), w_in, conv_w.astype(F32), w_out)


def kernel(x, rel_bias, final_norm_w, ffn1_norm, ffn1_wg, ffn1_wu, ffn1_wd, mix_norm, ffn2_norm, ffn2_wg, ffn2_wu, ffn2_wd, hyb_w_in, hyb_w_out, diff_lq1, diff_lk1, diff_lq2, diff_lk2, diff_subln_w, ssm_conv_w, ssm_conv_b, ssm_dt_bias, ssm_a_log, ssm_d, ssm_norm_w, sc_w_in, sc_conv_w, sc_w_out):
    B, S, D = x.shape
    T = B * S
    xt = x.reshape(T, D)
    bf = lambda w: w.astype(BF16)

    lam_inits = [0.8 - 0.6 * math.exp(-0.3 * i) for i in range(0, DEPTH, 2)]
    tiles, lams = _prep_call(rel_bias, diff_lq1, diff_lk1, diff_lq2, diff_lk2, lam_inits)

    mix = None
    for i in range(DEPTH):
        j = i // 2
        xt = _ffn_call(xt, ffn1_norm[i], ffn1_wg, ffn1_wu, ffn1_wd, i)
        if i % 2 == 0:
            w_in = hyb_w_in[j]
            qkv_end = 3 * DIFF_WIDTH
            w_pad = bf(jnp.pad(w_in, ((0, 0), (0, LANES - SSM_HEADS))))
            wvt = bf(jnp.concatenate([w_in[:, 0:DIFF_WIDTH], w_in[:, 2 * DIFF_WIDTH:qkv_end]], axis=1).T)
            q, k, vt, y = _hyb_call(xt, mix_norm[i], w_pad, wvt, ssm_conv_w[j], ssm_conv_b[j],
                                    ssm_dt_bias[j], ssm_a_log[j], ssm_d[j], ssm_norm_w[j], batch=B)
            o = _attn_call(q, k, vt, tiles, lams[j * SUBLANES:(j + 1) * SUBLANES], diff_subln_w[j],
                           batch=B, lam_init=lam_inits[j])
            mix = (o, y, hyb_w_out, j)
        else:
            xt = _sc_call(xt, mix_norm[i], bf(sc_w_in[j]), sc_conv_w[j], bf(sc_w_out[j]), batch=B)
            mix = None
        xt = _ffn_call(xt, ffn2_norm[i], ffn2_wg, ffn2_wu, ffn2_wd, i, mix=mix,
                       final_w=final_norm_w if i == DEPTH - 1 else None)
    return xt.reshape(B, S, D)
```

```python
import functools
import math

import jax
import jax.numpy as jnp
from jax import lax
from jax.experimental import pallas as pl
from jax.experimental.pallas import tpu as pltpu

F32 = jnp.float32
BF16 = jnp.bfloat16

DEPTH = 4
N_DIFF_HEADS = 4
DIFF_HEAD_DIM = 64
DIFF_WIDTH = N_DIFF_HEADS * 2 * DIFF_HEAD_DIM
NUM_BUCKETS = 32
MAX_DISTANCE = 128
SSM_HEADS = 8
SSM_HEAD_DIM = 64
SSM_WIDTH = SSM_HEADS * SSM_HEAD_DIM
SSM_GROUPS = 2
SSM_STATE = 128
SSM_CONV = 4
SSM_CHUNK = 128
HEADS_PER_GROUP = SSM_HEADS // SSM_GROUPS
SSM_XBC = SSM_WIDTH + 2 * SSM_GROUPS * SSM_STATE
SC_CONV = 3

LANES = 128
SUBLANES = 8
V7X_SCOPED_VMEM_CAP = 60000 * 1024

FFN_TM = 512
FFN_FCHUNK = 256
PROJ_TM = 512
ATT_T = 512
ATT_STRIP = 256
ATT_ONES = 16
ATT_FAR_TILE = 2
HALO = SUBLANES

NEG_BIG = -1e30
LOG2E = math.log2(math.e)


def _vmem_limit(nbytes):
    return int(min(V7X_SCOPED_VMEM_CAP, nbytes * 5 // 4 + (8 << 20)))


def _dot(a, b):
    return jnp.dot(a, b, preferred_element_type=F32)


def _dot_nt(a, b):
    return lax.dot_general(a, b, (((1,), (1,)), ((), ())), preferred_element_type=F32)


def _rms(x, w, eps):
    ms = jnp.mean(x * x, axis=-1, keepdims=True)
    return x * lax.rsqrt(ms + eps) * w


def _silu(x):
    h = 0.5 * x
    return h + h * jnp.tanh(h)


def _resident(shape):
    nd = len(shape)
    return pl.BlockSpec(shape, lambda *_: (0,) * nd, pipeline_mode=pl.Buffered(1))


def _ffn_kernel(*refs, pre_mix, final_norm, f_total):
    refs = list(refs)
    x_ref = refs.pop(0)
    if pre_mix:
        o_ref, y_ref, wo_ref = refs.pop(0), refs.pop(0), refs.pop(0)
    nw_ref, wg_ref, wu_ref, wd_ref = refs.pop(0), refs.pop(0), refs.pop(0), refs.pop(0)
    if final_norm:
        fw_ref = refs.pop(0)
    out_ref = refs.pop(0)

    x = x_ref[...]
    if pre_mix:
        half = o_ref.shape[1]
        x = (x + _dot(o_ref[...], wo_ref[0:half, :].astype(BF16))
             + _dot(y_ref[...], wo_ref[half:, :].astype(BF16)))
    h = _rms(x, nw_ref[...], 1e-6).astype(BF16)
    acc = None
    for c0 in range(0, f_total, FFN_FCHUNK):
        c1 = c0 + FFN_FCHUNK
        g = _dot(h, wg_ref[:, c0:c1].astype(BF16))
        u = _dot(h, wu_ref[:, c0:c1].astype(BF16))
        a = (_silu(g) * u).astype(BF16)
        d = _dot(a, wd_ref[c0:c1, :].astype(BF16))
        acc = d if acc is None else acc + d
    y = x + 0.5 * acc
    if final_norm:
        y = _rms(y, fw_ref[...], 1e-6)
    out_ref[...] = y


def _layer_resident(stacked, layer):
    _, r, c = stacked.shape
    return pl.BlockSpec((None, r, c), lambda *_: (layer, 0, 0), pipeline_mode=pl.Buffered(1))


def _ffn_call(x, nw, wg, wu, wd, layer, *, mix=None, final_w=None, tm=FFN_TM):
    T, D = x.shape
    F = wg.shape[2]
    assert T % tm == 0 and F % FFN_FCHUNK == 0
    row = lambda i: (i, 0)
    args = [x]
    in_specs = [pl.BlockSpec((tm, D), row)]
    nbytes = 4 * tm * D * 4 + 3 * D * F * 4 + 6 * tm * FFN_FCHUNK * 4 + 3 * tm * D * 4
    if mix is not None:
        o, y, wo, mix_layer = mix
        args += [o, y, wo]
        in_specs += [pl.BlockSpec((tm, o.shape[1]), row), pl.BlockSpec((tm, y.shape[1]), row),
                     _layer_resident(wo, mix_layer)]
        nbytes += wo.shape[1] * wo.shape[2] * 4 + 4 * tm * o.shape[1] * 2 * 2
    args += [nw.reshape(1, D), wg, wu, wd]
    in_specs += [_resident((1, D)), _layer_resident(wg, layer), _layer_resident(wu, layer),
                 _layer_resident(wd, layer)]
    if final_w is not None:
        args.append(final_w.reshape(1, D))
        in_specs.append(_resident((1, D)))
    kern = functools.partial(_ffn_kernel, pre_mix=mix is not None, final_norm=final_w is not None,
                             f_total=F)
    return pl.pallas_call(
        kern,
        grid=(T // tm,),
        in_specs=in_specs,
        out_specs=pl.BlockSpec((tm, D), row),
        out_shape=jax.ShapeDtypeStruct((T, D), F32),
        compiler_params=pltpu.CompilerParams(
            dimension_semantics=("arbitrary",), vmem_limit_bytes=_vmem_limit(nbytes)),
        name="ffn",
    )(*args)


def _prep_kernel(tab_ref, lq1_ref, lk1_ref, lq2_ref, lk2_ref, tile_ref, lam_ref, *, t, lam_inits):
    hd = pl.program_id(0)
    key = lax.broadcasted_iota(jnp.int32, (t, t), 0)
    qry = lax.broadcasted_iota(jnp.int32, (t, t), 1)
    max_exact = NUM_BUCKETS // 2
    far = tab_ref[hd, NUM_BUCKETS - 1]
    tile_ref[ATT_FAR_TILE] = jnp.zeros((t, t), F32)
    for off in range(ATT_FAR_TILE):
        dist = off * t + qry - key
        d = jnp.maximum(dist, 0)
        large = max_exact + (
            jnp.log(jnp.maximum(d, 1).astype(F32) / max_exact)
            / math.log(MAX_DISTANCE / max_exact) * (NUM_BUCKETS - max_exact)).astype(jnp.int32)
        bucket = jnp.where(d < max_exact, d, jnp.minimum(large, NUM_BUCKETS - 1))
        bias = jnp.full((t, t), far, F32)
        for b in range(NUM_BUCKETS - 1):
            bias = jnp.where(bucket == b, tab_ref[hd, b], bias)
        val = (bias - far) * LOG2E
        if off == 0:
            val = jnp.where(dist >= 0, val, NEG_BIG)
        tile_ref[off] = val
    for j, lam_init in enumerate(lam_inits):
        s1 = jnp.sum(lq1_ref[j:j + 1, :] * lk1_ref[j:j + 1, :], axis=-1, keepdims=True)
        s2 = jnp.sum(lq2_ref[j:j + 1, :] * lk2_ref[j:j + 1, :], axis=-1, keepdims=True)
        lam = jnp.exp(s1) - jnp.exp(s2) + lam_init
        lam_ref[j * SUBLANES:(j + 1) * SUBLANES, :] = jnp.broadcast_to(lam, (SUBLANES, LANES))


def _prep_call(rel_bias, lq1, lk1, lq2, lk2, lam_inits, *, t=ATT_T):
    ne = lq1.shape[0]
    H = rel_bias.shape[1]
    whole = lambda shape: pl.BlockSpec(shape, lambda h: (0,) * len(shape))
    return pl.pallas_call(
        functools.partial(_prep_kernel, t=t, lam_inits=tuple(lam_inits)),
        grid=(H,),
        in_specs=[pl.BlockSpec(memory_space=pltpu.SMEM)] + [whole(lq1.shape)] * 4,
        out_specs=[pl.BlockSpec((None, ATT_FAR_TILE + 1, t, t), lambda h: (h, 0, 0, 0)),
                   whole((ne * SUBLANES, LANES))],
        out_shape=[jax.ShapeDtypeStruct((H, ATT_FAR_TILE + 1, t, t), F32),
                   jax.ShapeDtypeStruct((ne * SUBLANES, LANES), F32)],
        compiler_params=pltpu.CompilerParams(
            dimension_semantics=("arbitrary",), vmem_limit_bytes=_vmem_limit(12 * t * t * 4)),
        name="attn_prep",
    )(rel_bias.T.astype(F32), lq1, lk1, lq2, lk2)


def _attn_kernel(qt_ref, k_ref, vt_ref, tile_ref, lam_ref, sw_ref, o_ref,
                 qm_ref, m_ref, acc_ref, s_ref, smax_ref, *, t, nq, unroll, q_split, lam_init):
    e = 2 * DIFF_HEAD_DIM
    first = lax.broadcasted_iota(jnp.int32, (LANES, t), 0) < DIFF_HEAD_DIM
    ones = jnp.ones((ATT_ONES, t), BF16)
    chains = [(w, slice(c0, c0 + ATT_STRIP)) for c0 in range(0, t, ATT_STRIP) for w in range(2)]

    def rows(blk):
        return pl.ds(pl.multiple_of(blk * t, t), t)

    def prepare(qi, carry):
        q = qt_ref[:, rows(qi)]
        zero = jnp.zeros_like(q)
        qm_ref[0, :, rows(qi)] = jnp.where(first, q, zero)
        qm_ref[1, :, rows(qi)] = jnp.where(first, zero, q)
        m_ref[qi] = jnp.full(m_ref.shape[1:], NEG_BIG, F32)
        acc_ref[qi] = jnp.zeros(acc_ref.shape[1:], F32)
        return carry

    def score_chain(ci, qi, j, slot, k, biased):
        w, cols = chains[ci]
        q_cols = pl.ds(pl.multiple_of(qi * t + cols.start, ATT_STRIP), ATT_STRIP)
        s = _dot(k, qm_ref[w, :, q_cols])
        if biased:
            near = jnp.minimum(qi - j, ATT_FAR_TILE)
            s = s + tile_ref[near, :, cols]
        s_ref[slot, ci] = s
        smax_ref[slot, ci] = jnp.max(s, axis=0, keepdims=True)

    def value_chain(ci, qi, slot, vta):
        w, cols = chains[ci]
        m_old = m_ref[qi, w, :, cols]
        m_new = jnp.maximum(m_old, smax_ref[slot, ci])
        p = jnp.exp2(s_ref[slot, ci] - m_new).astype(BF16)
        acc_ref[qi, w, :, cols] = jnp.exp2(m_old - m_new) * acc_ref[qi, w, :, cols] + _dot(vta, p)
        m_ref[qi, w, :, cols] = m_new

    def finalize(pair, carry):
        for qi in (2 * pair, 2 * pair + 1):
            lam = lam_ref[0:1, 0:1]
            o1 = acc_ref[qi, 0, 0:e, :] * (1.0 / acc_ref[qi, 0, e:e + 1, :])
            o2 = acc_ref[qi, 1, 0:e, :] * (1.0 / acc_ref[qi, 1, e:e + 1, :])
            out_t = o1 - lam * o2
            ms = jnp.mean(out_t * out_t, axis=0, keepdims=True)
            out_t = out_t * lax.rsqrt(ms + 1e-5) * sw_ref[...] * (1.0 - lam_init)
            o_ref[rows(qi), :] = out_t.T.astype(BF16)
        return carry

    def sweep(first_pair, next_pair, last_pair, count, biased):
        def step(qv, jv, slot):
            qs, js = next_pair(qv, jv)
            past_end = qs >= nq
            qs = jnp.where(past_end, last_pair[0], qs)
            js = jnp.where(past_end, last_pair[1], js)
            k_next = k_ref[rows(js), :]
            vta = jnp.concatenate([vt_ref[:, rows(jv)], ones], axis=0)
            for ci in range(len(chains)):
                score_chain(ci, qs, js, 1 - slot, k_next, biased)
                value_chain(ci, qv, slot, vta)
            return qs, js

        def body(_, carry):
            qv, jv = carry
            for u in range(unroll):
                qv, jv = step(qv, jv, u % 2)
            return qv, jv

        k0 = k_ref[rows(first_pair[1]), :]
        for ci in range(len(chains)):
            score_chain(ci, first_pair[0], first_pair[1], 0, k0, biased)
        lax.fori_loop(0, count // unroll, body, (jnp.int32(first_pair[0]), jnp.int32(first_pair[1])))

    def next_near(qv, jv):
        wrap = jv == qv
        qn = qv + 1
        return jnp.where(wrap, qn, qv), jnp.where(wrap, jnp.where(qn < q_split, 0, qn - 1), jv + 1)

    def next_far(qv, jv):
        wrap = jv == qv - ATT_FAR_TILE
        return jnp.where(wrap, qv + 1, qv), jnp.where(wrap, 0, jv + 1)

    n_near = q_split * (q_split + 1) // 2 + ATT_FAR_TILE * (nq - q_split)
    n_far = nq * (nq + 1) // 2 - n_near
    lax.fori_loop(0, nq, prepare, 0)
    sweep((0, 0), next_near, (nq - 1, nq - 1), n_near, True)
    if n_far:
        sweep((q_split, 0), next_far, (nq - 1, nq - 1 - ATT_FAR_TILE), n_far, False)
    lax.fori_loop(0, nq // 2, finalize, 0)


def _attn_plan(nq):
    for unroll in (8, 4, 2):
        for q_split in range(ATT_FAR_TILE, nq + 1):
            n_near = q_split * (q_split + 1) // 2 + ATT_FAR_TILE * (nq - q_split)
            n_far = nq * (nq + 1) // 2 - n_near
            if n_near % unroll == 0 and n_far % unroll == 0:
                return unroll, q_split
    raise ValueError(f"unsupported number of query blocks: {nq}")


def _attn_call(qt, k, vt, tiles, lam, subln_w, *, batch, lam_init, t=ATT_T):
    T, W = k.shape
    S = T // batch
    H = W // LANES
    nq = S // t
    e = 2 * DIFF_HEAD_DIM
    assert e == LANES and S % t == 0 and nq % 2 == 0
    unroll, q_split = _attn_plan(nq)
    n_chains = 2 * (t // ATT_STRIP)
    n_tiles = ATT_FAR_TILE + 1
    nbytes = (2 * 4 * (S * LANES * 2) + 2 * n_tiles * t * t * 4 + 2 * S * LANES * 2
              + nq * 2 * (e + ATT_ONES + SUBLANES) * t * 4 + 2 * n_chains * t * ATT_STRIP * 4 + 2 * t * t * 4)
    return pl.pallas_call(
        functools.partial(_attn_kernel, t=t, nq=nq, unroll=unroll, q_split=q_split, lam_init=lam_init),
        grid=(batch, H),
        in_specs=[
            pl.BlockSpec((LANES, S), lambda b, h: (h, b)),
            pl.BlockSpec((S, LANES), lambda b, h: (b, h)),
            pl.BlockSpec((LANES, S), lambda b, h: (h, b)),
            pl.BlockSpec((None, n_tiles, t, t), lambda b, h: (h, 0, 0, 0)),
            pl.BlockSpec((SUBLANES, LANES), lambda b, h: (0, 0)),
            pl.BlockSpec((e, 1), lambda b, h: (0, 0)),
        ],
        out_specs=pl.BlockSpec((S, LANES), lambda b, h: (b, h)),
        out_shape=jax.ShapeDtypeStruct((T, W), BF16),
        scratch_shapes=[pltpu.VMEM((2, LANES, S), BF16),
                        pltpu.VMEM((nq, 2, 1, t), F32), pltpu.VMEM((nq, 2, e + ATT_ONES, t), F32),
                        pltpu.VMEM((2, n_chains, t, ATT_STRIP), F32),
                        pltpu.VMEM((2, n_chains, 1, ATT_STRIP), F32)],
        compiler_params=pltpu.CompilerParams(
            dimension_semantics=("arbitrary", "arbitrary"), vmem_limit_bytes=_vmem_limit(nbytes)),
        name="diff_attn",
    )(qt, k, vt, tiles, lam, subln_w.reshape(e, 1))


def _split3(a):
    hi = a.astype(BF16)
    r1 = a - hi.astype(F32)
    mid = r1.astype(BF16)
    lo = (r1 - mid.astype(F32)).astype(BF16)
    return hi, mid, lo


def _hyb_kernel(x_ref, nw_ref, w_ref, wvt_ref, cw_ref, cb_ref, dtb_ref, alog_ref, dsk_ref, ynw_ref,
                qt_ref, k_ref, vt_ref, y_ref, xpad_ref, z_scr, dt_scr, st_ref, *, tm, tiles_per_seq):
    i = pl.program_id(0)
    L = SSM_CHUNK
    gw = HEADS_PER_GROUP * SSM_HEAD_DIM
    qw = DIFF_WIDTH
    c_z = 2 * qw + DIFF_WIDTH
    c_xbc = c_z + SSM_WIDTH
    c_dt = c_xbc + SSM_XBC

    @pl.when(i == 0)
    def _():
        xpad_ref[...] = jnp.zeros(xpad_ref.shape, F32)
        z_scr[...] = jnp.zeros(z_scr.shape, F32)
        dt_scr[...] = jnp.zeros(dt_scr.shape, F32)
        st_ref[...] = jnp.zeros(st_ref.shape, F32)

    def stage(cur, prev):
        row = lax.broadcasted_iota(jnp.int32, (L, L), 0)
        col = lax.broadcasted_iota(jnp.int32, (L, L), 1)
        causal = row >= col
        tri = jnp.where(causal, 1.0, 0.0).astype(BF16)
        lane_g = lax.broadcasted_iota(jnp.int32, (L, gw), 1) // SSM_HEAD_DIM
        lane_lo = lax.broadcasted_iota(jnp.int32, (L, LANES), 1) < SSM_HEAD_DIM
        lane_lo1 = lax.broadcasted_iota(jnp.int32, (1, LANES), 1) < SSM_HEAD_DIM
        a_neg = -jnp.exp(alog_ref[...])
        keep_state = jnp.where((i - 1) % tiles_per_seq == 0, 0.0, 1.0)

        def ssd_chunk(c):
            r0 = c * L
            dt = dt_scr[prev, r0:r0 + L, :]
            hi, mid, lo = _split3(a_neg * dt)
            cs = _dot(tri, hi) + _dot(tri, mid) + _dot(tri, lo)
            yield

            conv = cb_ref[...]
            for kk in range(SSM_CONV):
                o = HALO - (SSM_CONV - 1) + kk + r0
                conv = conv + cw_ref[kk:kk + 1, :] * xpad_ref[prev, o:o + L, :]
            xc = _silu(conv)
            xs = xc[:, 0:SSM_WIDTH]
            bgs, cgs, gmats = [], [], []
            for g in range(SSM_GROUPS):
                b0 = SSM_WIDTH + g * SSM_STATE
                c0 = SSM_WIDTH + SSM_GROUPS * SSM_STATE + g * SSM_STATE
                bgs.append(xc[:, b0:b0 + SSM_STATE])
                cgs.append(xc[:, c0:c0 + SSM_STATE].astype(BF16))
                gmats.append(_dot_nt(cgs[g], bgs[g].astype(BF16)))
            yield

            cs_t = cs.T
            dt_t = dt.T
            ydiags, st_news, e_mats, decs_g = [], [], [], []
            for g in range(SSM_GROUPS):
                bg_t = bgs[g].T
                xg = xs[:, g * gw:(g + 1) * gw]
                ydiag = None
                st_new = None
                ecols, decs = [], []
                for hh in range(HEADS_PER_GROUP):
                    h = g * HEADS_PER_GROUP + hh
                    colb = jnp.broadcast_to(cs[:, h:h + 1], (L, L))
                    rowb = cs_t[h:h + 1, :]
                    dtrow = dt_t[h:h + 1, :]
                    decay = jnp.exp(jnp.where(causal, colb - rowb, NEG_BIG))
                    mh = (gmats[g] * decay * dtrow).astype(BF16)
                    xm = jnp.where(lane_g == hh, xg, 0.0).astype(BF16)
                    last = colb[L - 1:L, :]
                    wrow = dtrow * jnp.exp(last - rowb)
                    btw = (bg_t * wrow).astype(BF16)
                    d1 = _dot(mh, xm)
                    d2 = _dot(btw, xm)
                    ydiag = d1 if ydiag is None else ydiag + d1
                    st_new = d2 if st_new is None else st_new + d2
                    ecols.append(jnp.exp(colb))
                    decs.append(jnp.exp(last))
                ydiags.append(ydiag)
                st_news.append(st_new)
                e_mats.append(jnp.concatenate([jnp.where(lane_lo, ecols[0], ecols[1]),
                                               jnp.where(lane_lo, ecols[2], ecols[3])], axis=1))
                decs_g.append(jnp.concatenate([jnp.where(lane_lo1, decs[0], decs[1]),
                                               jnp.where(lane_lo1, decs[2], decs[3])], axis=1))
                if g == 0:
                    yield

            ys = []
            for g in range(SSM_GROUPS):
                st_old = st_ref[g] * keep_state if c == 0 else st_ref[g]
                ys.append(ydiags[g] + _dot(cgs[g], st_old.astype(BF16)) * e_mats[g])
                st_ref[g] = st_old * decs_g[g] + st_news[g]
            y = jnp.concatenate(ys, axis=1) + dsk_ref[...] * xs
            gated = y * z_scr[prev, r0:r0 + L, :]
            outs = []
            for g in range(SSM_GROUPS):
                part = gated[:, g * gw:(g + 1) * gw]
                ms = jnp.mean(part * part, axis=-1, keepdims=True)
                outs.append(part * lax.rsqrt(ms + 1e-5))
            y_ref[r0:r0 + L, :] = (jnp.concatenate(outs, axis=1) * ynw_ref[...]).astype(BF16)
            yield

        h = _rms(x_ref[...], nw_ref[...], 1e-6).astype(BF16)
        scale = DIFF_HEAD_DIM ** -0.5 * LOG2E
        piece = 2 * LANES

        def pieces():
            for r in range(0, qw, piece):
                qt_ref[r:r + piece, :] = (_dot_nt(wvt_ref[r:r + piece, :], h) * scale).astype(BF16)
                yield
            for c in range(0, qw, piece):
                k_ref[:, c:c + piece] = _dot(h, w_ref[:, qw + c:qw + c + piece]).astype(BF16)
                yield
            for r in range(0, DIFF_WIDTH, piece):
                vt_ref[r:r + piece, :] = _dot_nt(wvt_ref[qw + r:qw + r + piece, :], h).astype(BF16)
                yield
            for c in range(0, SSM_WIDTH, piece):
                z_scr[cur, :, c:c + piece] = _silu(_dot(h, w_ref[:, c_z + c:c_z + c + piece]))
                yield
            dt_raw = _dot(h, w_ref[:, c_dt:c_dt + LANES]) + dtb_ref[...]
            dt_scr[cur] = jnp.maximum(dt_raw, 0.0) + jnp.log1p(jnp.exp(-jnp.abs(dt_raw)))
            yield
            tail = xpad_ref[prev, tm:tm + HALO, :]
            xpad_ref[cur, 0:HALO, :] = jnp.where(i % tiles_per_seq == 0, jnp.zeros_like(tail), tail)
            for c in range(0, SSM_XBC, piece):
                xpad_ref[cur, HALO:HALO + tm, c:c + piece] = _dot(h, w_ref[:, c_xbc + c:c_xbc + c + piece])
                yield

        pieces_after_group = (0, 2, 1, 0)
        proj = pieces()
        for c in range(tm // L):
            for group, _ in enumerate(ssd_chunk(c)):
                for _ in range(pieces_after_group[group]):
                    next(proj, None)
        for _ in proj:
            pass

    pl.when(i % 2 == 0)(functools.partial(stage, 0, 1))
    pl.when(i % 2 == 1)(functools.partial(stage, 1, 0))


def _hyb_call(x, nw, w_pad, wvt, conv_w, conv_b, dt_bias, a_log, d_skip, norm_w, *, batch, tm=PROJ_TM):
    T, D = x.shape
    S = T // batch
    assert S % tm == 0 and tm % SSM_CHUNK == 0
    nt = T // tm
    cur = lambda i: (jnp.minimum(i, nt - 1), 0)
    cur_t = lambda i: (0, jnp.minimum(i, nt - 1))
    prev = lambda i: (jnp.maximum(i - 1, 0), 0)
    pad = lambda v: jnp.pad(v.astype(F32), (0, LANES - v.shape[0])).reshape(1, LANES)
    dsk = jnp.repeat(d_skip.astype(F32), SSM_HEAD_DIM).reshape(1, SSM_WIDTH)
    outs = [
        jax.ShapeDtypeStruct((DIFF_WIDTH, T), BF16),
        jax.ShapeDtypeStruct((T, DIFF_WIDTH), BF16),
        jax.ShapeDtypeStruct((DIFF_WIDTH, T), BF16),
        jax.ShapeDtypeStruct((T, SSM_WIDTH), BF16),
    ]
    out_specs = [pl.BlockSpec((DIFF_WIDTH, tm), cur_t), pl.BlockSpec((tm, DIFF_WIDTH), cur),
                 pl.BlockSpec((DIFF_WIDTH, tm), cur_t), pl.BlockSpec((tm, SSM_WIDTH), prev)]
    nbytes = ((w_pad.size + wvt.size) * 2 + 2 * tm * D * 4 + 2 * tm * 2 * 4 * DIFF_WIDTH
              + 2 * (tm + HALO) * SSM_XBC * 4 + 2 * tm * (SSM_WIDTH + LANES) * 4 + 16 * tm * SSM_XBC * 4)
    return pl.pallas_call(
        functools.partial(_hyb_kernel, tm=tm, tiles_per_seq=S // tm),
        grid=(nt + 1,),
        in_specs=[pl.BlockSpec((tm, D), cur), _resident((1, D)), _resident(w_pad.shape),
                  _resident(wvt.shape), _resident((SSM_CONV, SSM_XBC)), _resident((1, SSM_XBC)),
                  _resident((1, LANES)), _resident((1, LANES)), _resident((1, SSM_WIDTH)),
                  _resident((1, SSM_WIDTH))],
        out_specs=out_specs,
        out_shape=outs,
        scratch_shapes=[pltpu.VMEM((2, tm + HALO, SSM_XBC), F32), pltpu.VMEM((2, tm, SSM_WIDTH), F32),
                        pltpu.VMEM((2, tm, LANES), F32),
                        pltpu.VMEM((SSM_GROUPS, SSM_STATE, HEADS_PER_GROUP * SSM_HEAD_DIM), F32)],
        compiler_params=pltpu.CompilerParams(
            dimension_semantics=("arbitrary",), vmem_limit_bytes=_vmem_limit(nbytes)),
        name="hyb_ssd",
    )(x, nw.reshape(1, D), w_pad, wvt, conv_w.astype(F32), conv_b.reshape(1, SSM_XBC).astype(F32),
      pad(dt_bias), pad(a_log), dsk, norm_w.reshape(1, SSM_WIDTH).astype(F32))


def _sc_kernel(x_ref, xprev_ref, nw_ref, win_ref, cw_ref, wout_ref, o_ref, vpad_ref, bg_ref, *,
               tm, tiles_per_seq):
    i = pl.program_id(0)
    d = x_ref.shape[1]

    @pl.when(i == 0)
    def _():
        vpad_ref[...] = jnp.zeros(vpad_ref.shape, F32)
        bg_ref[...] = jnp.zeros(bg_ref.shape, F32)

    def finish_chunk(prev, c0, c1):
        conv = cw_ref[SC_CONV - 1:SC_CONV, c0:c1] * vpad_ref[prev, HALO:HALO + tm, c0:c1]
        for kk in range(SC_CONV - 1):
            o = HALO - (SC_CONV - 1) + kk
            conv = conv + cw_ref[kk:kk + 1, c0:c1] * vpad_ref[prev, o:o + tm, c0:c1]
        return _dot((bg_ref[prev, :, c0:c1] * conv).astype(BF16), wout_ref[c0:c1, :])

    def stage(cur, prev):
        cw = 2 * LANES
        chunks = [(c0, c0 + cw) for c0 in range(0, d, cw)]
        h = _rms(x_ref[...], nw_ref[...], 1e-6).astype(BF16)
        bgate = _dot(h, win_ref[:, 0:d])
        out = finish_chunk(prev, *chunks[0])
        cgate = _dot(h, win_ref[:, d:2 * d])
        out = out + finish_chunk(prev, *chunks[1])
        u = _dot(h, win_ref[:, 2 * d:3 * d])
        for c0, c1 in chunks[2:]:
            out = out + finish_chunk(prev, c0, c1)
        o_ref[...] = xprev_ref[...] + out

        tail = vpad_ref[prev, tm:tm + HALO, :]
        vpad_ref[cur, 0:HALO, :] = jnp.where(i % tiles_per_seq == 0, jnp.zeros_like(tail), tail)
        vpad_ref[cur, HALO:HALO + tm, :] = cgate * u
        bg_ref[cur] = bgate

    pl.when(i % 2 == 0)(functools.partial(stage, 0, 1))
    pl.when(i % 2 == 1)(functools.partial(stage, 1, 0))


def _sc_call(x, nw, w_in, conv_w, w_out, *, batch, tm=PROJ_TM):
    T, D = x.shape
    S = T // batch
    assert S % tm == 0
    nt = T // tm
    cur = lambda i: (jnp.minimum(i, nt - 1), 0)
    prev = lambda i: (jnp.maximum(i - 1, 0), 0)
    nbytes = (w_in.size + w_out.size) * 2 + 6 * tm * D * 4 + 4 * tm * D * 4 + 8 * tm * D * 4
    return pl.pallas_call(
        functools.partial(_sc_kernel, tm=tm, tiles_per_seq=S // tm),
        grid=(nt + 1,),
        in_specs=[pl.BlockSpec((tm, D), cur), pl.BlockSpec((tm, D), prev), _resident((1, D)),
                  _resident(w_in.shape), _resident(conv_w.shape), _resident(w_out.shape)],
        out_specs=pl.BlockSpec((tm, D), prev),
        out_shape=jax.ShapeDtypeStruct((T, D), F32),
        scratch_shapes=[pltpu.VMEM((2, tm + HALO, D), F32), pltpu.VMEM((2, tm, D), F32)],
        compiler_params=pltpu.CompilerParams(
            dimension_semantics=("arbitrary",), vmem_limit_bytes=_vmem_limit(nbytes)),
        name="short_conv",
    )(x, x, nw.reshape(1, D), w_in, conv_w.astype(F32), w_out)


def kernel(x, rel_bias, final_norm_w, ffn1_norm, ffn1_wg, ffn1_wu, ffn1_wd, mix_norm, ffn2_norm, ffn2_wg, ffn2_wu, ffn2_wd, hyb_w_in, hyb_w_out, diff_lq1, diff_lk1, diff_lq2, diff_lk2, diff_subln_w, ssm_conv_w, ssm_conv_b, ssm_dt_bias, ssm_a_log, ssm_d, ssm_norm_w, sc_w_in, sc_conv_w, sc_w_out):
    B, S, D = x.shape
    T = B * S
    xt = x.reshape(T, D)
    bf = lambda w: w.astype(BF16)

    lam_inits = [0.8 - 0.6 * math.exp(-0.3 * i) for i in range(0, DEPTH, 2)]
    tiles, lams = _prep_call(rel_bias, diff_lq1, diff_lk1, diff_lq2, diff_lk2, lam_inits)

    mix = None
    for i in range(DEPTH):
        j = i // 2
        xt = _ffn_call(xt, ffn1_norm[i], ffn1_wg, ffn1_wu, ffn1_wd, i)
        if i % 2 == 0:
            w_in = hyb_w_in[j]
            qkv_end = 3 * DIFF_WIDTH
            w_pad = bf(jnp.pad(w_in, ((0, 0), (0, LANES - SSM_HEADS))))
            wvt = bf(jnp.concatenate([w_in[:, 0:DIFF_WIDTH], w_in[:, 2 * DIFF_WIDTH:qkv_end]], axis=1).T)
            q, k, vt, y = _hyb_call(xt, mix_norm[i], w_pad, wvt, ssm_conv_w[j], ssm_conv_b[j],
                                    ssm_dt_bias[j], ssm_a_log[j], ssm_d[j], ssm_norm_w[j], batch=B)
            o = _attn_call(q, k, vt, tiles, lams[j * SUBLANES:(j + 1) * SUBLANES], diff_subln_w[j],
                           batch=B, lam_init=lam_inits[j])
            mix = (o, y, hyb_w_out, j)
        else:
            xt = _sc_call(xt, mix_norm[i], bf(sc_w_in[j]), sc_conv_w[j], bf(sc_w_out[j]), batch=B)
            mix = None
        xt = _ffn_call(xt, ffn2_norm[i], ffn2_wg, ffn2_wu, ffn2_wd, i, mix=mix,
                       final_w=final_norm_w if i == DEPTH - 1 else None)
    return xt.reshape(B, S, D)
```

```python
import functools
import math

import jax
import jax.numpy as jnp
from jax import lax
from jax.experimental import pallas as pl
from jax.experimental.pallas import tpu as pltpu

F32 = jnp.float32
BF16 = jnp.bfloat16

DEPTH = 4
N_DIFF_HEADS = 4
DIFF_HEAD_DIM = 64
DIFF_WIDTH = N_DIFF_HEADS * 2 * DIFF_HEAD_DIM
NUM_BUCKETS = 32
MAX_DISTANCE = 128
SSM_HEADS = 8
SSM_HEAD_DIM = 64
SSM_WIDTH = SSM_HEADS * SSM_HEAD_DIM
SSM_GROUPS = 2
SSM_STATE = 128
SSM_CONV = 4
SSM_CHUNK = 128
HEADS_PER_GROUP = SSM_HEADS // SSM_GROUPS
SSM_XBC = SSM_WIDTH + 2 * SSM_GROUPS * SSM_STATE
SC_CONV = 3

LANES = 128
SUBLANES = 8
V7X_SCOPED_VMEM_CAP = 60000 * 1024

FFN_TM = 512
FFN_FCHUNK = 256
PROJ_TM = 512
ATT_T = 512
ATT_STRIP = 256
ATT_ONES = 16
ATT_FAR_TILE = 2
HALO = SUBLANES

NEG_BIG = -1e30
LOG2E = math.log2(math.e)


def _vmem_limit(nbytes):
    return int(min(V7X_SCOPED_VMEM_CAP, nbytes * 5 // 4 + (8 << 20)))


def _dot(a, b):
    return jnp.dot(a, b, preferred_element_type=F32)


def _dot_nt(a, b):
    return lax.dot_general(a, b, (((1,), (1,)), ((), ())), preferred_element_type=F32)


def _rms(x, w, eps):
    ms = jnp.mean(x * x, axis=-1, keepdims=True)
    return x * lax.rsqrt(ms + eps) * w


def _silu(x):
    h = 0.5 * x
    return h + h * jnp.tanh(h)


def _resident(shape):
    nd = len(shape)
    return pl.BlockSpec(shape, lambda *_: (0,) * nd, pipeline_mode=pl.Buffered(1))


def _ffn_kernel(*refs, pre_mix, final_norm, f_total):
    refs = list(refs)
    x_ref = refs.pop(0)
    if pre_mix:
        o_ref, y_ref, wo_ref = refs.pop(0), refs.pop(0), refs.pop(0)
    nw_ref, wg_ref, wu_ref, wd_ref = refs.pop(0), refs.pop(0), refs.pop(0), refs.pop(0)
    if final_norm:
        fw_ref = refs.pop(0)
    out_ref = refs.pop(0)

    x = x_ref[...]
    if pre_mix:
        half = o_ref.shape[1]
        x = (x + _dot(o_ref[...], wo_ref[0:half, :].astype(BF16))
             + _dot(y_ref[...], wo_ref[half:, :].astype(BF16)))
    h = _rms(x, nw_ref[...], 1e-6).astype(BF16)
    acts = []
    for c0 in range(0, f_total, FFN_FCHUNK):
        c1 = c0 + FFN_FCHUNK
        g = _dot(h, wg_ref[:, c0:c1].astype(BF16))
        u = _dot(h, wu_ref[:, c0:c1].astype(BF16))
        acts.append((_silu(g) * u).astype(BF16))
    acc = _dot(jnp.concatenate(acts, axis=1), wd_ref[...].astype(BF16))
    y = x + 0.5 * acc
    if final_norm:
        y = _rms(y, fw_ref[...], 1e-6)
    out_ref[...] = y


def _layer_resident(stacked, layer):
    _, r, c = stacked.shape
    return pl.BlockSpec((None, r, c), lambda *_: (layer, 0, 0), pipeline_mode=pl.Buffered(1))


def _ffn_call(x, nw, wg, wu, wd, layer, *, mix=None, final_w=None, tm=FFN_TM):
    T, D = x.shape
    F = wg.shape[2]
    assert T % tm == 0 and F % FFN_FCHUNK == 0
    row = lambda i: (i, 0)
    args = [x]
    in_specs = [pl.BlockSpec((tm, D), row)]
    nbytes = 4 * tm * D * 4 + 3 * D * F * 4 + 6 * tm * FFN_FCHUNK * 4 + 3 * tm * D * 4
    if mix is not None:
        o, y, wo, mix_layer = mix
        args += [o, y, wo]
        in_specs += [pl.BlockSpec((tm, o.shape[1]), row), pl.BlockSpec((tm, y.shape[1]), row),
                     _layer_resident(wo, mix_layer)]
        nbytes += wo.shape[1] * wo.shape[2] * 4 + 4 * tm * o.shape[1] * 2 * 2
    args += [nw.reshape(1, D), wg, wu, wd]
    in_specs += [_resident((1, D)), _layer_resident(wg, layer), _layer_resident(wu, layer),
                 _layer_resident(wd, layer)]
    if final_w is not None:
        args.append(final_w.reshape(1, D))
        in_specs.append(_resident((1, D)))
    kern = functools.partial(_ffn_kernel, pre_mix=mix is not None, final_norm=final_w is not None,
                             f_total=F)
    return pl.pallas_call(
        kern,
        grid=(T // tm,),
        in_specs=in_specs,
        out_specs=pl.BlockSpec((tm, D), row),
        out_shape=jax.ShapeDtypeStruct((T, D), F32),
        compiler_params=pltpu.CompilerParams(
            dimension_semantics=("arbitrary",), vmem_limit_bytes=_vmem_limit(nbytes)),
        name="ffn",
    )(*args)


def _prep_kernel(tab_ref, lq1_ref, lk1_ref, lq2_ref, lk2_ref, tile_ref, lam_ref, *, t, lam_inits):
    hd = pl.program_id(0)
    key = lax.broadcasted_iota(jnp.int32, (t, t), 0)
    qry = lax.broadcasted_iota(jnp.int32, (t, t), 1)
    max_exact = NUM_BUCKETS // 2
    far = tab_ref[hd, NUM_BUCKETS - 1]
    tile_ref[ATT_FAR_TILE] = jnp.zeros((t, t), F32)
    for off in range(ATT_FAR_TILE):
        dist = off * t + qry - key
        d = jnp.maximum(dist, 0)
        large = max_exact + (
            jnp.log(jnp.maximum(d, 1).astype(F32) / max_exact)
            / math.log(MAX_DISTANCE / max_exact) * (NUM_BUCKETS - max_exact)).astype(jnp.int32)
        bucket = jnp.where(d < max_exact, d, jnp.minimum(large, NUM_BUCKETS - 1))
        bias = jnp.full((t, t), far, F32)
        for b in range(NUM_BUCKETS - 1):
            bias = jnp.where(bucket == b, tab_ref[hd, b], bias)
        val = (bias - far) * LOG2E
        if off == 0:
            val = jnp.where(dist >= 0, val, NEG_BIG)
        tile_ref[off] = val
    for j, lam_init in enumerate(lam_inits):
        s1 = jnp.sum(lq1_ref[j:j + 1, :] * lk1_ref[j:j + 1, :], axis=-1, keepdims=True)
        s2 = jnp.sum(lq2_ref[j:j + 1, :] * lk2_ref[j:j + 1, :], axis=-1, keepdims=True)
        lam = jnp.exp(s1) - jnp.exp(s2) + lam_init
        lam_ref[j * SUBLANES:(j + 1) * SUBLANES, :] = jnp.broadcast_to(lam, (SUBLANES, LANES))


def _prep_call(rel_bias, lq1, lk1, lq2, lk2, lam_inits, *, t=ATT_T):
    ne = lq1.shape[0]
    H = rel_bias.shape[1]
    whole = lambda shape: pl.BlockSpec(shape, lambda h: (0,) * len(shape))
    return pl.pallas_call(
        functools.partial(_prep_kernel, t=t, lam_inits=tuple(lam_inits)),
        grid=(H,),
        in_specs=[pl.BlockSpec(memory_space=pltpu.SMEM)] + [whole(lq1.shape)] * 4,
        out_specs=[pl.BlockSpec((None, ATT_FAR_TILE + 1, t, t), lambda h: (h, 0, 0, 0)),
                   whole((ne * SUBLANES, LANES))],
        out_shape=[jax.ShapeDtypeStruct((H, ATT_FAR_TILE + 1, t, t), F32),
                   jax.ShapeDtypeStruct((ne * SUBLANES, LANES), F32)],
        compiler_params=pltpu.CompilerParams(
            dimension_semantics=("arbitrary",), vmem_limit_bytes=_vmem_limit(12 * t * t * 4)),
        name="attn_prep",
    )(rel_bias.T.astype(F32), lq1, lk1, lq2, lk2)


def _attn_kernel(qt_ref, k_ref, vt_ref, tile_ref, lam_ref, sw_ref, o_ref,
                 qm_ref, m_ref, acc_ref, s_ref, smax_ref, *, t, nq, unroll, q_split, lam_init):
    e = 2 * DIFF_HEAD_DIM
    first = lax.broadcasted_iota(jnp.int32, (LANES, t), 0) < DIFF_HEAD_DIM
    ones = jnp.ones((ATT_ONES, t), BF16)
    chains = [(w, slice(c0, c0 + ATT_STRIP)) for c0 in range(0, t, ATT_STRIP) for w in range(2)]

    def rows(blk):
        return pl.ds(pl.multiple_of(blk * t, t), t)

    def prepare(qi, carry):
        q = qt_ref[:, rows(qi)]
        zero = jnp.zeros_like(q)
        qm_ref[0, :, rows(qi)] = jnp.where(first, q, zero)
        qm_ref[1, :, rows(qi)] = jnp.where(first, zero, q)
        m_ref[qi] = jnp.full(m_ref.shape[1:], NEG_BIG, F32)
        acc_ref[qi] = jnp.zeros(acc_ref.shape[1:], F32)
        return carry

    def score_chain(ci, qi, j, slot, k, biased):
        w, cols = chains[ci]
        q_cols = pl.ds(pl.multiple_of(qi * t + cols.start, ATT_STRIP), ATT_STRIP)
        s = _dot(k, qm_ref[w, :, q_cols])
        if biased:
            near = jnp.minimum(qi - j, ATT_FAR_TILE)
            s = s + tile_ref[near, :, cols]
        s_ref[slot, ci] = s
        smax_ref[slot, ci] = jnp.max(s, axis=0, keepdims=True)

    def value_chain(ci, qi, slot, vta):
        w, cols = chains[ci]
        m_old = m_ref[qi, w, :, cols]
        m_new = jnp.maximum(m_old, smax_ref[slot, ci])
        p = jnp.exp2(s_ref[slot, ci] - m_new).astype(BF16)
        acc_ref[qi, w, :, cols] = jnp.exp2(m_old - m_new) * acc_ref[qi, w, :, cols] + _dot(vta, p)
        m_ref[qi, w, :, cols] = m_new

    def finalize(pair, carry):
        for qi in (2 * pair, 2 * pair + 1):
            lam = lam_ref[0:1, 0:1]
            o1 = acc_ref[qi, 0, 0:e, :] * (1.0 / acc_ref[qi, 0, e:e + 1, :])
            o2 = acc_ref[qi, 1, 0:e, :] * (1.0 / acc_ref[qi, 1, e:e + 1, :])
            out_t = o1 - lam * o2
            ms = jnp.mean(out_t * out_t, axis=0, keepdims=True)
            out_t = out_t * lax.rsqrt(ms + 1e-5) * sw_ref[...] * (1.0 - lam_init)
            o_ref[rows(qi), :] = out_t.T.astype(BF16)
        return carry

    def sweep(first_pair, next_pair, last_pair, count, biased):
        def step(qv, jv, slot):
            qs, js = next_pair(qv, jv)
            past_end = qs >= nq
            qs = jnp.where(past_end, last_pair[0], qs)
            js = jnp.where(past_end, last_pair[1], js)
            k_next = k_ref[rows(js), :]
            vta = jnp.concatenate([vt_ref[:, rows(jv)], ones], axis=0)
            for ci in range(len(chains)):
                score_chain(ci, qs, js, 1 - slot, k_next, biased)
                value_chain(ci, qv, slot, vta)
            return qs, js

        def body(_, carry):
            qv, jv = carry
            for u in range(unroll):
                qv, jv = step(qv, jv, u % 2)
            return qv, jv

        k0 = k_ref[rows(first_pair[1]), :]
        for ci in range(len(chains)):
            score_chain(ci, first_pair[0], first_pair[1], 0, k0, biased)
        lax.fori_loop(0, count // unroll, body, (jnp.int32(first_pair[0]), jnp.int32(first_pair[1])))

    def next_near(qv, jv):
        wrap = jv == qv
        qn = qv + 1
        return jnp.where(wrap, qn, qv), jnp.where(wrap, jnp.where(qn < q_split, 0, qn - 1), jv + 1)

    def next_far(qv, jv):
        wrap = jv == qv - ATT_FAR_TILE
        return jnp.where(wrap, qv + 1, qv), jnp.where(wrap, 0, jv + 1)

    n_near = q_split * (q_split + 1) // 2 + ATT_FAR_TILE * (nq - q_split)
    n_far = nq * (nq + 1) // 2 - n_near
    lax.fori_loop(0, nq, prepare, 0)
    sweep((0, 0), next_near, (nq - 1, nq - 1), n_near, True)
    if n_far:
        sweep((q_split, 0), next_far, (nq - 1, nq - 1 - ATT_FAR_TILE), n_far, False)
    lax.fori_loop(0, nq // 2, finalize, 0)


def _attn_plan(nq):
    for unroll in (8, 4, 2):
        for q_split in range(ATT_FAR_TILE, nq + 1):
            n_near = q_split * (q_split + 1) // 2 + ATT_FAR_TILE * (nq - q_split)
            n_far = nq * (nq + 1) // 2 - n_near
            if n_near % unroll == 0 and n_far % unroll == 0:
                return unroll, q_split
    raise ValueError(f"unsupported number of query blocks: {nq}")


def _attn_call(qt, k, vt, tiles, lam, subln_w, *, batch, lam_init, t=ATT_T):
    T, W = k.shape
    S = T // batch
    H = W // LANES
    nq = S // t
    e = 2 * DIFF_HEAD_DIM
    assert e == LANES and S % t == 0 and nq % 2 == 0
    unroll, q_split = _attn_plan(nq)
    n_chains = 2 * (t // ATT_STRIP)
    n_tiles = ATT_FAR_TILE + 1
    nbytes = (2 * 4 * (S * LANES * 2) + 2 * n_tiles * t * t * 4 + 2 * S * LANES * 2
              + nq * 2 * (e + ATT_ONES + SUBLANES) * t * 4 + 2 * n_chains * t * ATT_STRIP * 4 + 2 * t * t * 4)
    return pl.pallas_call(
        functools.partial(_attn_kernel, t=t, nq=nq, unroll=unroll, q_split=q_split, lam_init=lam_init),
        grid=(batch, H),
        in_specs=[
            pl.BlockSpec((LANES, S), lambda b, h: (h, b)),
            pl.BlockSpec((S, LANES), lambda b, h: (b, h)),
            pl.BlockSpec((LANES, S), lambda b, h: (h, b)),
            pl.BlockSpec((None, n_tiles, t, t), lambda b, h: (h, 0, 0, 0)),
            pl.BlockSpec((SUBLANES, LANES), lambda b, h: (0, 0)),
            pl.BlockSpec((e, 1), lambda b, h: (0, 0)),
        ],
        out_specs=pl.BlockSpec((S, LANES), lambda b, h: (b, h)),
        out_shape=jax.ShapeDtypeStruct((T, W), BF16),
        scratch_shapes=[pltpu.VMEM((2, LANES, S), BF16),
                        pltpu.VMEM((nq, 2, 1, t), F32), pltpu.VMEM((nq, 2, e + ATT_ONES, t), F32),
                        pltpu.VMEM((2, n_chains, t, ATT_STRIP), F32),
                        pltpu.VMEM((2, n_chains, 1, ATT_STRIP), F32)],
        compiler_params=pltpu.CompilerParams(
            dimension_semantics=("arbitrary", "arbitrary"), vmem_limit_bytes=_vmem_limit(nbytes)),
        name="diff_attn",
    )(qt, k, vt, tiles, lam, subln_w.reshape(e, 1))


def _split3(a):
    hi = a.astype(BF16)
    r1 = a - hi.astype(F32)
    mid = r1.astype(BF16)
    lo = (r1 - mid.astype(F32)).astype(BF16)
    return hi, mid, lo


def _hyb_kernel(x_ref, nw_ref, w_ref, wvt_ref, cw_ref, cb_ref, dtb_ref, alog_ref, dsk_ref, ynw_ref,
                qt_ref, k_ref, vt_ref, y_ref, xpad_ref, z_scr, dt_scr, st_ref, *, tm, tiles_per_seq):
    i = pl.program_id(0)
    L = SSM_CHUNK
    gw = HEADS_PER_GROUP * SSM_HEAD_DIM
    qw = DIFF_WIDTH
    c_z = 2 * qw + DIFF_WIDTH
    c_xbc = c_z + SSM_WIDTH
    c_dt = c_xbc + SSM_XBC

    @pl.when(i == 0)
    def _():
        xpad_ref[...] = jnp.zeros(xpad_ref.shape, F32)
        z_scr[...] = jnp.zeros(z_scr.shape, F32)
        dt_scr[...] = jnp.zeros(dt_scr.shape, F32)
        st_ref[...] = jnp.zeros(st_ref.shape, F32)

    def stage(cur, prev):
        row = lax.broadcasted_iota(jnp.int32, (L, L), 0)
        col = lax.broadcasted_iota(jnp.int32, (L, L), 1)
        causal = row >= col
        tri = jnp.where(causal, 1.0, 0.0).astype(BF16)
        lane_g = lax.broadcasted_iota(jnp.int32, (L, gw), 1) // SSM_HEAD_DIM
        lane_lo = lax.broadcasted_iota(jnp.int32, (L, LANES), 1) < SSM_HEAD_DIM
        lane_lo1 = lax.broadcasted_iota(jnp.int32, (1, LANES), 1) < SSM_HEAD_DIM
        a_neg = -jnp.exp(alog_ref[...])
        keep_state = jnp.where((i - 1) % tiles_per_seq == 0, 0.0, 1.0)

        def ssd_chunk(c):
            r0 = c * L
            dt = dt_scr[prev, r0:r0 + L, :]
            hi, mid, lo = _split3(a_neg * dt)
            cs = _dot(tri, hi) + _dot(tri, mid) + _dot(tri, lo)
            yield

            conv = cb_ref[...]
            for kk in range(SSM_CONV):
                o = HALO - (SSM_CONV - 1) + kk + r0
                conv = conv + cw_ref[kk:kk + 1, :] * xpad_ref[prev, o:o + L, :]
            xc = _silu(conv)
            xs = xc[:, 0:SSM_WIDTH]
            bgs, cgs, gmats = [], [], []
            for g in range(SSM_GROUPS):
                b0 = SSM_WIDTH + g * SSM_STATE
                c0 = SSM_WIDTH + SSM_GROUPS * SSM_STATE + g * SSM_STATE
                bgs.append(xc[:, b0:b0 + SSM_STATE])
                cgs.append(xc[:, c0:c0 + SSM_STATE].astype(BF16))
                gmats.append(_dot_nt(cgs[g], bgs[g].astype(BF16)))
            yield

            cs_t = cs.T
            dt_t = dt.T
            ydiags, st_news, e_mats, decs_g = [], [], [], []
            for g in range(SSM_GROUPS):
                bg_t = bgs[g].T
                xg = xs[:, g * gw:(g + 1) * gw]
                ydiag = None
                st_new = None
                ecols, decs = [], []
                for hh in range(HEADS_PER_GROUP):
                    h = g * HEADS_PER_GROUP + hh
                    colb = jnp.broadcast_to(cs[:, h:h + 1], (L, L))
                    rowb = cs_t[h:h + 1, :]
                    dtrow = dt_t[h:h + 1, :]
                    decay = jnp.exp(jnp.where(causal, colb - rowb, NEG_BIG))
                    mh = (gmats[g] * decay * dtrow).astype(BF16)
                    xm = jnp.where(lane_g == hh, xg, 0.0).astype(BF16)
                    last = colb[L - 1:L, :]
                    wrow = dtrow * jnp.exp(last - rowb)
                    btw = (bg_t * wrow).astype(BF16)
                    d1 = _dot(mh, xm)
                    d2 = _dot(btw, xm)
                    ydiag = d1 if ydiag is None else ydiag + d1
                    st_new = d2 if st_new is None else st_new + d2
                    ecols.append(jnp.exp(colb))
                    decs.append(jnp.exp(last))
                ydiags.append(ydiag)
                st_news.append(st_new)
                e_mats.append(jnp.concatenate([jnp.where(lane_lo, ecols[0], ecols[1]),
                                               jnp.where(lane_lo, ecols[2], ecols[3])], axis=1))
                decs_g.append(jnp.concatenate([jnp.where(lane_lo1, decs[0], decs[1]),
                                               jnp.where(lane_lo1, decs[2], decs[3])], axis=1))
                if g == 0:
                    yield

            ys = []
            for g in range(SSM_GROUPS):
                st_old = st_ref[g] * keep_state if c == 0 else st_ref[g]
                ys.append(ydiags[g] + _dot(cgs[g], st_old.astype(BF16)) * e_mats[g])
                st_ref[g] = st_old * decs_g[g] + st_news[g]
            y = jnp.concatenate(ys, axis=1) + dsk_ref[...] * xs
            gated = y * z_scr[prev, r0:r0 + L, :]
            outs = []
            for g in range(SSM_GROUPS):
                part = gated[:, g * gw:(g + 1) * gw]
                ms = jnp.mean(part * part, axis=-1, keepdims=True)
                outs.append(part * lax.rsqrt(ms + 1e-5))
            y_ref[r0:r0 + L, :] = (jnp.concatenate(outs, axis=1) * ynw_ref[...]).astype(BF16)
            yield

        h = _rms(x_ref[...], nw_ref[...], 1e-6).astype(BF16)
        scale = DIFF_HEAD_DIM ** -0.5 * LOG2E
        piece = 2 * LANES

        def pieces():
            for r in range(0, qw, piece):
                qt_ref[r:r + piece, :] = (_dot_nt(wvt_ref[r:r + piece, :], h) * scale).astype(BF16)
                yield
            for c in range(0, qw, piece):
                k_ref[:, c:c + piece] = _dot(h, w_ref[:, qw + c:qw + c + piece]).astype(BF16)
                yield
            for r in range(0, DIFF_WIDTH, piece):
                vt_ref[r:r + piece, :] = _dot_nt(wvt_ref[qw + r:qw + r + piece, :], h).astype(BF16)
                yield
            for c in range(0, SSM_WIDTH, piece):
                z_scr[cur, :, c:c + piece] = _silu(_dot(h, w_ref[:, c_z + c:c_z + c + piece]))
                yield
            dt_raw = _dot(h, w_ref[:, c_dt:c_dt + LANES]) + dtb_ref[...]
            dt_scr[cur] = jnp.maximum(dt_raw, 0.0) + jnp.log1p(jnp.exp(-jnp.abs(dt_raw)))
            yield
            tail = xpad_ref[prev, tm:tm + HALO, :]
            xpad_ref[cur, 0:HALO, :] = jnp.where(i % tiles_per_seq == 0, jnp.zeros_like(tail), tail)
            for c in range(0, SSM_XBC, piece):
                xpad_ref[cur, HALO:HALO + tm, c:c + piece] = _dot(h, w_ref[:, c_xbc + c:c_xbc + c + piece])
                yield

        pieces_after_group = (0, 2, 1, 0)
        proj = pieces()
        for c in range(tm // L):
            for group, _ in enumerate(ssd_chunk(c)):
                for _ in range(pieces_after_group[group]):
                    next(proj, None)
        for _ in proj:
            pass

    pl.when(i % 2 == 0)(functools.partial(stage, 0, 1))
    pl.when(i % 2 == 1)(functools.partial(stage, 1, 0))


def _hyb_call(x, nw, w_pad, wvt, conv_w, conv_b, dt_bias, a_log, d_skip, norm_w, *, batch, tm=PROJ_TM):
    T, D = x.shape
    S = T // batch
    assert S % tm == 0 and tm % SSM_CHUNK == 0
    nt = T // tm
    cur = lambda i: (jnp.minimum(i, nt - 1), 0)
    cur_t = lambda i: (0, jnp.minimum(i, nt - 1))
    prev = lambda i: (jnp.maximum(i - 1, 0), 0)
    pad = lambda v: jnp.pad(v.astype(F32), (0, LANES - v.shape[0])).reshape(1, LANES)
    dsk = jnp.repeat(d_skip.astype(F32), SSM_HEAD_DIM).reshape(1, SSM_WIDTH)
    outs = [
        jax.ShapeDtypeStruct((DIFF_WIDTH, T), BF16),
        jax.ShapeDtypeStruct((T, DIFF_WIDTH), BF16),
        jax.ShapeDtypeStruct((DIFF_WIDTH, T), BF16),
        jax.ShapeDtypeStruct((T, SSM_WIDTH), BF16),
    ]
    out_specs = [pl.BlockSpec((DIFF_WIDTH, tm), cur_t), pl.BlockSpec((tm, DIFF_WIDTH), cur),
                 pl.BlockSpec((DIFF_WIDTH, tm), cur_t), pl.BlockSpec((tm, SSM_WIDTH), prev)]
    nbytes = ((w_pad.size + wvt.size) * 2 + 2 * tm * D * 4 + 2 * tm * 2 * 4 * DIFF_WIDTH
              + 2 * (tm + HALO) * SSM_XBC * 4 + 2 * tm * (SSM_WIDTH + LANES) * 4 + 16 * tm * SSM_XBC * 4)
    return pl.pallas_call(
        functools.partial(_hyb_kernel, tm=tm, tiles_per_seq=S // tm),
        grid=(nt + 1,),
        in_specs=[pl.BlockSpec((tm, D), cur), _resident((1, D)), _resident(w_pad.shape),
                  _resident(wvt.shape), _resident((SSM_CONV, SSM_XBC)), _resident((1, SSM_XBC)),
                  _resident((1, LANES)), _resident((1, LANES)), _resident((1, SSM_WIDTH)),
                  _resident((1, SSM_WIDTH))],
        out_specs=out_specs,
        out_shape=outs,
        scratch_shapes=[pltpu.VMEM((2, tm + HALO, SSM_XBC), F32), pltpu.VMEM((2, tm, SSM_WIDTH), F32),
                        pltpu.VMEM((2, tm, LANES), F32),
                        pltpu.VMEM((SSM_GROUPS, SSM_STATE, HEADS_PER_GROUP * SSM_HEAD_DIM), F32)],
        compiler_params=pltpu.CompilerParams(
            dimension_semantics=("arbitrary",), vmem_limit_bytes=_vmem_limit(nbytes)),
        name="hyb_ssd",
    )(x, nw.reshape(1, D), w_pad, wvt, conv_w.astype(F32), conv_b.reshape(1, SSM_XBC).astype(F32),
      pad(dt_bias), pad(a_log), dsk, norm_w.reshape(1, SSM_WIDTH).astype(F32))


def _sc_kernel(x_ref, xprev_ref, nw_ref, win_ref, cw_ref, wout_ref, o_ref, vpad_ref, bg_ref, *,
               tm, tiles_per_seq):
    i = pl.program_id(0)
    d = x_ref.shape[1]

    @pl.when(i == 0)
    def _():
        vpad_ref[...] = jnp.zeros(vpad_ref.shape, F32)
        bg_ref[...] = jnp.zeros(bg_ref.shape, F32)

    def finish_chunk(prev, c0, c1):
        conv = cw_ref[SC_CONV - 1:SC_CONV, c0:c1] * vpad_ref[prev, HALO:HALO + tm, c0:c1]
        for kk in range(SC_CONV - 1):
            o = HALO - (SC_CONV - 1) + kk
            conv = conv + cw_ref[kk:kk + 1, c0:c1] * vpad_ref[prev, o:o + tm, c0:c1]
        return _dot((bg_ref[prev, :, c0:c1] * conv).astype(BF16), wout_ref[c0:c1, :])

    def stage(cur, prev):
        cw = 2 * LANES
        chunks = [(c0, c0 + cw) for c0 in range(0, d, cw)]
        h = _rms(x_ref[...], nw_ref[...], 1e-6).astype(BF16)
        bgate = _dot(h, win_ref[:, 0:d])
        out = finish_chunk(prev, *chunks[0])
        cgate = _dot(h, win_ref[:, d:2 * d])
        out = out + finish_chunk(prev, *chunks[1])
        u = _dot(h, win_ref[:, 2 * d:3 * d])
        for c0, c1 in chunks[2:]:
            out = out + finish_chunk(prev, c0, c1)
        o_ref[...] = xprev_ref[...] + out

        tail = vpad_ref[prev, tm:tm + HALO, :]
        vpad_ref[cur, 0:HALO, :] = jnp.where(i % tiles_per_seq == 0, jnp.zeros_like(tail), tail)
        vpad_ref[cur, HALO:HALO + tm, :] = cgate * u
        bg_ref[cur] = bgate

    pl.when(i % 2 == 0)(functools.partial(stage, 0, 1))
    pl.when(i % 2 == 1)(functools.partial(stage, 1, 0))


def _sc_call(x, nw, w_in, conv_w, w_out, *, batch, tm=PROJ_TM):
    T, D = x.shape
    S = T // batch
    assert S % tm == 0
    nt = T // tm
    cur = lambda i: (jnp.minimum(i, nt - 1), 0)
    prev = lambda i: (jnp.maximum(i - 1, 0), 0)
    nbytes = (w_in.size + w_out.size) * 2 + 6 * tm * D * 4 + 4 * tm * D * 4 + 8 * tm * D * 4
    return pl.pallas_call(
        functools.partial(_sc_kernel, tm=tm, tiles_per_seq=S // tm),
        grid=(nt + 1,),
        in_specs=[pl.BlockSpec((tm, D), cur), pl.BlockSpec((tm, D), prev), _resident((1, D)),
                  _resident(w_in.shape), _resident(conv_w.shape), _resident(w_out.shape)],
        out_specs=pl.BlockSpec((tm, D), prev),
        out_shape=jax.ShapeDtypeStruct((T, D), F32),
        scratch_shapes=[pltpu.VMEM((2, tm + HALO, D), F32), pltpu.VMEM((2, tm, D), F32)],
        compiler_params=pltpu.CompilerParams(
            dimension_semantics=("arbitrary",), vmem_limit_bytes=_vmem_limit(nbytes)),
        name="short_conv",
    )(x, x, nw.reshape(1, D), w_in, conv_w.astype(F32), w_out)


def kernel(x, rel_bias, final_norm_w, ffn1_norm, ffn1_wg, ffn1_wu, ffn1_wd, mix_norm, ffn2_norm, ffn2_wg, ffn2_wu, ffn2_wd, hyb_w_in, hyb_w_out, diff_lq1, diff_lk1, diff_lq2, diff_lk2, diff_subln_w, ssm_conv_w, ssm_conv_b, ssm_dt_bias, ssm_a_log, ssm_d, ssm_norm_w, sc_w_in, sc_conv_w, sc_w_out):
    B, S, D = x.shape
    T = B * S
    xt = x.reshape(T, D)
    bf = lambda w: w.astype(BF16)

    lam_inits = [0.8 - 0.6 * math.exp(-0.3 * i) for i in range(0, DEPTH, 2)]
    tiles, lams = _prep_call(rel_bias, diff_lq1, diff_lk1, diff_lq2, diff_lk2, lam_inits)

    mix = None
    for i in range(DEPTH):
        j = i // 2
        xt = _ffn_call(xt, ffn1_norm[i], ffn1_wg, ffn1_wu, ffn1_wd, i)
        if i % 2 == 0:
            w_in = hyb_w_in[j]
            qkv_end = 3 * DIFF_WIDTH
            w_pad = bf(jnp.pad(w_in, ((0, 0), (0, LANES - SSM_HEADS))))
            wvt = bf(jnp.concatenate([w_in[:, 0:DIFF_WIDTH], w_in[:, 2 * DIFF_WIDTH:qkv_end]], axis=1).T)
            q, k, vt, y = _hyb_call(xt, mix_norm[i], w_pad, wvt, ssm_conv_w[j], ssm_conv_b[j],
                                    ssm_dt_bias[j], ssm_a_log[j], ssm_d[j], ssm_norm_w[j], batch=B)
            o = _attn_call(q, k, vt, tiles, lams[j * SUBLANES:(j + 1) * SUBLANES], diff_subln_w[j],
                           batch=B, lam_init=lam_inits[j])
            mix = (o, y, hyb_w_out, j)
        else:
            xt = _sc_call(xt, mix_norm[i], bf(sc_w_in[j]), sc_conv_w[j], bf(sc_w_out[j]), batch=B)
            mix = None
        xt = _ffn_call(xt, ffn2_norm[i], ffn2_wg, ffn2_wu, ffn2_wd, i, mix=mix,
                       final_w=final_norm_w if i == DEPTH - 1 else None)
    return xt.reshape(B, S, D)
```

```python
import functools
import math

import jax
import jax.numpy as jnp
from jax import lax
from jax.experimental import pallas as pl
from jax.experimental.pallas import tpu as pltpu

F32 = jnp.float32
BF16 = jnp.bfloat16

DEPTH = 4
N_DIFF_HEADS = 4
DIFF_HEAD_DIM = 64
DIFF_WIDTH = N_DIFF_HEADS * 2 * DIFF_HEAD_DIM
NUM_BUCKETS = 32
MAX_DISTANCE = 128
SSM_HEADS = 8
SSM_HEAD_DIM = 64
SSM_WIDTH = SSM_HEADS * SSM_HEAD_DIM
SSM_GROUPS = 2
SSM_STATE = 128
SSM_CONV = 4
SSM_CHUNK = 128
HEADS_PER_GROUP = SSM_HEADS // SSM_GROUPS
SSM_XBC = SSM_WIDTH + 2 * SSM_GROUPS * SSM_STATE
SC_CONV = 3

LANES = 128
SUBLANES = 8
V7X_SCOPED_VMEM_CAP = 60000 * 1024

FFN_TM = 512
FFN_FCHUNK = 256
PROJ_TM = 512
ATT_T = 512
ATT_STRIP = 256
ATT_ONES = 16
ATT_MAX_UNROLL = 26
ATT_FAR_TILE = 2
HALO = SUBLANES

NEG_BIG = -1e30
LOG2E = math.log2(math.e)


def _vmem_limit(nbytes):
    return int(min(V7X_SCOPED_VMEM_CAP, nbytes * 5 // 4 + (8 << 20)))


def _dot(a, b):
    return jnp.dot(a, b, preferred_element_type=F32)


def _dot_nt(a, b):
    return lax.dot_general(a, b, (((1,), (1,)), ((), ())), preferred_element_type=F32)


def _rms(x, w, eps):
    ms = jnp.mean(x * x, axis=-1, keepdims=True)
    return x * lax.rsqrt(ms + eps) * w


def _silu(x):
    h = 0.5 * x
    return h + h * jnp.tanh(h)


def _resident(shape):
    nd = len(shape)
    return pl.BlockSpec(shape, lambda *_: (0,) * nd, pipeline_mode=pl.Buffered(1))


def _ffn_kernel(*refs, pre_mix, final_norm, f_total):
    refs = list(refs)
    x_ref = refs.pop(0)
    if pre_mix:
        o_ref, y_ref, wo_ref = refs.pop(0), refs.pop(0), refs.pop(0)
    nw_ref, wg_ref, wu_ref, wd_ref = refs.pop(0), refs.pop(0), refs.pop(0), refs.pop(0)
    if final_norm:
        fw_ref = refs.pop(0)
    out_ref = refs.pop(0)

    x = x_ref[...]
    if pre_mix:
        half = o_ref.shape[1]
        x = (x + _dot(o_ref[...], wo_ref[0:half, :].astype(BF16))
             + _dot(y_ref[...], wo_ref[half:, :].astype(BF16)))
    h = _rms(x, nw_ref[...], 1e-6).astype(BF16)
    acts = []
    for c0 in range(0, f_total, FFN_FCHUNK):
        c1 = c0 + FFN_FCHUNK
        g = _dot(h, wg_ref[:, c0:c1].astype(BF16))
        u = _dot(h, wu_ref[:, c0:c1].astype(BF16))
        acts.append((_silu(g) * u).astype(BF16))
    acc = _dot(jnp.concatenate(acts, axis=1), wd_ref[...].astype(BF16))
    y = x + 0.5 * acc
    if final_norm:
        y = _rms(y, fw_ref[...], 1e-6)
    out_ref[...] = y


def _layer_resident(stacked, layer):
    _, r, c = stacked.shape
    return pl.BlockSpec((None, r, c), lambda *_: (layer, 0, 0), pipeline_mode=pl.Buffered(1))


def _ffn_call(x, nw, wg, wu, wd, layer, *, mix=None, final_w=None, tm=FFN_TM):
    T, D = x.shape
    F = wg.shape[2]
    assert T % tm == 0 and F % FFN_FCHUNK == 0
    row = lambda i: (i, 0)
    args = [x]
    in_specs = [pl.BlockSpec((tm, D), row)]
    nbytes = 4 * tm * D * 4 + 3 * D * F * 4 + 6 * tm * FFN_FCHUNK * 4 + 3 * tm * D * 4
    if mix is not None:
        o, y, wo, mix_layer = mix
        args += [o, y, wo]
        in_specs += [pl.BlockSpec((tm, o.shape[1]), row), pl.BlockSpec((tm, y.shape[1]), row),
                     _layer_resident(wo, mix_layer)]
        nbytes += wo.shape[1] * wo.shape[2] * 4 + 4 * tm * o.shape[1] * 2 * 2
    args += [nw.reshape(1, D), wg, wu, wd]
    in_specs += [_resident((1, D)), _layer_resident(wg, layer), _layer_resident(wu, layer),
                 _layer_resident(wd, layer)]
    if final_w is not None:
        args.append(final_w.reshape(1, D))
        in_specs.append(_resident((1, D)))
    kern = functools.partial(_ffn_kernel, pre_mix=mix is not None, final_norm=final_w is not None,
                             f_total=F)
    return pl.pallas_call(
        kern,
        grid=(T // tm,),
        in_specs=in_specs,
        out_specs=pl.BlockSpec((tm, D), row),
        out_shape=jax.ShapeDtypeStruct((T, D), F32),
        compiler_params=pltpu.CompilerParams(
            dimension_semantics=("arbitrary",), vmem_limit_bytes=_vmem_limit(nbytes)),
        name="ffn",
    )(*args)


def _prep_kernel(tab_ref, lq1_ref, lk1_ref, lq2_ref, lk2_ref, tile_ref, lam_ref, *, t, lam_inits):
    hd = pl.program_id(0)
    key = lax.broadcasted_iota(jnp.int32, (t, t), 0)
    qry = lax.broadcasted_iota(jnp.int32, (t, t), 1)
    max_exact = NUM_BUCKETS // 2
    far = tab_ref[hd, NUM_BUCKETS - 1]
    tile_ref[ATT_FAR_TILE] = jnp.zeros((t, t), F32)
    for off in range(ATT_FAR_TILE):
        dist = off * t + qry - key
        d = jnp.maximum(dist, 0)
        large = max_exact + (
            jnp.log(jnp.maximum(d, 1).astype(F32) / max_exact)
            / math.log(MAX_DISTANCE / max_exact) * (NUM_BUCKETS - max_exact)).astype(jnp.int32)
        bucket = jnp.where(d < max_exact, d, jnp.minimum(large, NUM_BUCKETS - 1))
        bias = jnp.full((t, t), far, F32)
        for b in range(NUM_BUCKETS - 1):
            bias = jnp.where(bucket == b, tab_ref[hd, b], bias)
        val = (bias - far) * LOG2E
        if off == 0:
            val = jnp.where(dist >= 0, val, NEG_BIG)
        tile_ref[off] = val
    for j, lam_init in enumerate(lam_inits):
        s1 = jnp.sum(lq1_ref[j:j + 1, :] * lk1_ref[j:j + 1, :], axis=-1, keepdims=True)
        s2 = jnp.sum(lq2_ref[j:j + 1, :] * lk2_ref[j:j + 1, :], axis=-1, keepdims=True)
        lam = jnp.exp(s1) - jnp.exp(s2) + lam_init
        lam_ref[j * SUBLANES:(j + 1) * SUBLANES, :] = jnp.broadcast_to(lam, (SUBLANES, LANES))


def _prep_call(rel_bias, lq1, lk1, lq2, lk2, lam_inits, *, t=ATT_T):
    ne = lq1.shape[0]
    H = rel_bias.shape[1]
    whole = lambda shape: pl.BlockSpec(shape, lambda h: (0,) * len(shape))
    return pl.pallas_call(
        functools.partial(_prep_kernel, t=t, lam_inits=tuple(lam_inits)),
        grid=(H,),
        in_specs=[pl.BlockSpec(memory_space=pltpu.SMEM)] + [whole(lq1.shape)] * 4,
        out_specs=[pl.BlockSpec((None, ATT_FAR_TILE + 1, t, t), lambda h: (h, 0, 0, 0)),
                   whole((ne * SUBLANES, LANES))],
        out_shape=[jax.ShapeDtypeStruct((H, ATT_FAR_TILE + 1, t, t), F32),
                   jax.ShapeDtypeStruct((ne * SUBLANES, LANES), F32)],
        compiler_params=pltpu.CompilerParams(
            dimension_semantics=("arbitrary",), vmem_limit_bytes=_vmem_limit(12 * t * t * 4)),
        name="attn_prep",
    )(rel_bias.T.astype(F32), lq1, lk1, lq2, lk2)


def _attn_kernel(qt_ref, k_ref, vt_ref, tile_ref, lam_ref, sw_ref, o_ref,
                 qm_ref, m_ref, acc_ref, s_ref, smax_ref, *, t, nq, unroll, q_split, lam_init):
    e = 2 * DIFF_HEAD_DIM
    first = lax.broadcasted_iota(jnp.int32, (LANES, t), 0) < DIFF_HEAD_DIM
    ones = jnp.ones((ATT_ONES, t), BF16)
    chains = [(w, slice(c0, c0 + ATT_STRIP)) for c0 in range(0, t, ATT_STRIP) for w in range(2)]

    def rows(blk):
        return pl.ds(pl.multiple_of(blk * t, t), t)

    def prepare(qi, carry):
        q = qt_ref[:, rows(qi)]
        zero = jnp.zeros_like(q)
        qm_ref[0, :, rows(qi)] = jnp.where(first, q, zero)
        qm_ref[1, :, rows(qi)] = jnp.where(first, zero, q)
        m_ref[qi] = jnp.full(m_ref.shape[1:], NEG_BIG, F32)
        acc_ref[qi] = jnp.zeros(acc_ref.shape[1:], F32)
        return carry

    def score_chain(ci, qi, j, slot, k, biased):
        w, cols = chains[ci]
        q_cols = pl.ds(pl.multiple_of(qi * t + cols.start, ATT_STRIP), ATT_STRIP)
        s = _dot(k, qm_ref[w, :, q_cols])
        if biased:
            near = jnp.minimum(qi - j, ATT_FAR_TILE)
            s = s + tile_ref[near, :, cols]
        s_ref[slot, ci] = s
        smax_ref[slot, ci] = jnp.max(s, axis=0, keepdims=True)

    def value_chain(ci, qi, slot, vta):
        w, cols = chains[ci]
        m_old = m_ref[qi, w, :, cols]
        m_new = jnp.maximum(m_old, smax_ref[slot, ci])
        p = jnp.exp2(s_ref[slot, ci] - m_new).astype(BF16)
        acc_ref[qi, w, :, cols] = jnp.exp2(m_old - m_new) * acc_ref[qi, w, :, cols] + _dot(vta, p)
        m_ref[qi, w, :, cols] = m_new

    def finalize(pair, carry):
        for qi in (2 * pair, 2 * pair + 1):
            lam = lam_ref[0:1, 0:1]
            o1 = acc_ref[qi, 0, 0:e, :] * (1.0 / acc_ref[qi, 0, e:e + 1, :])
            o2 = acc_ref[qi, 1, 0:e, :] * (1.0 / acc_ref[qi, 1, e:e + 1, :])
            out_t = o1 - lam * o2
            ms = jnp.mean(out_t * out_t, axis=0, keepdims=True)
            out_t = out_t * lax.rsqrt(ms + 1e-5) * sw_ref[...] * (1.0 - lam_init)
            o_ref[rows(qi), :] = out_t.T.astype(BF16)
        return carry

    def sweep(first_pair, next_pair, last_pair, count, biased):
        def step(qv, jv, slot):
            qs, js = next_pair(qv, jv)
            past_end = qs >= nq
            qs = jnp.where(past_end, last_pair[0], qs)
            js = jnp.where(past_end, last_pair[1], js)
            k_next = k_ref[rows(js), :]
            vta = jnp.concatenate([vt_ref[:, rows(jv)], ones], axis=0)
            for ci in range(len(chains)):
                score_chain(ci, qs, js, 1 - slot, k_next, biased)
                value_chain(ci, qv, slot, vta)
            return qs, js

        trip = max(u for u in range(unroll, ATT_MAX_UNROLL + 1, 2) if count % u == 0)

        def body(_, carry):
            qv, jv = carry
            for u in range(trip):
                qv, jv = step(qv, jv, u % 2)
            return qv, jv

        k0 = k_ref[rows(first_pair[1]), :]
        for ci in range(len(chains)):
            score_chain(ci, first_pair[0], first_pair[1], 0, k0, biased)
        lax.fori_loop(0, count // trip, body, (jnp.int32(first_pair[0]), jnp.int32(first_pair[1])))

    def next_near(qv, jv):
        wrap = jv == qv
        qn = qv + 1
        return jnp.where(wrap, qn, qv), jnp.where(wrap, jnp.where(qn < q_split, 0, qn - 1), jv + 1)

    def next_far(qv, jv):
        wrap = jv == qv - ATT_FAR_TILE
        return jnp.where(wrap, qv + 1, qv), jnp.where(wrap, 0, jv + 1)

    n_near = q_split * (q_split + 1) // 2 + ATT_FAR_TILE * (nq - q_split)
    n_far = nq * (nq + 1) // 2 - n_near
    lax.fori_loop(0, nq, prepare, 0)
    sweep((0, 0), next_near, (nq - 1, nq - 1), n_near, True)
    if n_far:
        sweep((q_split, 0), next_far, (nq - 1, nq - 1 - ATT_FAR_TILE), n_far, False)
    lax.fori_loop(0, nq // 2, finalize, 0)


def _attn_plan(nq):
    for unroll in (8, 4, 2):
        for q_split in range(ATT_FAR_TILE, nq + 1):
            n_near = q_split * (q_split + 1) // 2 + ATT_FAR_TILE * (nq - q_split)
            n_far = nq * (nq + 1) // 2 - n_near
            if n_near % unroll == 0 and n_far % unroll == 0:
                return unroll, q_split
    raise ValueError(f"unsupported number of query blocks: {nq}")


def _attn_call(qt, k, vt, tiles, lam, subln_w, *, batch, lam_init, t=ATT_T):
    T, W = k.shape
    S = T // batch
    H = W // LANES
    nq = S // t
    e = 2 * DIFF_HEAD_DIM
    assert e == LANES and S % t == 0 and nq % 2 == 0
    unroll, q_split = _attn_plan(nq)
    n_chains = 2 * (t // ATT_STRIP)
    n_tiles = ATT_FAR_TILE + 1
    nbytes = (2 * 4 * (S * LANES * 2) + 2 * n_tiles * t * t * 4 + 2 * S * LANES * 2
              + nq * 2 * (e + ATT_ONES + SUBLANES) * t * 4 + 2 * n_chains * t * ATT_STRIP * 4 + 2 * t * t * 4)
    return pl.pallas_call(
        functools.partial(_attn_kernel, t=t, nq=nq, unroll=unroll, q_split=q_split, lam_init=lam_init),
        grid=(batch, H),
        in_specs=[
            pl.BlockSpec((LANES, S), lambda b, h: (h, b)),
            pl.BlockSpec((S, LANES), lambda b, h: (b, h)),
            pl.BlockSpec((LANES, S), lambda b, h: (h, b)),
            pl.BlockSpec((None, n_tiles, t, t), lambda b, h: (h, 0, 0, 0)),
            pl.BlockSpec((SUBLANES, LANES), lambda b, h: (0, 0)),
            pl.BlockSpec((e, 1), lambda b, h: (0, 0)),
        ],
        out_specs=pl.BlockSpec((S, LANES), lambda b, h: (b, h)),
        out_shape=jax.ShapeDtypeStruct((T, W), BF16),
        scratch_shapes=[pltpu.VMEM((2, LANES, S), BF16),
                        pltpu.VMEM((nq, 2, 1, t), F32), pltpu.VMEM((nq, 2, e + ATT_ONES, t), F32),
                        pltpu.VMEM((2, n_chains, t, ATT_STRIP), F32),
                        pltpu.VMEM((2, n_chains, 1, ATT_STRIP), F32)],
        compiler_params=pltpu.CompilerParams(
            dimension_semantics=("arbitrary", "arbitrary"), vmem_limit_bytes=_vmem_limit(nbytes)),
        name="diff_attn",
    )(qt, k, vt, tiles, lam, subln_w.reshape(e, 1))


def _split3(a):
    hi = a.astype(BF16)
    r1 = a - hi.astype(F32)
    mid = r1.astype(BF16)
    lo = (r1 - mid.astype(F32)).astype(BF16)
    return hi, mid, lo


def _hyb_kernel(x_ref, nw_ref, w_ref, wvt_ref, cw_ref, cb_ref, dtb_ref, alog_ref, dsk_ref, ynw_ref,
                qt_ref, k_ref, vt_ref, y_ref, xpad_ref, z_scr, dt_scr, st_ref, *, tm, tiles_per_seq):
    i = pl.program_id(0)
    L = SSM_CHUNK
    gw = HEADS_PER_GROUP * SSM_HEAD_DIM
    qw = DIFF_WIDTH
    c_z = 2 * qw + DIFF_WIDTH
    c_xbc = c_z + SSM_WIDTH
    c_dt = c_xbc + SSM_XBC

    @pl.when(i == 0)
    def _():
        xpad_ref[...] = jnp.zeros(xpad_ref.shape, F32)
        z_scr[...] = jnp.zeros(z_scr.shape, F32)
        dt_scr[...] = jnp.zeros(dt_scr.shape, F32)
        st_ref[...] = jnp.zeros(st_ref.shape, F32)

    def stage(cur, prev):
        row = lax.broadcasted_iota(jnp.int32, (L, L), 0)
        col = lax.broadcasted_iota(jnp.int32, (L, L), 1)
        causal = row >= col
        tri = jnp.where(causal, 1.0, 0.0).astype(BF16)
        lane_g = lax.broadcasted_iota(jnp.int32, (L, gw), 1) // SSM_HEAD_DIM
        lane_lo = lax.broadcasted_iota(jnp.int32, (L, LANES), 1) < SSM_HEAD_DIM
        lane_lo1 = lax.broadcasted_iota(jnp.int32, (1, LANES), 1) < SSM_HEAD_DIM
        a_neg = -jnp.exp(alog_ref[...])
        keep_state = jnp.where((i - 1) % tiles_per_seq == 0, 0.0, 1.0)

        def ssd_chunk(c):
            r0 = c * L
            dt = dt_scr[prev, r0:r0 + L, :]
            hi, mid, lo = _split3(a_neg * dt)
            cs = _dot(tri, hi) + _dot(tri, mid) + _dot(tri, lo)
            yield

            conv = cb_ref[...]
            for kk in range(SSM_CONV):
                o = HALO - (SSM_CONV - 1) + kk + r0
                conv = conv + cw_ref[kk:kk + 1, :] * xpad_ref[prev, o:o + L, :]
            xc = _silu(conv)
            xs = xc[:, 0:SSM_WIDTH]
            bgs, cgs, gmats = [], [], []
            for g in range(SSM_GROUPS):
                b0 = SSM_WIDTH + g * SSM_STATE
                c0 = SSM_WIDTH + SSM_GROUPS * SSM_STATE + g * SSM_STATE
                bgs.append(xc[:, b0:b0 + SSM_STATE])
                cgs.append(xc[:, c0:c0 + SSM_STATE].astype(BF16))
                gmats.append(_dot_nt(cgs[g], bgs[g].astype(BF16)))
            yield

            cs_t = cs.T
            dt_t = dt.T
            ydiags, st_news, e_mats, decs_g = [], [], [], []
            for g in range(SSM_GROUPS):
                bg_t = bgs[g].T
                xg = xs[:, g * gw:(g + 1) * gw]
                ydiag = None
                st_new = None
                ecols, decs = [], []
                for hh in range(HEADS_PER_GROUP):
                    h = g * HEADS_PER_GROUP + hh
                    colb = jnp.broadcast_to(cs[:, h:h + 1], (L, L))
                    rowb = cs_t[h:h + 1, :]
                    dtrow = dt_t[h:h + 1, :]
                    decay = jnp.exp(jnp.where(causal, colb - rowb, NEG_BIG))
                    mh = (gmats[g] * decay * dtrow).astype(BF16)
                    xm = jnp.where(lane_g == hh, xg, 0.0).astype(BF16)
                    last = colb[L - 1:L, :]
                    wrow = dtrow * jnp.exp(last - rowb)
                    btw = (bg_t * wrow).astype(BF16)
                    d1 = _dot(mh, xm)
                    d2 = _dot(btw, xm)
                    ydiag = d1 if ydiag is None else ydiag + d1
                    st_new = d2 if st_new is None else st_new + d2
                    ecols.append(jnp.exp(colb))
                    decs.append(jnp.exp(last))
                ydiags.append(ydiag)
                st_news.append(st_new)
                e_mats.append(jnp.concatenate([jnp.where(lane_lo, ecols[0], ecols[1]),
                                               jnp.where(lane_lo, ecols[2], ecols[3])], axis=1))
                decs_g.append(jnp.concatenate([jnp.where(lane_lo1, decs[0], decs[1]),
                                               jnp.where(lane_lo1, decs[2], decs[3])], axis=1))
                if g == 0:
                    yield

            ys = []
            for g in range(SSM_GROUPS):
                st_old = st_ref[g] * keep_state if c == 0 else st_ref[g]
                ys.append(ydiags[g] + _dot(cgs[g], st_old.astype(BF16)) * e_mats[g])
                st_ref[g] = st_old * decs_g[g] + st_news[g]
            y = jnp.concatenate(ys, axis=1) + dsk_ref[...] * xs
            gated = y * z_scr[prev, r0:r0 + L, :]
            outs = []
            for g in range(SSM_GROUPS):
                part = gated[:, g * gw:(g + 1) * gw]
                ms = jnp.mean(part * part, axis=-1, keepdims=True)
                outs.append(part * lax.rsqrt(ms + 1e-5))
            y_ref[r0:r0 + L, :] = (jnp.concatenate(outs, axis=1) * ynw_ref[...]).astype(BF16)
            yield

        h = _rms(x_ref[...], nw_ref[...], 1e-6).astype(BF16)
        scale = DIFF_HEAD_DIM ** -0.5 * LOG2E
        piece = 2 * LANES

        def pieces():
            for r in range(0, qw, piece):
                qt_ref[r:r + piece, :] = (_dot_nt(wvt_ref[r:r + piece, :], h) * scale).astype(BF16)
                yield
            for c in range(0, qw, piece):
                k_ref[:, c:c + piece] = _dot(h, w_ref[:, qw + c:qw + c + piece]).astype(BF16)
                yield
            for r in range(0, DIFF_WIDTH, piece):
                vt_ref[r:r + piece, :] = _dot_nt(wvt_ref[qw + r:qw + r + piece, :], h).astype(BF16)
                yield
            for c in range(0, SSM_WIDTH, piece):
                z_scr[cur, :, c:c + piece] = _silu(_dot(h, w_ref[:, c_z + c:c_z + c + piece]))
                yield
            dt_raw = _dot(h, w_ref[:, c_dt:c_dt + LANES]) + dtb_ref[...]
            dt_scr[cur] = jnp.maximum(dt_raw, 0.0) + jnp.log1p(jnp.exp(-jnp.abs(dt_raw)))
            yield
            tail = xpad_ref[prev, tm:tm + HALO, :]
            xpad_ref[cur, 0:HALO, :] = jnp.where(i % tiles_per_seq == 0, jnp.zeros_like(tail), tail)
            for c in range(0, SSM_XBC, piece):
                xpad_ref[cur, HALO:HALO + tm, c:c + piece] = _dot(h, w_ref[:, c_xbc + c:c_xbc + c + piece])
                yield

        pieces_after_group = (0, 2, 1, 0)
        proj = pieces()
        for c in range(tm // L):
            for group, _ in enumerate(ssd_chunk(c)):
                for _ in range(pieces_after_group[group]):
                    next(proj, None)
        for _ in proj:
            pass

    pl.when(i % 2 == 0)(functools.partial(stage, 0, 1))
    pl.when(i % 2 == 1)(functools.partial(stage, 1, 0))


def _hyb_call(x, nw, w_pad, wvt, conv_w, conv_b, dt_bias, a_log, d_skip, norm_w, *, batch, tm=PROJ_TM):
    T, D = x.shape
    S = T // batch
    assert S % tm == 0 and tm % SSM_CHUNK == 0
    nt = T // tm
    cur = lambda i: (jnp.minimum(i, nt - 1), 0)
    cur_t = lambda i: (0, jnp.minimum(i, nt - 1))
    prev = lambda i: (jnp.maximum(i - 1, 0), 0)
    pad = lambda v: jnp.pad(v.astype(F32), (0, LANES - v.shape[0])).reshape(1, LANES)
    dsk = jnp.repeat(d_skip.astype(F32), SSM_HEAD_DIM).reshape(1, SSM_WIDTH)
    outs = [
        jax.ShapeDtypeStruct((DIFF_WIDTH, T), BF16),
        jax.ShapeDtypeStruct((T, DIFF_WIDTH), BF16),
        jax.ShapeDtypeStruct((DIFF_WIDTH, T), BF16),
        jax.ShapeDtypeStruct((T, SSM_WIDTH), BF16),
    ]
    out_specs = [pl.BlockSpec((DIFF_WIDTH, tm), cur_t), pl.BlockSpec((tm, DIFF_WIDTH), cur),
                 pl.BlockSpec((DIFF_WIDTH, tm), cur_t), pl.BlockSpec((tm, SSM_WIDTH), prev)]
    nbytes = ((w_pad.size + wvt.size) * 2 + 2 * tm * D * 4 + 2 * tm * 2 * 4 * DIFF_WIDTH
              + 2 * (tm + HALO) * SSM_XBC * 4 + 2 * tm * (SSM_WIDTH + LANES) * 4 + 16 * tm * SSM_XBC * 4)
    return pl.pallas_call(
        functools.partial(_hyb_kernel, tm=tm, tiles_per_seq=S // tm),
        grid=(nt + 1,),
        in_specs=[pl.BlockSpec((tm, D), cur), _resident((1, D)), _resident(w_pad.shape),
                  _resident(wvt.shape), _resident((SSM_CONV, SSM_XBC)), _resident((1, SSM_XBC)),
                  _resident((1, LANES)), _resident((1, LANES)), _resident((1, SSM_WIDTH)),
                  _resident((1, SSM_WIDTH))],
        out_specs=out_specs,
        out_shape=outs,
        scratch_shapes=[pltpu.VMEM((2, tm + HALO, SSM_XBC), F32), pltpu.VMEM((2, tm, SSM_WIDTH), F32),
                        pltpu.VMEM((2, tm, LANES), F32),
                        pltpu.VMEM((SSM_GROUPS, SSM_STATE, HEADS_PER_GROUP * SSM_HEAD_DIM), F32)],
        compiler_params=pltpu.CompilerParams(
            dimension_semantics=("arbitrary",), vmem_limit_bytes=_vmem_limit(nbytes)),
        name="hyb_ssd",
    )(x, nw.reshape(1, D), w_pad, wvt, conv_w.astype(F32), conv_b.reshape(1, SSM_XBC).astype(F32),
      pad(dt_bias), pad(a_log), dsk, norm_w.reshape(1, SSM_WIDTH).astype(F32))


def _sc_kernel(x_ref, xprev_ref, nw_ref, win_ref, cw_ref, wout_ref, o_ref, vpad_ref, bg_ref, *,
               tm, tiles_per_seq):
    i = pl.program_id(0)
    d = x_ref.shape[1]

    @pl.when(i == 0)
    def _():
        vpad_ref[...] = jnp.zeros(vpad_ref.shape, F32)
        bg_ref[...] = jnp.zeros(bg_ref.shape, F32)

    def finish_chunk(prev, c0, c1):
        conv = cw_ref[SC_CONV - 1:SC_CONV, c0:c1] * vpad_ref[prev, HALO:HALO + tm, c0:c1]
        for kk in range(SC_CONV - 1):
            o = HALO - (SC_CONV - 1) + kk
            conv = conv + cw_ref[kk:kk + 1, c0:c1] * vpad_ref[prev, o:o + tm, c0:c1]
        return _dot((bg_ref[prev, :, c0:c1] * conv).astype(BF16), wout_ref[c0:c1, :])

    def stage(cur, prev):
        cw = 2 * LANES
        chunks = [(c0, c0 + cw) for c0 in range(0, d, cw)]
        h = _rms(x_ref[...], nw_ref[...], 1e-6).astype(BF16)
        bgate = _dot(h, win_ref[:, 0:d])
        out = finish_chunk(prev, *chunks[0])
        cgate = _dot(h, win_ref[:, d:2 * d])
        out = out + finish_chunk(prev, *chunks[1])
        u = _dot(h, win_ref[:, 2 * d:3 * d])
        for c0, c1 in chunks[2:]:
            out = out + finish_chunk(prev, c0, c1)
        o_ref[...] = xprev_ref[...] + out

        tail = vpad_ref[prev, tm:tm + HALO, :]
        vpad_ref[cur, 0:HALO, :] = jnp.where(i % tiles_per_seq == 0, jnp.zeros_like(tail), tail)
        vpad_ref[cur, HALO:HALO + tm, :] = cgate * u
        bg_ref[cur] = bgate

    pl.when(i % 2 == 0)(functools.partial(stage, 0, 1))
    pl.when(i % 2 == 1)(functools.partial(stage, 1, 0))


def _sc_call(x, nw, w_in, conv_w, w_out, *, batch, tm=PROJ_TM):
    T, D = x.shape
    S = T // batch
    assert S % tm == 0
    nt = T // tm
    cur = lambda i: (jnp.minimum(i, nt - 1), 0)
    prev = lambda i: (jnp.maximum(i - 1, 0), 0)
    nbytes = (w_in.size + w_out.size) * 2 + 6 * tm * D * 4 + 4 * tm * D * 4 + 8 * tm * D * 4
    return pl.pallas_call(
        functools.partial(_sc_kernel, tm=tm, tiles_per_seq=S // tm),
        grid=(nt + 1,),
        in_specs=[pl.BlockSpec((tm, D), cur), pl.BlockSpec((tm, D), prev), _resident((1, D)),
                  _resident(w_in.shape), _resident(conv_w.shape), _resident(w_out.shape)],
        out_specs=pl.BlockSpec((tm, D), prev),
        out_shape=jax.ShapeDtypeStruct((T, D), F32),
        scratch_shapes=[pltpu.VMEM((2, tm + HALO, D), F32), pltpu.VMEM((2, tm, D), F32)],
        compiler_params=pltpu.CompilerParams(
            dimension_semantics=("arbitrary",), vmem_limit_bytes=_vmem_limit(nbytes)),
        name="short_conv",
    )(x, x, nw.reshape(1, D), w_in, conv_w.astype(F32), w_out)


def kernel(x, rel_bias, final_norm_w, ffn1_norm, ffn1_wg, ffn1_wu, ffn1_wd, mix_norm, ffn2_norm, ffn2_wg, ffn2_wu, ffn2_wd, hyb_w_in, hyb_w_out, diff_lq1, diff_lk1, diff_lq2, diff_lk2, diff_subln_w, ssm_conv_w, ssm_conv_b, ssm_dt_bias, ssm_a_log, ssm_d, ssm_norm_w, sc_w_in, sc_conv_w, sc_w_out):
    B, S, D = x.shape
    T = B * S
    xt = x.reshape(T, D)
    bf = lambda w: w.astype(BF16)

    lam_inits = [0.8 - 0.6 * math.exp(-0.3 * i) for i in range(0, DEPTH, 2)]
    tiles, lams = _prep_call(rel_bias, diff_lq1, diff_lk1, diff_lq2, diff_lk2, lam_inits)

    mix = None
    for i in range(DEPTH):
        j = i // 2
        xt = _ffn_call(xt, ffn1_norm[i], ffn1_wg, ffn1_wu, ffn1_wd, i)
        if i % 2 == 0:
            w_in = hyb_w_in[j]
            qkv_end = 3 * DIFF_WIDTH
            w_pad = bf(jnp.pad(w_in, ((0, 0), (0, LANES - SSM_HEADS))))
            wvt = bf(jnp.concatenate([w_in[:, 0:DIFF_WIDTH], w_in[:, 2 * DIFF_WIDTH:qkv_end]], axis=1).T)
            q, k, vt, y = _hyb_call(xt, mix_norm[i], w_pad, wvt, ssm_conv_w[j], ssm_conv_b[j],
                                    ssm_dt_bias[j], ssm_a_log[j], ssm_d[j], ssm_norm_w[j], batch=B)
            o = _attn_call(q, k, vt, tiles, lams[j * SUBLANES:(j + 1) * SUBLANES], diff_subln_w[j],
                           batch=B, lam_init=lam_inits[j])
            mix = (o, y, hyb_w_out, j)
        else:
            xt = _sc_call(xt, mix_norm[i], bf(sc_w_in[j]), sc_conv_w[j], bf(sc_w_out[j]), batch=B)
            mix = None
        xt = _ffn_call(xt, ffn2_norm[i], ffn2_wg, ffn2_wu, ffn2_wd, i, mix=mix,
                       final_w=final_norm_w if i == DEPTH - 1 else None)
    return xt.reshape(B, S, D)
```

```python
import functools
import math

import jax
import jax.numpy as jnp
from jax import lax
from jax.experimental import pallas as pl
from jax.experimental.pallas import tpu as pltpu

F32 = jnp.float32
BF16 = jnp.bfloat16

DEPTH = 4
N_DIFF_HEADS = 4
DIFF_HEAD_DIM = 64
DIFF_WIDTH = N_DIFF_HEADS * 2 * DIFF_HEAD_DIM
NUM_BUCKETS = 32
MAX_DISTANCE = 128
SSM_HEADS = 8
SSM_HEAD_DIM = 64
SSM_WIDTH = SSM_HEADS * SSM_HEAD_DIM
SSM_GROUPS = 2
SSM_STATE = 128
SSM_CONV = 4
SSM_CHUNK = 128
HEADS_PER_GROUP = SSM_HEADS // SSM_GROUPS
SSM_XBC = SSM_WIDTH + 2 * SSM_GROUPS * SSM_STATE
SC_CONV = 3

LANES = 128
SUBLANES = 8
V7X_SCOPED_VMEM_CAP = 60000 * 1024

FFN_TM = 512
FFN_FCHUNK = 256
PROJ_TM = 512
ATT_T = 512
ATT_STRIP = 256
ATT_ONES = 16
ATT_MAX_UNROLL = 26
ATT_FAR_TILE = 2
HALO = SUBLANES

NEG_BIG = -1e30
LOG2E = math.log2(math.e)


def _vmem_limit(nbytes):
    return int(min(V7X_SCOPED_VMEM_CAP, nbytes * 5 // 4 + (8 << 20)))


def _dot(a, b):
    return jnp.dot(a, b, preferred_element_type=F32)


def _dot_nt(a, b):
    return lax.dot_general(a, b, (((1,), (1,)), ((), ())), preferred_element_type=F32)


def _rms(x, w, eps):
    ms = jnp.mean(x * x, axis=-1, keepdims=True)
    return x * lax.rsqrt(ms + eps) * w


def _silu(x):
    h = 0.5 * x
    return h + h * jnp.tanh(h)


def _resident(shape):
    nd = len(shape)
    return pl.BlockSpec(shape, lambda *_: (0,) * nd, pipeline_mode=pl.Buffered(1))


def _ffn_kernel(*refs, pre_mix, final_norm, f_total):
    refs = list(refs)
    x_ref = refs.pop(0)
    if pre_mix:
        o_ref, y_ref, wo_ref = refs.pop(0), refs.pop(0), refs.pop(0)
    nw_ref, wg_ref, wu_ref, wd_ref = refs.pop(0), refs.pop(0), refs.pop(0), refs.pop(0)
    if final_norm:
        fw_ref = refs.pop(0)
    out_ref = refs.pop(0)

    x = x_ref[...]
    if pre_mix:
        half = o_ref.shape[1]
        x = (x + _dot(o_ref[...], wo_ref[0:half, :].astype(BF16))
             + _dot(y_ref[...], wo_ref[half:, :].astype(BF16)))
    h = _rms(x, nw_ref[...], 1e-6).astype(BF16)
    acts = []
    for c0 in range(0, f_total, FFN_FCHUNK):
        c1 = c0 + FFN_FCHUNK
        g = _dot(h, wg_ref[:, c0:c1].astype(BF16))
        u = _dot(h, wu_ref[:, c0:c1].astype(BF16))
        acts.append((_silu(g) * u).astype(BF16))
    acc = _dot(jnp.concatenate(acts, axis=1), wd_ref[...].astype(BF16))
    y = x + 0.5 * acc
    if final_norm:
        y = _rms(y, fw_ref[...], 1e-6)
    out_ref[...] = y


def _layer_resident(stacked, layer):
    _, r, c = stacked.shape
    return pl.BlockSpec((None, r, c), lambda *_: (layer, 0, 0), pipeline_mode=pl.Buffered(1))


def _ffn_call(x, nw, wg, wu, wd, layer, *, mix=None, final_w=None, tm=FFN_TM):
    T, D = x.shape
    F = wg.shape[2]
    assert T % tm == 0 and F % FFN_FCHUNK == 0
    row = lambda i: (i, 0)
    args = [x]
    in_specs = [pl.BlockSpec((tm, D), row)]
    nbytes = 4 * tm * D * 4 + 3 * D * F * 4 + 6 * tm * FFN_FCHUNK * 4 + 3 * tm * D * 4
    if mix is not None:
        o, y, wo, mix_layer = mix
        args += [o, y, wo]
        in_specs += [pl.BlockSpec((tm, o.shape[1]), row), pl.BlockSpec((tm, y.shape[1]), row),
                     _layer_resident(wo, mix_layer)]
        nbytes += wo.shape[1] * wo.shape[2] * 4 + 4 * tm * o.shape[1] * 2 * 2
    args += [nw.reshape(1, D), wg, wu, wd]
    in_specs += [_resident((1, D)), _layer_resident(wg, layer), _layer_resident(wu, layer),
                 _layer_resident(wd, layer)]
    if final_w is not None:
        args.append(final_w.reshape(1, D))
        in_specs.append(_resident((1, D)))
    kern = functools.partial(_ffn_kernel, pre_mix=mix is not None, final_norm=final_w is not None,
                             f_total=F)
    return pl.pallas_call(
        kern,
        grid=(T // tm,),
        in_specs=in_specs,
        out_specs=pl.BlockSpec((tm, D), row),
        out_shape=jax.ShapeDtypeStruct((T, D), F32),
        compiler_params=pltpu.CompilerParams(
            dimension_semantics=("arbitrary",), vmem_limit_bytes=_vmem_limit(nbytes)),
        name="ffn",
    )(*args)


def _prep_kernel(tab_ref, lq1_ref, lk1_ref, lq2_ref, lk2_ref, tile_ref, lam_ref, *, t, lam_inits):
    hd = pl.program_id(0)
    key = lax.broadcasted_iota(jnp.int32, (t, t), 0)
    qry = lax.broadcasted_iota(jnp.int32, (t, t), 1)
    max_exact = NUM_BUCKETS // 2
    far = tab_ref[hd, NUM_BUCKETS - 1]
    tile_ref[ATT_FAR_TILE] = jnp.zeros((t, t), F32)
    for off in range(ATT_FAR_TILE):
        dist = off * t + qry - key
        d = jnp.maximum(dist, 0)
        large = max_exact + (
            jnp.log(jnp.maximum(d, 1).astype(F32) / max_exact)
            / math.log(MAX_DISTANCE / max_exact) * (NUM_BUCKETS - max_exact)).astype(jnp.int32)
        bucket = jnp.where(d < max_exact, d, jnp.minimum(large, NUM_BUCKETS - 1))
        bias = jnp.full((t, t), far, F32)
        for b in range(NUM_BUCKETS - 1):
            bias = jnp.where(bucket == b, tab_ref[hd, b], bias)
        val = (bias - far) * LOG2E
        if off == 0:
            val = jnp.where(dist >= 0, val, NEG_BIG)
        tile_ref[off] = val
    for j, lam_init in enumerate(lam_inits):
        s1 = jnp.sum(lq1_ref[j:j + 1, :] * lk1_ref[j:j + 1, :], axis=-1, keepdims=True)
        s2 = jnp.sum(lq2_ref[j:j + 1, :] * lk2_ref[j:j + 1, :], axis=-1, keepdims=True)
        lam = jnp.exp(s1) - jnp.exp(s2) + lam_init
        lam_ref[j * SUBLANES:(j + 1) * SUBLANES, :] = jnp.broadcast_to(lam, (SUBLANES, LANES))


def _prep_call(rel_bias, lq1, lk1, lq2, lk2, lam_inits, *, t=ATT_T):
    ne = lq1.shape[0]
    H = rel_bias.shape[1]
    whole = lambda shape: pl.BlockSpec(shape, lambda h: (0,) * len(shape))
    return pl.pallas_call(
        functools.partial(_prep_kernel, t=t, lam_inits=tuple(lam_inits)),
        grid=(H,),
        in_specs=[pl.BlockSpec(memory_space=pltpu.SMEM)] + [whole(lq1.shape)] * 4,
        out_specs=[pl.BlockSpec((None, ATT_FAR_TILE + 1, t, t), lambda h: (h, 0, 0, 0)),
                   whole((ne * SUBLANES, LANES))],
        out_shape=[jax.ShapeDtypeStruct((H, ATT_FAR_TILE + 1, t, t), F32),
                   jax.ShapeDtypeStruct((ne * SUBLANES, LANES), F32)],
        compiler_params=pltpu.CompilerParams(
            dimension_semantics=("arbitrary",), vmem_limit_bytes=_vmem_limit(12 * t * t * 4)),
        name="attn_prep",
    )(rel_bias.T.astype(F32), lq1, lk1, lq2, lk2)


def _attn_kernel(qt_ref, k_ref, vt_ref, tile_ref, lam_ref, sw_ref, o_ref,
                 qm_ref, m_ref, acc_ref, s_ref, smax_ref, *, t, nq, unroll, q_split, lam_init):
    e = 2 * DIFF_HEAD_DIM
    first = lax.broadcasted_iota(jnp.int32, (LANES, t), 0) < DIFF_HEAD_DIM
    ones = jnp.ones((ATT_ONES, t), BF16)
    chains = [(w, slice(c0, c0 + ATT_STRIP)) for c0 in range(0, t, ATT_STRIP) for w in range(2)]

    def rows(blk):
        return pl.ds(pl.multiple_of(blk * t, t), t)

    def prepare(qi, carry):
        q = qt_ref[:, rows(qi)]
        zero = jnp.zeros_like(q)
        qm_ref[0, :, rows(qi)] = jnp.where(first, q, zero)
        qm_ref[1, :, rows(qi)] = jnp.where(first, zero, q)
        m_ref[qi] = jnp.full(m_ref.shape[1:], NEG_BIG, F32)
        acc_ref[qi] = jnp.zeros(acc_ref.shape[1:], F32)
        return carry

    def score_chain(ci, qi, j, slot, k, biased):
        w, cols = chains[ci]
        q_cols = pl.ds(pl.multiple_of(qi * t + cols.start, ATT_STRIP), ATT_STRIP)
        s = _dot(k, qm_ref[w, :, q_cols])
        if biased:
            near = jnp.minimum(qi - j, ATT_FAR_TILE)
            s = s + tile_ref[near, :, cols]
        s_ref[slot, ci] = s
        smax_ref[slot, ci] = jnp.max(s, axis=0, keepdims=True)

    def value_chain(ci, qi, slot, vta):
        w, cols = chains[ci]
        m_old = m_ref[qi, w, :, cols]
        m_new = jnp.maximum(m_old, smax_ref[slot, ci])
        p = jnp.exp2(s_ref[slot, ci] - m_new).astype(BF16)
        acc_ref[qi, w, :, cols] = jnp.exp2(m_old - m_new) * acc_ref[qi, w, :, cols] + _dot(vta, p)
        m_ref[qi, w, :, cols] = m_new

    def finalize(pair, carry):
        for qi in (2 * pair, 2 * pair + 1):
            lam = lam_ref[0:1, 0:1]
            o1 = acc_ref[qi, 0, 0:e, :] * (1.0 / acc_ref[qi, 0, e:e + 1, :])
            o2 = acc_ref[qi, 1, 0:e, :] * (1.0 / acc_ref[qi, 1, e:e + 1, :])
            out_t = o1 - lam * o2
            ms = jnp.mean(out_t * out_t, axis=0, keepdims=True)
            out_t = out_t * lax.rsqrt(ms + 1e-5) * sw_ref[...] * (1.0 - lam_init)
            o_ref[rows(qi), :] = out_t.T.astype(BF16)
        return carry

    def sweep(first_pair, next_pair, last_pair, count, biased):
        def step(qv, jv, slot):
            qs, js = next_pair(qv, jv)
            past_end = qs >= nq
            qs = jnp.where(past_end, last_pair[0], qs)
            js = jnp.where(past_end, last_pair[1], js)
            k_next = k_ref[rows(js), :]
            vta = jnp.concatenate([vt_ref[:, rows(jv)], ones], axis=0)
            for ci in range(len(chains)):
                score_chain(ci, qs, js, 1 - slot, k_next, biased)
                value_chain(ci, qv, slot, vta)
            return qs, js

        trip = max(u for u in range(unroll, ATT_MAX_UNROLL + 1, 2) if count % u == 0)

        def body(_, carry):
            qv, jv = carry
            for u in range(trip):
                qv, jv = step(qv, jv, u % 2)
            return qv, jv

        k0 = k_ref[rows(first_pair[1]), :]
        for ci in range(len(chains)):
            score_chain(ci, first_pair[0], first_pair[1], 0, k0, biased)
        lax.fori_loop(0, count // trip, body, (jnp.int32(first_pair[0]), jnp.int32(first_pair[1])))

    def next_near(qv, jv):
        wrap = jv == qv
        qn = qv + 1
        return jnp.where(wrap, qn, qv), jnp.where(wrap, jnp.where(qn < q_split, 0, qn - 1), jv + 1)

    def next_far(qv, jv):
        wrap = jv == qv - ATT_FAR_TILE
        return jnp.where(wrap, qv + 1, qv), jnp.where(wrap, 0, jv + 1)

    n_near = q_split * (q_split + 1) // 2 + ATT_FAR_TILE * (nq - q_split)
    n_far = nq * (nq + 1) // 2 - n_near
    lax.fori_loop(0, nq, prepare, 0)
    sweep((0, 0), next_near, (nq - 1, nq - 1), n_near, True)
    if n_far:
        sweep((q_split, 0), next_far, (nq - 1, nq - 1 - ATT_FAR_TILE), n_far, False)
    lax.fori_loop(0, nq // 2, finalize, 0)


def _attn_plan(nq):
    for unroll in (8, 4, 2):
        for q_split in range(ATT_FAR_TILE, nq + 1):
            n_near = q_split * (q_split + 1) // 2 + ATT_FAR_TILE * (nq - q_split)
            n_far = nq * (nq + 1) // 2 - n_near
            if n_near % unroll == 0 and n_far % unroll == 0:
                return unroll, q_split
    raise ValueError(f"unsupported number of query blocks: {nq}")


def _attn_call(qt, k, vt, tiles, lam, subln_w, *, batch, lam_init, t=ATT_T):
    T, W = k.shape
    S = T // batch
    H = W // LANES
    nq = S // t
    e = 2 * DIFF_HEAD_DIM
    assert e == LANES and S % t == 0 and nq % 2 == 0
    unroll, q_split = _attn_plan(nq)
    n_chains = 2 * (t // ATT_STRIP)
    n_tiles = ATT_FAR_TILE + 1
    nbytes = (2 * 4 * (S * LANES * 2) + 2 * n_tiles * t * t * 4 + 2 * S * LANES * 2
              + nq * 2 * (e + ATT_ONES + SUBLANES) * t * 4 + 2 * n_chains * t * ATT_STRIP * 4 + 2 * t * t * 4)
    return pl.pallas_call(
        functools.partial(_attn_kernel, t=t, nq=nq, unroll=unroll, q_split=q_split, lam_init=lam_init),
        grid=(batch, H),
        in_specs=[
            pl.BlockSpec((LANES, S), lambda b, h: (h, b)),
            pl.BlockSpec((S, LANES), lambda b, h: (b, h)),
            pl.BlockSpec((LANES, S), lambda b, h: (h, b)),
            pl.BlockSpec((None, n_tiles, t, t), lambda b, h: (h, 0, 0, 0)),
            pl.BlockSpec((SUBLANES, LANES), lambda b, h: (0, 0)),
            pl.BlockSpec((e, 1), lambda b, h: (0, 0)),
        ],
        out_specs=pl.BlockSpec((S, LANES), lambda b, h: (b, h)),
        out_shape=jax.ShapeDtypeStruct((T, W), BF16),
        scratch_shapes=[pltpu.VMEM((2, LANES, S), BF16),
                        pltpu.VMEM((nq, 2, 1, t), F32), pltpu.VMEM((nq, 2, e + ATT_ONES, t), F32),
                        pltpu.VMEM((2, n_chains, t, ATT_STRIP), F32),
                        pltpu.VMEM((2, n_chains, 1, ATT_STRIP), F32)],
        compiler_params=pltpu.CompilerParams(
            dimension_semantics=("arbitrary", "arbitrary"), vmem_limit_bytes=_vmem_limit(nbytes)),
        name="diff_attn",
    )(qt, k, vt, tiles, lam, subln_w.reshape(e, 1))


def _split3(a):
    hi = a.astype(BF16)
    r1 = a - hi.astype(F32)
    mid = r1.astype(BF16)
    lo = (r1 - mid.astype(F32)).astype(BF16)
    return hi, mid, lo


def _hyb_kernel(x_ref, nw_ref, w_ref, wvt_ref, cw_ref, cb_ref, dtb_ref, alog_ref, dsk_ref, ynw_ref,
                qt_ref, k_ref, vt_ref, y_ref, xpad_ref, z_scr, dt_scr, st_ref, *, tm, tiles_per_seq):
    i = pl.program_id(0)
    L = SSM_CHUNK
    gw = HEADS_PER_GROUP * SSM_HEAD_DIM
    qw = DIFF_WIDTH
    c_z = 2 * qw + DIFF_WIDTH
    c_xbc = c_z + SSM_WIDTH
    c_dt = c_xbc + SSM_XBC

    @pl.when(i == 0)
    def _():
        xpad_ref[...] = jnp.zeros(xpad_ref.shape, F32)
        z_scr[...] = jnp.zeros(z_scr.shape, F32)
        dt_scr[...] = jnp.zeros(dt_scr.shape, F32)
        st_ref[...] = jnp.zeros(st_ref.shape, F32)

    def stage(cur, prev):
        row = lax.broadcasted_iota(jnp.int32, (L, L), 0)
        col = lax.broadcasted_iota(jnp.int32, (L, L), 1)
        causal = row >= col
        tri = jnp.where(causal, 1.0, 0.0).astype(BF16)
        lane_g = lax.broadcasted_iota(jnp.int32, (L, gw), 1) // SSM_HEAD_DIM
        lane_lo = lax.broadcasted_iota(jnp.int32, (L, LANES), 1) < SSM_HEAD_DIM
        lane_lo1 = lax.broadcasted_iota(jnp.int32, (1, LANES), 1) < SSM_HEAD_DIM
        a_neg = -jnp.exp(alog_ref[...])
        keep_state = jnp.where((i - 1) % tiles_per_seq == 0, 0.0, 1.0)

        def ssd_chunk(c):
            r0 = c * L
            dt = dt_scr[prev, r0:r0 + L, :]
            hi, mid, lo = _split3(a_neg * dt)
            cs = _dot(tri, hi) + _dot(tri, mid) + _dot(tri, lo)
            yield

            conv = cb_ref[...]
            for kk in range(SSM_CONV):
                o = HALO - (SSM_CONV - 1) + kk + r0
                conv = conv + cw_ref[kk:kk + 1, :] * xpad_ref[prev, o:o + L, :]
            xc = _silu(conv)
            xs = xc[:, 0:SSM_WIDTH]
            bgs, cgs, gmats = [], [], []
            for g in range(SSM_GROUPS):
                b0 = SSM_WIDTH + g * SSM_STATE
                c0 = SSM_WIDTH + SSM_GROUPS * SSM_STATE + g * SSM_STATE
                bgs.append(xc[:, b0:b0 + SSM_STATE])
                cgs.append(xc[:, c0:c0 + SSM_STATE].astype(BF16))
                gmats.append(_dot_nt(cgs[g], bgs[g].astype(BF16)))
            yield

            cs_t = cs.T
            dt_t = dt.T
            ydiags, st_news, e_mats, decs_g = [], [], [], []
            for g in range(SSM_GROUPS):
                bg_t = bgs[g].T
                xg = xs[:, g * gw:(g + 1) * gw]
                ydiag = None
                st_new = None
                ecols, decs = [], []
                for hh in range(HEADS_PER_GROUP):
                    h = g * HEADS_PER_GROUP + hh
                    colb = jnp.broadcast_to(cs[:, h:h + 1], (L, L))
                    rowb = cs_t[h:h + 1, :]
                    dtrow = dt_t[h:h + 1, :]
                    decay = jnp.exp(jnp.where(causal, colb - rowb, NEG_BIG))
                    mh = (gmats[g] * decay * dtrow).astype(BF16)
                    xm = jnp.where(lane_g == hh, xg, 0.0).astype(BF16)
                    last = colb[L - 1:L, :]
                    wrow = dtrow * jnp.exp(last - rowb)
                    btw = (bg_t * wrow).astype(BF16)
                    d1 = _dot(mh, xm)
                    d2 = _dot(btw, xm)
                    ydiag = d1 if ydiag is None else ydiag + d1
                    st_new = d2 if st_new is None else st_new + d2
                    ecols.append(jnp.exp(colb))
                    decs.append(jnp.exp(last))
                ydiags.append(ydiag)
                st_news.append(st_new)
                e_mats.append(jnp.concatenate([jnp.where(lane_lo, ecols[0], ecols[1]),
                                               jnp.where(lane_lo, ecols[2], ecols[3])], axis=1))
                decs_g.append(jnp.concatenate([jnp.where(lane_lo1, decs[0], decs[1]),
                                               jnp.where(lane_lo1, decs[2], decs[3])], axis=1))
                if g == 0:
                    yield

            ys = []
            for g in range(SSM_GROUPS):
                st_old = st_ref[g] * keep_state if c == 0 else st_ref[g]
                ys.append(ydiags[g] + _dot(cgs[g], st_old.astype(BF16)) * e_mats[g])
                st_ref[g] = st_old * decs_g[g] + st_news[g]
            y = jnp.concatenate(ys, axis=1) + dsk_ref[...] * xs
            gated = y * z_scr[prev, r0:r0 + L, :]
            outs = []
            for g in range(SSM_GROUPS):
                part = gated[:, g * gw:(g + 1) * gw]
                ms = jnp.mean(part * part, axis=-1, keepdims=True)
                outs.append(part * lax.rsqrt(ms + 1e-5))
            y_ref[r0:r0 + L, :] = (jnp.concatenate(outs, axis=1) * ynw_ref[...]).astype(BF16)
            yield

        h = _rms(x_ref[...], nw_ref[...], 1e-6).astype(BF16)
        scale = DIFF_HEAD_DIM ** -0.5 * LOG2E
        piece = 2 * LANES

        def pieces():
            for r in range(0, qw, piece):
                qt_ref[r:r + piece, :] = (_dot_nt(wvt_ref[r:r + piece, :], h) * scale).astype(BF16)
                yield
            for c in range(0, qw, piece):
                k_ref[:, c:c + piece] = _dot(h, w_ref[:, qw + c:qw + c + piece]).astype(BF16)
                yield
            for r in range(0, DIFF_WIDTH, piece):
                vt_ref[r:r + piece, :] = _dot_nt(wvt_ref[qw + r:qw + r + piece, :], h).astype(BF16)
                yield
            for c in range(0, SSM_WIDTH, piece):
                z_scr[cur, :, c:c + piece] = _silu(_dot(h, w_ref[:, c_z + c:c_z + c + piece]))
                yield
            dt_raw = _dot(h, w_ref[:, c_dt:c_dt + LANES]) + dtb_ref[...]
            dt_scr[cur] = jnp.maximum(dt_raw, 0.0) + jnp.log1p(jnp.exp(-jnp.abs(dt_raw)))
            yield
            tail = xpad_ref[prev, tm:tm + HALO, :]
            xpad_ref[cur, 0:HALO, :] = jnp.where(i % tiles_per_seq == 0, jnp.zeros_like(tail), tail)
            for c in range(0, SSM_XBC, piece):
                xpad_ref[cur, HALO:HALO + tm, c:c + piece] = _dot(h, w_ref[:, c_xbc + c:c_xbc + c + piece])
                yield

        pieces_after_group = (0, 2, 1, 0)
        proj = pieces()
        for c in range(tm // L):
            for group, _ in enumerate(ssd_chunk(c)):
                for _ in range(pieces_after_group[group]):
                    next(proj, None)
        for _ in proj:
            pass

    pl.when(i % 2 == 0)(functools.partial(stage, 0, 1))
    pl.when(i % 2 == 1)(functools.partial(stage, 1, 0))


def _hyb_call(x, nw, w_pad, wvt, conv_w, conv_b, dt_bias, a_log, d_skip, norm_w, *, batch, tm=PROJ_TM):
    T, D = x.shape
    S = T // batch
    assert S % tm == 0 and tm % SSM_CHUNK == 0
    nt = T // tm
    cur = lambda i: (jnp.minimum(i, nt - 1), 0)
    cur_t = lambda i: (0, jnp.minimum(i, nt - 1))
    prev = lambda i: (jnp.maximum(i - 1, 0), 0)
    pad = lambda v: jnp.pad(v.astype(F32), (0, LANES - v.shape[0])).reshape(1, LANES)
    dsk = jnp.repeat(d_skip.astype(F32), SSM_HEAD_DIM).reshape(1, SSM_WIDTH)
    outs = [
        jax.ShapeDtypeStruct((DIFF_WIDTH, T), BF16),
        jax.ShapeDtypeStruct((T, DIFF_WIDTH), BF16),
        jax.ShapeDtypeStruct((DIFF_WIDTH, T), BF16),
        jax.ShapeDtypeStruct((T, SSM_WIDTH), BF16),
    ]
    out_specs = [pl.BlockSpec((DIFF_WIDTH, tm), cur_t), pl.BlockSpec((tm, DIFF_WIDTH), cur),
                 pl.BlockSpec((DIFF_WIDTH, tm), cur_t), pl.BlockSpec((tm, SSM_WIDTH), prev)]
    nbytes = ((w_pad.size + wvt.size) * 2 + 2 * tm * D * 4 + 2 * tm * 2 * 4 * DIFF_WIDTH
              + 2 * (tm + HALO) * SSM_XBC * 4 + 2 * tm * (SSM_WIDTH + LANES) * 4 + 16 * tm * SSM_XBC * 4)
    return pl.pallas_call(
        functools.partial(_hyb_kernel, tm=tm, tiles_per_seq=S // tm),
        grid=(nt + 1,),
        in_specs=[pl.BlockSpec((tm, D), cur), _resident((1, D)), _resident(w_pad.shape),
                  _resident(wvt.shape), _resident((SSM_CONV, SSM_XBC)), _resident((1, SSM_XBC)),
                  _resident((1, LANES)), _resident((1, LANES)), _resident((1, SSM_WIDTH)),
                  _resident((1, SSM_WIDTH))],
        out_specs=out_specs,
        out_shape=outs,
        scratch_shapes=[pltpu.VMEM((2, tm + HALO, SSM_XBC), F32), pltpu.VMEM((2, tm, SSM_WIDTH), F32),
                        pltpu.VMEM((2, tm, LANES), F32),
                        pltpu.VMEM((SSM_GROUPS, SSM_STATE, HEADS_PER_GROUP * SSM_HEAD_DIM), F32)],
        compiler_params=pltpu.CompilerParams(
            dimension_semantics=("arbitrary",), vmem_limit_bytes=_vmem_limit(nbytes)),
        name="hyb_ssd",
    )(x, nw.reshape(1, D), w_pad, wvt, conv_w.astype(F32), conv_b.reshape(1, SSM_XBC).astype(F32),
      pad(dt_bias), pad(a_log), dsk, norm_w.reshape(1, SSM_WIDTH).astype(F32))


def _sc_kernel(x_ref, xprev_ref, nw_ref, win_ref, cw_ref, wout_ref, o_ref, vpad_ref, bg_ref, *,
               tm, tiles_per_seq):
    i = pl.program_id(0)
    d = x_ref.shape[1]

    @pl.when(i == 0)
    def _():
        vpad_ref[...] = jnp.zeros(vpad_ref.shape, F32)
        bg_ref[...] = jnp.zeros(bg_ref.shape, F32)

    def finish_chunk(prev, c0, c1):
        conv = cw_ref[SC_CONV - 1:SC_CONV, c0:c1] * vpad_ref[prev, HALO:HALO + tm, c0:c1]
        for kk in range(SC_CONV - 1):
            o = HALO - (SC_CONV - 1) + kk
            conv = conv + cw_ref[kk:kk + 1, c0:c1] * vpad_ref[prev, o:o + tm, c0:c1]
        return _dot((bg_ref[prev, :, c0:c1] * conv).astype(BF16), wout_ref[c0:c1, :].astype(BF16))

    def stage(cur, prev):
        cw = 2 * LANES
        chunks = [(c0, c0 + cw) for c0 in range(0, d, cw)]
        h = _rms(x_ref[...], nw_ref[...], 1e-6).astype(BF16)
        bgate = _dot(h, win_ref[:, 0:d].astype(BF16))
        out = finish_chunk(prev, *chunks[0])
        cgate = _dot(h, win_ref[:, d:2 * d].astype(BF16))
        out = out + finish_chunk(prev, *chunks[1])
        u = _dot(h, win_ref[:, 2 * d:3 * d].astype(BF16))
        for c0, c1 in chunks[2:]:
            out = out + finish_chunk(prev, c0, c1)
        o_ref[...] = xprev_ref[...] + out

        tail = vpad_ref[prev, tm:tm + HALO, :]
        vpad_ref[cur, 0:HALO, :] = jnp.where(i % tiles_per_seq == 0, jnp.zeros_like(tail), tail)
        vpad_ref[cur, HALO:HALO + tm, :] = cgate * u
        bg_ref[cur] = bgate

    pl.when(i % 2 == 0)(functools.partial(stage, 0, 1))
    pl.when(i % 2 == 1)(functools.partial(stage, 1, 0))


def _sc_call(x, nw, w_in, conv_w, w_out, layer, *, batch, tm=PROJ_TM):
    T, D = x.shape
    S = T // batch
    assert S % tm == 0
    nt = T // tm
    cur = lambda i: (jnp.minimum(i, nt - 1), 0)
    prev = lambda i: (jnp.maximum(i - 1, 0), 0)
    nbytes = (w_in[layer].size + w_out[layer].size) * 4 + 6 * tm * D * 4 + 4 * tm * D * 4 + 8 * tm * D * 4
    return pl.pallas_call(
        functools.partial(_sc_kernel, tm=tm, tiles_per_seq=S // tm),
        grid=(nt + 1,),
        in_specs=[pl.BlockSpec((tm, D), cur), pl.BlockSpec((tm, D), prev), _resident((1, D)),
                  _layer_resident(w_in, layer), _resident(conv_w.shape), _layer_resident(w_out, layer)],
        out_specs=pl.BlockSpec((tm, D), prev),
        out_shape=jax.ShapeDtypeStruct((T, D), F32),
        scratch_shapes=[pltpu.VMEM((2, tm + HALO, D), F32), pltpu.VMEM((2, tm, D), F32)],
        compiler_params=pltpu.CompilerParams(
            dimension_semantics=("arbitrary",), vmem_limit_bytes=_vmem_limit(nbytes)),
        name="short_conv",
    )(x, x, nw.reshape(1, D), w_in, conv_w.astype(F32), w_out)


def kernel(x, rel_bias, final_norm_w, ffn1_norm, ffn1_wg, ffn1_wu, ffn1_wd, mix_norm, ffn2_norm, ffn2_wg, ffn2_wu, ffn2_wd, hyb_w_in, hyb_w_out, diff_lq1, diff_lk1, diff_lq2, diff_lk2, diff_subln_w, ssm_conv_w, ssm_conv_b, ssm_dt_bias, ssm_a_log, ssm_d, ssm_norm_w, sc_w_in, sc_conv_w, sc_w_out):
    B, S, D = x.shape
    T = B * S
    xt = x.reshape(T, D)
    bf = lambda w: w.astype(BF16)

    lam_inits = [0.8 - 0.6 * math.exp(-0.3 * i) for i in range(0, DEPTH, 2)]
    tiles, lams = _prep_call(rel_bias, diff_lq1, diff_lk1, diff_lq2, diff_lk2, lam_inits)

    mix = None
    for i in range(DEPTH):
        j = i // 2
        xt = _ffn_call(xt, ffn1_norm[i], ffn1_wg, ffn1_wu, ffn1_wd, i)
        if i % 2 == 0:
            w_in = hyb_w_in[j]
            qkv_end = 3 * DIFF_WIDTH
            w_pad = bf(jnp.pad(w_in, ((0, 0), (0, LANES - SSM_HEADS))))
            wvt = bf(jnp.concatenate([w_in[:, 0:DIFF_WIDTH], w_in[:, 2 * DIFF_WIDTH:qkv_end]], axis=1).T)
            q, k, vt, y = _hyb_call(xt, mix_norm[i], w_pad, wvt, ssm_conv_w[j], ssm_conv_b[j],
                                    ssm_dt_bias[j], ssm_a_log[j], ssm_d[j], ssm_norm_w[j], batch=B)
            o = _attn_call(q, k, vt, tiles, lams[j * SUBLANES:(j + 1) * SUBLANES], diff_subln_w[j],
                           batch=B, lam_init=lam_inits[j])
            mix = (o, y, hyb_w_out, j)
        else:
            xt = _sc_call(xt, mix_norm[i], sc_w_in, sc_conv_w[j], sc_w_out, j, batch=B)
            mix = None
        xt = _ffn_call(xt, ffn2_norm[i], ffn2_wg, ffn2_wu, ffn2_wd, i, mix=mix,
                       final_w=final_norm_w if i == DEPTH - 1 else None)
    return xt.reshape(B, S, D)
```

```python
import functools
import math

import jax
import jax.numpy as jnp
from jax import lax
from jax.experimental import pallas as pl
from jax.experimental.pallas import tpu as pltpu

F32 = jnp.float32
BF16 = jnp.bfloat16

DEPTH = 4
N_DIFF_HEADS = 4
DIFF_HEAD_DIM = 64
DIFF_WIDTH = N_DIFF_HEADS * 2 * DIFF_HEAD_DIM
NUM_BUCKETS = 32
MAX_DISTANCE = 128
SSM_HEADS = 8
SSM_HEAD_DIM = 64
SSM_WIDTH = SSM_HEADS * SSM_HEAD_DIM
SSM_GROUPS = 2
SSM_STATE = 128
SSM_CONV = 4
SSM_CHUNK = 128
HEADS_PER_GROUP = SSM_HEADS // SSM_GROUPS
SSM_XBC = SSM_WIDTH + 2 * SSM_GROUPS * SSM_STATE
SC_CONV = 3

LANES = 128
SUBLANES = 8
V7X_SCOPED_VMEM_CAP = 60000 * 1024

FFN_TM = 512
FFN_FCHUNK = 256
PROJ_TM = 512
ATT_T = 512
ATT_STRIP = 256
ATT_ONES = 16
ATT_MAX_UNROLL = 26
ATT_FAR_TILE = 2
HALO = SUBLANES

NEG_BIG = -1e30
LOG2E = math.log2(math.e)


def _vmem_limit(nbytes):
    return int(min(V7X_SCOPED_VMEM_CAP, nbytes * 5 // 4 + (8 << 20)))


def _dot(a, b):
    return jnp.dot(a, b, preferred_element_type=F32)


def _dot_nt(a, b):
    return lax.dot_general(a, b, (((1,), (1,)), ((), ())), preferred_element_type=F32)


def _rms(x, w, eps):
    ms = jnp.mean(x * x, axis=-1, keepdims=True)
    return x * lax.rsqrt(ms + eps) * w


def _silu(x):
    h = 0.5 * x
    return h + h * jnp.tanh(h)


def _resident(shape):
    nd = len(shape)
    return pl.BlockSpec(shape, lambda *_: (0,) * nd, pipeline_mode=pl.Buffered(1))


def _ffn_kernel(*refs, pre_mix, final_norm, f_total):
    refs = list(refs)
    x_ref = refs.pop(0)
    if pre_mix:
        o_ref, y_ref, wo_ref = refs.pop(0), refs.pop(0), refs.pop(0)
    nw_ref, wg_ref, wu_ref, wd_ref = refs.pop(0), refs.pop(0), refs.pop(0), refs.pop(0)
    if final_norm:
        fw_ref = refs.pop(0)
    out_ref = refs.pop(0)

    x = x_ref[...]
    if pre_mix:
        half = o_ref.shape[1]
        x = (x + _dot(o_ref[...], wo_ref[0:half, :].astype(BF16))
             + _dot(y_ref[...], wo_ref[half:, :].astype(BF16)))
    h = (x * nw_ref[...]).astype(BF16)
    r = lax.rsqrt(jnp.mean(x * x, axis=-1, keepdims=True) + 1e-6)
    acts = []
    for c0 in range(0, f_total, FFN_FCHUNK):
        c1 = c0 + FFN_FCHUNK
        g = _dot(h, wg_ref[:, c0:c1].astype(BF16)) * r
        u = _dot(h, wu_ref[:, c0:c1].astype(BF16)) * r
        acts.append((_silu(g) * u).astype(BF16))
    acc = _dot(jnp.concatenate(acts, axis=1), wd_ref[...].astype(BF16))
    y = x + 0.5 * acc
    if final_norm:
        y = _rms(y, fw_ref[...], 1e-6)
    out_ref[...] = y


def _layer_resident(stacked, layer):
    _, r, c = stacked.shape
    return pl.BlockSpec((None, r, c), lambda *_: (layer, 0, 0), pipeline_mode=pl.Buffered(1))


def _ffn_call(x, nw, wg, wu, wd, layer, *, mix=None, final_w=None, tm=FFN_TM):
    T, D = x.shape
    F = wg.shape[2]
    assert T % tm == 0 and F % FFN_FCHUNK == 0
    row = lambda i: (i, 0)
    args = [x]
    in_specs = [pl.BlockSpec((tm, D), row)]
    nbytes = 4 * tm * D * 4 + 3 * D * F * 4 + 6 * tm * FFN_FCHUNK * 4 + 3 * tm * D * 4
    if mix is not None:
        o, y, wo, mix_layer = mix
        args += [o, y, wo]
        in_specs += [pl.BlockSpec((tm, o.shape[1]), row), pl.BlockSpec((tm, y.shape[1]), row),
                     _layer_resident(wo, mix_layer)]
        nbytes += wo.shape[1] * wo.shape[2] * 4 + 4 * tm * o.shape[1] * 2 * 2
    args += [nw.reshape(1, D), wg, wu, wd]
    in_specs += [_resident((1, D)), _layer_resident(wg, layer), _layer_resident(wu, layer),
                 _layer_resident(wd, layer)]
    if final_w is not None:
        args.append(final_w.reshape(1, D))
        in_specs.append(_resident((1, D)))
    kern = functools.partial(_ffn_kernel, pre_mix=mix is not None, final_norm=final_w is not None,
                             f_total=F)
    return pl.pallas_call(
        kern,
        grid=(T // tm,),
        in_specs=in_specs,
        out_specs=pl.BlockSpec((tm, D), row),
        out_shape=jax.ShapeDtypeStruct((T, D), F32),
        compiler_params=pltpu.CompilerParams(
            dimension_semantics=("arbitrary",), vmem_limit_bytes=_vmem_limit(nbytes)),
        name="ffn",
    )(*args)


def _prep_kernel(tab_ref, lq1_ref, lk1_ref, lq2_ref, lk2_ref, tile_ref, lam_ref, *, t, lam_inits):
    hd = pl.program_id(0)
    key = lax.broadcasted_iota(jnp.int32, (t, t), 0)
    qry = lax.broadcasted_iota(jnp.int32, (t, t), 1)
    max_exact = NUM_BUCKETS // 2
    far = tab_ref[hd, NUM_BUCKETS - 1]
    tile_ref[ATT_FAR_TILE] = jnp.zeros((t, t), F32)
    for off in range(ATT_FAR_TILE):
        dist = off * t + qry - key
        d = jnp.maximum(dist, 0)
        large = max_exact + (
            jnp.log(jnp.maximum(d, 1).astype(F32) / max_exact)
            / math.log(MAX_DISTANCE / max_exact) * (NUM_BUCKETS - max_exact)).astype(jnp.int32)
        bucket = jnp.where(d < max_exact, d, jnp.minimum(large, NUM_BUCKETS - 1))
        bias = jnp.full((t, t), far, F32)
        for b in range(NUM_BUCKETS - 1):
            bias = jnp.where(bucket == b, tab_ref[hd, b], bias)
        val = (bias - far) * LOG2E
        if off == 0:
            val = jnp.where(dist >= 0, val, NEG_BIG)
        tile_ref[off] = val
    for j, lam_init in enumerate(lam_inits):
        s1 = jnp.sum(lq1_ref[j:j + 1, :] * lk1_ref[j:j + 1, :], axis=-1, keepdims=True)
        s2 = jnp.sum(lq2_ref[j:j + 1, :] * lk2_ref[j:j + 1, :], axis=-1, keepdims=True)
        lam = jnp.exp(s1) - jnp.exp(s2) + lam_init
        lam_ref[j * SUBLANES:(j + 1) * SUBLANES, :] = jnp.broadcast_to(lam, (SUBLANES, LANES))


def _prep_call(rel_bias, lq1, lk1, lq2, lk2, lam_inits, *, t=ATT_T):
    ne = lq1.shape[0]
    H = rel_bias.shape[1]
    whole = lambda shape: pl.BlockSpec(shape, lambda h: (0,) * len(shape))
    return pl.pallas_call(
        functools.partial(_prep_kernel, t=t, lam_inits=tuple(lam_inits)),
        grid=(H,),
        in_specs=[pl.BlockSpec(memory_space=pltpu.SMEM)] + [whole(lq1.shape)] * 4,
        out_specs=[pl.BlockSpec((None, ATT_FAR_TILE + 1, t, t), lambda h: (h, 0, 0, 0)),
                   whole((ne * SUBLANES, LANES))],
        out_shape=[jax.ShapeDtypeStruct((H, ATT_FAR_TILE + 1, t, t), F32),
                   jax.ShapeDtypeStruct((ne * SUBLANES, LANES), F32)],
        compiler_params=pltpu.CompilerParams(
            dimension_semantics=("arbitrary",), vmem_limit_bytes=_vmem_limit(12 * t * t * 4)),
        name="attn_prep",
    )(rel_bias.T.astype(F32), lq1, lk1, lq2, lk2)


def _attn_kernel(qt_ref, k_ref, vt_ref, tile_ref, lam_ref, sw_ref, o_ref,
                 qm_ref, m_ref, acc_ref, s_ref, smax_ref, *, t, nq, unroll, q_split, lam_init):
    e = 2 * DIFF_HEAD_DIM
    first = lax.broadcasted_iota(jnp.int32, (LANES, t), 0) < DIFF_HEAD_DIM
    ones = jnp.ones((ATT_ONES, t), BF16)
    chains = [(w, slice(c0, c0 + ATT_STRIP)) for c0 in range(0, t, ATT_STRIP) for w in range(2)]

    def rows(blk):
        return pl.ds(pl.multiple_of(blk * t, t), t)

    def prepare(qi, carry):
        q = qt_ref[:, rows(qi)]
        zero = jnp.zeros_like(q)
        qm_ref[0, :, rows(qi)] = jnp.where(first, q, zero)
        qm_ref[1, :, rows(qi)] = jnp.where(first, zero, q)
        m_ref[qi] = jnp.full(m_ref.shape[1:], NEG_BIG, F32)
        acc_ref[qi] = jnp.zeros(acc_ref.shape[1:], F32)
        return carry

    def score_chain(ci, qi, j, slot, k, biased):
        w, cols = chains[ci]
        q_cols = pl.ds(pl.multiple_of(qi * t + cols.start, ATT_STRIP), ATT_STRIP)
        s = _dot(k, qm_ref[w, :, q_cols])
        if biased:
            near = jnp.minimum(qi - j, ATT_FAR_TILE)
            s = s + tile_ref[near, :, cols]
        s_ref[slot, ci] = s
        smax_ref[slot, ci] = jnp.max(s, axis=0, keepdims=True)

    def value_chain(ci, qi, slot, vta):
        w, cols = chains[ci]
        m_old = m_ref[qi, w, :, cols]
        m_new = jnp.maximum(m_old, smax_ref[slot, ci])
        p = jnp.exp2(s_ref[slot, ci] - m_new).astype(BF16)
        acc_ref[qi, w, :, cols] = jnp.exp2(m_old - m_new) * acc_ref[qi, w, :, cols] + _dot(vta, p)
        m_ref[qi, w, :, cols] = m_new

    def finalize(pair, carry):
        for qi in (2 * pair, 2 * pair + 1):
            lam = lam_ref[0:1, 0:1]
            o1 = acc_ref[qi, 0, 0:e, :] * (1.0 / acc_ref[qi, 0, e:e + 1, :])
            o2 = acc_ref[qi, 1, 0:e, :] * (1.0 / acc_ref[qi, 1, e:e + 1, :])
            out_t = o1 - lam * o2
            ms = jnp.mean(out_t * out_t, axis=0, keepdims=True)
            out_t = out_t * lax.rsqrt(ms + 1e-5) * sw_ref[...] * (1.0 - lam_init)
            o_ref[rows(qi), :] = out_t.T.astype(BF16)
        return carry

    def sweep(first_pair, next_pair, last_pair, count, biased):
        def step(qv, jv, slot):
            qs, js = next_pair(qv, jv)
            past_end = qs >= nq
            qs = jnp.where(past_end, last_pair[0], qs)
            js = jnp.where(past_end, last_pair[1], js)
            k_next = k_ref[rows(js), :]
            vta = jnp.concatenate([vt_ref[:, rows(jv)], ones], axis=0)
            for ci in range(len(chains)):
                score_chain(ci, qs, js, 1 - slot, k_next, biased)
                value_chain(ci, qv, slot, vta)
            return qs, js

        trip = max(u for u in range(unroll, ATT_MAX_UNROLL + 1, 2) if count % u == 0)

        def body(_, carry):
            qv, jv = carry
            for u in range(trip):
                qv, jv = step(qv, jv, u % 2)
            return qv, jv

        k0 = k_ref[rows(first_pair[1]), :]
        for ci in range(len(chains)):
            score_chain(ci, first_pair[0], first_pair[1], 0, k0, biased)
        lax.fori_loop(0, count // trip, body, (jnp.int32(first_pair[0]), jnp.int32(first_pair[1])))

    def next_near(qv, jv):
        wrap = jv == qv
        qn = qv + 1
        return jnp.where(wrap, qn, qv), jnp.where(wrap, jnp.where(qn < q_split, 0, qn - 1), jv + 1)

    def next_far(qv, jv):
        wrap = jv == qv - ATT_FAR_TILE
        return jnp.where(wrap, qv + 1, qv), jnp.where(wrap, 0, jv + 1)

    n_near = q_split * (q_split + 1) // 2 + ATT_FAR_TILE * (nq - q_split)
    n_far = nq * (nq + 1) // 2 - n_near
    lax.fori_loop(0, nq, prepare, 0)
    sweep((0, 0), next_near, (nq - 1, nq - 1), n_near, True)
    if n_far:
        sweep((q_split, 0), next_far, (nq - 1, nq - 1 - ATT_FAR_TILE), n_far, False)
    lax.fori_loop(0, nq // 2, finalize, 0)


def _attn_plan(nq):
    for unroll in (8, 4, 2):
        for q_split in range(ATT_FAR_TILE, nq + 1):
            n_near = q_split * (q_split + 1) // 2 + ATT_FAR_TILE * (nq - q_split)
            n_far = nq * (nq + 1) // 2 - n_near
            if n_near % unroll == 0 and n_far % unroll == 0:
                return unroll, q_split
    raise ValueError(f"unsupported number of query blocks: {nq}")


def _attn_call(qt, k, vt, tiles, lam, subln_w, *, batch, lam_init, t=ATT_T):
    T, W = k.shape
    S = T // batch
    H = W // LANES
    nq = S // t
    e = 2 * DIFF_HEAD_DIM
    assert e == LANES and S % t == 0 and nq % 2 == 0
    unroll, q_split = _attn_plan(nq)
    n_chains = 2 * (t // ATT_STRIP)
    n_tiles = ATT_FAR_TILE + 1
    nbytes = (2 * 4 * (S * LANES * 2) + 2 * n_tiles * t * t * 4 + 2 * S * LANES * 2
              + nq * 2 * (e + ATT_ONES + SUBLANES) * t * 4 + 2 * n_chains * t * ATT_STRIP * 4 + 2 * t * t * 4)
    return pl.pallas_call(
        functools.partial(_attn_kernel, t=t, nq=nq, unroll=unroll, q_split=q_split, lam_init=lam_init),
        grid=(batch, H),
        in_specs=[
            pl.BlockSpec((LANES, S), lambda b, h: (h, b)),
            pl.BlockSpec((S, LANES), lambda b, h: (b, h)),
            pl.BlockSpec((LANES, S), lambda b, h: (h, b)),
            pl.BlockSpec((None, n_tiles, t, t), lambda b, h: (h, 0, 0, 0)),
            pl.BlockSpec((SUBLANES, LANES), lambda b, h: (0, 0)),
            pl.BlockSpec((e, 1), lambda b, h: (0, 0)),
        ],
        out_specs=pl.BlockSpec((S, LANES), lambda b, h: (b, h)),
        out_shape=jax.ShapeDtypeStruct((T, W), BF16),
        scratch_shapes=[pltpu.VMEM((2, LANES, S), BF16),
                        pltpu.VMEM((nq, 2, 1, t), F32), pltpu.VMEM((nq, 2, e + ATT_ONES, t), F32),
                        pltpu.VMEM((2, n_chains, t, ATT_STRIP), F32),
                        pltpu.VMEM((2, n_chains, 1, ATT_STRIP), F32)],
        compiler_params=pltpu.CompilerParams(
            dimension_semantics=("arbitrary", "arbitrary"), vmem_limit_bytes=_vmem_limit(nbytes)),
        name="diff_attn",
    )(qt, k, vt, tiles, lam, subln_w.reshape(e, 1))


def _split3(a):
    hi = a.astype(BF16)
    r1 = a - hi.astype(F32)
    mid = r1.astype(BF16)
    lo = (r1 - mid.astype(F32)).astype(BF16)
    return hi, mid, lo


def _hyb_kernel(x_ref, nw_ref, w_ref, wvt_ref, cw_ref, cb_ref, dtb_ref, alog_ref, dsk_ref, ynw_ref,
                qt_ref, k_ref, vt_ref, y_ref, xpad_ref, z_scr, dt_scr, st_ref, *, tm, tiles_per_seq):
    i = pl.program_id(0)
    L = SSM_CHUNK
    gw = HEADS_PER_GROUP * SSM_HEAD_DIM
    qw = DIFF_WIDTH
    c_z = 2 * qw + DIFF_WIDTH
    c_xbc = c_z + SSM_WIDTH
    c_dt = c_xbc + SSM_XBC

    @pl.when(i == 0)
    def _():
        xpad_ref[...] = jnp.zeros(xpad_ref.shape, F32)
        z_scr[...] = jnp.zeros(z_scr.shape, F32)
        dt_scr[...] = jnp.zeros(dt_scr.shape, F32)
        st_ref[...] = jnp.zeros(st_ref.shape, F32)

    def stage(cur, prev):
        row = lax.broadcasted_iota(jnp.int32, (L, L), 0)
        col = lax.broadcasted_iota(jnp.int32, (L, L), 1)
        causal = row >= col
        tri = jnp.where(causal, 1.0, 0.0).astype(BF16)
        lane_g = lax.broadcasted_iota(jnp.int32, (L, gw), 1) // SSM_HEAD_DIM
        lane_lo = lax.broadcasted_iota(jnp.int32, (L, LANES), 1) < SSM_HEAD_DIM
        lane_lo1 = lax.broadcasted_iota(jnp.int32, (1, LANES), 1) < SSM_HEAD_DIM
        a_neg = -jnp.exp(alog_ref[...])
        keep_state = jnp.where((i - 1) % tiles_per_seq == 0, 0.0, 1.0)

        def ssd_chunk(c):
            r0 = c * L
            dt = dt_scr[prev, r0:r0 + L, :]
            hi, mid, lo = _split3(a_neg * dt)
            cs = _dot(tri, hi) + _dot(tri, mid) + _dot(tri, lo)
            yield

            conv = cb_ref[...]
            for kk in range(SSM_CONV):
                o = HALO - (SSM_CONV - 1) + kk + r0
                conv = conv + cw_ref[kk:kk + 1, :] * xpad_ref[prev, o:o + L, :]
            xc = _silu(conv)
            xs = xc[:, 0:SSM_WIDTH]
            bgs, cgs, gmats = [], [], []
            for g in range(SSM_GROUPS):
                b0 = SSM_WIDTH + g * SSM_STATE
                c0 = SSM_WIDTH + SSM_GROUPS * SSM_STATE + g * SSM_STATE
                bgs.append(xc[:, b0:b0 + SSM_STATE])
                cgs.append(xc[:, c0:c0 + SSM_STATE].astype(BF16))
                gmats.append(_dot_nt(cgs[g], bgs[g].astype(BF16)))
            yield

            cs_t = cs.T
            dt_t = dt.T
            ydiags, st_news, e_mats, decs_g = [], [], [], []
            for g in range(SSM_GROUPS):
                bg_t = bgs[g].T
                xg = xs[:, g * gw:(g + 1) * gw]
                ydiag = None
                st_new = None
                ecols, decs = [], []
                for hh in range(HEADS_PER_GROUP):
                    h = g * HEADS_PER_GROUP + hh
                    colb = jnp.broadcast_to(cs[:, h:h + 1], (L, L))
                    rowb = cs_t[h:h + 1, :]
                    dtrow = dt_t[h:h + 1, :]
                    decay = jnp.exp(jnp.where(causal, colb - rowb, NEG_BIG))
                    mh = (gmats[g] * decay * dtrow).astype(BF16)
                    xm = jnp.where(lane_g == hh, xg, 0.0).astype(BF16)
                    last = colb[L - 1:L, :]
                    wrow = dtrow * jnp.exp(last - rowb)
                    btw = (bg_t * wrow).astype(BF16)
                    d1 = _dot(mh, xm)
                    d2 = _dot(btw, xm)
                    ydiag = d1 if ydiag is None else ydiag + d1
                    st_new = d2 if st_new is None else st_new + d2
                    ecols.append(jnp.exp(colb))
                    decs.append(jnp.exp(last))
                ydiags.append(ydiag)
                st_news.append(st_new)
                e_mats.append(jnp.concatenate([jnp.where(lane_lo, ecols[0], ecols[1]),
                                               jnp.where(lane_lo, ecols[2], ecols[3])], axis=1))
                decs_g.append(jnp.concatenate([jnp.where(lane_lo1, decs[0], decs[1]),
                                               jnp.where(lane_lo1, decs[2], decs[3])], axis=1))
                if g == 0:
                    yield

            ys = []
            for g in range(SSM_GROUPS):
                st_old = st_ref[g] * keep_state if c == 0 else st_ref[g]
                ys.append(ydiags[g] + _dot(cgs[g], st_old.astype(BF16)) * e_mats[g])
                st_ref[g] = st_old * decs_g[g] + st_news[g]
            y = jnp.concatenate(ys, axis=1) + dsk_ref[...] * xs
            gated = y * z_scr[prev, r0:r0 + L, :]
            outs = []
            for g in range(SSM_GROUPS):
                part = gated[:, g * gw:(g + 1) * gw]
                ms = jnp.mean(part * part, axis=-1, keepdims=True)
                outs.append(part * lax.rsqrt(ms + 1e-5))
            y_ref[r0:r0 + L, :] = (jnp.concatenate(outs, axis=1) * ynw_ref[...]).astype(BF16)
            yield

        h = _rms(x_ref[...], nw_ref[...], 1e-6).astype(BF16)
        scale = DIFF_HEAD_DIM ** -0.5 * LOG2E
        piece = 2 * LANES

        def pieces():
            for r in range(0, qw, piece):
                qt_ref[r:r + piece, :] = (_dot_nt(wvt_ref[r:r + piece, :], h) * scale).astype(BF16)
                yield
            for c in range(0, qw, piece):
                k_ref[:, c:c + piece] = _dot(h, w_ref[:, qw + c:qw + c + piece]).astype(BF16)
                yield
            for r in range(0, DIFF_WIDTH, piece):
                vt_ref[r:r + piece, :] = _dot_nt(wvt_ref[qw + r:qw + r + piece, :], h).astype(BF16)
                yield
            for c in range(0, SSM_WIDTH, piece):
                z_scr[cur, :, c:c + piece] = _silu(_dot(h, w_ref[:, c_z + c:c_z + c + piece]))
                yield
            dt_raw = _dot(h, w_ref[:, c_dt:c_dt + LANES]) + dtb_ref[...]
            dt_scr[cur] = jnp.maximum(dt_raw, 0.0) + jnp.log1p(jnp.exp(-jnp.abs(dt_raw)))
            yield
            tail = xpad_ref[prev, tm:tm + HALO, :]
            xpad_ref[cur, 0:HALO, :] = jnp.where(i % tiles_per_seq == 0, jnp.zeros_like(tail), tail)
            for c in range(0, SSM_XBC, piece):
                xpad_ref[cur, HALO:HALO + tm, c:c + piece] = _dot(h, w_ref[:, c_xbc + c:c_xbc + c + piece])
                yield

        pieces_after_group = (0, 2, 1, 0)
        proj = pieces()
        for c in range(tm // L):
            for group, _ in enumerate(ssd_chunk(c)):
                for _ in range(pieces_after_group[group]):
                    next(proj, None)
        for _ in proj:
            pass

    pl.when(i % 2 == 0)(functools.partial(stage, 0, 1))
    pl.when(i % 2 == 1)(functools.partial(stage, 1, 0))


def _hyb_call(x, nw, w_pad, wvt, conv_w, conv_b, dt_bias, a_log, d_skip, norm_w, *, batch, tm=PROJ_TM):
    T, D = x.shape
    S = T // batch
    assert S % tm == 0 and tm % SSM_CHUNK == 0
    nt = T // tm
    cur = lambda i: (jnp.minimum(i, nt - 1), 0)
    cur_t = lambda i: (0, jnp.minimum(i, nt - 1))
    prev = lambda i: (jnp.maximum(i - 1, 0), 0)
    pad = lambda v: jnp.pad(v.astype(F32), (0, LANES - v.shape[0])).reshape(1, LANES)
    dsk = jnp.repeat(d_skip.astype(F32), SSM_HEAD_DIM).reshape(1, SSM_WIDTH)
    outs = [
        jax.ShapeDtypeStruct((DIFF_WIDTH, T), BF16),
        jax.ShapeDtypeStruct((T, DIFF_WIDTH), BF16),
        jax.ShapeDtypeStruct((DIFF_WIDTH, T), BF16),
        jax.ShapeDtypeStruct((T, SSM_WIDTH), BF16),
    ]
    out_specs = [pl.BlockSpec((DIFF_WIDTH, tm), cur_t), pl.BlockSpec((tm, DIFF_WIDTH), cur),
                 pl.BlockSpec((DIFF_WIDTH, tm), cur_t), pl.BlockSpec((tm, SSM_WIDTH), prev)]
    nbytes = ((w_pad.size + wvt.size) * 2 + 2 * tm * D * 4 + 2 * tm * 2 * 4 * DIFF_WIDTH
              + 2 * (tm + HALO) * SSM_XBC * 4 + 2 * tm * (SSM_WIDTH + LANES) * 4 + 16 * tm * SSM_XBC * 4)
    return pl.pallas_call(
        functools.partial(_hyb_kernel, tm=tm, tiles_per_seq=S // tm),
        grid=(nt + 1,),
        in_specs=[pl.BlockSpec((tm, D), cur), _resident((1, D)), _resident(w_pad.shape),
                  _resident(wvt.shape), _resident((SSM_CONV, SSM_XBC)), _resident((1, SSM_XBC)),
                  _resident((1, LANES)), _resident((1, LANES)), _resident((1, SSM_WIDTH)),
                  _resident((1, SSM_WIDTH))],
        out_specs=out_specs,
        out_shape=outs,
        scratch_shapes=[pltpu.VMEM((2, tm + HALO, SSM_XBC), F32), pltpu.VMEM((2, tm, SSM_WIDTH), F32),
                        pltpu.VMEM((2, tm, LANES), F32),
                        pltpu.VMEM((SSM_GROUPS, SSM_STATE, HEADS_PER_GROUP * SSM_HEAD_DIM), F32)],
        compiler_params=pltpu.CompilerParams(
            dimension_semantics=("arbitrary",), vmem_limit_bytes=_vmem_limit(nbytes)),
        name="hyb_ssd",
    )(x, nw.reshape(1, D), w_pad, wvt, conv_w.astype(F32), conv_b.reshape(1, SSM_XBC).astype(F32),
      pad(dt_bias), pad(a_log), dsk, norm_w.reshape(1, SSM_WIDTH).astype(F32))


def _sc_kernel(x_ref, xprev_ref, nw_ref, win_ref, cw_ref, wout_ref, o_ref, vpad_ref, bg_ref, *,
               tm, tiles_per_seq):
    i = pl.program_id(0)
    d = x_ref.shape[1]

    @pl.when(i == 0)
    def _():
        vpad_ref[...] = jnp.zeros(vpad_ref.shape, F32)
        bg_ref[...] = jnp.zeros(bg_ref.shape, F32)

    def finish_chunk(prev, c0, c1):
        conv = cw_ref[SC_CONV - 1:SC_CONV, c0:c1] * vpad_ref[prev, HALO:HALO + tm, c0:c1]
        for kk in range(SC_CONV - 1):
            o = HALO - (SC_CONV - 1) + kk
            conv = conv + cw_ref[kk:kk + 1, c0:c1] * vpad_ref[prev, o:o + tm, c0:c1]
        return _dot((bg_ref[prev, :, c0:c1] * conv).astype(BF16), wout_ref[c0:c1, :].astype(BF16))

    def stage(cur, prev):
        cw = 2 * LANES
        chunks = [(c0, c0 + cw) for c0 in range(0, d, cw)]
        h = _rms(x_ref[...], nw_ref[...], 1e-6).astype(BF16)
        bgate = _dot(h, win_ref[:, 0:d].astype(BF16))
        out = finish_chunk(prev, *chunks[0])
        cgate = _dot(h, win_ref[:, d:2 * d].astype(BF16))
        out = out + finish_chunk(prev, *chunks[1])
        u = _dot(h, win_ref[:, 2 * d:3 * d].astype(BF16))
        for c0, c1 in chunks[2:]:
            out = out + finish_chunk(prev, c0, c1)
        o_ref[...] = xprev_ref[...] + out

        tail = vpad_ref[prev, tm:tm + HALO, :]
        vpad_ref[cur, 0:HALO, :] = jnp.where(i % tiles_per_seq == 0, jnp.zeros_like(tail), tail)
        vpad_ref[cur, HALO:HALO + tm, :] = cgate * u
        bg_ref[cur] = bgate

    pl.when(i % 2 == 0)(functools.partial(stage, 0, 1))
    pl.when(i % 2 == 1)(functools.partial(stage, 1, 0))


def _sc_call(x, nw, w_in, conv_w, w_out, layer, *, batch, tm=PROJ_TM):
    T, D = x.shape
    S = T // batch
    assert S % tm == 0
    nt = T // tm
    cur = lambda i: (jnp.minimum(i, nt - 1), 0)
    prev = lambda i: (jnp.maximum(i - 1, 0), 0)
    nbytes = (w_in[layer].size + w_out[layer].size) * 4 + 6 * tm * D * 4 + 4 * tm * D * 4 + 8 * tm * D * 4
    return pl.pallas_call(
        functools.partial(_sc_kernel, tm=tm, tiles_per_seq=S // tm),
        grid=(nt + 1,),
        in_specs=[pl.BlockSpec((tm, D), cur), pl.BlockSpec((tm, D), prev), _resident((1, D)),
                  _layer_resident(w_in, layer), _resident(conv_w.shape), _layer_resident(w_out, layer)],
        out_specs=pl.BlockSpec((tm, D), prev),
        out_shape=jax.ShapeDtypeStruct((T, D), F32),
        scratch_shapes=[pltpu.VMEM((2, tm + HALO, D), F32), pltpu.VMEM((2, tm, D), F32)],
        compiler_params=pltpu.CompilerParams(
            dimension_semantics=("arbitrary",), vmem_limit_bytes=_vmem_limit(nbytes)),
        name="short_conv",
    )(x, x, nw.reshape(1, D), w_in, conv_w.astype(F32), w_out)


def kernel(x, rel_bias, final_norm_w, ffn1_norm, ffn1_wg, ffn1_wu, ffn1_wd, mix_norm, ffn2_norm, ffn2_wg, ffn2_wu, ffn2_wd, hyb_w_in, hyb_w_out, diff_lq1, diff_lk1, diff_lq2, diff_lk2, diff_subln_w, ssm_conv_w, ssm_conv_b, ssm_dt_bias, ssm_a_log, ssm_d, ssm_norm_w, sc_w_in, sc_conv_w, sc_w_out):
    B, S, D = x.shape
    T = B * S
    xt = x.reshape(T, D)
    bf = lambda w: w.astype(BF16)

    lam_inits = [0.8 - 0.6 * math.exp(-0.3 * i) for i in range(0, DEPTH, 2)]
    tiles, lams = _prep_call(rel_bias, diff_lq1, diff_lk1, diff_lq2, diff_lk2, lam_inits)

    mix = None
    for i in range(DEPTH):
        j = i // 2
        xt = _ffn_call(xt, ffn1_norm[i], ffn1_wg, ffn1_wu, ffn1_wd, i)
        if i % 2 == 0:
            w_in = hyb_w_in[j]
            qkv_end = 3 * DIFF_WIDTH
            w_pad = bf(jnp.pad(w_in, ((0, 0), (0, LANES - SSM_HEADS))))
            wvt = bf(jnp.concatenate([w_in[:, 0:DIFF_WIDTH], w_in[:, 2 * DIFF_WIDTH:qkv_end]], axis=1).T)
            q, k, vt, y = _hyb_call(xt, mix_norm[i], w_pad, wvt, ssm_conv_w[j], ssm_conv_b[j],
                                    ssm_dt_bias[j], ssm_a_log[j], ssm_d[j], ssm_norm_w[j], batch=B)
            o = _attn_call(q, k, vt, tiles, lams[j * SUBLANES:(j + 1) * SUBLANES], diff_subln_w[j],
                           batch=B, lam_init=lam_inits[j])
            mix = (o, y, hyb_w_out, j)
        else:
            xt = _sc_call(xt, mix_norm[i], sc_w_in, sc_conv_w[j], sc_w_out, j, batch=B)
            mix = None
        xt = _ffn_call(xt, ffn2_norm[i], ffn2_wg, ffn2_wu, ffn2_wd, i, mix=mix,
                       final_w=final_norm_w if i == DEPTH - 1 else None)
    return xt.reshape(B, S, D)
```

```python
import functools
import math

import jax
import jax.numpy as jnp
from jax import lax
from jax.experimental import pallas as pl
from jax.experimental.pallas import tpu as pltpu

F32 = jnp.float32
BF16 = jnp.bfloat16

DEPTH = 4
N_DIFF_HEADS = 4
DIFF_HEAD_DIM = 64
DIFF_WIDTH = N_DIFF_HEADS * 2 * DIFF_HEAD_DIM
NUM_BUCKETS = 32
MAX_DISTANCE = 128
SSM_HEADS = 8
SSM_HEAD_DIM = 64
SSM_WIDTH = SSM_HEADS * SSM_HEAD_DIM
SSM_GROUPS = 2
SSM_STATE = 128
SSM_CONV = 4
SSM_CHUNK = 128
HEADS_PER_GROUP = SSM_HEADS // SSM_GROUPS
SSM_XBC = SSM_WIDTH + 2 * SSM_GROUPS * SSM_STATE
SC_CONV = 3

LANES = 128
SUBLANES = 8
V7X_SCOPED_VMEM_CAP = 60000 * 1024

FFN_TM = 512
FFN_FCHUNK = 256
PROJ_TM = 512
ATT_T = 512
ATT_STRIP = 256
ATT_ONES = 16
ATT_MAX_UNROLL = 26
ATT_FAR_TILE = 2
HALO = SUBLANES

NEG_BIG = -1e30
LOG2E = math.log2(math.e)


def _vmem_limit(nbytes):
    return int(min(V7X_SCOPED_VMEM_CAP, nbytes * 5 // 4 + (8 << 20)))


def _dot(a, b):
    return jnp.dot(a, b, preferred_element_type=F32)


def _dot_nt(a, b):
    return lax.dot_general(a, b, (((1,), (1,)), ((), ())), preferred_element_type=F32)


def _rms(x, w, eps):
    ms = jnp.mean(x * x, axis=-1, keepdims=True)
    return x * lax.rsqrt(ms + eps) * w


def _silu(x):
    h = 0.5 * x
    return h + h * jnp.tanh(h)


def _resident(shape):
    nd = len(shape)
    return pl.BlockSpec(shape, lambda *_: (0,) * nd, pipeline_mode=pl.Buffered(1))


def _ffn_kernel(*refs, pre_mix, final_norm, f_total):
    refs = list(refs)
    x_ref = refs.pop(0)
    if pre_mix:
        o_ref, y_ref, wo_ref = refs.pop(0), refs.pop(0), refs.pop(0)
    nw_ref, wg_ref, wu_ref, wd_ref = refs.pop(0), refs.pop(0), refs.pop(0), refs.pop(0)
    if final_norm:
        fw_ref = refs.pop(0)
    out_ref = refs.pop(0)

    x = x_ref[...]
    if pre_mix:
        half = o_ref.shape[1]
        x = (x + _dot(o_ref[...], wo_ref[0:half, :].astype(BF16))
             + _dot(y_ref[...], wo_ref[half:, :].astype(BF16)))
    h = (x * nw_ref[...]).astype(BF16)
    r = lax.rsqrt(jnp.mean(x * x, axis=-1, keepdims=True) + 1e-6)
    acts = []
    for c0 in range(0, f_total, FFN_FCHUNK):
        c1 = c0 + FFN_FCHUNK
        g = _dot(h, wg_ref[:, c0:c1].astype(BF16)) * r
        u = _dot(h, wu_ref[:, c0:c1].astype(BF16)) * r
        acts.append((_silu(g) * u).astype(BF16))
    acc = _dot(jnp.concatenate(acts, axis=1), wd_ref[...].astype(BF16))
    y = x + 0.5 * acc
    if final_norm:
        y = _rms(y, fw_ref[...], 1e-6)
    out_ref[...] = y


def _layer_resident(stacked, layer):
    _, r, c = stacked.shape
    return pl.BlockSpec((None, r, c), lambda *_: (layer, 0, 0), pipeline_mode=pl.Buffered(1))


def _ffn_call(x, nw, wg, wu, wd, layer, *, mix=None, final_w=None, tm=FFN_TM):
    T, D = x.shape
    F = wg.shape[2]
    assert T % tm == 0 and F % FFN_FCHUNK == 0
    row = lambda i: (i, 0)
    args = [x]
    in_specs = [pl.BlockSpec((tm, D), row)]
    nbytes = 4 * tm * D * 4 + 3 * D * F * 4 + 6 * tm * FFN_FCHUNK * 4 + 3 * tm * D * 4
    if mix is not None:
        o, y, wo, mix_layer = mix
        args += [o, y, wo]
        in_specs += [pl.BlockSpec((tm, o.shape[1]), row), pl.BlockSpec((tm, y.shape[1]), row),
                     _layer_resident(wo, mix_layer)]
        nbytes += wo.shape[1] * wo.shape[2] * 4 + 4 * tm * o.shape[1] * 2 * 2
    args += [nw.reshape(1, D), wg, wu, wd]
    in_specs += [_resident((1, D)), _layer_resident(wg, layer), _layer_resident(wu, layer),
                 _layer_resident(wd, layer)]
    if final_w is not None:
        args.append(final_w.reshape(1, D))
        in_specs.append(_resident((1, D)))
    kern = functools.partial(_ffn_kernel, pre_mix=mix is not None, final_norm=final_w is not None,
                             f_total=F)
    return pl.pallas_call(
        kern,
        grid=(T // tm,),
        in_specs=in_specs,
        out_specs=pl.BlockSpec((tm, D), row),
        out_shape=jax.ShapeDtypeStruct((T, D), F32),
        compiler_params=pltpu.CompilerParams(
            dimension_semantics=("arbitrary",), vmem_limit_bytes=_vmem_limit(nbytes)),
        name="ffn",
    )(*args)


def _prep_kernel(tab_ref, lq1_ref, lk1_ref, lq2_ref, lk2_ref, tile_ref, lam_ref, *, t, lam_inits):
    hd = pl.program_id(0)
    key = lax.broadcasted_iota(jnp.int32, (t, t), 0)
    qry = lax.broadcasted_iota(jnp.int32, (t, t), 1)
    max_exact = NUM_BUCKETS // 2
    far = tab_ref[hd, NUM_BUCKETS - 1]
    tile_ref[ATT_FAR_TILE] = jnp.zeros((t, t), F32)
    for off in range(ATT_FAR_TILE):
        dist = off * t + qry - key
        d = jnp.maximum(dist, 0)
        large = max_exact + (
            jnp.log(jnp.maximum(d, 1).astype(F32) / max_exact)
            / math.log(MAX_DISTANCE / max_exact) * (NUM_BUCKETS - max_exact)).astype(jnp.int32)
        bucket = jnp.where(d < max_exact, d, jnp.minimum(large, NUM_BUCKETS - 1))
        bias = jnp.full((t, t), far, F32)
        for b in range(NUM_BUCKETS - 1):
            bias = jnp.where(bucket == b, tab_ref[hd, b], bias)
        val = (bias - far) * LOG2E
        if off == 0:
            val = jnp.where(dist >= 0, val, NEG_BIG)
        tile_ref[off] = val
    for j, lam_init in enumerate(lam_inits):
        s1 = jnp.sum(lq1_ref[j:j + 1, :] * lk1_ref[j:j + 1, :], axis=-1, keepdims=True)
        s2 = jnp.sum(lq2_ref[j:j + 1, :] * lk2_ref[j:j + 1, :], axis=-1, keepdims=True)
        lam = jnp.exp(s1) - jnp.exp(s2) + lam_init
        lam_ref[j * SUBLANES:(j + 1) * SUBLANES, :] = jnp.broadcast_to(lam, (SUBLANES, LANES))


def _prep_call(rel_bias, lq1, lk1, lq2, lk2, lam_inits, *, t=ATT_T):
    ne = lq1.shape[0]
    H = rel_bias.shape[1]
    whole = lambda shape: pl.BlockSpec(shape, lambda h: (0,) * len(shape))
    return pl.pallas_call(
        functools.partial(_prep_kernel, t=t, lam_inits=tuple(lam_inits)),
        grid=(H,),
        in_specs=[pl.BlockSpec(memory_space=pltpu.SMEM)] + [whole(lq1.shape)] * 4,
        out_specs=[pl.BlockSpec((None, ATT_FAR_TILE + 1, t, t), lambda h: (h, 0, 0, 0)),
                   whole((ne * SUBLANES, LANES))],
        out_shape=[jax.ShapeDtypeStruct((H, ATT_FAR_TILE + 1, t, t), F32),
                   jax.ShapeDtypeStruct((ne * SUBLANES, LANES), F32)],
        compiler_params=pltpu.CompilerParams(
            dimension_semantics=("arbitrary",), vmem_limit_bytes=_vmem_limit(12 * t * t * 4)),
        name="attn_prep",
    )(rel_bias.T.astype(F32), lq1, lk1, lq2, lk2)


def _attn_kernel(qt_ref, k_ref, vt_ref, tile_ref, lam_ref, sw_ref, o_ref,
                 qm_ref, m_ref, acc_ref, s_ref, smax_ref, *, t, nq, unroll, q_split, lam_init):
    e = 2 * DIFF_HEAD_DIM
    first = lax.broadcasted_iota(jnp.int32, (LANES, t), 0) < DIFF_HEAD_DIM
    ones = jnp.ones((ATT_ONES, t), BF16)
    chains = [(w, slice(c0, c0 + ATT_STRIP)) for c0 in range(0, t, ATT_STRIP) for w in range(2)]

    def rows(blk):
        return pl.ds(pl.multiple_of(blk * t, t), t)

    def prepare(qi, carry):
        q = qt_ref[:, rows(qi)]
        zero = jnp.zeros_like(q)
        qm_ref[0, :, rows(qi)] = jnp.where(first, q, zero)
        qm_ref[1, :, rows(qi)] = jnp.where(first, zero, q)
        m_ref[qi] = jnp.full(m_ref.shape[1:], NEG_BIG, F32)
        acc_ref[qi] = jnp.zeros(acc_ref.shape[1:], F32)
        return carry

    def score_chain(ci, qi, j, slot, k, biased):
        w, cols = chains[ci]
        q_cols = pl.ds(pl.multiple_of(qi * t + cols.start, ATT_STRIP), ATT_STRIP)
        s = _dot(k, qm_ref[w, :, q_cols])
        if biased:
            near = jnp.minimum(qi - j, ATT_FAR_TILE)
            s = s + tile_ref[near, :, cols]
        s_ref[slot, ci] = s
        smax_ref[slot, ci] = jnp.max(s, axis=0, keepdims=True)

    def value_chain(ci, qi, slot, vta):
        w, cols = chains[ci]
        m_old = m_ref[qi, w, :, cols]
        m_new = jnp.maximum(m_old, smax_ref[slot, ci])
        p = jnp.exp2(s_ref[slot, ci] - m_new).astype(BF16)
        acc_ref[qi, w, :, cols] = jnp.exp2(m_old - m_new) * acc_ref[qi, w, :, cols] + _dot(vta, p)
        m_ref[qi, w, :, cols] = m_new

    def finalize(pair, carry):
        for qi in (2 * pair, 2 * pair + 1):
            lam = lam_ref[0:1, 0:1]
            o1 = acc_ref[qi, 0, 0:e, :] * (1.0 / acc_ref[qi, 0, e:e + 1, :])
            o2 = acc_ref[qi, 1, 0:e, :] * (1.0 / acc_ref[qi, 1, e:e + 1, :])
            out_t = o1 - lam * o2
            ms = jnp.mean(out_t * out_t, axis=0, keepdims=True)
            out_t = out_t * lax.rsqrt(ms + 1e-5) * sw_ref[...] * (1.0 - lam_init)
            o_ref[rows(qi), :] = out_t.T.astype(BF16)
        return carry

    def sweep(first_pair, next_pair, last_pair, count, biased):
        def step(qv, jv, slot):
            qs, js = next_pair(qv, jv)
            past_end = qs >= nq
            qs = jnp.where(past_end, last_pair[0], qs)
            js = jnp.where(past_end, last_pair[1], js)
            k_next = k_ref[rows(js), :]
            vta = jnp.concatenate([vt_ref[:, rows(jv)], ones], axis=0)
            for ci in range(len(chains)):
                score_chain(ci, qs, js, 1 - slot, k_next, biased)
                value_chain(ci, qv, slot, vta)
            return qs, js

        trip = max(u for u in range(unroll, ATT_MAX_UNROLL + 1, 2) if count % u == 0)

        def body(_, carry):
            qv, jv = carry
            for u in range(trip):
                qv, jv = step(qv, jv, u % 2)
            return qv, jv

        k0 = k_ref[rows(first_pair[1]), :]
        for ci in range(len(chains)):
            score_chain(ci, first_pair[0], first_pair[1], 0, k0, biased)
        lax.fori_loop(0, count // trip, body, (jnp.int32(first_pair[0]), jnp.int32(first_pair[1])))

    def next_near(qv, jv):
        wrap = jv == qv
        qn = qv + 1
        return jnp.where(wrap, qn, qv), jnp.where(wrap, jnp.where(qn < q_split, 0, qn - 1), jv + 1)

    def next_far(qv, jv):
        wrap = jv == qv - ATT_FAR_TILE
        return jnp.where(wrap, qv + 1, qv), jnp.where(wrap, 0, jv + 1)

    n_near = q_split * (q_split + 1) // 2 + ATT_FAR_TILE * (nq - q_split)
    n_far = nq * (nq + 1) // 2 - n_near
    lax.fori_loop(0, nq, prepare, 0)
    sweep((0, 0), next_near, (nq - 1, nq - 1), n_near, True)
    if n_far:
        sweep((q_split, 0), next_far, (nq - 1, nq - 1 - ATT_FAR_TILE), n_far, False)
    lax.fori_loop(0, nq // 2, finalize, 0)


def _attn_plan(nq):
    for unroll in (8, 4, 2):
        for q_split in range(ATT_FAR_TILE, nq + 1):
            n_near = q_split * (q_split + 1) // 2 + ATT_FAR_TILE * (nq - q_split)
            n_far = nq * (nq + 1) // 2 - n_near
            if n_near % unroll == 0 and n_far % unroll == 0:
                return unroll, q_split
    raise ValueError(f"unsupported number of query blocks: {nq}")


def _attn_call(qt, k, vt, tiles, lam, subln_w, *, batch, lam_init, t=ATT_T):
    T, W = k.shape
    S = T // batch
    H = W // LANES
    nq = S // t
    e = 2 * DIFF_HEAD_DIM
    assert e == LANES and S % t == 0 and nq % 2 == 0
    unroll, q_split = _attn_plan(nq)
    n_chains = 2 * (t // ATT_STRIP)
    n_tiles = ATT_FAR_TILE + 1
    nbytes = (2 * 4 * (S * LANES * 2) + 2 * n_tiles * t * t * 4 + 2 * S * LANES * 2
              + nq * 2 * (e + ATT_ONES + SUBLANES) * t * 4 + 2 * n_chains * t * ATT_STRIP * 4 + 2 * t * t * 4)
    return pl.pallas_call(
        functools.partial(_attn_kernel, t=t, nq=nq, unroll=unroll, q_split=q_split, lam_init=lam_init),
        grid=(batch, H),
        in_specs=[
            pl.BlockSpec((LANES, S), lambda b, h: (h, b)),
            pl.BlockSpec((S, LANES), lambda b, h: (b, h)),
            pl.BlockSpec((LANES, S), lambda b, h: (h, b)),
            pl.BlockSpec((None, n_tiles, t, t), lambda b, h: (h, 0, 0, 0)),
            pl.BlockSpec((SUBLANES, LANES), lambda b, h: (0, 0)),
            pl.BlockSpec((e, 1), lambda b, h: (0, 0)),
        ],
        out_specs=pl.BlockSpec((S, LANES), lambda b, h: (b, h)),
        out_shape=jax.ShapeDtypeStruct((T, W), BF16),
        scratch_shapes=[pltpu.VMEM((2, LANES, S), BF16),
                        pltpu.VMEM((nq, 2, 1, t), F32), pltpu.VMEM((nq, 2, e + ATT_ONES, t), F32),
                        pltpu.VMEM((2, n_chains, t, ATT_STRIP), F32),
                        pltpu.VMEM((2, n_chains, 1, ATT_STRIP), F32)],
        compiler_params=pltpu.CompilerParams(
            dimension_semantics=("arbitrary", "arbitrary"), vmem_limit_bytes=_vmem_limit(nbytes)),
        name="diff_attn",
    )(qt, k, vt, tiles, lam, subln_w.reshape(e, 1))


def _split3(a):
    hi = a.astype(BF16)
    r1 = a - hi.astype(F32)
    mid = r1.astype(BF16)
    lo = (r1 - mid.astype(F32)).astype(BF16)
    return hi, mid, lo


def _hyb_kernel(x_ref, nw_ref, w_ref, wvt_ref, cw_ref, cb_ref, dtb_ref, alog_ref, dsk_ref, ynw_ref,
                qt_ref, k_ref, vt_ref, y_ref, xpad_ref, z_scr, dt_scr, st_ref, *, tm, tiles_per_seq):
    i = pl.program_id(0)
    L = SSM_CHUNK
    gw = HEADS_PER_GROUP * SSM_HEAD_DIM
    qw = DIFF_WIDTH
    c_z = 2 * qw + DIFF_WIDTH
    c_xbc = c_z + SSM_WIDTH
    c_dt = c_xbc + SSM_XBC

    @pl.when(i == 0)
    def _():
        xpad_ref[...] = jnp.zeros(xpad_ref.shape, F32)
        z_scr[...] = jnp.zeros(z_scr.shape, F32)
        dt_scr[...] = jnp.zeros(dt_scr.shape, F32)
        st_ref[...] = jnp.zeros(st_ref.shape, F32)

    def stage(cur, prev):
        row = lax.broadcasted_iota(jnp.int32, (L, L), 0)
        col = lax.broadcasted_iota(jnp.int32, (L, L), 1)
        causal = row >= col
        tri = jnp.where(causal, 1.0, 0.0).astype(BF16)
        lane_g = lax.broadcasted_iota(jnp.int32, (L, gw), 1) // SSM_HEAD_DIM
        lane_lo = lax.broadcasted_iota(jnp.int32, (L, LANES), 1) < SSM_HEAD_DIM
        lane_lo1 = lax.broadcasted_iota(jnp.int32, (1, LANES), 1) < SSM_HEAD_DIM
        a_neg = -jnp.exp(alog_ref[...])
        keep_state = jnp.where((i - 1) % tiles_per_seq == 0, 0.0, 1.0)

        def ssd_chunk(c):
            r0 = c * L
            dt = dt_scr[prev, r0:r0 + L, :]
            hi, mid, lo = _split3(a_neg * dt)
            cs = _dot(tri, hi) + _dot(tri, mid) + _dot(tri, lo)
            yield

            conv = cb_ref[...]
            for kk in range(SSM_CONV):
                o = HALO - (SSM_CONV - 1) + kk + r0
                conv = conv + cw_ref[kk:kk + 1, :] * xpad_ref[prev, o:o + L, :]
            xc = _silu(conv)
            xs = xc[:, 0:SSM_WIDTH]
            bgs, cgs, gmats = [], [], []
            for g in range(SSM_GROUPS):
                b0 = SSM_WIDTH + g * SSM_STATE
                c0 = SSM_WIDTH + SSM_GROUPS * SSM_STATE + g * SSM_STATE
                bgs.append(xc[:, b0:b0 + SSM_STATE])
                cgs.append(xc[:, c0:c0 + SSM_STATE].astype(BF16))
                gmats.append(_dot_nt(cgs[g], bgs[g].astype(BF16)))
            yield

            cs_t = cs.T
            dt_t = dt.T
            ydiags, st_news, e_mats, decs_g = [], [], [], []
            for g in range(SSM_GROUPS):
                bg_t = bgs[g].T
                xg = xs[:, g * gw:(g + 1) * gw]
                ydiag = None
                st_new = None
                ecols, decs = [], []
                for hh in range(HEADS_PER_GROUP):
                    h = g * HEADS_PER_GROUP + hh
                    colb = jnp.broadcast_to(cs[:, h:h + 1], (L, L))
                    rowb = cs_t[h:h + 1, :]
                    dtrow = dt_t[h:h + 1, :]
                    decay = jnp.exp(jnp.where(causal, colb - rowb, NEG_BIG))
                    mh = (gmats[g] * decay * dtrow).astype(BF16)
                    xm = jnp.where(lane_g == hh, xg, 0.0).astype(BF16)
                    last = colb[L - 1:L, :]
                    wrow = dtrow * jnp.exp(last - rowb)
                    btw = (bg_t * wrow).astype(BF16)
                    d1 = _dot(mh, xm)
                    d2 = _dot(btw, xm)
                    ydiag = d1 if ydiag is None else ydiag + d1
                    st_new = d2 if st_new is None else st_new + d2
                    ecols.append(jnp.exp(colb))
                    decs.append(jnp.exp(last))
                ydiags.append(ydiag)
                st_news.append(st_new)
                e_mats.append(jnp.concatenate([jnp.where(lane_lo, ecols[0], ecols[1]),
                                               jnp.where(lane_lo, ecols[2], ecols[3])], axis=1))
                decs_g.append(jnp.concatenate([jnp.where(lane_lo1, decs[0], decs[1]),
                                               jnp.where(lane_lo1, decs[2], decs[3])], axis=1))
                if g == 0:
                    yield

            ys = []
            for g in range(SSM_GROUPS):
                st_old = st_ref[g] * keep_state if c == 0 else st_ref[g]
                ys.append(ydiags[g] + _dot(cgs[g], st_old.astype(BF16)) * e_mats[g])
                st_ref[g] = st_old * decs_g[g] + st_news[g]
            y = jnp.concatenate(ys, axis=1) + dsk_ref[...] * xs
            gated = y * z_scr[prev, r0:r0 + L, :]
            outs = []
            for g in range(SSM_GROUPS):
                part = gated[:, g * gw:(g + 1) * gw]
                ms = jnp.mean(part * part, axis=-1, keepdims=True)
                outs.append(part * lax.rsqrt(ms + 1e-5))
            y_ref[r0:r0 + L, :] = (jnp.concatenate(outs, axis=1) * ynw_ref[...]).astype(BF16)
            yield

        h = _rms(x_ref[...], nw_ref[...], 1e-6).astype(BF16)
        scale = DIFF_HEAD_DIM ** -0.5 * LOG2E
        piece = 2 * LANES

        def pieces():
            for r in range(0, qw, piece):
                qt_ref[r:r + piece, :] = (_dot_nt(wvt_ref[r:r + piece, :], h) * scale).astype(BF16)
                yield
            for c in range(0, qw, piece):
                k_ref[:, c:c + piece] = _dot(h, w_ref[:, qw + c:qw + c + piece]).astype(BF16)
                yield
            for r in range(0, DIFF_WIDTH, piece):
                vt_ref[r:r + piece, :] = _dot_nt(wvt_ref[qw + r:qw + r + piece, :], h).astype(BF16)
                yield
            for c in range(0, SSM_WIDTH, piece):
                z_scr[cur, :, c:c + piece] = _silu(_dot(h, w_ref[:, c_z + c:c_z + c + piece]))
                yield
            dt_raw = _dot(h, w_ref[:, c_dt:c_dt + LANES]) + dtb_ref[...]
            dt_scr[cur] = jnp.maximum(dt_raw, 0.0) + jnp.log1p(jnp.exp(-jnp.abs(dt_raw)))
            yield
            tail = xpad_ref[prev, tm:tm + HALO, :]
            xpad_ref[cur, 0:HALO, :] = jnp.where(i % tiles_per_seq == 0, jnp.zeros_like(tail), tail)
            for c in range(0, SSM_XBC, piece):
                xpad_ref[cur, HALO:HALO + tm, c:c + piece] = _dot(h, w_ref[:, c_xbc + c:c_xbc + c + piece])
                yield

        pieces_after_group = (0, 2, 1, 0)
        proj = pieces()
        for c in range(tm // L):
            for group, _ in enumerate(ssd_chunk(c)):
                for _ in range(pieces_after_group[group]):
                    next(proj, None)
        for _ in proj:
            pass

    pl.when(i % 2 == 0)(functools.partial(stage, 0, 1))
    pl.when(i % 2 == 1)(functools.partial(stage, 1, 0))


def _hyb_call(x, nw, w_pad, wvt, conv_w, conv_b, dt_bias, a_log, d_skip, norm_w, *, batch, tm=PROJ_TM):
    T, D = x.shape
    S = T // batch
    assert S % tm == 0 and tm % SSM_CHUNK == 0
    nt = T // tm
    cur = lambda i: (jnp.minimum(i, nt - 1), 0)
    cur_t = lambda i: (0, jnp.minimum(i, nt - 1))
    prev = lambda i: (jnp.maximum(i - 1, 0), 0)
    pad = lambda v: jnp.pad(v.astype(F32), (0, LANES - v.shape[0])).reshape(1, LANES)
    dsk = jnp.repeat(d_skip.astype(F32), SSM_HEAD_DIM).reshape(1, SSM_WIDTH)
    outs = [
        jax.ShapeDtypeStruct((DIFF_WIDTH, T), BF16),
        jax.ShapeDtypeStruct((T, DIFF_WIDTH), BF16),
        jax.ShapeDtypeStruct((DIFF_WIDTH, T), BF16),
        jax.ShapeDtypeStruct((T, SSM_WIDTH), BF16),
    ]
    out_specs = [pl.BlockSpec((DIFF_WIDTH, tm), cur_t), pl.BlockSpec((tm, DIFF_WIDTH), cur),
                 pl.BlockSpec((DIFF_WIDTH, tm), cur_t), pl.BlockSpec((tm, SSM_WIDTH), prev)]
    nbytes = ((w_pad.size + wvt.size) * 2 + 2 * tm * D * 4 + 2 * tm * 2 * 4 * DIFF_WIDTH
              + 2 * (tm + HALO) * SSM_XBC * 4 + 2 * tm * (SSM_WIDTH + LANES) * 4 + 16 * tm * SSM_XBC * 4)
    return pl.pallas_call(
        functools.partial(_hyb_kernel, tm=tm, tiles_per_seq=S // tm),
        grid=(nt + 1,),
        in_specs=[pl.BlockSpec((tm, D), cur), _resident((1, D)), _resident(w_pad.shape),
                  _resident(wvt.shape), _resident((SSM_CONV, SSM_XBC)), _resident((1, SSM_XBC)),
                  _resident((1, LANES)), _resident((1, LANES)), _resident((1, SSM_WIDTH)),
                  _resident((1, SSM_WIDTH))],
        out_specs=out_specs,
        out_shape=outs,
        scratch_shapes=[pltpu.VMEM((2, tm + HALO, SSM_XBC), F32), pltpu.VMEM((2, tm, SSM_WIDTH), F32),
                        pltpu.VMEM((2, tm, LANES), F32),
                        pltpu.VMEM((SSM_GROUPS, SSM_STATE, HEADS_PER_GROUP * SSM_HEAD_DIM), F32)],
        compiler_params=pltpu.CompilerParams(
            dimension_semantics=("arbitrary",), vmem_limit_bytes=_vmem_limit(nbytes)),
        name="hyb_ssd",
    )(x, nw.reshape(1, D), w_pad, wvt, conv_w.astype(F32), conv_b.reshape(1, SSM_XBC).astype(F32),
      pad(dt_bias), pad(a_log), dsk, norm_w.reshape(1, SSM_WIDTH).astype(F32))


def _sc_kernel(x_ref, xprev_ref, nw_ref, win_ref, cw_ref, wout_ref, o_ref, vpad_ref, bg_ref, *,
               tm, tiles_per_seq):
    i = pl.program_id(0)
    d = x_ref.shape[1]

    @pl.when(i == 0)
    def _():
        vpad_ref[...] = jnp.zeros(vpad_ref.shape, F32)
        bg_ref[...] = jnp.zeros(bg_ref.shape, F32)

    def finish_chunk(prev, c0, c1):
        conv = cw_ref[SC_CONV - 1:SC_CONV, c0:c1] * vpad_ref[prev, HALO:HALO + tm, c0:c1]
        for kk in range(SC_CONV - 1):
            o = HALO - (SC_CONV - 1) + kk
            conv = conv + cw_ref[kk:kk + 1, c0:c1] * vpad_ref[prev, o:o + tm, c0:c1]
        return _dot((bg_ref[prev, :, c0:c1] * conv).astype(BF16), wout_ref[c0:c1, :].astype(BF16))

    def stage(cur, prev):
        cw = 2 * LANES
        chunks = [(c0, c0 + cw) for c0 in range(0, d, cw)]
        x = x_ref[...]
        h = (x * nw_ref[...]).astype(BF16)
        r = lax.rsqrt(jnp.mean(x * x, axis=-1, keepdims=True) + 1e-6)
        bgate = _dot(h, win_ref[:, 0:d].astype(BF16)) * r
        out = finish_chunk(prev, *chunks[0])
        cgate = _dot(h, win_ref[:, d:2 * d].astype(BF16)) * r
        out = out + finish_chunk(prev, *chunks[1])
        u = _dot(h, win_ref[:, 2 * d:3 * d].astype(BF16)) * r
        for c0, c1 in chunks[2:]:
            out = out + finish_chunk(prev, c0, c1)
        o_ref[...] = xprev_ref[...] + out

        tail = vpad_ref[prev, tm:tm + HALO, :]
        vpad_ref[cur, 0:HALO, :] = jnp.where(i % tiles_per_seq == 0, jnp.zeros_like(tail), tail)
        vpad_ref[cur, HALO:HALO + tm, :] = cgate * u
        bg_ref[cur] = bgate

    pl.when(i % 2 == 0)(functools.partial(stage, 0, 1))
    pl.when(i % 2 == 1)(functools.partial(stage, 1, 0))


def _sc_call(x, nw, w_in, conv_w, w_out, layer, *, batch, tm=PROJ_TM):
    T, D = x.shape
    S = T // batch
    assert S % tm == 0
    nt = T // tm
    cur = lambda i: (jnp.minimum(i, nt - 1), 0)
    prev = lambda i: (jnp.maximum(i - 1, 0), 0)
    nbytes = (w_in[layer].size + w_out[layer].size) * 4 + 6 * tm * D * 4 + 4 * tm * D * 4 + 8 * tm * D * 4
    return pl.pallas_call(
        functools.partial(_sc_kernel, tm=tm, tiles_per_seq=S // tm),
        grid=(nt + 1,),
        in_specs=[pl.BlockSpec((tm, D), cur), pl.BlockSpec((tm, D), prev), _resident((1, D)),
                  _layer_resident(w_in, layer), _resident(conv_w.shape), _layer_resident(w_out, layer)],
        out_specs=pl.BlockSpec((tm, D), prev),
        out_shape=jax.ShapeDtypeStruct((T, D), F32),
        scratch_shapes=[pltpu.VMEM((2, tm + HALO, D), F32), pltpu.VMEM((2, tm, D), F32)],
        compiler_params=pltpu.CompilerParams(
            dimension_semantics=("arbitrary",), vmem_limit_bytes=_vmem_limit(nbytes)),
        name="short_conv",
    )(x, x, nw.reshape(1, D), w_in, conv_w.astype(F32), w_out)


def kernel(x, rel_bias, final_norm_w, ffn1_norm, ffn1_wg, ffn1_wu, ffn1_wd, mix_norm, ffn2_norm, ffn2_wg, ffn2_wu, ffn2_wd, hyb_w_in, hyb_w_out, diff_lq1, diff_lk1, diff_lq2, diff_lk2, diff_subln_w, ssm_conv_w, ssm_conv_b, ssm_dt_bias, ssm_a_log, ssm_d, ssm_norm_w, sc_w_in, sc_conv_w, sc_w_out):
    B, S, D = x.shape
    T = B * S
    xt = x.reshape(T, D)
    bf = lambda w: w.astype(BF16)

    lam_inits = [0.8 - 0.6 * math.exp(-0.3 * i) for i in range(0, DEPTH, 2)]
    tiles, lams = _prep_call(rel_bias, diff_lq1, diff_lk1, diff_lq2, diff_lk2, lam_inits)

    mix = None
    for i in range(DEPTH):
        j = i // 2
        xt = _ffn_call(xt, ffn1_norm[i], ffn1_wg, ffn1_wu, ffn1_wd, i)
        if i % 2 == 0:
            w_in = hyb_w_in[j]
            qkv_end = 3 * DIFF_WIDTH
            w_pad = bf(jnp.pad(w_in, ((0, 0), (0, LANES - SSM_HEADS))))
            wvt = bf(jnp.concatenate([w_in[:, 0:DIFF_WIDTH], w_in[:, 2 * DIFF_WIDTH:qkv_end]], axis=1).T)
            q, k, vt, y = _hyb_call(xt, mix_norm[i], w_pad, wvt, ssm_conv_w[j], ssm_conv_b[j],
                                    ssm_dt_bias[j], ssm_a_log[j], ssm_d[j], ssm_norm_w[j], batch=B)
            o = _attn_call(q, k, vt, tiles, lams[j * SUBLANES:(j + 1) * SUBLANES], diff_subln_w[j],
                           batch=B, lam_init=lam_inits[j])
            mix = (o, y, hyb_w_out, j)
        else:
            xt = _sc_call(xt, mix_norm[i], sc_w_in, sc_conv_w[j], sc_w_out, j, batch=B)
            mix = None
        xt = _ffn_call(xt, ffn2_norm[i], ffn2_wg, ffn2_wu, ffn2_wd, i, mix=mix,
                       final_w=final_norm_w if i == DEPTH - 1 else None)
    return xt.reshape(B, S, D)
```

```python
import functools
import math

import jax
import jax.numpy as jnp
from jax import lax
from jax.experimental import pallas as pl
from jax.experimental.pallas import tpu as pltpu

F32 = jnp.float32
BF16 = jnp.bfloat16

DEPTH = 4
N_DIFF_HEADS = 4
DIFF_HEAD_DIM = 64
DIFF_WIDTH = N_DIFF_HEADS * 2 * DIFF_HEAD_DIM
NUM_BUCKETS = 32
MAX_DISTANCE = 128
SSM_HEADS = 8
SSM_HEAD_DIM = 64
SSM_WIDTH = SSM_HEADS * SSM_HEAD_DIM
SSM_GROUPS = 2
SSM_STATE = 128
SSM_CONV = 4
SSM_CHUNK = 128
HEADS_PER_GROUP = SSM_HEADS // SSM_GROUPS
SSM_XBC = SSM_WIDTH + 2 * SSM_GROUPS * SSM_STATE
SC_CONV = 3

LANES = 128
SUBLANES = 8
V7X_SCOPED_VMEM_CAP = 60000 * 1024

FFN_TM = 512
FFN_FCHUNK = 256
PROJ_TM = 512
ATT_T = 512
ATT_STRIP = 256
ATT_ONES = 16
ATT_MAX_UNROLL = 26
ATT_FAR_TILE = 2
HALO = SUBLANES

NEG_BIG = -1e30
LOG2E = math.log2(math.e)


def _vmem_limit(nbytes):
    return int(min(V7X_SCOPED_VMEM_CAP, nbytes * 5 // 4 + (8 << 20)))


def _dot(a, b):
    return jnp.dot(a, b, preferred_element_type=F32)


def _dot_nt(a, b):
    return lax.dot_general(a, b, (((1,), (1,)), ((), ())), preferred_element_type=F32)


def _rms(x, w, eps):
    ms = jnp.mean(x * x, axis=-1, keepdims=True)
    return x * lax.rsqrt(ms + eps) * w


def _silu(x):
    h = 0.5 * x
    return h + h * jnp.tanh(h)


def _resident(shape):
    nd = len(shape)
    return pl.BlockSpec(shape, lambda *_: (0,) * nd, pipeline_mode=pl.Buffered(1))


def _ffn_kernel(*refs, pre_mix, final_norm, f_total):
    refs = list(refs)
    x_ref = refs.pop(0)
    if pre_mix:
        o_ref, y_ref, wo_ref = refs.pop(0), refs.pop(0), refs.pop(0)
    nw_ref, wg_ref, wu_ref, wd_ref = refs.pop(0), refs.pop(0), refs.pop(0), refs.pop(0)
    if final_norm:
        fw_ref = refs.pop(0)
    out_ref = refs.pop(0)

    x = x_ref[...]
    if pre_mix:
        half = o_ref.shape[1]
        x = (x + _dot(o_ref[...], wo_ref[0:half, :].astype(BF16))
             + _dot(y_ref[...], wo_ref[half:, :].astype(BF16)))
    h = (x * nw_ref[...]).astype(BF16)
    r = lax.rsqrt(jnp.mean(x * x, axis=-1, keepdims=True) + 1e-6)
    acts = []
    for c0 in range(0, f_total, FFN_FCHUNK):
        c1 = c0 + FFN_FCHUNK
        g = _dot(h, wg_ref[:, c0:c1].astype(BF16)) * r
        u = _dot(h, wu_ref[:, c0:c1].astype(BF16)) * r
        acts.append((_silu(g) * u).astype(BF16))
    acc = _dot(jnp.concatenate(acts, axis=1), wd_ref[...].astype(BF16))
    y = x + 0.5 * acc
    if final_norm:
        y = _rms(y, fw_ref[...], 1e-6)
    out_ref[...] = y


def _layer_resident(stacked, layer):
    _, r, c = stacked.shape
    return pl.BlockSpec((None, r, c), lambda *_: (layer, 0, 0), pipeline_mode=pl.Buffered(1))


def _ffn_call(x, nw, wg, wu, wd, layer, *, mix=None, final_w=None, tm=FFN_TM):
    T, D = x.shape
    F = wg.shape[2]
    assert T % tm == 0 and F % FFN_FCHUNK == 0
    row = lambda i: (i, 0)
    args = [x]
    in_specs = [pl.BlockSpec((tm, D), row)]
    nbytes = 4 * tm * D * 4 + 3 * D * F * 4 + 6 * tm * FFN_FCHUNK * 4 + 3 * tm * D * 4
    if mix is not None:
        o, y, wo, mix_layer = mix
        args += [o, y, wo]
        in_specs += [pl.BlockSpec((tm, o.shape[1]), row), pl.BlockSpec((tm, y.shape[1]), row),
                     _layer_resident(wo, mix_layer)]
        nbytes += wo.shape[1] * wo.shape[2] * 4 + 4 * tm * o.shape[1] * 2 * 2
    args += [nw.reshape(1, D), wg, wu, wd]
    in_specs += [_resident((1, D)), _layer_resident(wg, layer), _layer_resident(wu, layer),
                 _layer_resident(wd, layer)]
    if final_w is not None:
        args.append(final_w.reshape(1, D))
        in_specs.append(_resident((1, D)))
    kern = functools.partial(_ffn_kernel, pre_mix=mix is not None, final_norm=final_w is not None,
                             f_total=F)
    return pl.pallas_call(
        kern,
        grid=(T // tm,),
        in_specs=in_specs,
        out_specs=pl.BlockSpec((tm, D), row),
        out_shape=jax.ShapeDtypeStruct((T, D), F32),
        compiler_params=pltpu.CompilerParams(
            dimension_semantics=("arbitrary",), vmem_limit_bytes=_vmem_limit(nbytes)),
        name="ffn",
    )(*args)


def _prep_kernel(tab_ref, lq1_ref, lk1_ref, lq2_ref, lk2_ref, tile_ref, lam_ref, *, t, lam_inits):
    hd = pl.program_id(0)
    key = lax.broadcasted_iota(jnp.int32, (t, t), 0)
    qry = lax.broadcasted_iota(jnp.int32, (t, t), 1)
    max_exact = NUM_BUCKETS // 2
    far = tab_ref[hd, NUM_BUCKETS - 1]
    tile_ref[ATT_FAR_TILE] = jnp.zeros((t, t), F32)
    for off in range(ATT_FAR_TILE):
        dist = off * t + qry - key
        d = jnp.maximum(dist, 0)
        large = max_exact + (
            jnp.log(jnp.maximum(d, 1).astype(F32) / max_exact)
            / math.log(MAX_DISTANCE / max_exact) * (NUM_BUCKETS - max_exact)).astype(jnp.int32)
        bucket = jnp.where(d < max_exact, d, jnp.minimum(large, NUM_BUCKETS - 1))
        bias = jnp.full((t, t), far, F32)
        for b in range(NUM_BUCKETS - 1):
            bias = jnp.where(bucket == b, tab_ref[hd, b], bias)
        val = (bias - far) * LOG2E
        if off == 0:
            val = jnp.where(dist >= 0, val, NEG_BIG)
        tile_ref[off] = val
    for j, lam_init in enumerate(lam_inits):
        s1 = jnp.sum(lq1_ref[j:j + 1, :] * lk1_ref[j:j + 1, :], axis=-1, keepdims=True)
        s2 = jnp.sum(lq2_ref[j:j + 1, :] * lk2_ref[j:j + 1, :], axis=-1, keepdims=True)
        lam = jnp.exp(s1) - jnp.exp(s2) + lam_init
        lam_ref[j * SUBLANES:(j + 1) * SUBLANES, :] = jnp.broadcast_to(lam, (SUBLANES, LANES))


def _prep_call(rel_bias, lq1, lk1, lq2, lk2, lam_inits, *, t=ATT_T):
    ne = lq1.shape[0]
    H = rel_bias.shape[1]
    whole = lambda shape: pl.BlockSpec(shape, lambda h: (0,) * len(shape))
    return pl.pallas_call(
        functools.partial(_prep_kernel, t=t, lam_inits=tuple(lam_inits)),
        grid=(H,),
        in_specs=[pl.BlockSpec(memory_space=pltpu.SMEM)] + [whole(lq1.shape)] * 4,
        out_specs=[pl.BlockSpec((None, ATT_FAR_TILE + 1, t, t), lambda h: (h, 0, 0, 0)),
                   whole((ne * SUBLANES, LANES))],
        out_shape=[jax.ShapeDtypeStruct((H, ATT_FAR_TILE + 1, t, t), F32),
                   jax.ShapeDtypeStruct((ne * SUBLANES, LANES), F32)],
        compiler_params=pltpu.CompilerParams(
            dimension_semantics=("arbitrary",), vmem_limit_bytes=_vmem_limit(12 * t * t * 4)),
        name="attn_prep",
    )(rel_bias.T.astype(F32), lq1, lk1, lq2, lk2)


def _attn_kernel(qt_ref, k_ref, vt_ref, tile_ref, lam_ref, sw_ref, o_ref,
                 qm_ref, m_ref, acc_ref, s_ref, smax_ref, *, t, nq, unroll, q_split, lam_init):
    e = 2 * DIFF_HEAD_DIM
    first = lax.broadcasted_iota(jnp.int32, (LANES, t), 0) < DIFF_HEAD_DIM
    ones = jnp.ones((ATT_ONES, t), BF16)
    chains = [(w, slice(c0, c0 + ATT_STRIP)) for c0 in range(0, t, ATT_STRIP) for w in range(2)]

    def rows(blk):
        return pl.ds(pl.multiple_of(blk * t, t), t)

    def prepare(qi, carry):
        q = qt_ref[:, rows(qi)]
        zero = jnp.zeros_like(q)
        qm_ref[0, :, rows(qi)] = jnp.where(first, q, zero)
        qm_ref[1, :, rows(qi)] = jnp.where(first, zero, q)
        m_ref[qi] = jnp.full(m_ref.shape[1:], NEG_BIG, F32)
        acc_ref[qi] = jnp.zeros(acc_ref.shape[1:], F32)
        return carry

    def score_chain(ci, qi, j, slot, k, biased):
        w, cols = chains[ci]
        q_cols = pl.ds(pl.multiple_of(qi * t + cols.start, ATT_STRIP), ATT_STRIP)
        s = _dot(k, qm_ref[w, :, q_cols])
        if biased:
            near = jnp.minimum(qi - j, ATT_FAR_TILE)
            s = s + tile_ref[near, :, cols]
        s_ref[slot, ci] = s
        smax_ref[slot, ci] = jnp.max(s, axis=0, keepdims=True)

    def value_chain(ci, qi, slot, vta):
        w, cols = chains[ci]
        m_old = m_ref[qi, w, :, cols]
        m_new = jnp.maximum(m_old, smax_ref[slot, ci])
        p = jnp.exp2(s_ref[slot, ci] - m_new).astype(BF16)
        acc_ref[qi, w, :, cols] = jnp.exp2(m_old - m_new) * acc_ref[qi, w, :, cols] + _dot(vta, p)
        m_ref[qi, w, :, cols] = m_new

    def finalize(pair, carry):
        for qi in (2 * pair, 2 * pair + 1):
            lam = lam_ref[0:1, 0:1]
            o1 = acc_ref[qi, 0, 0:e, :] * (1.0 / acc_ref[qi, 0, e:e + 1, :])
            o2 = acc_ref[qi, 1, 0:e, :] * (1.0 / acc_ref[qi, 1, e:e + 1, :])
            out_t = o1 - lam * o2
            ms = jnp.mean(out_t * out_t, axis=0, keepdims=True)
            out_t = out_t * lax.rsqrt(ms + 1e-5) * sw_ref[...] * (1.0 - lam_init)
            o_ref[rows(qi), :] = out_t.T.astype(BF16)
        return carry

    def sweep(first_pair, next_pair, last_pair, count, biased):
        def step(qv, jv, slot):
            qs, js = next_pair(qv, jv)
            past_end = qs >= nq
            qs = jnp.where(past_end, last_pair[0], qs)
            js = jnp.where(past_end, last_pair[1], js)
            k_next = k_ref[rows(js), :]
            vta = jnp.concatenate([vt_ref[:, rows(jv)], ones], axis=0)
            for ci in range(len(chains)):
                score_chain(ci, qs, js, 1 - slot, k_next, biased)
                value_chain(ci, qv, slot, vta)
            return qs, js

        trip = max(u for u in range(unroll, ATT_MAX_UNROLL + 1, 2) if count % u == 0)

        def body(_, carry):
            qv, jv = carry
            for u in range(trip):
                qv, jv = step(qv, jv, u % 2)
            return qv, jv

        k0 = k_ref[rows(first_pair[1]), :]
        for ci in range(len(chains)):
            score_chain(ci, first_pair[0], first_pair[1], 0, k0, biased)
        lax.fori_loop(0, count // trip, body, (jnp.int32(first_pair[0]), jnp.int32(first_pair[1])))

    def next_near(qv, jv):
        wrap = jv == qv
        qn = qv + 1
        return jnp.where(wrap, qn, qv), jnp.where(wrap, jnp.where(qn < q_split, 0, qn - 1), jv + 1)

    def next_far(qv, jv):
        wrap = jv == qv - ATT_FAR_TILE
        return jnp.where(wrap, qv + 1, qv), jnp.where(wrap, 0, jv + 1)

    n_near = q_split * (q_split + 1) // 2 + ATT_FAR_TILE * (nq - q_split)
    n_far = nq * (nq + 1) // 2 - n_near
    lax.fori_loop(0, nq, prepare, 0)
    sweep((0, 0), next_near, (nq - 1, nq - 1), n_near, True)
    if n_far:
        sweep((q_split, 0), next_far, (nq - 1, nq - 1 - ATT_FAR_TILE), n_far, False)
    lax.fori_loop(0, nq // 2, finalize, 0)


def _attn_plan(nq):
    for unroll in (8, 4, 2):
        for q_split in range(ATT_FAR_TILE, nq + 1):
            n_near = q_split * (q_split + 1) // 2 + ATT_FAR_TILE * (nq - q_split)
            n_far = nq * (nq + 1) // 2 - n_near
            if n_near % unroll == 0 and n_far % unroll == 0:
                return unroll, q_split
    raise ValueError(f"unsupported number of query blocks: {nq}")


def _attn_call(qt, k, vt, tiles, lam, subln_w, *, batch, lam_init, t=ATT_T):
    T, W = k.shape
    S = T // batch
    H = W // LANES
    nq = S // t
    e = 2 * DIFF_HEAD_DIM
    assert e == LANES and S % t == 0 and nq % 2 == 0
    unroll, q_split = _attn_plan(nq)
    n_chains = 2 * (t // ATT_STRIP)
    n_tiles = ATT_FAR_TILE + 1
    nbytes = (2 * 4 * (S * LANES * 2) + 2 * n_tiles * t * t * 4 + 2 * S * LANES * 2
              + nq * 2 * (e + ATT_ONES + SUBLANES) * t * 4 + 2 * n_chains * t * ATT_STRIP * 4 + 2 * t * t * 4)
    return pl.pallas_call(
        functools.partial(_attn_kernel, t=t, nq=nq, unroll=unroll, q_split=q_split, lam_init=lam_init),
        grid=(batch, H),
        in_specs=[
            pl.BlockSpec((LANES, S), lambda b, h: (h, b)),
            pl.BlockSpec((S, LANES), lambda b, h: (b, h)),
            pl.BlockSpec((LANES, S), lambda b, h: (h, b)),
            pl.BlockSpec((None, n_tiles, t, t), lambda b, h: (h, 0, 0, 0)),
            pl.BlockSpec((SUBLANES, LANES), lambda b, h: (0, 0)),
            pl.BlockSpec((e, 1), lambda b, h: (0, 0)),
        ],
        out_specs=pl.BlockSpec((S, LANES), lambda b, h: (b, h)),
        out_shape=jax.ShapeDtypeStruct((T, W), BF16),
        scratch_shapes=[pltpu.VMEM((2, LANES, S), BF16),
                        pltpu.VMEM((nq, 2, 1, t), F32), pltpu.VMEM((nq, 2, e + ATT_ONES, t), F32),
                        pltpu.VMEM((2, n_chains, t, ATT_STRIP), F32),
                        pltpu.VMEM((2, n_chains, 1, ATT_STRIP), F32)],
        compiler_params=pltpu.CompilerParams(
            dimension_semantics=("arbitrary", "arbitrary"), vmem_limit_bytes=_vmem_limit(nbytes)),
        name="diff_attn",
    )(qt, k, vt, tiles, lam, subln_w.reshape(e, 1))


def _split3(a):
    hi = a.astype(BF16)
    r1 = a - hi.astype(F32)
    mid = r1.astype(BF16)
    lo = (r1 - mid.astype(F32)).astype(BF16)
    return hi, mid, lo


def _hyb_kernel(x_ref, nw_ref, w_ref, wvt_ref, cw_ref, cb_ref, dtb_ref, alog_ref, dsk_ref, ynw_ref,
                qt_ref, k_ref, vt_ref, y_ref, xpad_ref, z_scr, dt_scr, st_ref, *, tm, tiles_per_seq):
    i = pl.program_id(0)
    L = SSM_CHUNK
    gw = HEADS_PER_GROUP * SSM_HEAD_DIM
    qw = DIFF_WIDTH
    c_z = 2 * qw + DIFF_WIDTH
    c_xbc = c_z + SSM_WIDTH
    c_dt = c_xbc + SSM_XBC

    @pl.when(i == 0)
    def _():
        xpad_ref[...] = jnp.zeros(xpad_ref.shape, F32)
        z_scr[...] = jnp.zeros(z_scr.shape, F32)
        dt_scr[...] = jnp.zeros(dt_scr.shape, F32)
        st_ref[...] = jnp.zeros(st_ref.shape, F32)

    def stage(cur, prev):
        row = lax.broadcasted_iota(jnp.int32, (L, L), 0)
        col = lax.broadcasted_iota(jnp.int32, (L, L), 1)
        causal = row >= col
        tri = jnp.where(causal, 1.0, 0.0).astype(BF16)
        lane_g = lax.broadcasted_iota(jnp.int32, (L, gw), 1) // SSM_HEAD_DIM
        lane_lo = lax.broadcasted_iota(jnp.int32, (L, LANES), 1) < SSM_HEAD_DIM
        lane_lo1 = lax.broadcasted_iota(jnp.int32, (1, LANES), 1) < SSM_HEAD_DIM
        a_neg = -jnp.exp(alog_ref[...])
        keep_state = jnp.where((i - 1) % tiles_per_seq == 0, 0.0, 1.0)

        def ssd_chunk(c):
            r0 = c * L
            dt = dt_scr[prev, r0:r0 + L, :]
            hi, mid, lo = _split3(a_neg * dt)
            cs = _dot(tri, hi) + _dot(tri, mid) + _dot(tri, lo)
            yield

            conv = cb_ref[...]
            for kk in range(SSM_CONV):
                o = HALO - (SSM_CONV - 1) + kk + r0
                conv = conv + cw_ref[kk:kk + 1, :] * xpad_ref[prev, o:o + L, :]
            xc = _silu(conv)
            xs = xc[:, 0:SSM_WIDTH]
            bgs, cgs, gmats = [], [], []
            for g in range(SSM_GROUPS):
                b0 = SSM_WIDTH + g * SSM_STATE
                c0 = SSM_WIDTH + SSM_GROUPS * SSM_STATE + g * SSM_STATE
                bgs.append(xc[:, b0:b0 + SSM_STATE])
                cgs.append(xc[:, c0:c0 + SSM_STATE].astype(BF16))
                gmats.append(_dot_nt(cgs[g], bgs[g].astype(BF16)))
            yield

            cs_t = cs.T
            dt_t = dt.T
            ydiags, st_news, e_mats, decs_g = [], [], [], []
            for g in range(SSM_GROUPS):
                bg_t = bgs[g].T
                xg = xs[:, g * gw:(g + 1) * gw]
                ydiag = None
                st_new = None
                ecols, decs = [], []
                for hh in range(HEADS_PER_GROUP):
                    h = g * HEADS_PER_GROUP + hh
                    colb = jnp.broadcast_to(cs[:, h:h + 1], (L, L))
                    rowb = cs_t[h:h + 1, :]
                    dtrow = dt_t[h:h + 1, :]
                    decay = jnp.exp(jnp.where(causal, colb - rowb, NEG_BIG))
                    mh = (gmats[g] * decay * dtrow).astype(BF16)
                    xm = jnp.where(lane_g == hh, xg, 0.0).astype(BF16)
                    last = colb[L - 1:L, :]
                    wrow = dtrow * jnp.exp(last - rowb)
                    btw = (bg_t * wrow).astype(BF16)
                    d1 = _dot(mh, xm)
                    d2 = _dot(btw, xm)
                    ydiag = d1 if ydiag is None else ydiag + d1
                    st_new = d2 if st_new is None else st_new + d2
                    ecols.append(jnp.exp(colb))
                    decs.append(jnp.exp(last))
                ydiags.append(ydiag)
                st_news.append(st_new)
                e_mats.append(jnp.concatenate([jnp.where(lane_lo, ecols[0], ecols[1]),
                                               jnp.where(lane_lo, ecols[2], ecols[3])], axis=1))
                decs_g.append(jnp.concatenate([jnp.where(lane_lo1, decs[0], decs[1]),
                                               jnp.where(lane_lo1, decs[2], decs[3])], axis=1))
                if g == 0:
                    yield

            ys = []
            for g in range(SSM_GROUPS):
                st_old = st_ref[g] * keep_state if c == 0 else st_ref[g]
                ys.append(ydiags[g] + _dot(cgs[g], st_old.astype(BF16)) * e_mats[g])
                st_ref[g] = st_old * decs_g[g] + st_news[g]
            y = jnp.concatenate(ys, axis=1) + dsk_ref[...] * xs
            gated = y * z_scr[prev, r0:r0 + L, :]
            outs = []
            for g in range(SSM_GROUPS):
                part = gated[:, g * gw:(g + 1) * gw]
                ms = jnp.mean(part * part, axis=-1, keepdims=True)
                outs.append(part * lax.rsqrt(ms + 1e-5))
            y_ref[r0:r0 + L, :] = (jnp.concatenate(outs, axis=1) * ynw_ref[...]).astype(BF16)
            yield

        h = _rms(x_ref[...], nw_ref[...], 1e-6).astype(BF16)
        scale = DIFF_HEAD_DIM ** -0.5 * LOG2E
        piece = 2 * LANES

        def pieces():
            for r in range(0, qw, piece):
                qt_ref[r:r + piece, :] = (_dot_nt(wvt_ref[r:r + piece, :], h) * scale).astype(BF16)
                yield
            for c in range(0, qw, piece):
                k_ref[:, c:c + piece] = _dot(h, w_ref[:, qw + c:qw + c + piece]).astype(BF16)
                yield
            for r in range(0, DIFF_WIDTH, piece):
                vt_ref[r:r + piece, :] = _dot_nt(wvt_ref[qw + r:qw + r + piece, :], h).astype(BF16)
                yield
            for c in range(0, SSM_WIDTH, piece):
                z_scr[cur, :, c:c + piece] = _silu(_dot(h, w_ref[:, c_z + c:c_z + c + piece]))
                yield
            dt_raw = _dot(h, w_ref[:, c_dt:c_dt + LANES]) + dtb_ref[...]
            dt_scr[cur] = jnp.maximum(dt_raw, 0.0) + jnp.log1p(jnp.exp(-jnp.abs(dt_raw)))
            yield
            tail = xpad_ref[prev, tm:tm + HALO, :]
            xpad_ref[cur, 0:HALO, :] = jnp.where(i % tiles_per_seq == 0, jnp.zeros_like(tail), tail)
            for c in range(0, SSM_XBC, piece):
                xpad_ref[cur, HALO:HALO + tm, c:c + piece] = _dot(h, w_ref[:, c_xbc + c:c_xbc + c + piece])
                yield

        pieces_after_group = (0, 2, 1, 0)
        proj = pieces()
        for c in range(tm // L):
            for group, _ in enumerate(ssd_chunk(c)):
                for _ in range(pieces_after_group[group]):
                    next(proj, None)
        for _ in proj:
            pass

    pl.when(i % 2 == 0)(functools.partial(stage, 0, 1))
    pl.when(i % 2 == 1)(functools.partial(stage, 1, 0))


def _hyb_call(x, nw, w_pad, wvt, conv_w, conv_b, dt_bias, a_log, d_skip, norm_w, *, batch, tm=PROJ_TM):
    T, D = x.shape
    S = T // batch
    assert S % tm == 0 and tm % SSM_CHUNK == 0
    nt = T // tm
    cur = lambda i: (jnp.minimum(i, nt - 1), 0)
    cur_t = lambda i: (0, jnp.minimum(i, nt - 1))
    prev = lambda i: (jnp.maximum(i - 1, 0), 0)
    pad = lambda v: jnp.pad(v.astype(F32), (0, LANES - v.shape[0])).reshape(1, LANES)
    dsk = jnp.repeat(d_skip.astype(F32), SSM_HEAD_DIM).reshape(1, SSM_WIDTH)
    outs = [
        jax.ShapeDtypeStruct((DIFF_WIDTH, T), BF16),
        jax.ShapeDtypeStruct((T, DIFF_WIDTH), BF16),
        jax.ShapeDtypeStruct((DIFF_WIDTH, T), BF16),
        jax.ShapeDtypeStruct((T, SSM_WIDTH), BF16),
    ]
    out_specs = [pl.BlockSpec((DIFF_WIDTH, tm), cur_t), pl.BlockSpec((tm, DIFF_WIDTH), cur),
                 pl.BlockSpec((DIFF_WIDTH, tm), cur_t), pl.BlockSpec((tm, SSM_WIDTH), prev)]
    nbytes = ((w_pad.size + wvt.size) * 2 + 2 * tm * D * 4 + 2 * tm * 2 * 4 * DIFF_WIDTH
              + 2 * (tm + HALO) * SSM_XBC * 4 + 2 * tm * (SSM_WIDTH + LANES) * 4 + 16 * tm * SSM_XBC * 4)
    return pl.pallas_call(
        functools.partial(_hyb_kernel, tm=tm, tiles_per_seq=S // tm),
        grid=(nt + 1,),
        in_specs=[pl.BlockSpec((tm, D), cur), _resident((1, D)), _resident(w_pad.shape),
                  _resident(wvt.shape), _resident((SSM_CONV, SSM_XBC)), _resident((1, SSM_XBC)),
                  _resident((1, LANES)), _resident((1, LANES)), _resident((1, SSM_WIDTH)),
                  _resident((1, SSM_WIDTH))],
        out_specs=out_specs,
        out_shape=outs,
        scratch_shapes=[pltpu.VMEM((2, tm + HALO, SSM_XBC), F32), pltpu.VMEM((2, tm, SSM_WIDTH), F32),
                        pltpu.VMEM((2, tm, LANES), F32),
                        pltpu.VMEM((SSM_GROUPS, SSM_STATE, HEADS_PER_GROUP * SSM_HEAD_DIM), F32)],
        compiler_params=pltpu.CompilerParams(
            dimension_semantics=("arbitrary",), vmem_limit_bytes=_vmem_limit(nbytes)),
        name="hyb_ssd",
    )(x, nw.reshape(1, D), w_pad, wvt, conv_w.astype(F32), conv_b.reshape(1, SSM_XBC).astype(F32),
      pad(dt_bias), pad(a_log), dsk, norm_w.reshape(1, SSM_WIDTH).astype(F32))


def _sc_kernel(x_ref, xprev_ref, nw_ref, win_ref, cw_ref, wout_ref, o_ref, vpad_ref, bg_ref, *,
               tm, tiles_per_seq):
    i = pl.program_id(0)
    d = x_ref.shape[1]

    @pl.when(i == 0)
    def _():
        vpad_ref[...] = jnp.zeros(vpad_ref.shape, F32)
        bg_ref[...] = jnp.zeros(bg_ref.shape, F32)

    def finish_chunk(prev, c0, c1):
        conv = cw_ref[SC_CONV - 1:SC_CONV, c0:c1] * vpad_ref[prev, HALO:HALO + tm, c0:c1]
        for kk in range(SC_CONV - 1):
            o = HALO - (SC_CONV - 1) + kk
            conv = conv + cw_ref[kk:kk + 1, c0:c1] * vpad_ref[prev, o:o + tm, c0:c1]
        return _dot((bg_ref[prev, :, c0:c1] * conv).astype(BF16), wout_ref[c0:c1, :].astype(BF16))

    def stage(cur, prev):
        cw = 2 * LANES
        chunks = [(c0, c0 + cw) for c0 in range(0, d, cw)]
        h = _rms(x_ref[...], nw_ref[...], 1e-6).astype(BF16)
        bgate = _dot(h, win_ref[:, 0:d].astype(BF16))
        out = finish_chunk(prev, *chunks[0])
        cgate = _dot(h, win_ref[:, d:2 * d].astype(BF16))
        out = out + finish_chunk(prev, *chunks[1])
        u = _dot(h, win_ref[:, 2 * d:3 * d].astype(BF16))
        for c0, c1 in chunks[2:]:
            out = out + finish_chunk(prev, c0, c1)
        o_ref[...] = xprev_ref[...] + out

        tail = vpad_ref[prev, tm:tm + HALO, :]
        vpad_ref[cur, 0:HALO, :] = jnp.where(i % tiles_per_seq == 0, jnp.zeros_like(tail), tail)
        vpad_ref[cur, HALO:HALO + tm, :] = cgate * u
        bg_ref[cur] = bgate

    pl.when(i % 2 == 0)(functools.partial(stage, 0, 1))
    pl.when(i % 2 == 1)(functools.partial(stage, 1, 0))


def _sc_call(x, nw, w_in, conv_w, w_out, layer, *, batch, tm=PROJ_TM):
    T, D = x.shape
    S = T // batch
    assert S % tm == 0
    nt = T // tm
    cur = lambda i: (jnp.minimum(i, nt - 1), 0)
    prev = lambda i: (jnp.maximum(i - 1, 0), 0)
    nbytes = (w_in[layer].size + w_out[layer].size) * 4 + 6 * tm * D * 4 + 4 * tm * D * 4 + 8 * tm * D * 4
    return pl.pallas_call(
        functools.partial(_sc_kernel, tm=tm, tiles_per_seq=S // tm),
        grid=(nt + 1,),
        in_specs=[pl.BlockSpec((tm, D), cur), pl.BlockSpec((tm, D), prev), _resident((1, D)),
                  _layer_resident(w_in, layer), _resident(conv_w.shape), _layer_resident(w_out, layer)],
        out_specs=pl.BlockSpec((tm, D), prev),
        out_shape=jax.ShapeDtypeStruct((T, D), F32),
        scratch_shapes=[pltpu.VMEM((2, tm + HALO, D), F32), pltpu.VMEM((2, tm, D), F32)],
        compiler_params=pltpu.CompilerParams(
            dimension_semantics=("arbitrary",), vmem_limit_bytes=_vmem_limit(nbytes)),
        name="short_conv",
    )(x, x, nw.reshape(1, D), w_in, conv_w.astype(F32), w_out)


def kernel(x, rel_bias, final_norm_w, ffn1_norm, ffn1_wg, ffn1_wu, ffn1_wd, mix_norm, ffn2_norm, ffn2_wg, ffn2_wu, ffn2_wd, hyb_w_in, hyb_w_out, diff_lq1, diff_lk1, diff_lq2, diff_lk2, diff_subln_w, ssm_conv_w, ssm_conv_b, ssm_dt_bias, ssm_a_log, ssm_d, ssm_norm_w, sc_w_in, sc_conv_w, sc_w_out):
    B, S, D = x.shape
    T = B * S
    xt = x.reshape(T, D)
    bf = lambda w: w.astype(BF16)

    lam_inits = [0.8 - 0.6 * math.exp(-0.3 * i) for i in range(0, DEPTH, 2)]
    tiles, lams = _prep_call(rel_bias, diff_lq1, diff_lk1, diff_lq2, diff_lk2, lam_inits)

    mix = None
    for i in range(DEPTH):
        j = i // 2
        xt = _ffn_call(xt, ffn1_norm[i], ffn1_wg, ffn1_wu, ffn1_wd, i)
        if i % 2 == 0:
            w_in = hyb_w_in[j]
            qkv_end = 3 * DIFF_WIDTH
            w_pad = bf(jnp.pad(w_in, ((0, 0), (0, LANES - SSM_HEADS))))
            wvt = bf(jnp.concatenate([w_in[:, 0:DIFF_WIDTH], w_in[:, 2 * DIFF_WIDTH:qkv_end]], axis=1).T)
            q, k, vt, y = _hyb_call(xt, mix_norm[i], w_pad, wvt, ssm_conv_w[j], ssm_conv_b[j],
                                    ssm_dt_bias[j], ssm_a_log[j], ssm_d[j], ssm_norm_w[j], batch=B)
            o = _attn_call(q, k, vt, tiles, lams[j * SUBLANES:(j + 1) * SUBLANES], diff_subln_w[j],
                           batch=B, lam_init=lam_inits[j])
            mix = (o, y, hyb_w_out, j)
        else:
            xt = _sc_call(xt, mix_norm[i], sc_w_in, sc_conv_w[j], sc_w_out, j, batch=B)
            mix = None
        xt = _ffn_call(xt, ffn2_norm[i], ffn2_wg, ffn2_wu, ffn2_wd, i, mix=mix,
                       final_w=final_norm_w if i == DEPTH - 1 else None)
    return xt.reshape(B, S, D)
```

```python
import functools
import math

import jax
import jax.numpy as jnp
from jax import lax
from jax.experimental import pallas as pl
from jax.experimental.pallas import tpu as pltpu

F32 = jnp.float32
BF16 = jnp.bfloat16

DEPTH = 4
N_DIFF_HEADS = 4
DIFF_HEAD_DIM = 64
DIFF_WIDTH = N_DIFF_HEADS * 2 * DIFF_HEAD_DIM
NUM_BUCKETS = 32
MAX_DISTANCE = 128
SSM_HEADS = 8
SSM_HEAD_DIM = 64
SSM_WIDTH = SSM_HEADS * SSM_HEAD_DIM
SSM_GROUPS = 2
SSM_STATE = 128
SSM_CONV = 4
SSM_CHUNK = 128
HEADS_PER_GROUP = SSM_HEADS // SSM_GROUPS
SSM_XBC = SSM_WIDTH + 2 * SSM_GROUPS * SSM_STATE
SC_CONV = 3

LANES = 128
SUBLANES = 8
V7X_SCOPED_VMEM_CAP = 60000 * 1024

FFN_TM = 512
FFN_FCHUNK = 256
PROJ_TM = 512
ATT_T = 512
ATT_STRIP = 256
ATT_ONES = 16
ATT_MAX_UNROLL = 26
ATT_FAR_TILE = 2
HALO = SUBLANES

NEG_BIG = -1e30
LOG2E = math.log2(math.e)


def _vmem_limit(nbytes):
    return int(min(V7X_SCOPED_VMEM_CAP, nbytes * 5 // 4 + (8 << 20)))


def _dot(a, b):
    return jnp.dot(a, b, preferred_element_type=F32)


def _dot_nt(a, b):
    return lax.dot_general(a, b, (((1,), (1,)), ((), ())), preferred_element_type=F32)


def _rms(x, w, eps):
    ms = jnp.mean(x * x, axis=-1, keepdims=True)
    return x * lax.rsqrt(ms + eps) * w


def _silu(x):
    h = 0.5 * x
    return h + h * jnp.tanh(h)


def _resident(shape):
    nd = len(shape)
    return pl.BlockSpec(shape, lambda *_: (0,) * nd, pipeline_mode=pl.Buffered(1))


def _ffn_kernel(*refs, pre_mix, final_norm, f_total):
    refs = list(refs)
    x_ref = refs.pop(0)
    if pre_mix:
        o_ref, y_ref, wo_ref = refs.pop(0), refs.pop(0), refs.pop(0)
    nw_ref, wg_ref, wu_ref, wd_ref = refs.pop(0), refs.pop(0), refs.pop(0), refs.pop(0)
    if final_norm:
        fw_ref = refs.pop(0)
    out_ref = refs.pop(0)

    x = x_ref[...]
    if pre_mix:
        mixed = jnp.concatenate([o_ref[...], y_ref[...]], axis=1)
        x = x + _dot(mixed, wo_ref[...].astype(BF16))
    h = (x * nw_ref[...]).astype(BF16)
    r = lax.rsqrt(jnp.mean(x * x, axis=-1, keepdims=True) + 1e-6)
    acts = []
    for c0 in range(0, f_total, FFN_FCHUNK):
        c1 = c0 + FFN_FCHUNK
        g = _dot(h, wg_ref[:, c0:c1].astype(BF16)) * r
        u = _dot(h, wu_ref[:, c0:c1].astype(BF16)) * r
        acts.append((_silu(g) * u).astype(BF16))
    acc = _dot(jnp.concatenate(acts, axis=1), wd_ref[...].astype(BF16))
    y = x + 0.5 * acc
    if final_norm:
        y = _rms(y, fw_ref[...], 1e-6)
    out_ref[...] = y


def _layer_resident(stacked, layer):
    _, r, c = stacked.shape
    return pl.BlockSpec((None, r, c), lambda *_: (layer, 0, 0), pipeline_mode=pl.Buffered(1))


def _ffn_call(x, nw, wg, wu, wd, layer, *, mix=None, final_w=None, tm=FFN_TM):
    T, D = x.shape
    F = wg.shape[2]
    assert T % tm == 0 and F % FFN_FCHUNK == 0
    row = lambda i: (i, 0)
    args = [x]
    in_specs = [pl.BlockSpec((tm, D), row)]
    nbytes = 4 * tm * D * 4 + 3 * D * F * 4 + 6 * tm * FFN_FCHUNK * 4 + 3 * tm * D * 4
    if mix is not None:
        o, y, wo, mix_layer = mix
        args += [o, y, wo]
        in_specs += [pl.BlockSpec((tm, o.shape[1]), row), pl.BlockSpec((tm, y.shape[1]), row),
                     _layer_resident(wo, mix_layer)]
        nbytes += wo.shape[1] * wo.shape[2] * 4 + 4 * tm * o.shape[1] * 2 * 2
    args += [nw.reshape(1, D), wg, wu, wd]
    in_specs += [_resident((1, D)), _layer_resident(wg, layer), _layer_resident(wu, layer),
                 _layer_resident(wd, layer)]
    if final_w is not None:
        args.append(final_w.reshape(1, D))
        in_specs.append(_resident((1, D)))
    kern = functools.partial(_ffn_kernel, pre_mix=mix is not None, final_norm=final_w is not None,
                             f_total=F)
    return pl.pallas_call(
        kern,
        grid=(T // tm,),
        in_specs=in_specs,
        out_specs=pl.BlockSpec((tm, D), row),
        out_shape=jax.ShapeDtypeStruct((T, D), F32),
        compiler_params=pltpu.CompilerParams(
            dimension_semantics=("arbitrary",), vmem_limit_bytes=_vmem_limit(nbytes)),
        name="ffn",
    )(*args)


def _prep_kernel(tab_ref, lq1_ref, lk1_ref, lq2_ref, lk2_ref, tile_ref, lam_ref, *, t, lam_inits):
    hd = pl.program_id(0)
    key = lax.broadcasted_iota(jnp.int32, (t, t), 0)
    qry = lax.broadcasted_iota(jnp.int32, (t, t), 1)
    max_exact = NUM_BUCKETS // 2
    far = tab_ref[hd, NUM_BUCKETS - 1]
    tile_ref[ATT_FAR_TILE] = jnp.zeros((t, t), F32)
    for off in range(ATT_FAR_TILE):
        dist = off * t + qry - key
        d = jnp.maximum(dist, 0)
        large = max_exact + (
            jnp.log(jnp.maximum(d, 1).astype(F32) / max_exact)
            / math.log(MAX_DISTANCE / max_exact) * (NUM_BUCKETS - max_exact)).astype(jnp.int32)
        bucket = jnp.where(d < max_exact, d, jnp.minimum(large, NUM_BUCKETS - 1))
        bias = jnp.full((t, t), far, F32)
        for b in range(NUM_BUCKETS - 1):
            bias = jnp.where(bucket == b, tab_ref[hd, b], bias)
        val = (bias - far) * LOG2E
        if off == 0:
            val = jnp.where(dist >= 0, val, NEG_BIG)
        tile_ref[off] = val
    for j, lam_init in enumerate(lam_inits):
        s1 = jnp.sum(lq1_ref[j:j + 1, :] * lk1_ref[j:j + 1, :], axis=-1, keepdims=True)
        s2 = jnp.sum(lq2_ref[j:j + 1, :] * lk2_ref[j:j + 1, :], axis=-1, keepdims=True)
        lam = jnp.exp(s1) - jnp.exp(s2) + lam_init
        lam_ref[j * SUBLANES:(j + 1) * SUBLANES, :] = jnp.broadcast_to(lam, (SUBLANES, LANES))


def _prep_call(rel_bias, lq1, lk1, lq2, lk2, lam_inits, *, t=ATT_T):
    ne = lq1.shape[0]
    H = rel_bias.shape[1]
    whole = lambda shape: pl.BlockSpec(shape, lambda h: (0,) * len(shape))
    return pl.pallas_call(
        functools.partial(_prep_kernel, t=t, lam_inits=tuple(lam_inits)),
        grid=(H,),
        in_specs=[pl.BlockSpec(memory_space=pltpu.SMEM)] + [whole(lq1.shape)] * 4,
        out_specs=[pl.BlockSpec((None, ATT_FAR_TILE + 1, t, t), lambda h: (h, 0, 0, 0)),
                   whole((ne * SUBLANES, LANES))],
        out_shape=[jax.ShapeDtypeStruct((H, ATT_FAR_TILE + 1, t, t), F32),
                   jax.ShapeDtypeStruct((ne * SUBLANES, LANES), F32)],
        compiler_params=pltpu.CompilerParams(
            dimension_semantics=("arbitrary",), vmem_limit_bytes=_vmem_limit(12 * t * t * 4)),
        name="attn_prep",
    )(rel_bias.T.astype(F32), lq1, lk1, lq2, lk2)


def _attn_kernel(qt_ref, k_ref, vt_ref, tile_ref, lam_ref, sw_ref, o_ref,
                 qm_ref, m_ref, acc_ref, s_ref, smax_ref, *, t, nq, unroll, q_split, lam_init):
    e = 2 * DIFF_HEAD_DIM
    first = lax.broadcasted_iota(jnp.int32, (LANES, t), 0) < DIFF_HEAD_DIM
    ones = jnp.ones((ATT_ONES, t), BF16)
    chains = [(w, slice(c0, c0 + ATT_STRIP)) for c0 in range(0, t, ATT_STRIP) for w in range(2)]

    def rows(blk):
        return pl.ds(pl.multiple_of(blk * t, t), t)

    def prepare(qi, carry):
        q = qt_ref[:, rows(qi)]
        zero = jnp.zeros_like(q)
        qm_ref[0, :, rows(qi)] = jnp.where(first, q, zero)
        qm_ref[1, :, rows(qi)] = jnp.where(first, zero, q)
        m_ref[qi] = jnp.full(m_ref.shape[1:], NEG_BIG, F32)
        acc_ref[qi] = jnp.zeros(acc_ref.shape[1:], F32)
        return carry

    def score_chain(ci, qi, j, slot, k, biased):
        w, cols = chains[ci]
        q_cols = pl.ds(pl.multiple_of(qi * t + cols.start, ATT_STRIP), ATT_STRIP)
        s = _dot(k, qm_ref[w, :, q_cols])
        if biased:
            near = jnp.minimum(qi - j, ATT_FAR_TILE)
            s = s + tile_ref[near, :, cols]
        s_ref[slot, ci] = s
        smax_ref[slot, ci] = jnp.max(s, axis=0, keepdims=True)

    def value_chain(ci, qi, slot, vta):
        w, cols = chains[ci]
        m_old = m_ref[qi, w, :, cols]
        m_new = jnp.maximum(m_old, smax_ref[slot, ci])
        p = jnp.exp2(s_ref[slot, ci] - m_new).astype(BF16)
        acc_ref[qi, w, :, cols] = jnp.exp2(m_old - m_new) * acc_ref[qi, w, :, cols] + _dot(vta, p)
        m_ref[qi, w, :, cols] = m_new

    def finalize(pair, carry):
        for qi in (2 * pair, 2 * pair + 1):
            lam = lam_ref[0:1, 0:1]
            o1 = acc_ref[qi, 0, 0:e, :] * (1.0 / acc_ref[qi, 0, e:e + 1, :])
            o2 = acc_ref[qi, 1, 0:e, :] * (1.0 / acc_ref[qi, 1, e:e + 1, :])
            out_t = o1 - lam * o2
            ms = jnp.mean(out_t * out_t, axis=0, keepdims=True)
            out_t = out_t * lax.rsqrt(ms + 1e-5) * sw_ref[...] * (1.0 - lam_init)
            o_ref[rows(qi), :] = out_t.T.astype(BF16)
        return carry

    def sweep(first_pair, next_pair, last_pair, count, biased):
        def step(qv, jv, slot):
            qs, js = next_pair(qv, jv)
            past_end = qs >= nq
            qs = jnp.where(past_end, last_pair[0], qs)
            js = jnp.where(past_end, last_pair[1], js)
            k_next = k_ref[rows(js), :]
            vta = jnp.concatenate([vt_ref[:, rows(jv)], ones], axis=0)
            for ci in range(len(chains)):
                score_chain(ci, qs, js, 1 - slot, k_next, biased)
                value_chain(ci, qv, slot, vta)
            return qs, js

        trip = max(u for u in range(unroll, ATT_MAX_UNROLL + 1, 2) if count % u == 0)

        def body(_, carry):
            qv, jv = carry
            for u in range(trip):
                qv, jv = step(qv, jv, u % 2)
            return qv, jv

        k0 = k_ref[rows(first_pair[1]), :]
        for ci in range(len(chains)):
            score_chain(ci, first_pair[0], first_pair[1], 0, k0, biased)
        lax.fori_loop(0, count // trip, body, (jnp.int32(first_pair[0]), jnp.int32(first_pair[1])))

    def next_near(qv, jv):
        wrap = jv == qv
        qn = qv + 1
        return jnp.where(wrap, qn, qv), jnp.where(wrap, jnp.where(qn < q_split, 0, qn - 1), jv + 1)

    def next_far(qv, jv):
        wrap = jv == qv - ATT_FAR_TILE
        return jnp.where(wrap, qv + 1, qv), jnp.where(wrap, 0, jv + 1)

    n_near = q_split * (q_split + 1) // 2 + ATT_FAR_TILE * (nq - q_split)
    n_far = nq * (nq + 1) // 2 - n_near
    lax.fori_loop(0, nq, prepare, 0)
    sweep((0, 0), next_near, (nq - 1, nq - 1), n_near, True)
    if n_far:
        sweep((q_split, 0), next_far, (nq - 1, nq - 1 - ATT_FAR_TILE), n_far, False)
    lax.fori_loop(0, nq // 2, finalize, 0)


def _attn_plan(nq):
    for unroll in (8, 4, 2):
        for q_split in range(ATT_FAR_TILE, nq + 1):
            n_near = q_split * (q_split + 1) // 2 + ATT_FAR_TILE * (nq - q_split)
            n_far = nq * (nq + 1) // 2 - n_near
            if n_near % unroll == 0 and n_far % unroll == 0:
                return unroll, q_split
    raise ValueError(f"unsupported number of query blocks: {nq}")


def _attn_call(qt, k, vt, tiles, lam, subln_w, *, batch, lam_init, t=ATT_T):
    T, W = k.shape
    S = T // batch
    H = W // LANES
    nq = S // t
    e = 2 * DIFF_HEAD_DIM
    assert e == LANES and S % t == 0 and nq % 2 == 0
    unroll, q_split = _attn_plan(nq)
    n_chains = 2 * (t // ATT_STRIP)
    n_tiles = ATT_FAR_TILE + 1
    nbytes = (2 * 4 * (S * LANES * 2) + 2 * n_tiles * t * t * 4 + 2 * S * LANES * 2
              + nq * 2 * (e + ATT_ONES + SUBLANES) * t * 4 + 2 * n_chains * t * ATT_STRIP * 4 + 2 * t * t * 4)
    return pl.pallas_call(
        functools.partial(_attn_kernel, t=t, nq=nq, unroll=unroll, q_split=q_split, lam_init=lam_init),
        grid=(batch, H),
        in_specs=[
            pl.BlockSpec((LANES, S), lambda b, h: (h, b)),
            pl.BlockSpec((S, LANES), lambda b, h: (b, h)),
            pl.BlockSpec((LANES, S), lambda b, h: (h, b)),
            pl.BlockSpec((None, n_tiles, t, t), lambda b, h: (h, 0, 0, 0)),
            pl.BlockSpec((SUBLANES, LANES), lambda b, h: (0, 0)),
            pl.BlockSpec((e, 1), lambda b, h: (0, 0)),
        ],
        out_specs=pl.BlockSpec((S, LANES), lambda b, h: (b, h)),
        out_shape=jax.ShapeDtypeStruct((T, W), BF16),
        scratch_shapes=[pltpu.VMEM((2, LANES, S), BF16),
                        pltpu.VMEM((nq, 2, 1, t), F32), pltpu.VMEM((nq, 2, e + ATT_ONES, t), F32),
                        pltpu.VMEM((2, n_chains, t, ATT_STRIP), F32),
                        pltpu.VMEM((2, n_chains, 1, ATT_STRIP), F32)],
        compiler_params=pltpu.CompilerParams(
            dimension_semantics=("arbitrary", "arbitrary"), vmem_limit_bytes=_vmem_limit(nbytes)),
        name="diff_attn",
    )(qt, k, vt, tiles, lam, subln_w.reshape(e, 1))


def _split3(a):
    hi = a.astype(BF16)
    r1 = a - hi.astype(F32)
    mid = r1.astype(BF16)
    lo = (r1 - mid.astype(F32)).astype(BF16)
    return hi, mid, lo


def _hyb_kernel(x_ref, nw_ref, w_ref, wvt_ref, cw_ref, cb_ref, dtb_ref, alog_ref, dsk_ref, ynw_ref,
                qt_ref, k_ref, vt_ref, y_ref, xpad_ref, z_scr, dt_scr, st_ref, *, tm, tiles_per_seq):
    i = pl.program_id(0)
    L = SSM_CHUNK
    gw = HEADS_PER_GROUP * SSM_HEAD_DIM
    qw = DIFF_WIDTH
    c_z = 2 * qw + DIFF_WIDTH
    c_xbc = c_z + SSM_WIDTH
    c_dt = c_xbc + SSM_XBC

    @pl.when(i == 0)
    def _():
        xpad_ref[...] = jnp.zeros(xpad_ref.shape, F32)
        z_scr[...] = jnp.zeros(z_scr.shape, F32)
        dt_scr[...] = jnp.zeros(dt_scr.shape, F32)
        st_ref[...] = jnp.zeros(st_ref.shape, F32)

    def stage(cur, prev):
        row = lax.broadcasted_iota(jnp.int32, (L, L), 0)
        col = lax.broadcasted_iota(jnp.int32, (L, L), 1)
        causal = row >= col
        tri = jnp.where(causal, 1.0, 0.0).astype(BF16)
        lane_g = lax.broadcasted_iota(jnp.int32, (L, gw), 1) // SSM_HEAD_DIM
        lane_lo = lax.broadcasted_iota(jnp.int32, (L, LANES), 1) < SSM_HEAD_DIM
        lane_lo1 = lax.broadcasted_iota(jnp.int32, (1, LANES), 1) < SSM_HEAD_DIM
        a_neg = -jnp.exp(alog_ref[...])
        keep_state = jnp.where((i - 1) % tiles_per_seq == 0, 0.0, 1.0)

        def ssd_chunk(c):
            r0 = c * L
            dt = dt_scr[prev, r0:r0 + L, :]
            hi, mid, lo = _split3(a_neg * dt)
            cs = _dot(tri, hi) + _dot(tri, mid) + _dot(tri, lo)
            yield

            conv = cb_ref[...]
            for kk in range(SSM_CONV):
                o = HALO - (SSM_CONV - 1) + kk + r0
                conv = conv + cw_ref[kk:kk + 1, :] * xpad_ref[prev, o:o + L, :]
            xc = _silu(conv)
            xs = xc[:, 0:SSM_WIDTH]
            bgs, cgs, gmats = [], [], []
            for g in range(SSM_GROUPS):
                b0 = SSM_WIDTH + g * SSM_STATE
                c0 = SSM_WIDTH + SSM_GROUPS * SSM_STATE + g * SSM_STATE
                bgs.append(xc[:, b0:b0 + SSM_STATE])
                cgs.append(xc[:, c0:c0 + SSM_STATE].astype(BF16))
                gmats.append(_dot_nt(cgs[g], bgs[g].astype(BF16)))
            yield

            cs_t = cs.T
            dt_t = dt.T
            ydiags, st_news, e_mats, decs_g = [], [], [], []
            for g in range(SSM_GROUPS):
                bg_t = bgs[g].T
                xg = xs[:, g * gw:(g + 1) * gw]
                ydiag = None
                st_new = None
                ecols, decs = [], []
                for hh in range(HEADS_PER_GROUP):
                    h = g * HEADS_PER_GROUP + hh
                    colb = jnp.broadcast_to(cs[:, h:h + 1], (L, L))
                    rowb = cs_t[h:h + 1, :]
                    dtrow = dt_t[h:h + 1, :]
                    decay = jnp.exp(jnp.where(causal, colb - rowb, NEG_BIG))
                    mh = (gmats[g] * decay * dtrow).astype(BF16)
                    xm = jnp.where(lane_g == hh, xg, 0.0).astype(BF16)
                    last = colb[L - 1:L, :]
                    wrow = dtrow * jnp.exp(last - rowb)
                    btw = (bg_t * wrow).astype(BF16)
                    d1 = _dot(mh, xm)
                    d2 = _dot(btw, xm)
                    ydiag = d1 if ydiag is None else ydiag + d1
                    st_new = d2 if st_new is None else st_new + d2
                    ecols.append(jnp.exp(colb))
                    decs.append(jnp.exp(last))
                ydiags.append(ydiag)
                st_news.append(st_new)
                e_mats.append(jnp.concatenate([jnp.where(lane_lo, ecols[0], ecols[1]),
                                               jnp.where(lane_lo, ecols[2], ecols[3])], axis=1))
                decs_g.append(jnp.concatenate([jnp.where(lane_lo1, decs[0], decs[1]),
                                               jnp.where(lane_lo1, decs[2], decs[3])], axis=1))
                if g == 0:
                    yield

            ys = []
            for g in range(SSM_GROUPS):
                st_old = st_ref[g] * keep_state if c == 0 else st_ref[g]
                ys.append(ydiags[g] + _dot(cgs[g], st_old.astype(BF16)) * e_mats[g])
                st_ref[g] = st_old * decs_g[g] + st_news[g]
            y = jnp.concatenate(ys, axis=1) + dsk_ref[...] * xs
            gated = y * z_scr[prev, r0:r0 + L, :]
            outs = []
            for g in range(SSM_GROUPS):
                part = gated[:, g * gw:(g + 1) * gw]
                ms = jnp.mean(part * part, axis=-1, keepdims=True)
                outs.append(part * lax.rsqrt(ms + 1e-5))
            y_ref[r0:r0 + L, :] = (jnp.concatenate(outs, axis=1) * ynw_ref[...]).astype(BF16)
            yield

        h = _rms(x_ref[...], nw_ref[...], 1e-6).astype(BF16)
        scale = DIFF_HEAD_DIM ** -0.5 * LOG2E
        piece = 2 * LANES

        def pieces():
            for r in range(0, qw, piece):
                qt_ref[r:r + piece, :] = (_dot_nt(wvt_ref[r:r + piece, :], h) * scale).astype(BF16)
                yield
            for c in range(0, qw, piece):
                k_ref[:, c:c + piece] = _dot(h, w_ref[:, qw + c:qw + c + piece]).astype(BF16)
                yield
            for r in range(0, DIFF_WIDTH, piece):
                vt_ref[r:r + piece, :] = _dot_nt(wvt_ref[qw + r:qw + r + piece, :], h).astype(BF16)
                yield
            for c in range(0, SSM_WIDTH, piece):
                z_scr[cur, :, c:c + piece] = _silu(_dot(h, w_ref[:, c_z + c:c_z + c + piece]))
                yield
            dt_raw = _dot(h, w_ref[:, c_dt:c_dt + LANES]) + dtb_ref[...]
            dt_scr[cur] = jnp.maximum(dt_raw, 0.0) + jnp.log1p(jnp.exp(-jnp.abs(dt_raw)))
            yield
            tail = xpad_ref[prev, tm:tm + HALO, :]
            xpad_ref[cur, 0:HALO, :] = jnp.where(i % tiles_per_seq == 0, jnp.zeros_like(tail), tail)
            for c in range(0, SSM_XBC, piece):
                xpad_ref[cur, HALO:HALO + tm, c:c + piece] = _dot(h, w_ref[:, c_xbc + c:c_xbc + c + piece])
                yield

        pieces_after_group = (0, 2, 1, 0)
        proj = pieces()
        for c in range(tm // L):
            for group, _ in enumerate(ssd_chunk(c)):
                for _ in range(pieces_after_group[group]):
                    next(proj, None)
        for _ in proj:
            pass

    pl.when(i % 2 == 0)(functools.partial(stage, 0, 1))
    pl.when(i % 2 == 1)(functools.partial(stage, 1, 0))


def _hyb_call(x, nw, w_pad, wvt, conv_w, conv_b, dt_bias, a_log, d_skip, norm_w, *, batch, tm=PROJ_TM):
    T, D = x.shape
    S = T // batch
    assert S % tm == 0 and tm % SSM_CHUNK == 0
    nt = T // tm
    cur = lambda i: (jnp.minimum(i, nt - 1), 0)
    cur_t = lambda i: (0, jnp.minimum(i, nt - 1))
    prev = lambda i: (jnp.maximum(i - 1, 0), 0)
    pad = lambda v: jnp.pad(v.astype(F32), (0, LANES - v.shape[0])).reshape(1, LANES)
    dsk = jnp.repeat(d_skip.astype(F32), SSM_HEAD_DIM).reshape(1, SSM_WIDTH)
    outs = [
        jax.ShapeDtypeStruct((DIFF_WIDTH, T), BF16),
        jax.ShapeDtypeStruct((T, DIFF_WIDTH), BF16),
        jax.ShapeDtypeStruct((DIFF_WIDTH, T), BF16),
        jax.ShapeDtypeStruct((T, SSM_WIDTH), BF16),
    ]
    out_specs = [pl.BlockSpec((DIFF_WIDTH, tm), cur_t), pl.BlockSpec((tm, DIFF_WIDTH), cur),
                 pl.BlockSpec((DIFF_WIDTH, tm), cur_t), pl.BlockSpec((tm, SSM_WIDTH), prev)]
    nbytes = ((w_pad.size + wvt.size) * 2 + 2 * tm * D * 4 + 2 * tm * 2 * 4 * DIFF_WIDTH
              + 2 * (tm + HALO) * SSM_XBC * 4 + 2 * tm * (SSM_WIDTH + LANES) * 4 + 16 * tm * SSM_XBC * 4)
    return pl.pallas_call(
        functools.partial(_hyb_kernel, tm=tm, tiles_per_seq=S // tm),
        grid=(nt + 1,),
        in_specs=[pl.BlockSpec((tm, D), cur), _resident((1, D)), _resident(w_pad.shape),
                  _resident(wvt.shape), _resident((SSM_CONV, SSM_XBC)), _resident((1, SSM_XBC)),
                  _resident((1, LANES)), _resident((1, LANES)), _resident((1, SSM_WIDTH)),
                  _resident((1, SSM_WIDTH))],
        out_specs=out_specs,
        out_shape=outs,
        scratch_shapes=[pltpu.VMEM((2, tm + HALO, SSM_XBC), F32), pltpu.VMEM((2, tm, SSM_WIDTH), F32),
                        pltpu.VMEM((2, tm, LANES), F32),
                        pltpu.VMEM((SSM_GROUPS, SSM_STATE, HEADS_PER_GROUP * SSM_HEAD_DIM), F32)],
        compiler_params=pltpu.CompilerParams(
            dimension_semantics=("arbitrary",), vmem_limit_bytes=_vmem_limit(nbytes)),
        name="hyb_ssd",
    )(x, nw.reshape(1, D), w_pad, wvt, conv_w.astype(F32), conv_b.reshape(1, SSM_XBC).astype(F32),
      pad(dt_bias), pad(a_log), dsk, norm_w.reshape(1, SSM_WIDTH).astype(F32))


def _sc_kernel(x_ref, xprev_ref, nw_ref, win_ref, cw_ref, wout_ref, o_ref, vpad_ref, bg_ref, *,
               tm, tiles_per_seq):
    i = pl.program_id(0)
    d = x_ref.shape[1]

    @pl.when(i == 0)
    def _():
        vpad_ref[...] = jnp.zeros(vpad_ref.shape, F32)
        bg_ref[...] = jnp.zeros(bg_ref.shape, F32)

    def finish_chunk(prev, c0, c1):
        conv = cw_ref[SC_CONV - 1:SC_CONV, c0:c1] * vpad_ref[prev, HALO:HALO + tm, c0:c1]
        for kk in range(SC_CONV - 1):
            o = HALO - (SC_CONV - 1) + kk
            conv = conv + cw_ref[kk:kk + 1, c0:c1] * vpad_ref[prev, o:o + tm, c0:c1]
        return _dot((bg_ref[prev, :, c0:c1] * conv).astype(BF16), wout_ref[c0:c1, :].astype(BF16))

    def stage(cur, prev):
        cw = 2 * LANES
        chunks = [(c0, c0 + cw) for c0 in range(0, d, cw)]
        h = _rms(x_ref[...], nw_ref[...], 1e-6).astype(BF16)
        bgate = _dot(h, win_ref[:, 0:d].astype(BF16))
        out = finish_chunk(prev, *chunks[0])
        cgate = _dot(h, win_ref[:, d:2 * d].astype(BF16))
        out = out + finish_chunk(prev, *chunks[1])
        u = _dot(h, win_ref[:, 2 * d:3 * d].astype(BF16))
        for c0, c1 in chunks[2:]:
            out = out + finish_chunk(prev, c0, c1)
        o_ref[...] = xprev_ref[...] + out

        tail = vpad_ref[prev, tm:tm + HALO, :]
        vpad_ref[cur, 0:HALO, :] = jnp.where(i % tiles_per_seq == 0, jnp.zeros_like(tail), tail)
        vpad_ref[cur, HALO:HALO + tm, :] = cgate * u
        bg_ref[cur] = bgate

    pl.when(i % 2 == 0)(functools.partial(stage, 0, 1))
    pl.when(i % 2 == 1)(functools.partial(stage, 1, 0))


def _sc_call(x, nw, w_in, conv_w, w_out, layer, *, batch, tm=PROJ_TM):
    T, D = x.shape
    S = T // batch
    assert S % tm == 0
    nt = T // tm
    cur = lambda i: (jnp.minimum(i, nt - 1), 0)
    prev = lambda i: (jnp.maximum(i - 1, 0), 0)
    nbytes = (w_in[layer].size + w_out[layer].size) * 4 + 6 * tm * D * 4 + 4 * tm * D * 4 + 8 * tm * D * 4
    return pl.pallas_call(
        functools.partial(_sc_kernel, tm=tm, tiles_per_seq=S // tm),
        grid=(nt + 1,),
        in_specs=[pl.BlockSpec((tm, D), cur), pl.BlockSpec((tm, D), prev), _resident((1, D)),
                  _layer_resident(w_in, layer), _resident(conv_w.shape), _layer_resident(w_out, layer)],
        out_specs=pl.BlockSpec((tm, D), prev),
        out_shape=jax.ShapeDtypeStruct((T, D), F32),
        scratch_shapes=[pltpu.VMEM((2, tm + HALO, D), F32), pltpu.VMEM((2, tm, D), F32)],
        compiler_params=pltpu.CompilerParams(
            dimension_semantics=("arbitrary",), vmem_limit_bytes=_vmem_limit(nbytes)),
        name="short_conv",
    )(x, x, nw.reshape(1, D), w_in, conv_w.astype(F32), w_out)


def kernel(x, rel_bias, final_norm_w, ffn1_norm, ffn1_wg, ffn1_wu, ffn1_wd, mix_norm, ffn2_norm, ffn2_wg, ffn2_wu, ffn2_wd, hyb_w_in, hyb_w_out, diff_lq1, diff_lk1, diff_lq2, diff_lk2, diff_subln_w, ssm_conv_w, ssm_conv_b, ssm_dt_bias, ssm_a_log, ssm_d, ssm_norm_w, sc_w_in, sc_conv_w, sc_w_out):
    B, S, D = x.shape
    T = B * S
    xt = x.reshape(T, D)
    bf = lambda w: w.astype(BF16)

    lam_inits = [0.8 - 0.6 * math.exp(-0.3 * i) for i in range(0, DEPTH, 2)]
    tiles, lams = _prep_call(rel_bias, diff_lq1, diff_lk1, diff_lq2, diff_lk2, lam_inits)

    mix = None
    for i in range(DEPTH):
        j = i // 2
        xt = _ffn_call(xt, ffn1_norm[i], ffn1_wg, ffn1_wu, ffn1_wd, i)
        if i % 2 == 0:
            w_in = hyb_w_in[j]
            qkv_end = 3 * DIFF_WIDTH
            w_pad = bf(jnp.pad(w_in, ((0, 0), (0, LANES - SSM_HEADS))))
            wvt = bf(jnp.concatenate([w_in[:, 0:DIFF_WIDTH], w_in[:, 2 * DIFF_WIDTH:qkv_end]], axis=1).T)
            q, k, vt, y = _hyb_call(xt, mix_norm[i], w_pad, wvt, ssm_conv_w[j], ssm_conv_b[j],
                                    ssm_dt_bias[j], ssm_a_log[j], ssm_d[j], ssm_norm_w[j], batch=B)
            o = _attn_call(q, k, vt, tiles, lams[j * SUBLANES:(j + 1) * SUBLANES], diff_subln_w[j],
                           batch=B, lam_init=lam_inits[j])
            mix = (o, y, hyb_w_out, j)
        else:
            xt = _sc_call(xt, mix_norm[i], sc_w_in, sc_conv_w[j], sc_w_out, j, batch=B)
            mix = None
        xt = _ffn_call(xt, ffn2_norm[i], ffn2_wg, ffn2_wu, ffn2_wd, i, mix=mix,
                       final_w=final_norm_w if i == DEPTH - 1 else None)
    return xt.reshape(B, S, D)
```

```python
import functools
import math

import jax
import jax.numpy as jnp
from jax import lax
from jax.experimental import pallas as pl
from jax.experimental.pallas import tpu as pltpu

F32 = jnp.float32
BF16 = jnp.bfloat16

DEPTH = 4
N_DIFF_HEADS = 4
DIFF_HEAD_DIM = 64
DIFF_WIDTH = N_DIFF_HEADS * 2 * DIFF_HEAD_DIM
NUM_BUCKETS = 32
MAX_DISTANCE = 128
SSM_HEADS = 8
SSM_HEAD_DIM = 64
SSM_WIDTH = SSM_HEADS * SSM_HEAD_DIM
SSM_GROUPS = 2
SSM_STATE = 128
SSM_CONV = 4
SSM_CHUNK = 128
HEADS_PER_GROUP = SSM_HEADS // SSM_GROUPS
SSM_XBC = SSM_WIDTH + 2 * SSM_GROUPS * SSM_STATE
SC_CONV = 3

LANES = 128
SUBLANES = 8
V7X_SCOPED_VMEM_CAP = 60000 * 1024

FFN_TM = 512
FFN_FCHUNK = 256
PROJ_TM = 512
ATT_T = 512
ATT_STRIP = 256
ATT_ONES = 16
ATT_MAX_UNROLL = 52
ATT_FAR_TILE = 2
HALO = SUBLANES

NEG_BIG = -1e30
LOG2E = math.log2(math.e)


def _vmem_limit(nbytes):
    return int(min(V7X_SCOPED_VMEM_CAP, nbytes * 5 // 4 + (8 << 20)))


def _dot(a, b):
    return jnp.dot(a, b, preferred_element_type=F32)


def _dot_nt(a, b):
    return lax.dot_general(a, b, (((1,), (1,)), ((), ())), preferred_element_type=F32)


def _rms(x, w, eps):
    ms = jnp.mean(x * x, axis=-1, keepdims=True)
    return x * lax.rsqrt(ms + eps) * w


def _silu(x):
    h = 0.5 * x
    return h + h * jnp.tanh(h)


def _resident(shape):
    nd = len(shape)
    return pl.BlockSpec(shape, lambda *_: (0,) * nd, pipeline_mode=pl.Buffered(1))


def _ffn_kernel(*refs, pre_mix, final_norm, f_total):
    refs = list(refs)
    x_ref = refs.pop(0)
    if pre_mix:
        o_ref, y_ref, wo_ref = refs.pop(0), refs.pop(0), refs.pop(0)
    nw_ref, wg_ref, wu_ref, wd_ref = refs.pop(0), refs.pop(0), refs.pop(0), refs.pop(0)
    if final_norm:
        fw_ref = refs.pop(0)
    out_ref = refs.pop(0)

    x = x_ref[...]
    if pre_mix:
        mixed = jnp.concatenate([o_ref[...], y_ref[...]], axis=1)
        x = x + _dot(mixed, wo_ref[...].astype(BF16))
    h = (x * nw_ref[...]).astype(BF16)
    r = lax.rsqrt(jnp.mean(x * x, axis=-1, keepdims=True) + 1e-6)
    acts = []
    for c0 in range(0, f_total, FFN_FCHUNK):
        c1 = c0 + FFN_FCHUNK
        g = _dot(h, wg_ref[:, c0:c1].astype(BF16)) * r
        u = _dot(h, wu_ref[:, c0:c1].astype(BF16)) * r
        acts.append((_silu(g) * u).astype(BF16))
    acc = _dot(jnp.concatenate(acts, axis=1), wd_ref[...].astype(BF16))
    y = x + 0.5 * acc
    if final_norm:
        y = _rms(y, fw_ref[...], 1e-6)
    out_ref[...] = y


def _layer_resident(stacked, layer):
    _, r, c = stacked.shape
    return pl.BlockSpec((None, r, c), lambda *_: (layer, 0, 0), pipeline_mode=pl.Buffered(1))


def _ffn_call(x, nw, wg, wu, wd, layer, *, mix=None, final_w=None, tm=FFN_TM):
    T, D = x.shape
    F = wg.shape[2]
    assert T % tm == 0 and F % FFN_FCHUNK == 0
    row = lambda i: (i, 0)
    args = [x]
    in_specs = [pl.BlockSpec((tm, D), row)]
    nbytes = 4 * tm * D * 4 + 3 * D * F * 4 + 6 * tm * FFN_FCHUNK * 4 + 3 * tm * D * 4
    if mix is not None:
        o, y, wo, mix_layer = mix
        args += [o, y, wo]
        in_specs += [pl.BlockSpec((tm, o.shape[1]), row), pl.BlockSpec((tm, y.shape[1]), row),
                     _layer_resident(wo, mix_layer)]
        nbytes += wo.shape[1] * wo.shape[2] * 4 + 4 * tm * o.shape[1] * 2 * 2
    args += [nw.reshape(1, D), wg, wu, wd]
    in_specs += [_resident((1, D)), _layer_resident(wg, layer), _layer_resident(wu, layer),
                 _layer_resident(wd, layer)]
    if final_w is not None:
        args.append(final_w.reshape(1, D))
        in_specs.append(_resident((1, D)))
    kern = functools.partial(_ffn_kernel, pre_mix=mix is not None, final_norm=final_w is not None,
                             f_total=F)
    return pl.pallas_call(
        kern,
        grid=(T // tm,),
        in_specs=in_specs,
        out_specs=pl.BlockSpec((tm, D), row),
        out_shape=jax.ShapeDtypeStruct((T, D), F32),
        compiler_params=pltpu.CompilerParams(
            dimension_semantics=("arbitrary",), vmem_limit_bytes=_vmem_limit(nbytes)),
        name="ffn",
    )(*args)


def _prep_kernel(tab_ref, lq1_ref, lk1_ref, lq2_ref, lk2_ref, tile_ref, lam_ref, *, t, lam_inits):
    hd = pl.program_id(0)
    key = lax.broadcasted_iota(jnp.int32, (t, t), 0)
    qry = lax.broadcasted_iota(jnp.int32, (t, t), 1)
    max_exact = NUM_BUCKETS // 2
    far = tab_ref[hd, NUM_BUCKETS - 1]
    tile_ref[ATT_FAR_TILE] = jnp.zeros((t, t), F32)
    for off in range(ATT_FAR_TILE):
        dist = off * t + qry - key
        d = jnp.maximum(dist, 0)
        large = max_exact + (
            jnp.log(jnp.maximum(d, 1).astype(F32) / max_exact)
            / math.log(MAX_DISTANCE / max_exact) * (NUM_BUCKETS - max_exact)).astype(jnp.int32)
        bucket = jnp.where(d < max_exact, d, jnp.minimum(large, NUM_BUCKETS - 1))
        bias = jnp.full((t, t), far, F32)
        for b in range(NUM_BUCKETS - 1):
            bias = jnp.where(bucket == b, tab_ref[hd, b], bias)
        val = (bias - far) * LOG2E
        if off == 0:
            val = jnp.where(dist >= 0, val, NEG_BIG)
        tile_ref[off] = val
    for j, lam_init in enumerate(lam_inits):
        s1 = jnp.sum(lq1_ref[j:j + 1, :] * lk1_ref[j:j + 1, :], axis=-1, keepdims=True)
        s2 = jnp.sum(lq2_ref[j:j + 1, :] * lk2_ref[j:j + 1, :], axis=-1, keepdims=True)
        lam = jnp.exp(s1) - jnp.exp(s2) + lam_init
        lam_ref[j * SUBLANES:(j + 1) * SUBLANES, :] = jnp.broadcast_to(lam, (SUBLANES, LANES))


def _prep_call(rel_bias, lq1, lk1, lq2, lk2, lam_inits, *, t=ATT_T):
    ne = lq1.shape[0]
    H = rel_bias.shape[1]
    whole = lambda shape: pl.BlockSpec(shape, lambda h: (0,) * len(shape))
    return pl.pallas_call(
        functools.partial(_prep_kernel, t=t, lam_inits=tuple(lam_inits)),
        grid=(H,),
        in_specs=[pl.BlockSpec(memory_space=pltpu.SMEM)] + [whole(lq1.shape)] * 4,
        out_specs=[pl.BlockSpec((None, ATT_FAR_TILE + 1, t, t), lambda h: (h, 0, 0, 0)),
                   whole((ne * SUBLANES, LANES))],
        out_shape=[jax.ShapeDtypeStruct((H, ATT_FAR_TILE + 1, t, t), F32),
                   jax.ShapeDtypeStruct((ne * SUBLANES, LANES), F32)],
        compiler_params=pltpu.CompilerParams(
            dimension_semantics=("arbitrary",), vmem_limit_bytes=_vmem_limit(12 * t * t * 4)),
        name="attn_prep",
    )(rel_bias.T.astype(F32), lq1, lk1, lq2, lk2)


def _attn_kernel(qt_ref, k_ref, vt_ref, tile_ref, lam_ref, sw_ref, o_ref,
                 qm_ref, m_ref, acc_ref, s_ref, smax_ref, *, t, nq, unroll, q_split, lam_init):
    e = 2 * DIFF_HEAD_DIM
    first = lax.broadcasted_iota(jnp.int32, (LANES, t), 0) < DIFF_HEAD_DIM
    ones = jnp.ones((ATT_ONES, t), BF16)
    chains = [(w, slice(c0, c0 + ATT_STRIP)) for c0 in range(0, t, ATT_STRIP) for w in range(2)]

    def rows(blk):
        return pl.ds(pl.multiple_of(blk * t, t), t)

    def prepare(qi, carry):
        q = qt_ref[:, rows(qi)]
        zero = jnp.zeros_like(q)
        qm_ref[0, :, rows(qi)] = jnp.where(first, q, zero)
        qm_ref[1, :, rows(qi)] = jnp.where(first, zero, q)
        m_ref[qi] = jnp.full(m_ref.shape[1:], NEG_BIG, F32)
        acc_ref[qi] = jnp.zeros(acc_ref.shape[1:], F32)
        return carry

    def score_chain(ci, qi, j, slot, k, biased):
        w, cols = chains[ci]
        q_cols = pl.ds(pl.multiple_of(qi * t + cols.start, ATT_STRIP), ATT_STRIP)
        s = _dot(k, qm_ref[w, :, q_cols])
        if biased:
            near = jnp.minimum(qi - j, ATT_FAR_TILE)
            s = s + tile_ref[near, :, cols]
        s_ref[slot, ci] = s
        smax_ref[slot, ci] = jnp.max(s, axis=0, keepdims=True)

    def value_chain(ci, qi, slot, vta):
        w, cols = chains[ci]
        m_old = m_ref[qi, w, :, cols]
        m_new = jnp.maximum(m_old, smax_ref[slot, ci])
        p = jnp.exp2(s_ref[slot, ci] - m_new).astype(BF16)
        acc_ref[qi, w, :, cols] = jnp.exp2(m_old - m_new) * acc_ref[qi, w, :, cols] + _dot(vta, p)
        m_ref[qi, w, :, cols] = m_new

    def finalize(pair, carry):
        for qi in (2 * pair, 2 * pair + 1):
            lam = lam_ref[0:1, 0:1]
            o1 = acc_ref[qi, 0, 0:e, :] * (1.0 / acc_ref[qi, 0, e:e + 1, :])
            o2 = acc_ref[qi, 1, 0:e, :] * (1.0 / acc_ref[qi, 1, e:e + 1, :])
            out_t = o1 - lam * o2
            ms = jnp.mean(out_t * out_t, axis=0, keepdims=True)
            out_t = out_t * lax.rsqrt(ms + 1e-5) * sw_ref[...] * (1.0 - lam_init)
            o_ref[rows(qi), :] = out_t.T.astype(BF16)
        return carry

    def sweep(first_pair, next_pair, last_pair, count, biased):
        def step(qv, jv, slot):
            qs, js = next_pair(qv, jv)
            past_end = qs >= nq
            qs = jnp.where(past_end, last_pair[0], qs)
            js = jnp.where(past_end, last_pair[1], js)
            k_next = k_ref[rows(js), :]
            vta = jnp.concatenate([vt_ref[:, rows(jv)], ones], axis=0)
            for ci in range(len(chains)):
                score_chain(ci, qs, js, 1 - slot, k_next, biased)
                value_chain(ci, qv, slot, vta)
            return qs, js

        trip = max(u for u in range(unroll, ATT_MAX_UNROLL + 1, 2) if count % u == 0)

        def body(_, carry):
            qv, jv = carry
            for u in range(trip):
                qv, jv = step(qv, jv, u % 2)
            return qv, jv

        k0 = k_ref[rows(first_pair[1]), :]
        for ci in range(len(chains)):
            score_chain(ci, first_pair[0], first_pair[1], 0, k0, biased)
        lax.fori_loop(0, count // trip, body, (jnp.int32(first_pair[0]), jnp.int32(first_pair[1])))

    def next_near(qv, jv):
        wrap = jv == qv
        qn = qv + 1
        return jnp.where(wrap, qn, qv), jnp.where(wrap, jnp.where(qn < q_split, 0, qn - 1), jv + 1)

    def next_far(qv, jv):
        wrap = jv == qv - ATT_FAR_TILE
        return jnp.where(wrap, qv + 1, qv), jnp.where(wrap, 0, jv + 1)

    n_near = q_split * (q_split + 1) // 2 + ATT_FAR_TILE * (nq - q_split)
    n_far = nq * (nq + 1) // 2 - n_near
    lax.fori_loop(0, nq, prepare, 0)
    sweep((0, 0), next_near, (nq - 1, nq - 1), n_near, True)
    if n_far:
        sweep((q_split, 0), next_far, (nq - 1, nq - 1 - ATT_FAR_TILE), n_far, False)
    lax.fori_loop(0, nq // 2, finalize, 0)


def _attn_plan(nq):
    for unroll in (8, 4, 2):
        for q_split in range(ATT_FAR_TILE, nq + 1):
            n_near = q_split * (q_split + 1) // 2 + ATT_FAR_TILE * (nq - q_split)
            n_far = nq * (nq + 1) // 2 - n_near
            if n_near % unroll == 0 and n_far % unroll == 0:
                return unroll, q_split
    raise ValueError(f"unsupported number of query blocks: {nq}")


def _attn_call(qt, k, vt, tiles, lam, subln_w, *, batch, lam_init, t=ATT_T):
    T, W = k.shape
    S = T // batch
    H = W // LANES
    nq = S // t
    e = 2 * DIFF_HEAD_DIM
    assert e == LANES and S % t == 0 and nq % 2 == 0
    unroll, q_split = _attn_plan(nq)
    n_chains = 2 * (t // ATT_STRIP)
    n_tiles = ATT_FAR_TILE + 1
    nbytes = (2 * 4 * (S * LANES * 2) + 2 * n_tiles * t * t * 4 + 2 * S * LANES * 2
              + nq * 2 * (e + ATT_ONES + SUBLANES) * t * 4 + 2 * n_chains * t * ATT_STRIP * 4 + 2 * t * t * 4)
    return pl.pallas_call(
        functools.partial(_attn_kernel, t=t, nq=nq, unroll=unroll, q_split=q_split, lam_init=lam_init),
        grid=(batch, H),
        in_specs=[
            pl.BlockSpec((LANES, S), lambda b, h: (h, b)),
            pl.BlockSpec((S, LANES), lambda b, h: (b, h)),
            pl.BlockSpec((LANES, S), lambda b, h: (h, b)),
            pl.BlockSpec((None, n_tiles, t, t), lambda b, h: (h, 0, 0, 0)),
            pl.BlockSpec((SUBLANES, LANES), lambda b, h: (0, 0)),
            pl.BlockSpec((e, 1), lambda b, h: (0, 0)),
        ],
        out_specs=pl.BlockSpec((S, LANES), lambda b, h: (b, h)),
        out_shape=jax.ShapeDtypeStruct((T, W), BF16),
        scratch_shapes=[pltpu.VMEM((2, LANES, S), BF16),
                        pltpu.VMEM((nq, 2, 1, t), F32), pltpu.VMEM((nq, 2, e + ATT_ONES, t), F32),
                        pltpu.VMEM((2, n_chains, t, ATT_STRIP), F32),
                        pltpu.VMEM((2, n_chains, 1, ATT_STRIP), F32)],
        compiler_params=pltpu.CompilerParams(
            dimension_semantics=("arbitrary", "arbitrary"), vmem_limit_bytes=_vmem_limit(nbytes)),
        name="diff_attn",
    )(qt, k, vt, tiles, lam, subln_w.reshape(e, 1))


def _split3(a):
    hi = a.astype(BF16)
    r1 = a - hi.astype(F32)
    mid = r1.astype(BF16)
    lo = (r1 - mid.astype(F32)).astype(BF16)
    return hi, mid, lo


def _hyb_kernel(x_ref, nw_ref, w_ref, wvt_ref, cw_ref, cb_ref, dtb_ref, alog_ref, dsk_ref, ynw_ref,
                qt_ref, k_ref, vt_ref, y_ref, xpad_ref, z_scr, dt_scr, st_ref, *, tm, tiles_per_seq):
    i = pl.program_id(0)
    L = SSM_CHUNK
    gw = HEADS_PER_GROUP * SSM_HEAD_DIM
    qw = DIFF_WIDTH
    c_z = 2 * qw + DIFF_WIDTH
    c_xbc = c_z + SSM_WIDTH
    c_dt = c_xbc + SSM_XBC

    @pl.when(i == 0)
    def _():
        xpad_ref[...] = jnp.zeros(xpad_ref.shape, F32)
        z_scr[...] = jnp.zeros(z_scr.shape, F32)
        dt_scr[...] = jnp.zeros(dt_scr.shape, F32)
        st_ref[...] = jnp.zeros(st_ref.shape, F32)

    def stage(cur, prev):
        row = lax.broadcasted_iota(jnp.int32, (L, L), 0)
        col = lax.broadcasted_iota(jnp.int32, (L, L), 1)
        causal = row >= col
        tri = jnp.where(causal, 1.0, 0.0).astype(BF16)
        lane_g = lax.broadcasted_iota(jnp.int32, (L, gw), 1) // SSM_HEAD_DIM
        lane_lo = lax.broadcasted_iota(jnp.int32, (L, LANES), 1) < SSM_HEAD_DIM
        lane_lo1 = lax.broadcasted_iota(jnp.int32, (1, LANES), 1) < SSM_HEAD_DIM
        a_neg = -jnp.exp(alog_ref[...])
        keep_state = jnp.where((i - 1) % tiles_per_seq == 0, 0.0, 1.0)

        def ssd_chunk(c):
            r0 = c * L
            dt = dt_scr[prev, r0:r0 + L, :]
            hi, mid, lo = _split3(a_neg * dt)
            cs = _dot(tri, hi) + _dot(tri, mid) + _dot(tri, lo)
            yield

            conv = cb_ref[...]
            for kk in range(SSM_CONV):
                o = HALO - (SSM_CONV - 1) + kk + r0
                conv = conv + cw_ref[kk:kk + 1, :] * xpad_ref[prev, o:o + L, :]
            xc = _silu(conv)
            xs = xc[:, 0:SSM_WIDTH]
            bgs, cgs, gmats = [], [], []
            for g in range(SSM_GROUPS):
                b0 = SSM_WIDTH + g * SSM_STATE
                c0 = SSM_WIDTH + SSM_GROUPS * SSM_STATE + g * SSM_STATE
                bgs.append(xc[:, b0:b0 + SSM_STATE])
                cgs.append(xc[:, c0:c0 + SSM_STATE].astype(BF16))
                gmats.append(_dot_nt(cgs[g], bgs[g].astype(BF16)))
            yield

            cs_t = cs.T
            dt_t = dt.T
            ydiags, st_news, e_mats, decs_g = [], [], [], []
            for g in range(SSM_GROUPS):
                bg_t = bgs[g].T
                xg = xs[:, g * gw:(g + 1) * gw]
                ydiag = None
                st_new = None
                ecols, decs = [], []
                for hh in range(HEADS_PER_GROUP):
                    h = g * HEADS_PER_GROUP + hh
                    colb = jnp.broadcast_to(cs[:, h:h + 1], (L, L))
                    rowb = cs_t[h:h + 1, :]
                    dtrow = dt_t[h:h + 1, :]
                    decay = jnp.exp(jnp.where(causal, colb - rowb, NEG_BIG))
                    mh = (gmats[g] * decay * dtrow).astype(BF16)
                    xm = jnp.where(lane_g == hh, xg, 0.0).astype(BF16)
                    last = colb[L - 1:L, :]
                    wrow = dtrow * jnp.exp(last - rowb)
                    btw = (bg_t * wrow).astype(BF16)
                    d1 = _dot(mh, xm)
                    d2 = _dot(btw, xm)
                    ydiag = d1 if ydiag is None else ydiag + d1
                    st_new = d2 if st_new is None else st_new + d2
                    ecols.append(jnp.exp(colb))
                    decs.append(jnp.exp(last))
                ydiags.append(ydiag)
                st_news.append(st_new)
                e_mats.append(jnp.concatenate([jnp.where(lane_lo, ecols[0], ecols[1]),
                                               jnp.where(lane_lo, ecols[2], ecols[3])], axis=1))
                decs_g.append(jnp.concatenate([jnp.where(lane_lo1, decs[0], decs[1]),
                                               jnp.where(lane_lo1, decs[2], decs[3])], axis=1))
                if g == 0:
                    yield

            ys = []
            for g in range(SSM_GROUPS):
                st_old = st_ref[g] * keep_state if c == 0 else st_ref[g]
                ys.append(ydiags[g] + _dot(cgs[g], st_old.astype(BF16)) * e_mats[g])
                st_ref[g] = st_old * decs_g[g] + st_news[g]
            y = jnp.concatenate(ys, axis=1) + dsk_ref[...] * xs
            gated = y * z_scr[prev, r0:r0 + L, :]
            outs = []
            for g in range(SSM_GROUPS):
                part = gated[:, g * gw:(g + 1) * gw]
                ms = jnp.mean(part * part, axis=-1, keepdims=True)
                outs.append(part * lax.rsqrt(ms + 1e-5))
            y_ref[r0:r0 + L, :] = (jnp.concatenate(outs, axis=1) * ynw_ref[...]).astype(BF16)
            yield

        h = _rms(x_ref[...], nw_ref[...], 1e-6).astype(BF16)
        scale = DIFF_HEAD_DIM ** -0.5 * LOG2E
        piece = 2 * LANES

        def pieces():
            for r in range(0, qw, piece):
                qt_ref[r:r + piece, :] = (_dot_nt(wvt_ref[r:r + piece, :], h) * scale).astype(BF16)
                yield
            for c in range(0, qw, piece):
                k_ref[:, c:c + piece] = _dot(h, w_ref[:, qw + c:qw + c + piece]).astype(BF16)
                yield
            for r in range(0, DIFF_WIDTH, piece):
                vt_ref[r:r + piece, :] = _dot_nt(wvt_ref[qw + r:qw + r + piece, :], h).astype(BF16)
                yield
            for c in range(0, SSM_WIDTH, piece):
                z_scr[cur, :, c:c + piece] = _silu(_dot(h, w_ref[:, c_z + c:c_z + c + piece]))
                yield
            dt_raw = _dot(h, w_ref[:, c_dt:c_dt + LANES]) + dtb_ref[...]
            dt_scr[cur] = jnp.maximum(dt_raw, 0.0) + jnp.log1p(jnp.exp(-jnp.abs(dt_raw)))
            yield
            tail = xpad_ref[prev, tm:tm + HALO, :]
            xpad_ref[cur, 0:HALO, :] = jnp.where(i % tiles_per_seq == 0, jnp.zeros_like(tail), tail)
            for c in range(0, SSM_XBC, piece):
                xpad_ref[cur, HALO:HALO + tm, c:c + piece] = _dot(h, w_ref[:, c_xbc + c:c_xbc + c + piece])
                yield

        pieces_after_group = (0, 2, 1, 0)
        proj = pieces()
        for c in range(tm // L):
            for group, _ in enumerate(ssd_chunk(c)):
                for _ in range(pieces_after_group[group]):
                    next(proj, None)
        for _ in proj:
            pass

    pl.when(i % 2 == 0)(functools.partial(stage, 0, 1))
    pl.when(i % 2 == 1)(functools.partial(stage, 1, 0))


def _hyb_call(x, nw, w_pad, wvt, conv_w, conv_b, dt_bias, a_log, d_skip, norm_w, *, batch, tm=PROJ_TM):
    T, D = x.shape
    S = T // batch
    assert S % tm == 0 and tm % SSM_CHUNK == 0
    nt = T // tm
    cur = lambda i: (jnp.minimum(i, nt - 1), 0)
    cur_t = lambda i: (0, jnp.minimum(i, nt - 1))
    prev = lambda i: (jnp.maximum(i - 1, 0), 0)
    pad = lambda v: jnp.pad(v.astype(F32), (0, LANES - v.shape[0])).reshape(1, LANES)
    dsk = jnp.repeat(d_skip.astype(F32), SSM_HEAD_DIM).reshape(1, SSM_WIDTH)
    outs = [
        jax.ShapeDtypeStruct((DIFF_WIDTH, T), BF16),
        jax.ShapeDtypeStruct((T, DIFF_WIDTH), BF16),
        jax.ShapeDtypeStruct((DIFF_WIDTH, T), BF16),
        jax.ShapeDtypeStruct((T, SSM_WIDTH), BF16),
    ]
    out_specs = [pl.BlockSpec((DIFF_WIDTH, tm), cur_t), pl.BlockSpec((tm, DIFF_WIDTH), cur),
                 pl.BlockSpec((DIFF_WIDTH, tm), cur_t), pl.BlockSpec((tm, SSM_WIDTH), prev)]
    nbytes = ((w_pad.size + wvt.size) * 2 + 2 * tm * D * 4 + 2 * tm * 2 * 4 * DIFF_WIDTH
              + 2 * (tm + HALO) * SSM_XBC * 4 + 2 * tm * (SSM_WIDTH + LANES) * 4 + 16 * tm * SSM_XBC * 4)
    return pl.pallas_call(
        functools.partial(_hyb_kernel, tm=tm, tiles_per_seq=S // tm),
        grid=(nt + 1,),
        in_specs=[pl.BlockSpec((tm, D), cur), _resident((1, D)), _resident(w_pad.shape),
                  _resident(wvt.shape), _resident((SSM_CONV, SSM_XBC)), _resident((1, SSM_XBC)),
                  _resident((1, LANES)), _resident((1, LANES)), _resident((1, SSM_WIDTH)),
                  _resident((1, SSM_WIDTH))],
        out_specs=out_specs,
        out_shape=outs,
        scratch_shapes=[pltpu.VMEM((2, tm + HALO, SSM_XBC), F32), pltpu.VMEM((2, tm, SSM_WIDTH), F32),
                        pltpu.VMEM((2, tm, LANES), F32),
                        pltpu.VMEM((SSM_GROUPS, SSM_STATE, HEADS_PER_GROUP * SSM_HEAD_DIM), F32)],
        compiler_params=pltpu.CompilerParams(
            dimension_semantics=("arbitrary",), vmem_limit_bytes=_vmem_limit(nbytes)),
        name="hyb_ssd",
    )(x, nw.reshape(1, D), w_pad, wvt, conv_w.astype(F32), conv_b.reshape(1, SSM_XBC).astype(F32),
      pad(dt_bias), pad(a_log), dsk, norm_w.reshape(1, SSM_WIDTH).astype(F32))


def _sc_kernel(x_ref, xprev_ref, nw_ref, win_ref, cw_ref, wout_ref, o_ref, vpad_ref, bg_ref, *,
               tm, tiles_per_seq):
    i = pl.program_id(0)
    d = x_ref.shape[1]

    @pl.when(i == 0)
    def _():
        vpad_ref[...] = jnp.zeros(vpad_ref.shape, F32)
        bg_ref[...] = jnp.zeros(bg_ref.shape, F32)

    def finish_chunk(prev, c0, c1):
        conv = cw_ref[SC_CONV - 1:SC_CONV, c0:c1] * vpad_ref[prev, HALO:HALO + tm, c0:c1]
        for kk in range(SC_CONV - 1):
            o = HALO - (SC_CONV - 1) + kk
            conv = conv + cw_ref[kk:kk + 1, c0:c1] * vpad_ref[prev, o:o + tm, c0:c1]
        return _dot((bg_ref[prev, :, c0:c1] * conv).astype(BF16), wout_ref[c0:c1, :].astype(BF16))

    def stage(cur, prev):
        cw = 2 * LANES
        chunks = [(c0, c0 + cw) for c0 in range(0, d, cw)]
        h = _rms(x_ref[...], nw_ref[...], 1e-6).astype(BF16)
        bgate = _dot(h, win_ref[:, 0:d].astype(BF16))
        out = finish_chunk(prev, *chunks[0])
        cgate = _dot(h, win_ref[:, d:2 * d].astype(BF16))
        out = out + finish_chunk(prev, *chunks[1])
        u = _dot(h, win_ref[:, 2 * d:3 * d].astype(BF16))
        for c0, c1 in chunks[2:]:
            out = out + finish_chunk(prev, c0, c1)
        o_ref[...] = xprev_ref[...] + out

        tail = vpad_ref[prev, tm:tm + HALO, :]
        vpad_ref[cur, 0:HALO, :] = jnp.where(i % tiles_per_seq == 0, jnp.zeros_like(tail), tail)
        vpad_ref[cur, HALO:HALO + tm, :] = cgate * u
        bg_ref[cur] = bgate

    pl.when(i % 2 == 0)(functools.partial(stage, 0, 1))
    pl.when(i % 2 == 1)(functools.partial(stage, 1, 0))


def _sc_call(x, nw, w_in, conv_w, w_out, layer, *, batch, tm=PROJ_TM):
    T, D = x.shape
    S = T // batch
    assert S % tm == 0
    nt = T // tm
    cur = lambda i: (jnp.minimum(i, nt - 1), 0)
    prev = lambda i: (jnp.maximum(i - 1, 0), 0)
    nbytes = (w_in[layer].size + w_out[layer].size) * 4 + 6 * tm * D * 4 + 4 * tm * D * 4 + 8 * tm * D * 4
    return pl.pallas_call(
        functools.partial(_sc_kernel, tm=tm, tiles_per_seq=S // tm),
        grid=(nt + 1,),
        in_specs=[pl.BlockSpec((tm, D), cur), pl.BlockSpec((tm, D), prev), _resident((1, D)),
                  _layer_resident(w_in, layer), _resident(conv_w.shape), _layer_resident(w_out, layer)],
        out_specs=pl.BlockSpec((tm, D), prev),
        out_shape=jax.ShapeDtypeStruct((T, D), F32),
        scratch_shapes=[pltpu.VMEM((2, tm + HALO, D), F32), pltpu.VMEM((2, tm, D), F32)],
        compiler_params=pltpu.CompilerParams(
            dimension_semantics=("arbitrary",), vmem_limit_bytes=_vmem_limit(nbytes)),
        name="short_conv",
    )(x, x, nw.reshape(1, D), w_in, conv_w.astype(F32), w_out)


def kernel(x, rel_bias, final_norm_w, ffn1_norm, ffn1_wg, ffn1_wu, ffn1_wd, mix_norm, ffn2_norm, ffn2_wg, ffn2_wu, ffn2_wd, hyb_w_in, hyb_w_out, diff_lq1, diff_lk1, diff_lq2, diff_lk2, diff_subln_w, ssm_conv_w, ssm_conv_b, ssm_dt_bias, ssm_a_log, ssm_d, ssm_norm_w, sc_w_in, sc_conv_w, sc_w_out):
    B, S, D = x.shape
    T = B * S
    xt = x.reshape(T, D)
    bf = lambda w: w.astype(BF16)

    lam_inits = [0.8 - 0.6 * math.exp(-0.3 * i) for i in range(0, DEPTH, 2)]
    tiles, lams = _prep_call(rel_bias, diff_lq1, diff_lk1, diff_lq2, diff_lk2, lam_inits)

    mix = None
    for i in range(DEPTH):
        j = i // 2
        xt = _ffn_call(xt, ffn1_norm[i], ffn1_wg, ffn1_wu, ffn1_wd, i)
        if i % 2 == 0:
            w_in = hyb_w_in[j]
            qkv_end = 3 * DIFF_WIDTH
            w_pad = bf(jnp.pad(w_in, ((0, 0), (0, LANES - SSM_HEADS))))
            wvt = bf(jnp.concatenate([w_in[:, 0:DIFF_WIDTH], w_in[:, 2 * DIFF_WIDTH:qkv_end]], axis=1).T)
            q, k, vt, y = _hyb_call(xt, mix_norm[i], w_pad, wvt, ssm_conv_w[j], ssm_conv_b[j],
                                    ssm_dt_bias[j], ssm_a_log[j], ssm_d[j], ssm_norm_w[j], batch=B)
            o = _attn_call(q, k, vt, tiles, lams[j * SUBLANES:(j + 1) * SUBLANES], diff_subln_w[j],
                           batch=B, lam_init=lam_inits[j])
            mix = (o, y, hyb_w_out, j)
        else:
            xt = _sc_call(xt, mix_norm[i], sc_w_in, sc_conv_w[j], sc_w_out, j, batch=B)
            mix = None
        xt = _ffn_call(xt, ffn2_norm[i], ffn2_wg, ffn2_wu, ffn2_wd, i, mix=mix,
                       final_w=final_norm_w if i == DEPTH - 1 else None)
    return xt.reshape(B, S, D)
```

```python
import functools
import math

import jax
import jax.numpy as jnp
from jax import lax
from jax.experimental import pallas as pl
from jax.experimental.pallas import tpu as pltpu

F32 = jnp.float32
BF16 = jnp.bfloat16

DEPTH = 4
N_DIFF_HEADS = 4
DIFF_HEAD_DIM = 64
DIFF_WIDTH = N_DIFF_HEADS * 2 * DIFF_HEAD_DIM
NUM_BUCKETS = 32
MAX_DISTANCE = 128
SSM_HEADS = 8
SSM_HEAD_DIM = 64
SSM_WIDTH = SSM_HEADS * SSM_HEAD_DIM
SSM_GROUPS = 2
SSM_STATE = 128
SSM_CONV = 4
SSM_CHUNK = 128
HEADS_PER_GROUP = SSM_HEADS // SSM_GROUPS
SSM_XBC = SSM_WIDTH + 2 * SSM_GROUPS * SSM_STATE
SC_CONV = 3

LANES = 128
SUBLANES = 8
V7X_SCOPED_VMEM_CAP = 60000 * 1024

FFN_TM = 512
FFN_FCHUNK = 256
PROJ_TM = 512
ATT_T = 512
ATT_STRIP = 256
ATT_ONES = 16
ATT_MAX_UNROLL = 26
ATT_FAR_TILE = 2
HALO = SUBLANES

NEG_BIG = -1e30
LOG2E = math.log2(math.e)


def _vmem_limit(nbytes):
    return int(min(V7X_SCOPED_VMEM_CAP, nbytes * 5 // 4 + (8 << 20)))


def _dot(a, b):
    return jnp.dot(a, b, preferred_element_type=F32)


def _dot_nt(a, b):
    return lax.dot_general(a, b, (((1,), (1,)), ((), ())), preferred_element_type=F32)


def _rms(x, w, eps):
    ms = jnp.mean(x * x, axis=-1, keepdims=True)
    return x * lax.rsqrt(ms + eps) * w


def _silu(x):
    h = 0.5 * x
    return h + h * jnp.tanh(h)


def _resident(shape):
    nd = len(shape)
    return pl.BlockSpec(shape, lambda *_: (0,) * nd, pipeline_mode=pl.Buffered(1))


def _ffn_kernel(*refs, pre_mix, final_norm, f_total):
    refs = list(refs)
    x_ref = refs.pop(0)
    if pre_mix:
        o_ref, y_ref, wo_ref = refs.pop(0), refs.pop(0), refs.pop(0)
    nw_ref, wg_ref, wu_ref, wd_ref = refs.pop(0), refs.pop(0), refs.pop(0), refs.pop(0)
    if final_norm:
        fw_ref = refs.pop(0)
    out_ref = refs.pop(0)

    x = x_ref[...]
    if pre_mix:
        mixed = jnp.concatenate([o_ref[...], y_ref[...]], axis=1)
        x = x + _dot(mixed, wo_ref[...].astype(BF16))
    h = (x * nw_ref[...]).astype(BF16)
    r = lax.rsqrt(jnp.mean(x * x, axis=-1, keepdims=True) + 1e-6)
    acts = []
    for c0 in range(0, f_total, FFN_FCHUNK):
        c1 = c0 + FFN_FCHUNK
        g = _dot(h, wg_ref[:, c0:c1].astype(BF16)) * r
        u = _dot(h, wu_ref[:, c0:c1].astype(BF16)) * r
        acts.append((_silu(g) * u).astype(BF16))
    acc = _dot(jnp.concatenate(acts, axis=1), wd_ref[...].astype(BF16))
    y = x + 0.5 * acc
    if final_norm:
        y = _rms(y, fw_ref[...], 1e-6)
    out_ref[...] = y


def _layer_resident(stacked, layer):
    _, r, c = stacked.shape
    return pl.BlockSpec((None, r, c), lambda *_: (layer, 0, 0), pipeline_mode=pl.Buffered(1))


def _ffn_call(x, nw, wg, wu, wd, layer, *, mix=None, final_w=None, tm=FFN_TM):
    T, D = x.shape
    F = wg.shape[2]
    assert T % tm == 0 and F % FFN_FCHUNK == 0
    row = lambda i: (i, 0)
    args = [x]
    in_specs = [pl.BlockSpec((tm, D), row)]
    nbytes = 4 * tm * D * 4 + 3 * D * F * 4 + 6 * tm * FFN_FCHUNK * 4 + 3 * tm * D * 4
    if mix is not None:
        o, y, wo, mix_layer = mix
        args += [o, y, wo]
        in_specs += [pl.BlockSpec((tm, o.shape[1]), row), pl.BlockSpec((tm, y.shape[1]), row),
                     _layer_resident(wo, mix_layer)]
        nbytes += wo.shape[1] * wo.shape[2] * 4 + 4 * tm * o.shape[1] * 2 * 2
    args += [nw.reshape(1, D), wg, wu, wd]
    in_specs += [_resident((1, D)), _layer_resident(wg, layer), _layer_resident(wu, layer),
                 _layer_resident(wd, layer)]
    if final_w is not None:
        args.append(final_w.reshape(1, D))
        in_specs.append(_resident((1, D)))
    kern = functools.partial(_ffn_kernel, pre_mix=mix is not None, final_norm=final_w is not None,
                             f_total=F)
    return pl.pallas_call(
        kern,
        grid=(T // tm,),
        in_specs=in_specs,
        out_specs=pl.BlockSpec((tm, D), row),
        out_shape=jax.ShapeDtypeStruct((T, D), F32),
        compiler_params=pltpu.CompilerParams(
            dimension_semantics=("arbitrary",), vmem_limit_bytes=_vmem_limit(nbytes)),
        name="ffn",
    )(*args)


def _prep_kernel(tab_ref, lq1_ref, lk1_ref, lq2_ref, lk2_ref, tile_ref, lam_ref, *, t, lam_inits):
    hd = pl.program_id(0)
    key = lax.broadcasted_iota(jnp.int32, (t, t), 0)
    qry = lax.broadcasted_iota(jnp.int32, (t, t), 1)
    max_exact = NUM_BUCKETS // 2
    far = tab_ref[hd, NUM_BUCKETS - 1]
    tile_ref[ATT_FAR_TILE] = jnp.zeros((t, t), F32)
    for off in range(ATT_FAR_TILE):
        dist = off * t + qry - key
        d = jnp.maximum(dist, 0)
        large = max_exact + (
            jnp.log(jnp.maximum(d, 1).astype(F32) / max_exact)
            / math.log(MAX_DISTANCE / max_exact) * (NUM_BUCKETS - max_exact)).astype(jnp.int32)
        bucket = jnp.where(d < max_exact, d, jnp.minimum(large, NUM_BUCKETS - 1))
        bias = jnp.full((t, t), far, F32)
        for b in range(NUM_BUCKETS - 1):
            bias = jnp.where(bucket == b, tab_ref[hd, b], bias)
        val = (bias - far) * LOG2E
        if off == 0:
            val = jnp.where(dist >= 0, val, NEG_BIG)
        tile_ref[off] = val
    for j, lam_init in enumerate(lam_inits):
        s1 = jnp.sum(lq1_ref[j:j + 1, :] * lk1_ref[j:j + 1, :], axis=-1, keepdims=True)
        s2 = jnp.sum(lq2_ref[j:j + 1, :] * lk2_ref[j:j + 1, :], axis=-1, keepdims=True)
        lam = jnp.exp(s1) - jnp.exp(s2) + lam_init
        lam_ref[j * SUBLANES:(j + 1) * SUBLANES, :] = jnp.broadcast_to(lam, (SUBLANES, LANES))


def _prep_call(rel_bias, lq1, lk1, lq2, lk2, lam_inits, *, t=ATT_T):
    ne = lq1.shape[0]
    H = rel_bias.shape[1]
    whole = lambda shape: pl.BlockSpec(shape, lambda h: (0,) * len(shape))
    return pl.pallas_call(
        functools.partial(_prep_kernel, t=t, lam_inits=tuple(lam_inits)),
        grid=(H,),
        in_specs=[pl.BlockSpec(memory_space=pltpu.SMEM)] + [whole(lq1.shape)] * 4,
        out_specs=[pl.BlockSpec((None, ATT_FAR_TILE + 1, t, t), lambda h: (h, 0, 0, 0)),
                   whole((ne * SUBLANES, LANES))],
        out_shape=[jax.ShapeDtypeStruct((H, ATT_FAR_TILE + 1, t, t), F32),
                   jax.ShapeDtypeStruct((ne * SUBLANES, LANES), F32)],
        compiler_params=pltpu.CompilerParams(
            dimension_semantics=("arbitrary",), vmem_limit_bytes=_vmem_limit(12 * t * t * 4)),
        name="attn_prep",
    )(rel_bias.T.astype(F32), lq1, lk1, lq2, lk2)


def _attn_kernel(qt_ref, k_ref, vt_ref, tile_ref, lam_ref, sw_ref, o_ref,
                 qm_ref, m_ref, acc_ref, s_ref, smax_ref, *, t, nq, unroll, q_split, lam_init):
    e = 2 * DIFF_HEAD_DIM
    first = lax.broadcasted_iota(jnp.int32, (LANES, t), 0) < DIFF_HEAD_DIM
    ones = jnp.ones((ATT_ONES, t), BF16)
    chains = [(w, slice(c0, c0 + ATT_STRIP)) for c0 in range(0, t, ATT_STRIP) for w in range(2)]

    def rows(blk):
        return pl.ds(pl.multiple_of(blk * t, t), t)

    def prepare(qi, carry):
        q = qt_ref[:, rows(qi)]
        zero = jnp.zeros_like(q)
        qm_ref[0, :, rows(qi)] = jnp.where(first, q, zero)
        qm_ref[1, :, rows(qi)] = jnp.where(first, zero, q)
        m_ref[qi] = jnp.full(m_ref.shape[1:], NEG_BIG, F32)
        acc_ref[qi] = jnp.zeros(acc_ref.shape[1:], F32)
        return carry

    def score_chain(ci, qi, j, slot, k, biased):
        w, cols = chains[ci]
        q_cols = pl.ds(pl.multiple_of(qi * t + cols.start, ATT_STRIP), ATT_STRIP)
        s = _dot(k, qm_ref[w, :, q_cols])
        if biased:
            near = jnp.minimum(qi - j, ATT_FAR_TILE)
            s = s + tile_ref[near, :, cols]
        s_ref[slot, ci] = s
        smax_ref[slot, ci] = jnp.max(s, axis=0, keepdims=True)

    def value_chain(ci, qi, slot, vta):
        w, cols = chains[ci]
        m_old = m_ref[qi, w, :, cols]
        m_new = jnp.maximum(m_old, smax_ref[slot, ci])
        p = jnp.exp2(s_ref[slot, ci] - m_new).astype(BF16)
        acc_ref[qi, w, :, cols] = jnp.exp2(m_old - m_new) * acc_ref[qi, w, :, cols] + _dot(vta, p)
        m_ref[qi, w, :, cols] = m_new

    def finalize(pair, carry):
        for qi in (2 * pair, 2 * pair + 1):
            lam = lam_ref[0:1, 0:1]
            o1 = acc_ref[qi, 0, 0:e, :] * (1.0 / acc_ref[qi, 0, e:e + 1, :])
            o2 = acc_ref[qi, 1, 0:e, :] * (1.0 / acc_ref[qi, 1, e:e + 1, :])
            out_t = o1 - lam * o2
            ms = jnp.mean(out_t * out_t, axis=0, keepdims=True)
            out_t = out_t * lax.rsqrt(ms + 1e-5) * sw_ref[...] * (1.0 - lam_init)
            o_ref[rows(qi), :] = out_t.T.astype(BF16)
        return carry

    def sweep(first_pair, next_pair, last_pair, count, biased):
        def step(qv, jv, slot):
            qs, js = next_pair(qv, jv)
            past_end = qs >= nq
            qs = jnp.where(past_end, last_pair[0], qs)
            js = jnp.where(past_end, last_pair[1], js)
            k_next = k_ref[rows(js), :]
            vta = jnp.concatenate([vt_ref[:, rows(jv)], ones], axis=0)
            for ci in range(len(chains)):
                score_chain(ci, qs, js, 1 - slot, k_next, biased)
                value_chain(ci, qv, slot, vta)
            return qs, js

        cap = unroll if biased else ATT_MAX_UNROLL
        trip = max(u for u in range(unroll, cap + 1, 2) if count % u == 0)

        def body(_, carry):
            qv, jv = carry
            for u in range(trip):
                qv, jv = step(qv, jv, u % 2)
            return qv, jv

        k0 = k_ref[rows(first_pair[1]), :]
        for ci in range(len(chains)):
            score_chain(ci, first_pair[0], first_pair[1], 0, k0, biased)
        lax.fori_loop(0, count // trip, body, (jnp.int32(first_pair[0]), jnp.int32(first_pair[1])))

    def next_near(qv, jv):
        wrap = jv == qv
        qn = qv + 1
        return jnp.where(wrap, qn, qv), jnp.where(wrap, jnp.where(qn < q_split, 0, qn - 1), jv + 1)

    def next_far(qv, jv):
        wrap = jv == qv - ATT_FAR_TILE
        return jnp.where(wrap, qv + 1, qv), jnp.where(wrap, 0, jv + 1)

    n_near = q_split * (q_split + 1) // 2 + ATT_FAR_TILE * (nq - q_split)
    n_far = nq * (nq + 1) // 2 - n_near
    lax.fori_loop(0, nq, prepare, 0)
    sweep((0, 0), next_near, (nq - 1, nq - 1), n_near, True)
    if n_far:
        sweep((q_split, 0), next_far, (nq - 1, nq - 1 - ATT_FAR_TILE), n_far, False)
    lax.fori_loop(0, nq // 2, finalize, 0)


def _attn_plan(nq):
    for unroll in (8, 4, 2):
        for q_split in range(ATT_FAR_TILE, nq + 1):
            n_near = q_split * (q_split + 1) // 2 + ATT_FAR_TILE * (nq - q_split)
            n_far = nq * (nq + 1) // 2 - n_near
            if n_near % unroll == 0 and n_far % unroll == 0:
                return unroll, q_split
    raise ValueError(f"unsupported number of query blocks: {nq}")


def _attn_call(qt, k, vt, tiles, lam, subln_w, *, batch, lam_init, t=ATT_T):
    T, W = k.shape
    S = T // batch
    H = W // LANES
    nq = S // t
    e = 2 * DIFF_HEAD_DIM
    assert e == LANES and S % t == 0 and nq % 2 == 0
    unroll, q_split = _attn_plan(nq)
    n_chains = 2 * (t // ATT_STRIP)
    n_tiles = ATT_FAR_TILE + 1
    nbytes = (2 * 4 * (S * LANES * 2) + 2 * n_tiles * t * t * 4 + 2 * S * LANES * 2
              + nq * 2 * (e + ATT_ONES + SUBLANES) * t * 4 + 2 * n_chains * t * ATT_STRIP * 4 + 2 * t * t * 4)
    return pl.pallas_call(
        functools.partial(_attn_kernel, t=t, nq=nq, unroll=unroll, q_split=q_split, lam_init=lam_init),
        grid=(batch, H),
        in_specs=[
            pl.BlockSpec((LANES, S), lambda b, h: (h, b)),
            pl.BlockSpec((S, LANES), lambda b, h: (b, h)),
            pl.BlockSpec((LANES, S), lambda b, h: (h, b)),
            pl.BlockSpec((None, n_tiles, t, t), lambda b, h: (h, 0, 0, 0)),
            pl.BlockSpec((SUBLANES, LANES), lambda b, h: (0, 0)),
            pl.BlockSpec((e, 1), lambda b, h: (0, 0)),
        ],
        out_specs=pl.BlockSpec((S, LANES), lambda b, h: (b, h)),
        out_shape=jax.ShapeDtypeStruct((T, W), BF16),
        scratch_shapes=[pltpu.VMEM((2, LANES, S), BF16),
                        pltpu.VMEM((nq, 2, 1, t), F32), pltpu.VMEM((nq, 2, e + ATT_ONES, t), F32),
                        pltpu.VMEM((2, n_chains, t, ATT_STRIP), F32),
                        pltpu.VMEM((2, n_chains, 1, ATT_STRIP), F32)],
        compiler_params=pltpu.CompilerParams(
            dimension_semantics=("arbitrary", "arbitrary"), vmem_limit_bytes=_vmem_limit(nbytes)),
        name="diff_attn",
    )(qt, k, vt, tiles, lam, subln_w.reshape(e, 1))


def _split3(a):
    hi = a.astype(BF16)
    r1 = a - hi.astype(F32)
    mid = r1.astype(BF16)
    lo = (r1 - mid.astype(F32)).astype(BF16)
    return hi, mid, lo


def _hyb_kernel(x_ref, nw_ref, w_ref, wvt_ref, cw_ref, cb_ref, dtb_ref, alog_ref, dsk_ref, ynw_ref,
                qt_ref, k_ref, vt_ref, y_ref, xpad_ref, z_scr, dt_scr, st_ref, *, tm, tiles_per_seq):
    i = pl.program_id(0)
    L = SSM_CHUNK
    gw = HEADS_PER_GROUP * SSM_HEAD_DIM
    qw = DIFF_WIDTH
    c_z = 2 * qw + DIFF_WIDTH
    c_xbc = c_z + SSM_WIDTH
    c_dt = c_xbc + SSM_XBC

    @pl.when(i == 0)
    def _():
        xpad_ref[...] = jnp.zeros(xpad_ref.shape, F32)
        z_scr[...] = jnp.zeros(z_scr.shape, F32)
        dt_scr[...] = jnp.zeros(dt_scr.shape, F32)
        st_ref[...] = jnp.zeros(st_ref.shape, F32)

    def stage(cur, prev):
        row = lax.broadcasted_iota(jnp.int32, (L, L), 0)
        col = lax.broadcasted_iota(jnp.int32, (L, L), 1)
        causal = row >= col
        tri = jnp.where(causal, 1.0, 0.0).astype(BF16)
        lane_g = lax.broadcasted_iota(jnp.int32, (L, gw), 1) // SSM_HEAD_DIM
        lane_lo = lax.broadcasted_iota(jnp.int32, (L, LANES), 1) < SSM_HEAD_DIM
        lane_lo1 = lax.broadcasted_iota(jnp.int32, (1, LANES), 1) < SSM_HEAD_DIM
        a_neg = -jnp.exp(alog_ref[...])
        keep_state = jnp.where((i - 1) % tiles_per_seq == 0, 0.0, 1.0)

        def ssd_chunk(c):
            r0 = c * L
            dt = dt_scr[prev, r0:r0 + L, :]
            hi, mid, lo = _split3(a_neg * dt)
            cs = _dot(tri, hi) + _dot(tri, mid) + _dot(tri, lo)
            yield

            conv = cb_ref[...]
            for kk in range(SSM_CONV):
                o = HALO - (SSM_CONV - 1) + kk + r0
                conv = conv + cw_ref[kk:kk + 1, :] * xpad_ref[prev, o:o + L, :]
            xc = _silu(conv)
            xs = xc[:, 0:SSM_WIDTH]
            bgs, cgs, gmats = [], [], []
            for g in range(SSM_GROUPS):
                b0 = SSM_WIDTH + g * SSM_STATE
                c0 = SSM_WIDTH + SSM_GROUPS * SSM_STATE + g * SSM_STATE
                bgs.append(xc[:, b0:b0 + SSM_STATE])
                cgs.append(xc[:, c0:c0 + SSM_STATE].astype(BF16))
                gmats.append(_dot_nt(cgs[g], bgs[g].astype(BF16)))
            yield

            cs_t = cs.T
            dt_t = dt.T
            ydiags, st_news, e_mats, decs_g = [], [], [], []
            for g in range(SSM_GROUPS):
                bg_t = bgs[g].T
                xg = xs[:, g * gw:(g + 1) * gw]
                ydiag = None
                st_new = None
                ecols, decs = [], []
                for hh in range(HEADS_PER_GROUP):
                    h = g * HEADS_PER_GROUP + hh
                    colb = jnp.broadcast_to(cs[:, h:h + 1], (L, L))
                    rowb = cs_t[h:h + 1, :]
                    dtrow = dt_t[h:h + 1, :]
                    decay = jnp.exp(jnp.where(causal, colb - rowb, NEG_BIG))
                    mh = (gmats[g] * decay * dtrow).astype(BF16)
                    xm = jnp.where(lane_g == hh, xg, 0.0).astype(BF16)
                    last = colb[L - 1:L, :]
                    wrow = dtrow * jnp.exp(last - rowb)
                    btw = (bg_t * wrow).astype(BF16)
                    d1 = _dot(mh, xm)
                    d2 = _dot(btw, xm)
                    ydiag = d1 if ydiag is None else ydiag + d1
                    st_new = d2 if st_new is None else st_new + d2
                    ecols.append(jnp.exp(colb))
                    decs.append(jnp.exp(last))
                ydiags.append(ydiag)
                st_news.append(st_new)
                e_mats.append(jnp.concatenate([jnp.where(lane_lo, ecols[0], ecols[1]),
                                               jnp.where(lane_lo, ecols[2], ecols[3])], axis=1))
                decs_g.append(jnp.concatenate([jnp.where(lane_lo1, decs[0], decs[1]),
                                               jnp.where(lane_lo1, decs[2], decs[3])], axis=1))
                if g == 0:
                    yield

            ys = []
            for g in range(SSM_GROUPS):
                st_old = st_ref[g] * keep_state if c == 0 else st_ref[g]
                ys.append(ydiags[g] + _dot(cgs[g], st_old.astype(BF16)) * e_mats[g])
                st_ref[g] = st_old * decs_g[g] + st_news[g]
            y = jnp.concatenate(ys, axis=1) + dsk_ref[...] * xs
            gated = y * z_scr[prev, r0:r0 + L, :]
            outs = []
            for g in range(SSM_GROUPS):
                part = gated[:, g * gw:(g + 1) * gw]
                ms = jnp.mean(part * part, axis=-1, keepdims=True)
                outs.append(part * lax.rsqrt(ms + 1e-5))
            y_ref[r0:r0 + L, :] = (jnp.concatenate(outs, axis=1) * ynw_ref[...]).astype(BF16)
            yield

        h = _rms(x_ref[...], nw_ref[...], 1e-6).astype(BF16)
        scale = DIFF_HEAD_DIM ** -0.5 * LOG2E
        piece = 2 * LANES

        def pieces():
            for r in range(0, qw, piece):
                qt_ref[r:r + piece, :] = (_dot_nt(wvt_ref[r:r + piece, :], h) * scale).astype(BF16)
                yield
            for c in range(0, qw, piece):
                k_ref[:, c:c + piece] = _dot(h, w_ref[:, qw + c:qw + c + piece]).astype(BF16)
                yield
            for r in range(0, DIFF_WIDTH, piece):
                vt_ref[r:r + piece, :] = _dot_nt(wvt_ref[qw + r:qw + r + piece, :], h).astype(BF16)
                yield
            for c in range(0, SSM_WIDTH, piece):
                z_scr[cur, :, c:c + piece] = _silu(_dot(h, w_ref[:, c_z + c:c_z + c + piece]))
                yield
            dt_raw = _dot(h, w_ref[:, c_dt:c_dt + LANES]) + dtb_ref[...]
            dt_scr[cur] = jnp.maximum(dt_raw, 0.0) + jnp.log1p(jnp.exp(-jnp.abs(dt_raw)))
            yield
            tail = xpad_ref[prev, tm:tm + HALO, :]
            xpad_ref[cur, 0:HALO, :] = jnp.where(i % tiles_per_seq == 0, jnp.zeros_like(tail), tail)
            for c in range(0, SSM_XBC, piece):
                xpad_ref[cur, HALO:HALO + tm, c:c + piece] = _dot(h, w_ref[:, c_xbc + c:c_xbc + c + piece])
                yield

        pieces_after_group = (0, 2, 1, 0)
        proj = pieces()
        for c in range(tm // L):
            for group, _ in enumerate(ssd_chunk(c)):
                for _ in range(pieces_after_group[group]):
                    next(proj, None)
        for _ in proj:
            pass

    pl.when(i % 2 == 0)(functools.partial(stage, 0, 1))
    pl.when(i % 2 == 1)(functools.partial(stage, 1, 0))


def _hyb_call(x, nw, w_pad, wvt, conv_w, conv_b, dt_bias, a_log, d_skip, norm_w, *, batch, tm=PROJ_TM):
    T, D = x.shape
    S = T // batch
    assert S % tm == 0 and tm % SSM_CHUNK == 0
    nt = T // tm
    cur = lambda i: (jnp.minimum(i, nt - 1), 0)
    cur_t = lambda i: (0, jnp.minimum(i, nt - 1))
    prev = lambda i: (jnp.maximum(i - 1, 0), 0)
    pad = lambda v: jnp.pad(v.astype(F32), (0, LANES - v.shape[0])).reshape(1, LANES)
    dsk = jnp.repeat(d_skip.astype(F32), SSM_HEAD_DIM).reshape(1, SSM_WIDTH)
    outs = [
        jax.ShapeDtypeStruct((DIFF_WIDTH, T), BF16),
        jax.ShapeDtypeStruct((T, DIFF_WIDTH), BF16),
        jax.ShapeDtypeStruct((DIFF_WIDTH, T), BF16),
        jax.ShapeDtypeStruct((T, SSM_WIDTH), BF16),
    ]
    out_specs = [pl.BlockSpec((DIFF_WIDTH, tm), cur_t), pl.BlockSpec((tm, DIFF_WIDTH), cur),
                 pl.BlockSpec((DIFF_WIDTH, tm), cur_t), pl.BlockSpec((tm, SSM_WIDTH), prev)]
    nbytes = ((w_pad.size + wvt.size) * 2 + 2 * tm * D * 4 + 2 * tm * 2 * 4 * DIFF_WIDTH
              + 2 * (tm + HALO) * SSM_XBC * 4 + 2 * tm * (SSM_WIDTH + LANES) * 4 + 16 * tm * SSM_XBC * 4)
    return pl.pallas_call(
        functools.partial(_hyb_kernel, tm=tm, tiles_per_seq=S // tm),
        grid=(nt + 1,),
        in_specs=[pl.BlockSpec((tm, D), cur), _resident((1, D)), _resident(w_pad.shape),
                  _resident(wvt.shape), _resident((SSM_CONV, SSM_XBC)), _resident((1, SSM_XBC)),
                  _resident((1, LANES)), _resident((1, LANES)), _resident((1, SSM_WIDTH)),
                  _resident((1, SSM_WIDTH))],
        out_specs=out_specs,
        out_shape=outs,
        scratch_shapes=[pltpu.VMEM((2, tm + HALO, SSM_XBC), F32), pltpu.VMEM((2, tm, SSM_WIDTH), F32),
                        pltpu.VMEM((2, tm, LANES), F32),
                        pltpu.VMEM((SSM_GROUPS, SSM_STATE, HEADS_PER_GROUP * SSM_HEAD_DIM), F32)],
        compiler_params=pltpu.CompilerParams(
            dimension_semantics=("arbitrary",), vmem_limit_bytes=_vmem_limit(nbytes)),
        name="hyb_ssd",
    )(x, nw.reshape(1, D), w_pad, wvt, conv_w.astype(F32), conv_b.reshape(1, SSM_XBC).astype(F32),
      pad(dt_bias), pad(a_log), dsk, norm_w.reshape(1, SSM_WIDTH).astype(F32))


def _sc_kernel(x_ref, xprev_ref, nw_ref, win_ref, cw_ref, wout_ref, o_ref, vpad_ref, bg_ref, *,
               tm, tiles_per_seq):
    i = pl.program_id(0)
    d = x_ref.shape[1]

    @pl.when(i == 0)
    def _():
        vpad_ref[...] = jnp.zeros(vpad_ref.shape, F32)
        bg_ref[...] = jnp.zeros(bg_ref.shape, F32)

    def finish_chunk(prev, c0, c1):
        conv = cw_ref[SC_CONV - 1:SC_CONV, c0:c1] * vpad_ref[prev, HALO:HALO + tm, c0:c1]
        for kk in range(SC_CONV - 1):
            o = HALO - (SC_CONV - 1) + kk
            conv = conv + cw_ref[kk:kk + 1, c0:c1] * vpad_ref[prev, o:o + tm, c0:c1]
        return _dot((bg_ref[prev, :, c0:c1] * conv).astype(BF16), wout_ref[c0:c1, :].astype(BF16))

    def stage(cur, prev):
        cw = 2 * LANES
        chunks = [(c0, c0 + cw) for c0 in range(0, d, cw)]
        h = _rms(x_ref[...], nw_ref[...], 1e-6).astype(BF16)
        bgate = _dot(h, win_ref[:, 0:d].astype(BF16))
        out = finish_chunk(prev, *chunks[0])
        cgate = _dot(h, win_ref[:, d:2 * d].astype(BF16))
        out = out + finish_chunk(prev, *chunks[1])
        u = _dot(h, win_ref[:, 2 * d:3 * d].astype(BF16))
        for c0, c1 in chunks[2:]:
            out = out + finish_chunk(prev, c0, c1)
        o_ref[...] = xprev_ref[...] + out

        tail = vpad_ref[prev, tm:tm + HALO, :]
        vpad_ref[cur, 0:HALO, :] = jnp.where(i % tiles_per_seq == 0, jnp.zeros_like(tail), tail)
        vpad_ref[cur, HALO:HALO + tm, :] = cgate * u
        bg_ref[cur] = bgate

    pl.when(i % 2 == 0)(functools.partial(stage, 0, 1))
    pl.when(i % 2 == 1)(functools.partial(stage, 1, 0))


def _sc_call(x, nw, w_in, conv_w, w_out, layer, *, batch, tm=PROJ_TM):
    T, D = x.shape
    S = T // batch
    assert S % tm == 0
    nt = T // tm
    cur = lambda i: (jnp.minimum(i, nt - 1), 0)
    prev = lambda i: (jnp.maximum(i - 1, 0), 0)
    nbytes = (w_in[layer].size + w_out[layer].size) * 4 + 6 * tm * D * 4 + 4 * tm * D * 4 + 8 * tm * D * 4
    return pl.pallas_call(
        functools.partial(_sc_kernel, tm=tm, tiles_per_seq=S // tm),
        grid=(nt + 1,),
        in_specs=[pl.BlockSpec((tm, D), cur), pl.BlockSpec((tm, D), prev), _resident((1, D)),
                  _layer_resident(w_in, layer), _resident(conv_w.shape), _layer_resident(w_out, layer)],
        out_specs=pl.BlockSpec((tm, D), prev),
        out_shape=jax.ShapeDtypeStruct((T, D), F32),
        scratch_shapes=[pltpu.VMEM((2, tm + HALO, D), F32), pltpu.VMEM((2, tm, D), F32)],
        compiler_params=pltpu.CompilerParams(
            dimension_semantics=("arbitrary",), vmem_limit_bytes=_vmem_limit(nbytes)),
        name="short_conv",
    )(x, x, nw.reshape(1, D), w_in, conv_w.astype(F32), w_out)


def kernel(x, rel_bias, final_norm_w, ffn1_norm, ffn1_wg, ffn1_wu, ffn1_wd, mix_norm, ffn2_norm, ffn2_wg, ffn2_wu, ffn2_wd, hyb_w_in, hyb_w_out, diff_lq1, diff_lk1, diff_lq2, diff_lk2, diff_subln_w, ssm_conv_w, ssm_conv_b, ssm_dt_bias, ssm_a_log, ssm_d, ssm_norm_w, sc_w_in, sc_conv_w, sc_w_out):
    B, S, D = x.shape
    T = B * S
    xt = x.reshape(T, D)
    bf = lambda w: w.astype(BF16)

    lam_inits = [0.8 - 0.6 * math.exp(-0.3 * i) for i in range(0, DEPTH, 2)]
    tiles, lams = _prep_call(rel_bias, diff_lq1, diff_lk1, diff_lq2, diff_lk2, lam_inits)

    mix = None
    for i in range(DEPTH):
        j = i // 2
        xt = _ffn_call(xt, ffn1_norm[i], ffn1_wg, ffn1_wu, ffn1_wd, i)
        if i % 2 == 0:
            w_in = hyb_w_in[j]
            qkv_end = 3 * DIFF_WIDTH
            w_pad = bf(jnp.pad(w_in, ((0, 0), (0, LANES - SSM_HEADS))))
            wvt = bf(jnp.concatenate([w_in[:, 0:DIFF_WIDTH], w_in[:, 2 * DIFF_WIDTH:qkv_end]], axis=1).T)
            q, k, vt, y = _hyb_call(xt, mix_norm[i], w_pad, wvt, ssm_conv_w[j], ssm_conv_b[j],
                                    ssm_dt_bias[j], ssm_a_log[j], ssm_d[j], ssm_norm_w[j], batch=B)
            o = _attn_call(q, k, vt, tiles, lams[j * SUBLANES:(j + 1) * SUBLANES], diff_subln_w[j],
                           batch=B, lam_init=lam_inits[j])
            mix = (o, y, hyb_w_out, j)
        else:
            xt = _sc_call(xt, mix_norm[i], sc_w_in, sc_conv_w[j], sc_w_out, j, batch=B)
            mix = None
        xt = _ffn_call(xt, ffn2_norm[i], ffn2_wg, ffn2_wu, ffn2_wd, i, mix=mix,
                       final_w=final_norm_w if i == DEPTH - 1 else None)
    return xt.reshape(B, S, D)
```
